```python
import math
import jax
import jax.numpy as jnp
from jax import lax
import numpy as np

D_MODEL = 2048
BATCH = 4
SEQ = 2048
DEPTH = 4

HEAD_DIM = 64
N_MIXERS = 4
MIXER_HEADS = D_MODEL // (N_MIXERS * HEAD_DIM)
GROUP_WIDTH = MIXER_HEADS * HEAD_DIM
MIX_WIDTH = N_MIXERS * GROUP_WIDTH
BLOCK = 128

A_HEADS = MIXER_HEADS
A_KV_HEADS = A_HEADS // 4
A_WINDOW = 128
B_HEADS = MIXER_HEADS
B_NOPE_DIM = HEAD_DIM
B_ROPE_DIM = HEAD_DIM // 2
B_V_DIM = HEAD_DIM
B_Q_LORA = 3 * D_MODEL // 16
B_KV_LORA = 4 * B_V_DIM
ROPE_THETA = 10000.0
C_HEADS = MIXER_HEADS
D_HEADS = MIXER_HEADS
D_PATTERNS = ((128, 1), (512, 4), (2048, 16))
NUM_BUCKETS = 32
T5_MAX_DISTANCE = 2048
D_FF = 5632
N_EXPERTS = 8
TOP_K = 2
D_FF_EXPERT = D_FF

IN_SIZES = (A_HEADS * HEAD_DIM, A_KV_HEADS * HEAD_DIM, A_KV_HEADS * HEAD_DIM,
            B_Q_LORA, B_KV_LORA, B_ROPE_DIM,
            GROUP_WIDTH, GROUP_WIDTH, GROUP_WIDTH,
            GROUP_WIDTH, GROUP_WIDTH, GROUP_WIDTH)
IN_WIDTH = sum(IN_SIZES)

NEG_INF = -1e30
LN_EPS = 1e-5
RMS_EPS = 1e-6

kernel_name = 'hybrid_parallel_heads_deepnorm_moe'


def layer_norm(x, g, b):
    xf = x.astype(jnp.float32)
    mu = jnp.mean(xf, axis=-1, keepdims=True)
    var = jnp.mean(jnp.square(xf - mu), axis=-1, keepdims=True)
    return ((xf - mu) * lax.rsqrt(var + LN_EPS) * g + b).astype(x.dtype)


def rms_norm(x, g):
    xf = x.astype(jnp.float32)
    return (xf * lax.rsqrt(jnp.mean(jnp.square(xf), axis=-1, keepdims=True) + RMS_EPS) * g).astype(x.dtype)


def split_columns(h, sizes):
    outs, start = [], 0
    for s in sizes:
        outs.append(h[..., start:start + s])
        start += s
    return outs


def to_heads(t, n_heads):
    b, s, _ = t.shape
    return t.reshape(b, s, n_heads, -1).transpose(0, 2, 1, 3)


def merge_heads(o):
    b, h, s, d = o.shape
    return o.transpose(0, 2, 1, 3).reshape(b, s, h * d)


def t5_bucket(dist):
    max_exact = NUM_BUCKETS // 2
    d = jnp.maximum(dist, 1).astype(jnp.float32)
    large = max_exact + (jnp.log(d / max_exact) / math.log(T5_MAX_DISTANCE / max_exact)
                         * (NUM_BUCKETS - max_exact)).astype(jnp.int32)
    large = jnp.minimum(large, NUM_BUCKETS - 1)
    return jnp.where(dist < max_exact, dist, large)


def band_bias(table, stride):
    dist = jnp.maximum(jnp.arange(BLOCK)[:, None] + BLOCK - jnp.arange(2 * BLOCK)[None, :], 0) * stride
    return jnp.transpose(table[t5_bucket(dist)], (2, 0, 1))


def rope_tables(s):
    inv = ROPE_THETA ** (-jnp.arange(0, B_ROPE_DIM, 2, dtype=jnp.float32) / B_ROPE_DIM)
    ang = jnp.arange(s, dtype=jnp.float32)[:, None] * inv[None, :]
    return jnp.cos(ang), jnp.sin(ang)


def apply_rope(t, cos, sin):
    t1, t2 = jnp.split(t.astype(jnp.float32), 2, axis=-1)
    c, s = cos[None, :, None, :], sin[None, :, None, :]
    return jnp.concatenate([t1 * c - t2 * s, t1 * s + t2 * c], axis=-1).astype(t.dtype)


def banded_attention(q, k, v, bias, max_dist, sinks=None):
    b, hq, length, dh = q.shape
    hkv = k.shape[1]
    g = hq // hkv
    nb = -(-length // BLOCK)
    lp = nb * BLOCK
    pad = lp - length
    qb = jnp.pad(q, ((0, 0), (0, 0), (0, pad), (0, 0))).reshape(b, hkv, g, nb, BLOCK, dh)

    def band(t):
        tp = jnp.pad(t, ((0, 0), (0, 0), (BLOCK, pad), (0, 0)))
        prev = tp[:, :, :lp].reshape(b, hkv, nb, BLOCK, dh)
        cur = tp[:, :, BLOCK:].reshape(b, hkv, nb, BLOCK, dh)
        return jnp.concatenate([prev, cur], axis=3)

    kb, vb = band(k), band(v)
    logits = jnp.einsum('bhgnqd,bhnkd->bhgnqk', qb, kb).astype(jnp.float32) * (dh ** -0.5)
    logits = logits + bias.astype(jnp.float32).reshape(hkv, g, 1, BLOCK, 2 * BLOCK)
    dist = jnp.arange(BLOCK)[:, None] + BLOCK - jnp.arange(2 * BLOCK)[None, :]
    key_real = (jnp.arange(nb)[:, None] * BLOCK + jnp.arange(2 * BLOCK)[None, :]) >= BLOCK
    mask = (dist >= 0)[None] & (dist <= max_dist)[None] & key_real[:, None, :]
    logits = jnp.where(mask, logits, NEG_INF)
    m = jnp.max(logits, axis=-1, keepdims=True)
    if sinks is not None:
        sink = sinks.astype(jnp.float32).reshape(hkv, g, 1, 1, 1)
        m = jnp.maximum(m, sink)
    p = jnp.exp(logits - m)
    den = jnp.sum(p, axis=-1, keepdims=True)
    if sinks is not None:
        den = den + jnp.exp(sink - m)
    out = jnp.einsum('bhgnqk,bhnkd->bhgnqd', (p / den).astype(v.dtype), vb)
    lse = (m + jnp.log(den))[..., 0]
    out = out.reshape(b, hq, lp, dh)[:, :, :length]
    lse = lse.reshape(b, hq, lp)[:, :, :length]
    return out, lse


def causal_block_attention(q, k, v, scale):
    b, h, s, dq = q.shape
    nb = s // BLOCK
    qb = jnp.moveaxis(q.reshape(b, h, nb, BLOCK, dq), 2, 0)
    kpos = jnp.arange(s)

    def one_block(args):
        q_blk, n = args
        logits = jnp.einsum('bhqd,bhkd->bhqk', q_blk, k).astype(jnp.float32) * scale
        qpos = n * BLOCK + jnp.arange(BLOCK)
        logits = jnp.where(kpos[None, :] <= qpos[:, None], logits, NEG_INF)
        p = jax.nn.softmax(logits, axis=-1)
        return jnp.einsum('bhqk,bhkd->bhqd', p.astype(v.dtype), v)

    out = lax.map(one_block, (qb, jnp.arange(nb)))
    return jnp.moveaxis(out, 0, 2).reshape(b, h, s, -1)


def stick_breaking_attention(q, k, v):
    b, h, s, dh = q.shape
    nb = s // BLOCK
    qb = jnp.moveaxis(q.reshape(b, h, nb, BLOCK, dh), 2, 0)
    kpos = jnp.arange(s)

    def one_block(args):
        q_blk, n = args
        z = jnp.einsum('bhqd,bhkd->bhqk', q_blk, k).astype(jnp.float32) * (dh ** -0.5)
        qpos = n * BLOCK + jnp.arange(BLOCK)
        strict = kpos[None, :] < qpos[:, None]
        log_keep = jnp.where(strict, jax.nn.log_sigmoid(-z), 0.0)
        after = lax.cumsum(log_keep, axis=3, reverse=True) - log_keep
        a = jnp.where(strict, jnp.exp(jax.nn.log_sigmoid(z) + after), 0.0)
        return jnp.einsum('bhqk,bhkd->bhqd', a.astype(v.dtype), v)

    out = lax.map(one_block, (qb, jnp.arange(nb)))
    return jnp.moveaxis(out, 0, 2).reshape(b, h, s, dh)


def dilated_attention(q, k, v, biases):
    b, h, s, dh = q.shape
    outs, lses = [], []
    for (window, rate), bias in zip(D_PATTERNS, biases):
        sub = s // rate

        def to_sub(t):
            return t.reshape(b, h, sub, rate, dh).transpose(0, 3, 1, 2, 4).reshape(b * rate, h, sub, dh)

        o, lse = banded_attention(to_sub(q), to_sub(k), to_sub(v), bias, window // rate)
        outs.append(o.reshape(b, rate, h, sub, dh).transpose(0, 2, 3, 1, 4).reshape(b, h, s, dh))
        lses.append(lse.reshape(b, rate, h, sub).transpose(0, 2, 3, 1).reshape(b, h, s))
    w = jax.nn.softmax(jnp.stack(lses, axis=0), axis=0)
    return jnp.einsum('pbhs,pbhsd->bhsd', w.astype(q.dtype), jnp.stack(outs, axis=0))


def latent_attention(cq, ckv, kr, q_norm, kv_norm, w_uq, w_ukv, cos, sin):
    b, s, _ = cq.shape
    q = jnp.einsum('bsr,re->bse', rms_norm(cq, q_norm), w_uq).reshape(b, s, B_HEADS, B_NOPE_DIM + B_ROPE_DIM)
    q = jnp.concatenate([q[..., :B_NOPE_DIM], apply_rope(q[..., B_NOPE_DIM:], cos, sin)], axis=-1)
    kv = jnp.einsum('bsr,re->bse', rms_norm(ckv, kv_norm), w_ukv).reshape(b, s, B_HEADS, B_NOPE_DIM + B_V_DIM)
    k_nope, v = kv[..., :B_NOPE_DIM], kv[..., B_NOPE_DIM:]
    k_rope = apply_rope(kr[:, :, None, :], cos, sin)
    k = jnp.concatenate([k_nope, jnp.broadcast_to(k_rope, (b, s, B_HEADS, B_ROPE_DIM))], axis=-1)
    return causal_block_attention(q.transpose(0, 2, 1, 3), k.transpose(0, 2, 1, 3), v.transpose(0, 2, 1, 3),
                                  (B_NOPE_DIM + B_ROPE_DIM) ** -0.5)


def hybrid_mixer(x, w_in, w_o, q_norm, kv_norm, w_uq, w_ukv, sinks, mix_g, bias_a, biases_d, cos, sin):
    h = jnp.einsum('bsd,de->bse', x, w_in)
    qa, ka, va, cq, ckv, kr, qc, kc, vc, qd, kd, vd = split_columns(h, IN_SIZES)
    out_a, _ = banded_attention(to_heads(qa, A_HEADS), to_heads(ka, A_KV_HEADS), to_heads(va, A_KV_HEADS),
                                bias_a, A_WINDOW - 1, sinks)
    out_b = latent_attention(cq, ckv, kr, q_norm, kv_norm, w_uq, w_ukv, cos, sin)
    out_c = stick_breaking_attention(to_heads(qc, C_HEADS), to_heads(kc, C_HEADS), to_heads(vc, C_HEADS))
    out_d = dilated_attention(to_heads(qd, D_HEADS), to_heads(kd, D_HEADS), to_heads(vd, D_HEADS), biases_d)
    groups = jnp.stack([merge_heads(o) for o in (out_a, out_b, out_c, out_d)], axis=2)
    groups = rms_norm(groups, mix_g)
    b, s = x.shape[0], x.shape[1]
    return jnp.einsum('bsm,md->bsd', groups.reshape(b, s, MIX_WIDTH), w_o)


def swiglu(x, w_gate, w_up, w_down):
    hid = jax.nn.silu(jnp.einsum('bsd,df->bsf', x, w_gate)) * jnp.einsum('bsd,df->bsf', x, w_up)
    return jnp.einsum('bsf,fd->bsd', hid, w_down)


def moe_swiglu(x, router, w_gate, w_up, w_down):
    logits = jnp.einsum('bsd,de->bse', x, router).astype(jnp.float32)
    top_logits, top_idx = lax.top_k(logits, TOP_K)
    gates = jax.nn.softmax(top_logits, axis=-1)
    combine = jnp.sum(jax.nn.one_hot(top_idx, N_EXPERTS, dtype=jnp.float32) * gates[..., None], axis=-2)
    out = jnp.zeros_like(x)
    for e in range(N_EXPERTS):
        out = out + combine[..., e:e + 1].astype(x.dtype) * swiglu(x, w_gate[e], w_up[e], w_down[e])
    return out


def setup_inputs(seed: int = 0) -> dict:
    key = jax.random.key(seed)
    ks = jax.random.split(key, 24)
    beta = (8 * DEPTH) ** -0.25
    n_dense = (DEPTH + 1) // 2
    n_moe = DEPTH // 2

    def normal(k, shape, scale):
        return jax.random.normal(k, shape, jnp.float32) * scale

    return {
        'x': normal(ks[0], (BATCH, SEQ, D_MODEL), 1.0),
        'w_in': normal(ks[1], (DEPTH, D_MODEL, IN_WIDTH), D_MODEL ** -0.5),
        'w_o': normal(ks[2], (DEPTH, MIX_WIDTH, D_MODEL), beta * MIX_WIDTH ** -0.5),
        'mla_q_norm': 1.0 + normal(ks[3], (DEPTH, B_Q_LORA), 0.02),
        'mla_kv_norm': 1.0 + normal(ks[4], (DEPTH, B_KV_LORA), 0.02),
        'mla_w_uq': normal(ks[5], (DEPTH, B_Q_LORA, B_HEADS * (B_NOPE_DIM + B_ROPE_DIM)), B_Q_LORA ** -0.5),
        'mla_w_ukv': normal(ks[6], (DEPTH, B_KV_LORA, B_HEADS * (B_NOPE_DIM + B_V_DIM)), B_KV_LORA ** -0.5),
        'attn_sinks': normal(ks[7], (DEPTH, A_HEADS), 1.0),
        'rel_bias_table': normal(ks[8], (NUM_BUCKETS, A_HEADS + D_HEADS), 0.5),
        'mix_norm_g': 1.0 + normal(ks[9], (DEPTH, N_MIXERS, GROUP_WIDTH), 0.02),
        'ln1_g': 1.0 + normal(ks[10], (DEPTH, D_MODEL), 0.02),
        'ln1_b': normal(ks[11], (DEPTH, D_MODEL), 0.02),
        'ln2_g': 1.0 + normal(ks[12], (DEPTH, D_MODEL), 0.02),
        'ln2_b': normal(ks[13], (DEPTH, D_MODEL), 0.02),
        'ffn_w_gate': normal(ks[14], (n_dense, D_MODEL, D_FF), D_MODEL ** -0.5),
        'ffn_w_up': normal(ks[15], (n_dense, D_MODEL, D_FF), D_MODEL ** -0.5),
        'ffn_w_down': normal(ks[16], (n_dense, D_FF, D_MODEL), beta * D_FF ** -0.5),
        'moe_router': normal(ks[17], (n_moe, D_MODEL, N_EXPERTS), D_MODEL ** -0.5),
        'moe_w_gate': normal(ks[18], (n_moe, N_EXPERTS, D_MODEL, D_FF_EXPERT), D_MODEL ** -0.5),
        'moe_w_up': normal(ks[19], (n_moe, N_EXPERTS, D_MODEL, D_FF_EXPERT), D_MODEL ** -0.5),
        'moe_w_down': normal(ks[20], (n_moe, N_EXPERTS, D_FF_EXPERT, D_MODEL), beta * D_FF_EXPERT ** -0.5),
    }


def reference(x, w_in, w_o, mla_q_norm, mla_kv_norm, mla_w_uq, mla_w_ukv, attn_sinks, rel_bias_table,
              mix_norm_g, ln1_g, ln1_b, ln2_g, ln2_b, ffn_w_gate, ffn_w_up, ffn_w_down,
              moe_router, moe_w_gate, moe_w_up, moe_w_down):
    alpha = (2 * DEPTH) ** 0.25
    s = x.shape[1]
    cos, sin = rope_tables(s)
    bias_a = band_bias(rel_bias_table[:, :A_HEADS], 1)
    biases_d = [band_bias(rel_bias_table[:, A_HEADS:], rate) for _, rate in D_PATTERNS]
    for layer in range(DEPTH):
        mixed = hybrid_mixer(x, w_in[layer], w_o[layer], mla_q_norm[layer], mla_kv_norm[layer],
                             mla_w_uq[layer], mla_w_ukv[layer], attn_sinks[layer], mix_norm_g[layer],
                             bias_a, biases_d, cos, sin)
        x = layer_norm(alpha * x + mixed, ln1_g[layer], ln1_b[layer])
        i = layer // 2
        if layer % 2 == 0:
            f = swiglu(x, ffn_w_gate[i], ffn_w_up[i], ffn_w_down[i])
        else:
            f = moe_swiglu(x, moe_router[i], moe_w_gate[i], moe_w_up[i], moe_w_down[i])
        x = layer_norm(alpha * x + f, ln2_g[layer], ln2_b[layer])
    return x
```

```python
import functools
import math

import jax
import jax.numpy as jnp
from jax import lax
from jax.experimental import pallas as pl
from jax.experimental.pallas import tpu as pltpu

D_MODEL = 2048
DEPTH = 4
HEAD_DIM = 64
N_HEADS = 8
GROUP_WIDTH = N_HEADS * HEAD_DIM
BLOCK = 128
A_KV_HEADS = 2
A_WINDOW = 128
B_NOPE_DIM = 64
B_ROPE_DIM = 32
B_Q_LORA = 384
B_KV_LORA = 256
ROPE_THETA = 10000.0
D_PATTERNS = ((128, 1), (512, 4), (2048, 16))
NUM_BUCKETS = 32
T5_MAX_DISTANCE = 2048
D_FF = 5632
N_EXPERTS = 8
TOP_K = 2
NEG_INF = -1e30
LN_EPS = 1e-5
RMS_EPS = 1e-6
ALPHA = (2 * DEPTH) ** 0.25

LANES = 128
VMEM_LIMIT = 56 * 1024 * 1024

COL_QA, COL_QC, COL_KC, COL_CQ, COL_KA = 0, 512, 1024, 1536, 1920
COL_VC, COL_QD, COL_KD, COL_VD, COL_CKV, COL_VA, COL_KR = 2048, 2560, 3072, 3584, 4096, 4352, 4480
IN_WIDTH_P = 4608
A_HEAD_ORDER = (0, 4, 1, 5, 2, 6, 3, 7)
ROPE_LANE0 = 64

BF16 = jnp.bfloat16
F32 = jnp.float32


def _params(*sem):
    return pltpu.CompilerParams(dimension_semantics=sem, vmem_limit_bytes=VMEM_LIMIT)


def _dot(a, b):
    return jnp.dot(a, b, preferred_element_type=F32)


def _dot_nt(a, b):
    return lax.dot_general(a, b, (((1,), (1,)), ((), ())), preferred_element_type=F32)


def _matmul_kernel(a_ref, w_ref, o_ref):
    o_ref[...] = _dot(a_ref[...].astype(BF16), w_ref[...]).astype(o_ref.dtype)


def matmul_ws(a, w, tm, tn, out_dtype):
    m, k = a.shape
    n = w.shape[1]
    return pl.pallas_call(
        _matmul_kernel,
        grid=(n // tn, m // tm),
        in_specs=[pl.BlockSpec((tm, k), lambda j, i: (i, 0)),
                  pl.BlockSpec((k, tn), lambda j, i: (0, j))],
        out_specs=pl.BlockSpec((tm, tn), lambda j, i: (i, j)),
        out_shape=jax.ShapeDtypeStruct((m, n), out_dtype),
        compiler_params=_params("arbitrary", "arbitrary"),
        name="matmul_ws",
    )(a, w)


def _banded_kernel(*refs, kv_tiles, heads, scale, use_sink, want_lse):
    if use_sink:
        q_ref, kp_ref, kc_ref, vp_ref, vc_ref, bias_ref, sink_ref = refs[:7]
        outs = refs[7:]
    else:
        q_ref, kp_ref, kc_ref, vp_ref, vc_ref, bias_ref = refs[:6]
        outs = refs[6:]
    o_ref = outs[0]
    first = pl.program_id(2) == 0
    lane = lax.broadcasted_iota(jnp.int32, (BLOCK, LANES), 1)
    lo = lane < HEAD_DIM
    for c in range(GROUP_WIDTH // LANES):
        q = q_ref[0, :, c * LANES:(c + 1) * LANES]
        kvc = 0 if kv_tiles == 1 else c
        kp = kp_ref[0, :, kvc * LANES:(kvc + 1) * LANES]
        kc = kc_ref[0, :, kvc * LANES:(kvc + 1) * LANES]
        vp = vp_ref[0, :, kvc * LANES:(kvc + 1) * LANES]
        vc = vc_ref[0, :, kvc * LANES:(kvc + 1) * LANES]
        o_half, lse_half = [], []
        for half in range(2):
            h = heads[c][half]
            qm = jnp.where(lo if half == 0 else jnp.logical_not(lo), q, jnp.zeros_like(q))
            sp = _dot_nt(qm, kp) * scale + bias_ref[h, :, 0:BLOCK]
            sc = _dot_nt(qm, kc) * scale + bias_ref[h, :, BLOCK:2 * BLOCK]
            sp = jnp.where(first, NEG_INF, sp)
            m = jnp.maximum(jnp.max(sp, axis=1, keepdims=True), jnp.max(sc, axis=1, keepdims=True))
            if use_sink:
                sink = sink_ref[h]
                m = jnp.maximum(m, sink)
            pp = jnp.exp(sp - m)
            pc = jnp.exp(sc - m)
            den = jnp.sum(pp, axis=1, keepdims=True) + jnp.sum(pc, axis=1, keepdims=True)
            if use_sink:
                den = den + jnp.exp(sink - m)
            o = (_dot(pp.astype(BF16), vp) + _dot(pc.astype(BF16), vc)) / den
            o_half.append(o)
            if want_lse:
                lse_half.append(jnp.broadcast_to(m + jnp.log(den), (BLOCK, LANES)))
        o_ref[0, :, c * LANES:(c + 1) * LANES] = jnp.where(lo, o_half[0], o_half[1]).astype(o_ref.dtype)
        if want_lse:
            outs[1][0, :, c * LANES:(c + 1) * LANES] = jnp.where(lo, lse_half[0], lse_half[1])


def banded_attention(h3, bias, sinks, *, batch, seq, rate, q_col, k_col, v_col, kv_width, heads,
                     want_lse, out_dtype):
    sub = seq // rate
    nb = sub // BLOCK
    hv = h3.reshape(batch, sub, rate * IN_WIDTH_P)
    qpb = IN_WIDTH_P // GROUP_WIDTH
    kpb = IN_WIDTH_P // kv_width
    qb, kb, vb = q_col // GROUP_WIDTH, k_col // kv_width, v_col // kv_width
    in_specs = [
        pl.BlockSpec((1, BLOCK, GROUP_WIDTH), lambda b, r, n: (b, n, r * qpb + qb)),
        pl.BlockSpec((1, BLOCK, kv_width), lambda b, r, n: (b, jnp.maximum(n - 1, 0), r * kpb + kb)),
        pl.BlockSpec((1, BLOCK, kv_width), lambda b, r, n: (b, n, r * kpb + kb)),
        pl.BlockSpec((1, BLOCK, kv_width), lambda b, r, n: (b, jnp.maximum(n - 1, 0), r * kpb + vb)),
        pl.BlockSpec((1, BLOCK, kv_width), lambda b, r, n: (b, n, r * kpb + vb)),
        pl.BlockSpec((N_HEADS, BLOCK, 2 * BLOCK), lambda b, r, n: (0, 0, 0)),
    ]
    args = [hv, hv, hv, hv, hv, bias]
    if sinks is not None:
        in_specs.append(pl.BlockSpec(memory_space=pltpu.SMEM))
        args.append(sinks)
    o_spec = pl.BlockSpec((1, BLOCK, GROUP_WIDTH), lambda b, r, n: (b, n, r))
    o_shape = jax.ShapeDtypeStruct((batch, sub, rate * GROUP_WIDTH), out_dtype)
    out_specs, out_shape = o_spec, o_shape
    if want_lse:
        out_specs = [o_spec, o_spec]
        out_shape = [o_shape, jax.ShapeDtypeStruct((batch, sub, rate * GROUP_WIDTH), F32)]
    res = pl.pallas_call(
        functools.partial(_banded_kernel, kv_tiles=kv_width // LANES, heads=heads,
                          scale=HEAD_DIM ** -0.5, use_sink=sinks is not None, want_lse=want_lse),
        grid=(batch, rate, nb),
        in_specs=in_specs, out_specs=out_specs, out_shape=out_shape,
        compiler_params=_params("arbitrary", "arbitrary", "arbitrary"),
        name=f"banded_r{rate}",
    )(*args)
    if want_lse:
        return [t.reshape(batch, seq, GROUP_WIDTH) for t in res]
    return res.reshape(batch, seq, GROUP_WIDTH)


def _mla_up_kernel(cq_ref, ckv_ref, kr_ref, qn_ref, kvn_ref, wq_ref, wqs_ref, wk_ref, wv_ref, psw_ref,
                   cos_ref, sin_ref, q_ref, k_ref, v_ref, *, scale):
    def rms(x_ref, g_ref):
        x = x_ref[...].astype(F32)
        return (x * lax.rsqrt(jnp.mean(x * x, axis=-1, keepdims=True) + RMS_EPS) * g_ref[...]).astype(BF16)

    xq = rms(cq_ref, qn_ref)
    xkv = rms(ckv_ref, kvn_ref)
    cos = cos_ref[...]
    sin = sin_ref[...]
    t = _dot(xq, wq_ref[...])
    ts = _dot(xq, wqs_ref[...])
    kn = _dot(xkv, wk_ref[...])
    kr = kr_ref[...]
    rk = kr.astype(F32) * cos + _dot(kr, psw_ref[...]) * sin
    for h in range(N_HEADS):
        sl = slice(h * LANES, (h + 1) * LANES)
        q_ref[:, sl] = ((t[:, sl] * cos + ts[:, sl] * sin) * scale).astype(BF16)
        k_ref[:, sl] = (kn[:, sl] + rk).astype(BF16)
    v_ref[...] = _dot(xkv, wv_ref[...]).astype(BF16)


def mla_up(h, q_norm, kv_norm, wq, wqs, wk, wv, psw, cos_t, sin_t, *, seq, tm):
    n = h.shape[0]
    w = N_HEADS * LANES
    const = lambda shape: pl.BlockSpec(shape, lambda i: (0, 0))
    spt = seq // tm
    return pl.pallas_call(
        functools.partial(_mla_up_kernel, scale=(B_NOPE_DIM + B_ROPE_DIM) ** -0.5),
        grid=(n // tm,),
        in_specs=[pl.BlockSpec((tm, B_Q_LORA), lambda i: (i, COL_CQ // B_Q_LORA)),
                  pl.BlockSpec((tm, B_KV_LORA), lambda i: (i, COL_CKV // B_KV_LORA)),
                  pl.BlockSpec((tm, LANES), lambda i: (i, COL_KR // LANES)),
                  const((1, B_Q_LORA)), const((1, B_KV_LORA)),
                  const((B_Q_LORA, w)), const((B_Q_LORA, w)), const((B_KV_LORA, w)),
                  const((B_KV_LORA, GROUP_WIDTH)), const((LANES, LANES)),
                  pl.BlockSpec((tm, LANES), lambda i: (i % spt, 0)),
                  pl.BlockSpec((tm, LANES), lambda i: (i % spt, 0))],
        out_specs=[pl.BlockSpec((tm, w), lambda i: (i, 0)),
                   pl.BlockSpec((tm, w), lambda i: (i, 0)),
                   pl.BlockSpec((tm, GROUP_WIDTH), lambda i: (i, 0))],
        out_shape=[jax.ShapeDtypeStruct((n, w), BF16), jax.ShapeDtypeStruct((n, w), BF16),
                   jax.ShapeDtypeStruct((n, GROUP_WIDTH), BF16)],
        compiler_params=_params("arbitrary"),
        name="mla_up",
    )(h, h, h, q_norm, kv_norm, wq, wqs, wk, wv, psw, cos_t, sin_t)


def _mla_attn_kernel(q_ref, k_ref, v_ref, o_ref, *, tq):
    i = pl.program_id(2)
    lane = lax.broadcasted_iota(jnp.int32, (tq, LANES), 1)
    lo = lane < HEAD_DIM
    row = lax.broadcasted_iota(jnp.int32, (tq, tq), 0)
    col = lax.broadcasted_iota(jnp.int32, (tq, tq), 1)
    causal = col <= row
    o_half = []
    for half in range(2):
        q = q_ref[0, :, half * LANES:(half + 1) * LANES]

        def step(j, carry, masked):
            m, l, acc = carry
            start = pl.multiple_of(j * tq, tq)
            k = k_ref[0, pl.ds(start, tq), half * LANES:(half + 1) * LANES]
            v = v_ref[0, pl.ds(start, tq), :]
            s = _dot_nt(q, k)
            if masked:
                s = jnp.where(causal, s, NEG_INF)
            m_new = jnp.maximum(m, jnp.max(s, axis=1, keepdims=True))
            p = jnp.exp(s - m_new)
            a = jnp.exp(m - m_new)
            l = a * l + jnp.sum(p, axis=1, keepdims=True)
            acc = a * acc + _dot(p.astype(BF16), v)
            return m_new, l, acc

        init = (jnp.full((tq, 1), NEG_INF, F32), jnp.zeros((tq, 1), F32), jnp.zeros((tq, LANES), F32))
        carry = lax.fori_loop(0, i, functools.partial(step, masked=False), init)
        m, l, acc = step(i, carry, True)
        o_half.append(acc / l)
    o_ref[0] = jnp.where(lo, o_half[0], o_half[1]).astype(o_ref.dtype)


def mla_attention(q, k, v, *, batch, seq, tq):
    return pl.pallas_call(
        functools.partial(_mla_attn_kernel, tq=tq),
        grid=(batch, N_HEADS // 2, seq // tq),
        in_specs=[pl.BlockSpec((1, tq, 2 * LANES), lambda b, p, i: (b, i, p)),
                  pl.BlockSpec((1, seq, 2 * LANES), lambda b, p, i: (b, 0, p)),
                  pl.BlockSpec((1, seq, LANES), lambda b, p, i: (b, 0, p))],
        out_specs=pl.BlockSpec((1, tq, LANES), lambda b, p, i: (b, i, p)),
        out_shape=jax.ShapeDtypeStruct((batch, seq, GROUP_WIDTH), BF16),
        compiler_params=_params("arbitrary", "arbitrary", "arbitrary"),
        name="mla_attn",
    )(q, k, v)


def _stick_kernel(q_ref, k_ref, v_ref, tri_ref, o_ref, *, tq, scale):
    i = pl.program_id(2)
    lane = lax.broadcasted_iota(jnp.int32, (tq, LANES), 1)
    lo = lane < HEAD_DIM
    row = lax.broadcasted_iota(jnp.int32, (tq, tq), 0)
    col = lax.broadcasted_iota(jnp.int32, (tq, tq), 1)
    strict = col < row
    tri = tri_ref[...]
    q = q_ref[0]
    o_half = []
    for half in range(2):
        qm = jnp.where(lo if half == 0 else jnp.logical_not(lo), q, jnp.zeros_like(q))

        def step(j, carry, masked):
            run, acc = carry
            start = pl.multiple_of(j * tq, tq)
            k = k_ref[0, pl.ds(start, tq), :]
            v = v_ref[0, pl.ds(start, tq), :]
            z = _dot_nt(qm, k) * scale
            log_keep = -(jnp.maximum(z, 0.0) + jnp.log1p(jnp.exp(-jnp.abs(z))))
            if masked:
                log_keep = jnp.where(strict, log_keep, 0.0)
            hi = log_keep.astype(BF16)
            lo_part = (log_keep - hi.astype(F32)).astype(BF16)
            after = _dot(hi, tri) + _dot(lo_part, tri) + run
            a = jnp.exp(z + log_keep + after)
            if masked:
                a = jnp.where(strict, a, 0.0)
            acc = acc + _dot(a.astype(BF16), v)
            run = run + jnp.sum(log_keep, axis=1, keepdims=True)
            return run, acc

        carry = step(i, (jnp.zeros((tq, 1), F32), jnp.zeros((tq, LANES), F32)), True)
        _, acc = lax.fori_loop(0, i, lambda t, c: step(i - 1 - t, c, False), carry)
        o_half.append(acc)
    o_ref[0] = jnp.where(lo, o_half[0], o_half[1]).astype(o_ref.dtype)


def stick_breaking_attention(h3, tri, *, batch, seq, tq):
    lb = lambda col: col // LANES
    return pl.pallas_call(
        functools.partial(_stick_kernel, tq=tq, scale=HEAD_DIM ** -0.5),
        grid=(batch, N_HEADS // 2, seq // tq),
        in_specs=[pl.BlockSpec((1, tq, LANES), lambda b, p, i: (b, i, lb(COL_QC) + p)),
                  pl.BlockSpec((1, seq, LANES), lambda b, p, i: (b, 0, lb(COL_KC) + p)),
                  pl.BlockSpec((1, seq, LANES), lambda b, p, i: (b, 0, lb(COL_VC) + p)),
                  pl.BlockSpec((tq, tq), lambda b, p, i: (0, 0))],
        out_specs=pl.BlockSpec((1, tq, LANES), lambda b, p, i: (b, i, p)),
        out_shape=jax.ShapeDtypeStruct((batch, seq, GROUP_WIDTH), BF16),
        compiler_params=_params("arbitrary", "arbitrary", "arbitrary"),
        name="stick_breaking",
    )(h3, h3, h3, tri)


def _layer_norm(r, g, b):
    mu = jnp.mean(r, axis=-1, keepdims=True)
    d = r - mu
    var = jnp.mean(d * d, axis=-1, keepdims=True)
    return d * lax.rsqrt(var + LN_EPS) * g + b


def _out_proj_kernel(ga_ref, gb_ref, gc_ref, o1_ref, o2_ref, o3_ref, l1_ref, l2_ref, l3_ref, x_ref,
                     mixg_ref, wo_ref, g_ref, b_ref, rt_ref, xo_ref, xb_ref, lg_ref):
    l1, l2, l3 = l1_ref[...], l2_ref[...], l3_ref[...]
    m = jnp.maximum(jnp.maximum(l1, l2), l3)
    e1, e2, e3 = jnp.exp(l1 - m), jnp.exp(l2 - m), jnp.exp(l3 - m)
    gd = (e1 * o1_ref[...] + e2 * o2_ref[...] + e3 * o3_ref[...]) / (e1 + e2 + e3)
    groups = (ga_ref[...].astype(F32), gb_ref[...].astype(F32), gc_ref[...].astype(F32), gd)
    mixed = None
    for gi, xg in enumerate(groups):
        y = xg * lax.rsqrt(jnp.mean(xg * xg, axis=-1, keepdims=True) + RMS_EPS) * mixg_ref[gi:gi + 1, :]
        part = _dot(y.astype(BF16), wo_ref[gi * GROUP_WIDTH:(gi + 1) * GROUP_WIDTH, :])
        mixed = part if mixed is None else mixed + part
    x1 = _layer_norm(ALPHA * x_ref[...].astype(F32) + mixed, g_ref[...], b_ref[...])
    xo_ref[...] = x1
    xb = x1.astype(BF16)
    xb_ref[...] = xb
    lg_ref[...] = _dot(xb, rt_ref[...])


def out_proj_ln(ga, gb, gc, d_outs, d_lses, x, mix_g, wo, ln_g, ln_b, router, *, tm):
    n = x.shape[0]
    row = lambda w: pl.BlockSpec((tm, w), lambda i: (i, 0))
    const = lambda shape: pl.BlockSpec(shape, lambda i: (0, 0))
    return pl.pallas_call(
        _out_proj_kernel,
        grid=(n // tm,),
        in_specs=[row(GROUP_WIDTH)] * 9 + [row(D_MODEL), const((4, GROUP_WIDTH)), const((D_MODEL, D_MODEL)),
                                          const((1, D_MODEL)), const((1, D_MODEL)), const((D_MODEL, LANES))],
        out_specs=[row(D_MODEL), row(D_MODEL), row(LANES)],
        out_shape=[jax.ShapeDtypeStruct((n, D_MODEL), F32), jax.ShapeDtypeStruct((n, D_MODEL), BF16),
                   jax.ShapeDtypeStruct((n, LANES), F32)],
        compiler_params=_params("arbitrary"),
        name="out_proj_ln",
    )(ga, gb, gc, *d_outs, *d_lses, x, mix_g, wo, ln_g, ln_b, router)


def _silu_mul(g, u):
    return g * (1.0 / (1.0 + jnp.exp(-g))) * u


def _gate_up_kernel(x_ref, wg_ref, wu_ref, o_ref):
    x = x_ref[...]
    o_ref[...] = _silu_mul(_dot(x, wg_ref[...]), _dot(x, wu_ref[...])).astype(o_ref.dtype)


def ffn_gate_up(xb, wg, wu, *, tm, tf):
    n = xb.shape[0]
    f = wg.shape[1]
    return pl.pallas_call(
        _gate_up_kernel,
        grid=(f // tf, n // tm),
        in_specs=[pl.BlockSpec((tm, D_MODEL), lambda j, i: (i, 0)),
                  pl.BlockSpec((D_MODEL, tf), lambda j, i: (0, j)),
                  pl.BlockSpec((D_MODEL, tf), lambda j, i: (0, j))],
        out_specs=pl.BlockSpec((tm, tf), lambda j, i: (i, j)),
        out_shape=jax.ShapeDtypeStruct((n, f), BF16),
        compiler_params=_params("arbitrary", "arbitrary"),
        name="ffn_gate_up",
    )(xb, wg, wu)


def _down_ln_kernel(h_ref, wd_ref, x_ref, g_ref, b_ref, xo_ref, xb_ref, acc_ref):
    kk = pl.program_id(1)

    @pl.when(kk == 0)
    def _():
        acc_ref[...] = jnp.zeros_like(acc_ref)

    acc_ref[...] += _dot(h_ref[...], wd_ref[...])

    @pl.when(kk == pl.num_programs(1) - 1)
    def _():
        x2 = _layer_norm(ALPHA * x_ref[...] + acc_ref[...], g_ref[...], b_ref[...])
        xo_ref[...] = x2
        xb_ref[...] = x2.astype(BF16)


def ffn_down_ln(hmid, wd, x, ln_g, ln_b, *, tm, tk):
    n, f = hmid.shape
    return pl.pallas_call(
        _down_ln_kernel,
        grid=(n // tm, f // tk),
        in_specs=[pl.BlockSpec((tm, tk), lambda i, k: (i, k)),
                  pl.BlockSpec((tk, D_MODEL), lambda i, k: (k, 0)),
                  pl.BlockSpec((tm, D_MODEL), lambda i, k: (i, 0)),
                  pl.BlockSpec((1, D_MODEL), lambda i, k: (0, 0)),
                  pl.BlockSpec((1, D_MODEL), lambda i, k: (0, 0))],
        out_specs=[pl.BlockSpec((tm, D_MODEL), lambda i, k: (i, 0)),
                   pl.BlockSpec((tm, D_MODEL), lambda i, k: (i, 0))],
        out_shape=[jax.ShapeDtypeStruct((n, D_MODEL), F32), jax.ShapeDtypeStruct((n, D_MODEL), BF16)],
        scratch_shapes=[pltpu.VMEM((tm, D_MODEL), F32)],
        compiler_params=_params("arbitrary", "arbitrary"),
        name="ffn_down_ln",
    )(hmid, wd, x, ln_g, ln_b)


def _moe_gate_up_kernel(te_ref, tv_ref, x_ref, wg_ref, wu_ref, o_ref):
    t = pl.program_id(1)

    @pl.when(tv_ref[t] > 0)
    def _():
        x = x_ref[...]
        o_ref[...] = _silu_mul(_dot(x, wg_ref[0]), _dot(x, wu_ref[0])).astype(o_ref.dtype)

    @pl.when(tv_ref[t] == 0)
    def _():
        o_ref[...] = jnp.zeros_like(o_ref)


def moe_gate_up(tile_expert, tile_valid, xs, wg, wu, *, tm, tf):
    p = xs.shape[0]
    f = wg.shape[2]
    grid_spec = pltpu.PrefetchScalarGridSpec(
        num_scalar_prefetch=2,
        grid=(f // tf, p // tm),
        in_specs=[pl.BlockSpec((tm, D_MODEL), lambda j, t, te, tv: (t, 0)),
                  pl.BlockSpec((1, D_MODEL, tf), lambda j, t, te, tv: (te[t], 0, j)),
                  pl.BlockSpec((1, D_MODEL, tf), lambda j, t, te, tv: (te[t], 0, j))],
        out_specs=pl.BlockSpec((tm, tf), lambda j, t, te, tv: (t, j)),
    )
    return pl.pallas_call(
        _moe_gate_up_kernel,
        grid_spec=grid_spec,
        out_shape=jax.ShapeDtypeStruct((p, f), BF16),
        compiler_params=_params("arbitrary", "arbitrary"),
        name="moe_gate_up",
    )(tile_expert, tile_valid, xs, wg, wu)


def _moe_down_kernel(te_ref, tv_ref, h_ref, wd_ref, gate_ref, o_ref):
    t = pl.program_id(1)

    @pl.when(tv_ref[t] > 0)
    def _():
        o_ref[...] = _dot(h_ref[...], wd_ref[0]) * gate_ref[...]

    @pl.when(tv_ref[t] == 0)
    def _():
        o_ref[...] = jnp.zeros_like(o_ref)


def moe_down(tile_expert, tile_valid, hs, wd, gate_rows, *, tm, tn):
    p, f = hs.shape
    grid_spec = pltpu.PrefetchScalarGridSpec(
        num_scalar_prefetch=2,
        grid=(D_MODEL // tn, p // tm),
        in_specs=[pl.BlockSpec((tm, f), lambda j, t, te, tv: (t, 0)),
                  pl.BlockSpec((1, f, tn), lambda j, t, te, tv: (te[t], 0, j)),
                  pl.BlockSpec((tm, 1), lambda j, t, te, tv: (t, 0))],
        out_specs=pl.BlockSpec((tm, tn), lambda j, t, te, tv: (t, j)),
    )
    return pl.pallas_call(
        _moe_down_kernel,
        grid_spec=grid_spec,
        out_shape=jax.ShapeDtypeStruct((p, D_MODEL), F32),
        compiler_params=_params("arbitrary", "arbitrary"),
        name="moe_down",
    )(tile_expert, tile_valid, hs, wd, gate_rows)


def _combine_ln_kernel(x_ref, y0_ref, y1_ref, g_ref, b_ref, xo_ref, xb_ref):
    x2 = _layer_norm(ALPHA * x_ref[...] + (y0_ref[...] + y1_ref[...]), g_ref[...], b_ref[...])
    xo_ref[...] = x2
    xb_ref[...] = x2.astype(BF16)


def combine_ln(x, y0, y1, ln_g, ln_b, *, tm):
    n = x.shape[0]
    row = pl.BlockSpec((tm, D_MODEL), lambda i: (i, 0))
    const = pl.BlockSpec((1, D_MODEL), lambda i: (0, 0))
    return pl.pallas_call(
        _combine_ln_kernel,
        grid=(n // tm,),
        in_specs=[row, row, row, const, const],
        out_specs=[row, row],
        out_shape=[jax.ShapeDtypeStruct((n, D_MODEL), F32), jax.ShapeDtypeStruct((n, D_MODEL), BF16)],
        compiler_params=_params("arbitrary"),
        name="combine_ln",
    )(x, y0, y1, ln_g, ln_b)


def _t5_bucket(dist):
    max_exact = NUM_BUCKETS // 2
    d = jnp.maximum(dist, 1).astype(F32)
    large = max_exact + (jnp.log(d / max_exact) / math.log(T5_MAX_DISTANCE / max_exact)
                         * (NUM_BUCKETS - max_exact)).astype(jnp.int32)
    large = jnp.minimum(large, NUM_BUCKETS - 1)
    return jnp.where(dist < max_exact, dist, large)


def _band_bias_masked(table, stride, max_dist):
    dist = jnp.arange(BLOCK)[:, None] + BLOCK - jnp.arange(2 * BLOCK)[None, :]
    bias = jnp.transpose(table[_t5_bucket(jnp.maximum(dist, 0) * stride)], (2, 0, 1)).astype(F32)
    valid = (dist >= 0) & (dist <= max_dist)
    return jnp.where(valid[None], bias, NEG_INF)


def _rope_lane_tables(seq):
    half = B_ROPE_DIM // 2
    inv = ROPE_THETA ** (-jnp.arange(0, B_ROPE_DIM, 2, dtype=F32) / B_ROPE_DIM)
    ang = jnp.arange(seq, dtype=F32)[:, None] * inv[None, :]
    cos, sin = jnp.cos(ang), jnp.sin(ang)
    ones = jnp.ones((seq, ROPE_LANE0), F32)
    zeros = jnp.zeros((seq, ROPE_LANE0), F32)
    pad = jnp.zeros((seq, LANES - ROPE_LANE0 - 2 * half), F32)
    cos_t = jnp.concatenate([ones, cos, cos, pad], axis=1)
    sin_t = jnp.concatenate([zeros, -sin, sin, pad], axis=1)
    return cos_t, sin_t


def _relayout_w_in(w_in):
    w = w_in.astype(BF16)
    d = w.shape[0]
    seg = {}
    start = 0
    for name, size in (("qa", 512), ("ka", 128), ("va", 128), ("cq", 384), ("ckv", 256), ("kr", 32),
                       ("qc", 512), ("kc", 512), ("vc", 512), ("qd", 512), ("kd", 512), ("vd", 512)):
        seg[name] = w[:, :, start:start + size]
        start += size
    qa = seg["qa"].reshape(d, D_MODEL, N_HEADS, HEAD_DIM)
    qa = jnp.stack([qa[:, :, hh] for hh in A_HEAD_ORDER], axis=2).reshape(d, D_MODEL, GROUP_WIDTH)
    zl = jnp.zeros((d, D_MODEL, ROPE_LANE0), BF16)
    zr = jnp.zeros((d, D_MODEL, LANES - ROPE_LANE0 - B_ROPE_DIM), BF16)
    kr = jnp.concatenate([zl, seg["kr"], zr], axis=2)
    return jnp.concatenate([qa, seg["qc"], seg["kc"], seg["cq"], seg["ka"], seg["vc"], seg["qd"], seg["kd"],
                            seg["vd"], seg["ckv"], seg["va"], kr], axis=2)


def _relayout_mla(w_uq, w_ukv):
    d = w_uq.shape[0]
    half = B_ROPE_DIM // 2
    wq = w_uq.astype(BF16).reshape(d, B_Q_LORA, N_HEADS, B_NOPE_DIM + B_ROPE_DIM)
    nope, r1, r2 = wq[..., :B_NOPE_DIM], wq[..., B_NOPE_DIM:B_NOPE_DIM + half], wq[..., B_NOPE_DIM + half:]
    z32 = jnp.zeros(wq.shape[:3] + (LANES - B_NOPE_DIM - B_ROPE_DIM,), BF16)
    z64 = jnp.zeros(wq.shape[:3] + (B_NOPE_DIM,), BF16)
    wq_t = jnp.concatenate([nope, r1, r2, z32], axis=-1).reshape(d, B_Q_LORA, N_HEADS * LANES)
    wq_s = jnp.concatenate([z64, r2, r1, z32], axis=-1).reshape(d, B_Q_LORA, N_HEADS * LANES)
    wkv = w_ukv.astype(BF16).reshape(d, B_KV_LORA, N_HEADS, 2 * HEAD_DIM)
    zk = jnp.zeros(wkv.shape[:3] + (LANES - B_NOPE_DIM,), BF16)
    wk_t = jnp.concatenate([wkv[..., :B_NOPE_DIM], zk], axis=-1).reshape(d, B_KV_LORA, N_HEADS * LANES)
    wv = wkv[..., B_NOPE_DIM:].reshape(d, B_KV_LORA, GROUP_WIDTH)
    return wq_t, wq_s, wk_t, wv


def _rope_swap_matrix():
    half = B_ROPE_DIM // 2
    src = jnp.arange(LANES)[:, None]
    dst = jnp.arange(LANES)[None, :]
    first = (dst >= ROPE_LANE0) & (dst < ROPE_LANE0 + half) & (src == dst + half)
    second = (dst >= ROPE_LANE0 + half) & (dst < ROPE_LANE0 + 2 * half) & (src == dst - half)
    return (first | second).astype(BF16)


def _route(logits, tm):
    n = logits.shape[0]
    top_logits, top_idx = lax.top_k(logits, TOP_K)
    gates = jax.nn.softmax(top_logits, axis=-1)
    onehot = jax.nn.one_hot(top_idx, N_EXPERTS, dtype=jnp.int32)
    member = jnp.sum(onehot, axis=1)
    rank = jnp.cumsum(member, axis=0) - member
    counts = jnp.sum(member, axis=0)
    padded = ((counts + tm - 1) // tm) * tm
    ends = jnp.cumsum(padded)
    starts = ends - padded
    pos = jnp.sum(onehot * (starts[None, None, :] + rank[:, None, :]), axis=-1)
    n_rows = TOP_K * n + N_EXPERTS * tm
    n_tiles = n_rows // tm
    src = jnp.zeros((n_rows,), jnp.int32).at[pos.reshape(-1)].set(
        jnp.repeat(jnp.arange(n, dtype=jnp.int32), TOP_K))
    gate_rows = jnp.zeros((n_rows,), F32).at[pos.reshape(-1)].set(gates.reshape(-1))
    tile_start = jnp.arange(n_tiles, dtype=jnp.int32) * tm
    tile_valid = (tile_start < ends[-1]).astype(jnp.int32)
    tile_expert = jnp.minimum(jnp.searchsorted(ends, tile_start, side="right"), N_EXPERTS - 1).astype(jnp.int32)
    last_expert = tile_expert[jnp.maximum(ends[-1] // tm - 1, 0)]
    tile_expert = jnp.where(tile_valid > 0, tile_expert, last_expert)
    return pos, src, gate_rows.reshape(n_rows, 1), tile_expert, tile_valid


def kernel(x, w_in, w_o, mla_q_norm, mla_kv_norm, mla_w_uq, mla_w_ukv, attn_sinks, rel_bias_table, mix_norm_g,
           ln1_g, ln1_b, ln2_g, ln2_b, ffn_w_gate, ffn_w_up, ffn_w_down, moe_router, moe_w_gate, moe_w_up,
           moe_w_down):
    batch, seq, _ = x.shape
    n = batch * seq
    perm = jnp.array(A_HEAD_ORDER)

    w_in_p = _relayout_w_in(w_in)
    wo_a = w_o[:, :GROUP_WIDTH].reshape(DEPTH, N_HEADS, HEAD_DIM, D_MODEL)[:, perm].reshape(
        DEPTH, GROUP_WIDTH, D_MODEL)
    w_o_p = jnp.concatenate([wo_a, w_o[:, GROUP_WIDTH:]], axis=1).astype(BF16)
    mg_a = mix_norm_g[:, 0].reshape(DEPTH, N_HEADS, HEAD_DIM)[:, perm].reshape(DEPTH, 1, GROUP_WIDTH)
    mix_g_p = jnp.concatenate([mg_a, mix_norm_g[:, 1:]], axis=1)
    wq_t, wq_s, wk_t, wv = _relayout_mla(mla_w_uq, mla_w_ukv)
    psw = _rope_swap_matrix()
    cos_t, sin_t = _rope_lane_tables(seq)
    tq = 256
    tri = (jnp.arange(tq)[:, None] > jnp.arange(tq)[None, :]).astype(BF16)
    bias_a = _band_bias_masked(rel_bias_table[:, :N_HEADS], 1, A_WINDOW - 1)
    biases_d = [_band_bias_masked(rel_bias_table[:, N_HEADS:], rate, window // rate) for window, rate in D_PATTERNS]
    router_p = jnp.pad(moe_router, ((0, 0), (0, 0), (0, LANES - N_EXPERTS))).astype(BF16)
    wg_d, wu_d, wd_d = ffn_w_gate.astype(BF16), ffn_w_up.astype(BF16), ffn_w_down.astype(BF16)
    wg_m, wu_m, wd_m = moe_w_gate.astype(BF16), moe_w_up.astype(BF16), moe_w_down.astype(BF16)
    heads_a = tuple((c, c + 4) for c in range(4))
    heads_d = tuple((2 * c, 2 * c + 1) for c in range(4))
    moe_tm = 256

    xf = x.reshape(n, D_MODEL)
    xb = xf
    for layer in range(DEPTH):
        h = matmul_ws(xb, w_in_p[layer], 512, 512, BF16)
        h3 = h.reshape(batch, seq, IN_WIDTH_P)
        ga = banded_attention(h3, bias_a, attn_sinks[layer], batch=batch, seq=seq, rate=1, q_col=COL_QA,
                              k_col=COL_KA, v_col=COL_VA, kv_width=LANES, heads=heads_a, want_lse=False,
                              out_dtype=BF16)
        q_b, k_b, v_b = mla_up(h, mla_q_norm[layer][None], mla_kv_norm[layer][None], wq_t[layer], wq_s[layer],
                               wk_t[layer], wv[layer], psw, cos_t, sin_t, seq=seq, tm=512)
        gb = mla_attention(q_b.reshape(batch, seq, -1), k_b.reshape(batch, seq, -1),
                           v_b.reshape(batch, seq, -1), batch=batch, seq=seq, tq=tq)
        gc = stick_breaking_attention(h3, tri, batch=batch, seq=seq, tq=tq)
        d_outs, d_lses = [], []
        for (window, rate), bias in zip(D_PATTERNS, biases_d):
            o, lse = banded_attention(h3, bias, None, batch=batch, seq=seq, rate=rate, q_col=COL_QD,
                                      k_col=COL_KD, v_col=COL_VD, kv_width=GROUP_WIDTH, heads=heads_d,
                                      want_lse=True, out_dtype=F32)
            d_outs.append(o.reshape(n, GROUP_WIDTH))
            d_lses.append(lse.reshape(n, GROUP_WIDTH))
        i = layer // 2
        router = router_p[i] if layer % 2 == 1 else jnp.zeros((D_MODEL, LANES), BF16)
        xf, xb, logits = out_proj_ln(ga.reshape(n, -1), gb.reshape(n, -1), gc.reshape(n, -1), d_outs, d_lses, xf,
                                     mix_g_p[layer], w_o_p[layer], ln1_g[layer][None], ln1_b[layer][None], router,
                                     tm=256)
        if layer % 2 == 0:
            hmid = ffn_gate_up(xb, wg_d[i], wu_d[i], tm=512, tf=512)
            xf, xb = ffn_down_ln(hmid, wd_d[i], xf, ln2_g[layer][None], ln2_b[layer][None], tm=512, tk=512)
        else:
            pos, src, gate_rows, tile_expert, tile_valid = _route(logits[:, :N_EXPERTS], moe_tm)
            xs = jnp.take(xb, src, axis=0)
            hs = moe_gate_up(tile_expert, tile_valid, xs, wg_m[i], wu_m[i], tm=moe_tm, tf=512)
            ys = moe_down(tile_expert, tile_valid, hs, wd_m[i], gate_rows, tm=moe_tm, tn=1024)
            y0 = jnp.take(ys, pos[:, 0], axis=0)
            y1 = jnp.take(ys, pos[:, 1], axis=0)
            xf, xb = combine_ln(xf, y0, y1, ln2_g[layer][None], ln2_b[layer][None], tm=512)
    return xf.reshape(batch, seq, D_MODEL)
```

```python
import functools
import math

import jax
import jax.numpy as jnp
from jax import lax
from jax.experimental import pallas as pl
from jax.experimental.pallas import tpu as pltpu

D_MODEL = 2048
DEPTH = 4
HEAD_DIM = 64
N_HEADS = 8
GROUP_WIDTH = N_HEADS * HEAD_DIM
BLOCK = 128
A_KV_HEADS = 2
A_WINDOW = 128
B_NOPE_DIM = 64
B_ROPE_DIM = 32
B_Q_LORA = 384
B_KV_LORA = 256
ROPE_THETA = 10000.0
D_PATTERNS = ((128, 1), (512, 4), (2048, 16))
NUM_BUCKETS = 32
T5_MAX_DISTANCE = 2048
D_FF = 5632
N_EXPERTS = 8
TOP_K = 2
NEG_INF = -1e30
LN_EPS = 1e-5
RMS_EPS = 1e-6
ALPHA = (2 * DEPTH) ** 0.25

LANES = 128
VMEM_LIMIT = 56 * 1024 * 1024

COL_QA, COL_QC, COL_KC, COL_CQ, COL_KA = 0, 512, 1024, 1536, 1920
COL_VC, COL_CKV, COL_VA, COL_KR = 2048, 2560, 2816, 2944
IN_WIDTH_P = 3072
IN_WIDTH_D = 3 * GROUP_WIDTH
A_HEAD_ORDER = (0, 4, 1, 5, 2, 6, 3, 7)
ROPE_LANE0 = 64
EXP_UNDERFLOW = 105.0
QK_SCALE = HEAD_DIM ** -0.5

BF16 = jnp.bfloat16
F32 = jnp.float32


def _params(*sem):
    return pltpu.CompilerParams(dimension_semantics=sem, vmem_limit_bytes=VMEM_LIMIT)


def _dot(a, b):
    return jnp.dot(a, b, preferred_element_type=F32)


def _dot_nt(a, b):
    return lax.dot_general(a, b, (((1,), (1,)), ((), ())), preferred_element_type=F32)


def _lo_lanes(rows):
    return lax.broadcasted_iota(jnp.int32, (rows, LANES), 1) < HEAD_DIM


def _split_pair(t, lo):
    zero = jnp.zeros_like(t)
    return jnp.concatenate([jnp.where(lo, t, zero), jnp.where(lo, zero, t)], axis=0)


def _matmul_kernel(a_ref, w_ref, o_ref):
    o_ref[...] = _dot(a_ref[...].astype(BF16), w_ref[...]).astype(o_ref.dtype)


def matmul_ws(a, w, tm, tn, out_dtype):
    m, k = a.shape
    n = w.shape[1]
    return pl.pallas_call(
        _matmul_kernel,
        grid=(n // tn, m // tm),
        in_specs=[pl.BlockSpec((tm, k), lambda j, i: (i, 0)),
                  pl.BlockSpec((k, tn), lambda j, i: (0, j))],
        out_specs=pl.BlockSpec((tm, tn), lambda j, i: (i, j)),
        out_shape=jax.ShapeDtypeStruct((m, n), out_dtype),
        compiler_params=_params("arbitrary", "arbitrary"),
        name="matmul_ws",
    )(a, w)


def _swa_kernel(q_ref, kp_ref, kc_ref, vp_ref, vc_ref, bias_ref, sink_ref, o_ref):
    lo = _lo_lanes(BLOCK)
    kb = jnp.concatenate([kp_ref[0], kc_ref[0]], axis=0)
    vb = jnp.concatenate([vp_ref[0], vc_ref[0]], axis=0)
    which = jnp.where(pl.program_id(1) == 0, 1, 0)
    for c in range(GROUP_WIDTH // LANES):
        rows = slice(2 * c * BLOCK, 2 * (c + 1) * BLOCK)
        qs = _split_pair(q_ref[0, :, c * LANES:(c + 1) * LANES], lo) * QK_SCALE
        s = _dot_nt(qs, kb) + bias_ref[which, rows, :]
        sink = sink_ref[rows, :]
        m = jnp.maximum(jnp.max(s, axis=1, keepdims=True), sink)
        p = jnp.exp(s - jnp.concatenate([m, m], axis=1))
        den = jnp.sum(p, axis=1, keepdims=True) + jnp.exp(sink - m)
        o = _dot(p.astype(BF16), vb) / den
        o_ref[0, :, c * LANES:(c + 1) * LANES] = jnp.where(lo, o[:BLOCK], o[BLOCK:]).astype(o_ref.dtype)


def swa_attention(h3, bias_rows, sink_rows, *, batch, seq):
    nb = seq // BLOCK
    kb, vb = COL_KA // LANES, COL_VA // LANES
    return pl.pallas_call(
        _swa_kernel,
        grid=(batch, nb),
        in_specs=[pl.BlockSpec((1, BLOCK, GROUP_WIDTH), lambda b, n: (b, n, COL_QA // GROUP_WIDTH)),
                  pl.BlockSpec((1, BLOCK, LANES), lambda b, n: (b, jnp.maximum(n - 1, 0), kb)),
                  pl.BlockSpec((1, BLOCK, LANES), lambda b, n: (b, n, kb)),
                  pl.BlockSpec((1, BLOCK, LANES), lambda b, n: (b, jnp.maximum(n - 1, 0), vb)),
                  pl.BlockSpec((1, BLOCK, LANES), lambda b, n: (b, n, vb)),
                  pl.BlockSpec((2, N_HEADS * BLOCK, 2 * BLOCK), lambda b, n: (0, 0, 0)),
                  pl.BlockSpec((N_HEADS * BLOCK, LANES), lambda b, n: (0, 0))],
        out_specs=pl.BlockSpec((1, BLOCK, GROUP_WIDTH), lambda b, n: (b, n, 0)),
        out_shape=jax.ShapeDtypeStruct((batch, seq, GROUP_WIDTH), BF16),
        compiler_params=_params("arbitrary", "arbitrary"),
        name="swa_attn",
    )(h3, h3, h3, h3, h3, bias_rows, sink_rows)


def _dilated_kernel(q_ref, k_ref, v_ref, b0_ref, b1_ref, b2_ref, o_ref, osc, lsc):
    seq = q_ref.shape[1]
    lo = _lo_lanes(BLOCK)
    for p,((_, rate), b_ref) in enumerate(zip(D_PATTERNS, (b0_ref, b1_ref, b2_ref))):
        nb = seq // (rate * BLOCK)

        def unit(u, carry, p=p, rate=rate, b_ref=b_ref, nb=nb):
            n = u // rate
            res = u - n * rate
            rows = pl.ds(n * (BLOCK * rate) + res, BLOCK, stride=rate)
            qs = _split_pair((q_ref[0, rows, :] * QK_SCALE).astype(BF16), lo)
            kc = k_ref[0, rows, :].astype(BF16)
            vc = v_ref[0, rows, :].astype(BF16)
            if nb > 1:
                prow = pl.ds(jnp.maximum(n - 1, 0) * (BLOCK * rate) + res, BLOCK, stride=rate)
                kb = jnp.concatenate([k_ref[0, prow, :].astype(BF16), kc], axis=0)
                vb = jnp.concatenate([v_ref[0, prow, :].astype(BF16), vc], axis=0)
                s = _dot_nt(qs, kb) + b_ref[jnp.where(n == 0, 1, 0), 0]
            else:
                vb = vc
                s = _dot_nt(qs, kc) + b_ref[0, 0, :, BLOCK:2 * BLOCK]
            m = jnp.max(s, axis=1, keepdims=True)
            e = jnp.exp(s - m)
            den = jnp.sum(e, axis=1, keepdims=True)
            o = _dot(e.astype(BF16), vb) / den
            lse = jnp.broadcast_to(m + jnp.log(den), (2 * BLOCK, LANES))
            osc[p, rows, :] = jnp.where(lo, o[:BLOCK], o[BLOCK:])
            lsc[p, rows, :] = jnp.where(lo, lse[:BLOCK], lse[BLOCK:])
            return carry

        lax.fori_loop(0, nb * rate, unit, 0, unroll=8)

    chunk = 2 * BLOCK

    def merge(t, carry):
        r = pl.ds(pl.multiple_of(t * chunk, chunk), chunk)
        l0, l1, l2 = lsc[0, r, :], lsc[1, r, :], lsc[2, r, :]
        m = jnp.maximum(jnp.maximum(l0, l1), l2)
        e0, e1, e2 = jnp.exp(l0 - m), jnp.exp(l1 - m), jnp.exp(l2 - m)
        o = (e0 * osc[0, r, :] + e1 * osc[1, r, :] + e2 * osc[2, r, :]) / (e0 + e1 + e2)
        o_ref[0, r, :] = o.astype(o_ref.dtype)
        return carry

    lax.fori_loop(0, seq // chunk, merge, 0)


def dilated_attention(hd3, biases, *, batch, seq):
    pairs = N_HEADS // 2
    bspec = pl.BlockSpec((2, 1, 2 * BLOCK, 2 * BLOCK), lambda b, p: (0, p, 0, 0))
    return pl.pallas_call(
        _dilated_kernel,
        grid=(batch, pairs),
        in_specs=[pl.BlockSpec((1, seq, LANES), lambda b, p: (b, 0, p)),
                  pl.BlockSpec((1, seq, LANES), lambda b, p: (b, 0, pairs + p)),
                  pl.BlockSpec((1, seq, LANES), lambda b, p: (b, 0, 2 * pairs + p)),
                  bspec, bspec, bspec],
        out_specs=pl.BlockSpec((1, seq, LANES), lambda b, p: (b, 0, p)),
        out_shape=jax.ShapeDtypeStruct((batch, seq, GROUP_WIDTH), BF16),
        scratch_shapes=[pltpu.VMEM((len(D_PATTERNS), seq, LANES), F32),
                        pltpu.VMEM((len(D_PATTERNS), seq, LANES), F32)],
        compiler_params=_params("arbitrary", "arbitrary"),
        name="dilated_attn",
    )(hd3, hd3, hd3, *biases)


def _mla_up_kernel(cq_ref, ckv_ref, kr_ref, qn_ref, kvn_ref, wq_ref, wqs_ref, wk_ref, wv_ref, psw_ref,
                   cos_ref, sin_ref, q_ref, k_ref, v_ref, *, scale):
    def rms(x_ref, g_ref):
        x = x_ref[...].astype(F32)
        return (x * lax.rsqrt(jnp.mean(x * x, axis=-1, keepdims=True) + RMS_EPS) * g_ref[...]).astype(BF16)

    xq = rms(cq_ref, qn_ref)
    xkv = rms(ckv_ref, kvn_ref)
    cos = cos_ref[...]
    sin = sin_ref[...]
    t = _dot(xq, wq_ref[...])
    ts = _dot(xq, wqs_ref[...])
    kn = _dot(xkv, wk_ref[...])
    kr = kr_ref[...]
    rk = kr.astype(F32) * cos + _dot(kr, psw_ref[...]) * sin
    for h in range(N_HEADS):
        sl = slice(h * LANES, (h + 1) * LANES)
        q_ref[:, sl] = ((t[:, sl] * cos + ts[:, sl] * sin) * scale).astype(BF16)
        k_ref[:, sl] = (kn[:, sl] + rk).astype(BF16)
    v_ref[...] = _dot(xkv, wv_ref[...]).astype(BF16)


def mla_up(h, q_norm, kv_norm, wq, wqs, wk, wv, psw, cos_t, sin_t, *, seq, tm):
    n = h.shape[0]
    w = N_HEADS * LANES
    const = lambda shape: pl.BlockSpec(shape, lambda i: (0, 0))
    spt = seq // tm
    return pl.pallas_call(
        functools.partial(_mla_up_kernel, scale=(B_NOPE_DIM + B_ROPE_DIM) ** -0.5),
        grid=(n // tm,),
        in_specs=[pl.BlockSpec((tm, B_Q_LORA), lambda i: (i, COL_CQ // B_Q_LORA)),
                  pl.BlockSpec((tm, B_KV_LORA), lambda i: (i, COL_CKV // B_KV_LORA)),
                  pl.BlockSpec((tm, LANES), lambda i: (i, COL_KR // LANES)),
                  const((1, B_Q_LORA)), const((1, B_KV_LORA)),
                  const((B_Q_LORA, w)), const((B_Q_LORA, w)), const((B_KV_LORA, w)),
                  const((B_KV_LORA, GROUP_WIDTH)), const((LANES, LANES)),
                  pl.BlockSpec((tm, LANES), lambda i: (i % spt, 0)),
                  pl.BlockSpec((tm, LANES), lambda i: (i % spt, 0))],
        out_specs=[pl.BlockSpec((tm, w), lambda i: (i, 0)),
                   pl.BlockSpec((tm, w), lambda i: (i, 0)),
                   pl.BlockSpec((tm, GROUP_WIDTH), lambda i: (i, 0))],
        out_shape=[jax.ShapeDtypeStruct((n, w), BF16), jax.ShapeDtypeStruct((n, w), BF16),
                   jax.ShapeDtypeStruct((n, GROUP_WIDTH), BF16)],
        compiler_params=_params("arbitrary"),
        name="mla_up",
    )(h, h, h, q_norm, kv_norm, wq, wqs, wk, wv, psw, cos_t, sin_t)


def _mla_attn_kernel(q_ref, k_ref, v_ref, o_ref, *, tq, heads):
    i = pl.program_id(2)
    lo = _lo_lanes(tq)
    row = lax.broadcasted_iota(jnp.int32, (tq, tq), 0)
    col = lax.broadcasted_iota(jnp.int32, (tq, tq), 1)
    causal = col <= row

    def step(j, carry, masked):
        start = pl.multiple_of(j * tq, tq)
        out = []
        for hh in range(heads):
            m, l, acc = carry[3 * hh:3 * hh + 3]
            q = q_ref[0, :, hh * LANES:(hh + 1) * LANES]
            s = _dot_nt(q, k_ref[0, pl.ds(start, tq), hh * LANES:(hh + 1) * LANES])
            if masked:
                s = jnp.where(causal, s, NEG_INF)
            m_new = jnp.maximum(m, jnp.max(s, axis=1, keepdims=True))
            p = jnp.exp(s - m_new)
            a = jnp.exp(m - m_new)
            pv = _dot(p.astype(BF16), v_ref[0, pl.ds(start, tq), (hh // 2) * LANES:(hh // 2 + 1) * LANES])
            out += [m_new, a * l + jnp.sum(p, axis=1, keepdims=True), a * acc + pv]
        return tuple(out)

    init = (jnp.full((tq, 1), NEG_INF, F32), jnp.zeros((tq, 1), F32), jnp.zeros((tq, LANES), F32)) * heads
    carry = lax.fori_loop(0, i, functools.partial(step, masked=False), init)
    carry = step(i, carry, True)
    for pp in range(heads // 2):
        (_, l0, a0), (_, l1, a1) = carry[6 * pp:6 * pp + 3], carry[6 * pp + 3:6 * pp + 6]
        o_ref[0, :, pp * LANES:(pp + 1) * LANES] = jnp.where(lo, a0 / l0, a1 / l1).astype(o_ref.dtype)


def mla_attention(q, k, v, *, batch, seq, tq, heads):
    pairs = heads // 2
    return pl.pallas_call(
        functools.partial(_mla_attn_kernel, tq=tq, heads=heads),
        grid=(batch, N_HEADS // heads, seq // tq),
        in_specs=[pl.BlockSpec((1, tq, heads * LANES), lambda b, g, i: (b, i, g)),
                  pl.BlockSpec((1, seq, heads * LANES), lambda b, g, i: (b, 0, g)),
                  pl.BlockSpec((1, seq, pairs * LANES), lambda b, g, i: (b, 0, g))],
        out_specs=pl.BlockSpec((1, tq, pairs * LANES), lambda b, g, i: (b, i, g)),
        out_shape=jax.ShapeDtypeStruct((batch, seq, GROUP_WIDTH), BF16),
        compiler_params=_params("arbitrary", "arbitrary", "arbitrary"),
        name="mla_attn",
    )(q, k, v)


def _stick_kernel(q_ref, k_ref, v_ref, tri_ref, o_ref, *, tq, pairs):
    i = pl.program_id(2)
    lo = _lo_lanes(tq)
    row = lax.broadcasted_iota(jnp.int32, (2 * tq, tq), 0)
    col = lax.broadcasted_iota(jnp.int32, (2 * tq, tq), 1)
    strict = col < jnp.where(row >= tq, row - tq, row)
    tri = tri_ref[...]
    qs = [_split_pair(q_ref[0, :, pp * LANES:(pp + 1) * LANES], lo) * QK_SCALE for pp in range(pairs)]

    def chunk(pp, j, run, masked):
        start = pl.multiple_of(j * tq, tq)
        z = _dot_nt(qs[pp], k_ref[0, pl.ds(start, tq), pp * LANES:(pp + 1) * LANES])
        soft = jnp.log(1.0 + jnp.exp(-jnp.abs(z)))
        keep = -(jnp.maximum(z, 0.0) + soft)
        hit = jnp.minimum(z, 0.0) - soft
        if masked:
            keep = jnp.where(strict, keep, 0.0)
        hi = keep.astype(BF16)
        rest = (keep - hi.astype(F32)).astype(BF16)
        after = _dot(hi, tri) + _dot(rest, tri)
        a = jnp.exp(hit + after)
        if masked:
            a = jnp.where(strict, a, 0.0)
        contrib = jnp.exp(run) * _dot(a.astype(BF16), v_ref[0, pl.ds(start, tq), pp * LANES:(pp + 1) * LANES])
        return contrib, run + jnp.sum(keep, axis=1, keepdims=True)

    state = []
    for pp in range(pairs):
        acc, run = chunk(pp, i, jnp.zeros((2 * tq, 1), F32), True)
        state += [run, acc]

    def cond(carry):
        t = carry[0]
        live = carry[1]
        for pp in range(1, pairs):
            live = jnp.maximum(live, carry[1 + 2 * pp])
        return jnp.logical_and(t < i, jnp.max(live) > -EXP_UNDERFLOW)

    def body(carry):
        t = carry[0]
        out = [t + 1]
        for pp in range(pairs):
            contrib, run = chunk(pp, i - 1 - t, carry[1 + 2 * pp], False)
            out += [run, carry[2 + 2 * pp] + contrib]
        return tuple(out)

    final = lax.while_loop(cond, body, (jnp.int32(0), *state))
    for pp in range(pairs):
        acc = final[2 + 2 * pp]
        o_ref[0, :, pp * LANES:(pp + 1) * LANES] = jnp.where(lo, acc[:tq], acc[tq:]).astype(o_ref.dtype)


def stick_breaking_attention(h3, tri, *, batch, seq, tq, pairs):
    w = pairs * LANES
    return pl.pallas_call(
        functools.partial(_stick_kernel, tq=tq, pairs=pairs),
        grid=(batch, GROUP_WIDTH // w, seq // tq),
        in_specs=[pl.BlockSpec((1, tq, w), lambda b, g, i: (b, i, COL_QC // w + g)),
                  pl.BlockSpec((1, seq, w), lambda b, g, i: (b, 0, COL_KC // w + g)),
                  pl.BlockSpec((1, seq, w), lambda b, g, i: (b, 0, COL_VC // w + g)),
                  pl.BlockSpec((tq, tq), lambda b, g, i: (0, 0))],
        out_specs=pl.BlockSpec((1, tq, w), lambda b, g, i: (b, i, g)),
        out_shape=jax.ShapeDtypeStruct((batch, seq, GROUP_WIDTH), BF16),
        compiler_params=_params("arbitrary", "arbitrary", "arbitrary"),
        name="stick_breaking",
    )(h3, h3, h3, tri)


def _layer_norm(r, g, b):
    mu = jnp.mean(r, axis=-1, keepdims=True)
    d = r - mu
    var = jnp.mean(d * d, axis=-1, keepdims=True)
    return d * lax.rsqrt(var + LN_EPS) * g + b


def _out_proj_kernel(ga_ref, gb_ref, gc_ref, gd_ref, x_ref, mixg_ref, wo_ref, g_ref, b_ref, rt_ref,
                     xo_ref, xb_ref, lg_ref):
    mixed = None
    for gi, grp in enumerate((ga_ref, gb_ref, gc_ref, gd_ref)):
        xg = grp[...].astype(F32)
        y = xg * lax.rsqrt(jnp.mean(xg * xg, axis=-1, keepdims=True) + RMS_EPS) * mixg_ref[gi:gi + 1, :]
        part = _dot(y.astype(BF16), wo_ref[gi * GROUP_WIDTH:(gi + 1) * GROUP_WIDTH, :])
        mixed = part if mixed is None else mixed + part
    x1 = _layer_norm(ALPHA * x_ref[...].astype(F32) + mixed, g_ref[...], b_ref[...])
    xo_ref[...] = x1
    xb = x1.astype(BF16)
    xb_ref[...] = xb
    lg_ref[...] = _dot(xb, rt_ref[...])


def out_proj_ln(ga, gb, gc, gd, x, mix_g, wo, ln_g, ln_b, router, *, tm):
    n = x.shape[0]
    row = lambda w: pl.BlockSpec((tm, w), lambda i: (i, 0))
    const = lambda shape: pl.BlockSpec(shape, lambda i: (0, 0))
    return pl.pallas_call(
        _out_proj_kernel,
        grid=(n // tm,),
        in_specs=[row(GROUP_WIDTH)] * 4 + [row(D_MODEL), const((4, GROUP_WIDTH)), const((D_MODEL, D_MODEL)),
                                          const((1, D_MODEL)), const((1, D_MODEL)), const((D_MODEL, LANES))],
        out_specs=[row(D_MODEL), row(D_MODEL), row(LANES)],
        out_shape=[jax.ShapeDtypeStruct((n, D_MODEL), F32), jax.ShapeDtypeStruct((n, D_MODEL), BF16),
                   jax.ShapeDtypeStruct((n, LANES), F32)],
        compiler_params=_params("arbitrary"),
        name="out_proj_ln",
    )(ga, gb, gc, gd, x, mix_g, wo, ln_g, ln_b, router)


def _silu_mul(g, u):
    return g * (1.0 / (1.0 + jnp.exp(-g))) * u


def _gate_up_kernel(x_ref, wg_ref, wu_ref, o_ref, wgb_ref, wub_ref):
    @pl.when(pl.program_id(1) == 0)
    def _():
        wgb_ref[...] = wg_ref[...].astype(BF16)
        wub_ref[...] = wu_ref[...].astype(BF16)

    x = x_ref[...]
    o_ref[...] = _silu_mul(_dot(x, wgb_ref[...]), _dot(x, wub_ref[...])).astype(o_ref.dtype)


def ffn_gate_up(xb, wg, wu, *, tm, tf):
    n = xb.shape[0]
    f = wg.shape[1]
    return pl.pallas_call(
        _gate_up_kernel,
        grid=(f // tf, n // tm),
        in_specs=[pl.BlockSpec((tm, D_MODEL), lambda j, i: (i, 0)),
                  pl.BlockSpec((D_MODEL, tf), lambda j, i: (0, j)),
                  pl.BlockSpec((D_MODEL, tf), lambda j, i: (0, j))],
        out_specs=pl.BlockSpec((tm, tf), lambda j, i: (i, j)),
        out_shape=jax.ShapeDtypeStruct((n, f), BF16),
        scratch_shapes=[pltpu.VMEM((D_MODEL, tf), BF16), pltpu.VMEM((D_MODEL, tf), BF16)],
        compiler_params=_params("arbitrary", "arbitrary"),
        name="ffn_gate_up",
    )(xb, wg, wu)


def _down_ln_kernel(h_ref, wd_ref, x_ref, g_ref, b_ref, xo_ref, xb_ref, acc_ref):
    kk = pl.program_id(1)

    @pl.when(kk == 0)
    def _():
        acc_ref[...] = jnp.zeros_like(acc_ref)

    acc_ref[...] += _dot(h_ref[...], wd_ref[...])

    @pl.when(kk == pl.num_programs(1) - 1)
    def _():
        x2 = _layer_norm(ALPHA * x_ref[...] + acc_ref[...], g_ref[...], b_ref[...])
        xo_ref[...] = x2
        xb_ref[...] = x2.astype(BF16)


def ffn_down_ln(hmid, wd, x, ln_g, ln_b, *, tm, tk):
    n, f = hmid.shape
    return pl.pallas_call(
        _down_ln_kernel,
        grid=(n // tm, f // tk),
        in_specs=[pl.BlockSpec((tm, tk), lambda i, k: (i, k)),
                  pl.BlockSpec((tk, D_MODEL), lambda i, k: (k, 0)),
                  pl.BlockSpec((tm, D_MODEL), lambda i, k: (i, 0)),
                  pl.BlockSpec((1, D_MODEL), lambda i, k: (0, 0)),
                  pl.BlockSpec((1, D_MODEL), lambda i, k: (0, 0))],
        out_specs=[pl.BlockSpec((tm, D_MODEL), lambda i, k: (i, 0)),
                   pl.BlockSpec((tm, D_MODEL), lambda i, k: (i, 0))],
        out_shape=[jax.ShapeDtypeStruct((n, D_MODEL), F32), jax.ShapeDtypeStruct((n, D_MODEL), BF16)],
        scratch_shapes=[pltpu.VMEM((tm, D_MODEL), F32)],
        compiler_params=_params("arbitrary", "arbitrary"),
        name="ffn_down_ln",
    )(hmid, wd, x, ln_g, ln_b)


def _new_expert(te_ref, t):
    return jnp.logical_or(t == 0, te_ref[t] != te_ref[jnp.maximum(t - 1, 0)])


def _moe_gate_up_kernel(te_ref, tv_ref, x_ref, wg_ref, wu_ref, o_ref, wgb_ref, wub_ref):
    t = pl.program_id(1)

    @pl.when(_new_expert(te_ref, t))
    def _():
        wgb_ref[...] = wg_ref[0].astype(BF16)
        wub_ref[...] = wu_ref[0].astype(BF16)

    @pl.when(tv_ref[t] > 0)
    def _():
        x = x_ref[...]
        o_ref[...] = _silu_mul(_dot(x, wgb_ref[...]), _dot(x, wub_ref[...])).astype(o_ref.dtype)

    @pl.when(tv_ref[t] == 0)
    def _():
        o_ref[...] = jnp.zeros_like(o_ref)


def moe_gate_up(tile_expert, tile_valid, xs, wg, wu, *, tm, tf):
    p = xs.shape[0]
    f = wg.shape[2]
    grid_spec = pltpu.PrefetchScalarGridSpec(
        num_scalar_prefetch=2,
        grid=(f // tf, p // tm),
        in_specs=[pl.BlockSpec((tm, D_MODEL), lambda j, t, te, tv: (t, 0)),
                  pl.BlockSpec((1, D_MODEL, tf), lambda j, t, te, tv: (te[t], 0, j)),
                  pl.BlockSpec((1, D_MODEL, tf), lambda j, t, te, tv: (te[t], 0, j))],
        out_specs=pl.BlockSpec((tm, tf), lambda j, t, te, tv: (t, j)),
        scratch_shapes=[pltpu.VMEM((D_MODEL, tf), BF16), pltpu.VMEM((D_MODEL, tf), BF16)],
    )
    return pl.pallas_call(
        _moe_gate_up_kernel,
        grid_spec=grid_spec,
        out_shape=jax.ShapeDtypeStruct((p, f), BF16),
        compiler_params=_params("arbitrary", "arbitrary"),
        name="moe_gate_up",
    )(tile_expert, tile_valid, xs, wg, wu)


def _moe_down_kernel(te_ref, tv_ref, h_ref, wd_ref, gate_ref, o_ref, wdb_ref):
    t = pl.program_id(1)

    @pl.when(_new_expert(te_ref, t))
    def _():
        wdb_ref[...] = wd_ref[0].astype(BF16)

    @pl.when(tv_ref[t] > 0)
    def _():
        o_ref[...] = _dot(h_ref[...], wdb_ref[...]) * gate_ref[...]

    @pl.when(tv_ref[t] == 0)
    def _():
        o_ref[...] = jnp.zeros_like(o_ref)


def moe_down(tile_expert, tile_valid, hs, wd, gate_rows, *, tm, tn):
    p, f = hs.shape
    grid_spec = pltpu.PrefetchScalarGridSpec(
        num_scalar_prefetch=2,
        grid=(D_MODEL // tn, p // tm),
        in_specs=[pl.BlockSpec((tm, f), lambda j, t, te, tv: (t, 0)),
                  pl.BlockSpec((1, f, tn), lambda j, t, te, tv: (te[t], 0, j)),
                  pl.BlockSpec((tm, 1), lambda j, t, te, tv: (t, 0))],
        out_specs=pl.BlockSpec((tm, tn), lambda j, t, te, tv: (t, j)),
        scratch_shapes=[pltpu.VMEM((f, tn), BF16)],
    )
    return pl.pallas_call(
        _moe_down_kernel,
        grid_spec=grid_spec,
        out_shape=jax.ShapeDtypeStruct((p, D_MODEL), F32),
        compiler_params=_params("arbitrary", "arbitrary"),
        name="moe_down",
    )(tile_expert, tile_valid, hs, wd, gate_rows)


def _combine_ln_kernel(x_ref, y0_ref, y1_ref, g_ref, b_ref, xo_ref, xb_ref):
    x2 = _layer_norm(ALPHA * x_ref[...] + (y0_ref[...] + y1_ref[...]), g_ref[...], b_ref[...])
    xo_ref[...] = x2
    xb_ref[...] = x2.astype(BF16)


def combine_ln(x, y0, y1, ln_g, ln_b, *, tm):
    n = x.shape[0]
    row = pl.BlockSpec((tm, D_MODEL), lambda i: (i, 0))
    const = pl.BlockSpec((1, D_MODEL), lambda i: (0, 0))
    return pl.pallas_call(
        _combine_ln_kernel,
        grid=(n // tm,),
        in_specs=[row, row, row, const, const],
        out_specs=[row, row],
        out_shape=[jax.ShapeDtypeStruct((n, D_MODEL), F32), jax.ShapeDtypeStruct((n, D_MODEL), BF16)],
        compiler_params=_params("arbitrary"),
        name="combine_ln",
    )(x, y0, y1, ln_g, ln_b)


def _t5_bucket(dist):
    max_exact = NUM_BUCKETS // 2
    d = jnp.maximum(dist, 1).astype(F32)
    large = max_exact + (jnp.log(d / max_exact) / math.log(T5_MAX_DISTANCE / max_exact)
                         * (NUM_BUCKETS - max_exact)).astype(jnp.int32)
    large = jnp.minimum(large, NUM_BUCKETS - 1)
    return jnp.where(dist < max_exact, dist, large)


def _band_bias_masked(table, stride, max_dist):
    dist = jnp.arange(BLOCK)[:, None] + BLOCK - jnp.arange(2 * BLOCK)[None, :]
    bias = jnp.transpose(table[_t5_bucket(jnp.maximum(dist, 0) * stride)], (2, 0, 1)).astype(F32)
    valid = (dist >= 0) & (dist <= max_dist)
    return jnp.where(valid[None], bias, NEG_INF)


def _rope_lane_tables(seq):
    half = B_ROPE_DIM // 2
    inv = ROPE_THETA ** (-jnp.arange(0, B_ROPE_DIM, 2, dtype=F32) / B_ROPE_DIM)
    ang = jnp.arange(seq, dtype=F32)[:, None] * inv[None, :]
    cos, sin = jnp.cos(ang), jnp.sin(ang)
    ones = jnp.ones((seq, ROPE_LANE0), F32)
    zeros = jnp.zeros((seq, ROPE_LANE0), F32)
    pad = jnp.zeros((seq, LANES - ROPE_LANE0 - 2 * half), F32)
    cos_t = jnp.concatenate([ones, cos, cos, pad], axis=1)
    sin_t = jnp.concatenate([zeros, -sin, sin, pad], axis=1)
    return cos_t, sin_t


def _permute_heads(t, axis, order):
    parts = [lax.slice_in_dim(t, hh * HEAD_DIM, (hh + 1) * HEAD_DIM, axis=axis) for hh in order]
    return jnp.concatenate(parts, axis=axis)


def _relayout_w_in(w_in):
    w = w_in.astype(BF16)
    d = w.shape[0]
    seg = {}
    start = 0
    for name, size in (("qa", 512), ("ka", 128), ("va", 128), ("cq", 384), ("ckv", 256), ("kr", 32),
                       ("qc", 512), ("kc", 512), ("vc", 512), ("qd", 512), ("kd", 512), ("vd", 512)):
        seg[name] = w[:, :, start:start + size]
        start += size
    qa = _permute_heads(seg["qa"], 2, A_HEAD_ORDER)
    zl = jnp.zeros((d, D_MODEL, ROPE_LANE0), BF16)
    zr = jnp.zeros((d, D_MODEL, LANES - ROPE_LANE0 - B_ROPE_DIM), BF16)
    kr = jnp.concatenate([zl, seg["kr"], zr], axis=2)
    main = jnp.concatenate([qa, seg["qc"], seg["kc"], seg["cq"], seg["ka"], seg["vc"], seg["ckv"], seg["va"], kr],
                           axis=2)
    return main, w[:, :, start - IN_WIDTH_D:start]


def _relayout_mla(w_uq, w_ukv):
    d = w_uq.shape[0]
    half = B_ROPE_DIM // 2
    wq = w_uq.astype(BF16).reshape(d, B_Q_LORA, N_HEADS, B_NOPE_DIM + B_ROPE_DIM)
    nope, r1, r2 = wq[..., :B_NOPE_DIM], wq[..., B_NOPE_DIM:B_NOPE_DIM + half], wq[..., B_NOPE_DIM + half:]
    z32 = jnp.zeros(wq.shape[:3] + (LANES - B_NOPE_DIM - B_ROPE_DIM,), BF16)
    z64 = jnp.zeros(wq.shape[:3] + (B_NOPE_DIM,), BF16)
    wq_t = jnp.concatenate([nope, r1, r2, z32], axis=-1).reshape(d, B_Q_LORA, N_HEADS * LANES)
    wq_s = jnp.concatenate([z64, r2, r1, z32], axis=-1).reshape(d, B_Q_LORA, N_HEADS * LANES)
    wkv = w_ukv.astype(BF16).reshape(d, B_KV_LORA, N_HEADS, 2 * HEAD_DIM)
    zk = jnp.zeros(wkv.shape[:3] + (LANES - B_NOPE_DIM,), BF16)
    wk_t = jnp.concatenate([wkv[..., :B_NOPE_DIM], zk], axis=-1).reshape(d, B_KV_LORA, N_HEADS * LANES)
    wv = wkv[..., B_NOPE_DIM:].reshape(d, B_KV_LORA, GROUP_WIDTH)
    return wq_t, wq_s, wk_t, wv


def _rope_swap_matrix():
    half = B_ROPE_DIM // 2
    src = jnp.arange(LANES)[:, None]
    dst = jnp.arange(LANES)[None, :]
    first = (dst >= ROPE_LANE0) & (dst < ROPE_LANE0 + half) & (src == dst + half)
    second = (dst >= ROPE_LANE0 + half) & (dst < ROPE_LANE0 + 2 * half) & (src == dst - half)
    return (first | second).astype(BF16)


def _route(logits, tm):
    n = logits.shape[0]
    top_logits, top_idx = lax.top_k(logits, TOP_K)
    gates = jax.nn.softmax(top_logits, axis=-1)
    onehot = jax.nn.one_hot(top_idx, N_EXPERTS, dtype=jnp.int32)
    member = jnp.sum(onehot, axis=1)
    rank = jnp.cumsum(member, axis=0) - member
    counts = jnp.sum(member, axis=0)
    padded = ((counts + tm - 1) // tm) * tm
    ends = jnp.cumsum(padded)
    starts = ends - padded
    pos = jnp.sum(onehot * (starts[None, None, :] + rank[:, None, :]), axis=-1)
    n_rows = TOP_K * n + N_EXPERTS * tm
    n_tiles = n_rows // tm
    src = jnp.zeros((n_rows,), jnp.int32).at[pos.reshape(-1)].set(
        jnp.repeat(jnp.arange(n, dtype=jnp.int32), TOP_K))
    gate_rows = jnp.zeros((n_rows,), F32).at[pos.reshape(-1)].set(gates.reshape(-1))
    tile_start = jnp.arange(n_tiles, dtype=jnp.int32) * tm
    tile_valid = (tile_start < ends[-1]).astype(jnp.int32)
    tile_expert = jnp.minimum(jnp.searchsorted(ends, tile_start, side="right"), N_EXPERTS - 1).astype(jnp.int32)
    last_expert = tile_expert[jnp.maximum(ends[-1] // tm - 1, 0)]
    tile_expert = jnp.where(tile_valid > 0, tile_expert, last_expert)
    return pos, src, gate_rows.reshape(n_rows, 1), tile_expert, tile_valid


def kernel(x, w_in, w_o, mla_q_norm, mla_kv_norm, mla_w_uq, mla_w_ukv, attn_sinks, rel_bias_table, mix_norm_g,
           ln1_g, ln1_b, ln2_g, ln2_b, ffn_w_gate, ffn_w_up, ffn_w_down, moe_router, moe_w_gate, moe_w_up,
           moe_w_down):
    batch, seq, _ = x.shape
    n = batch * seq

    w_in_p, w_in_d = _relayout_w_in(w_in)
    w_o_p = jnp.concatenate([_permute_heads(w_o[:, :GROUP_WIDTH], 1, A_HEAD_ORDER), w_o[:, GROUP_WIDTH:]],
                            axis=1).astype(BF16)
    mix_g_p = jnp.concatenate([_permute_heads(mix_norm_g[:, :1], 2, A_HEAD_ORDER), mix_norm_g[:, 1:]], axis=1)
    wq_t, wq_s, wk_t, wv = _relayout_mla(mla_w_uq, mla_w_ukv)
    psw = _rope_swap_matrix()
    cos_t, sin_t = _rope_lane_tables(seq)
    tq = 256
    tri = (jnp.arange(tq)[:, None] > jnp.arange(tq)[None, :]).astype(BF16)
    order = list(A_HEAD_ORDER)
    def with_first_block_variant(b):
        prev = jnp.arange(2 * BLOCK) < BLOCK
        return jnp.stack([b, jnp.where(prev, NEG_INF, b)], axis=0)

    bias_a = _band_bias_masked(rel_bias_table[:, :N_HEADS], 1, A_WINDOW - 1)
    bias_a_rows = with_first_block_variant(jnp.concatenate([bias_a[hh] for hh in order], axis=0))
    sink_rows = jnp.concatenate([jnp.broadcast_to(attn_sinks[:, hh, None, None], (DEPTH, BLOCK, LANES))
                                 for hh in order], axis=1)
    biases_d = [with_first_block_variant(_band_bias_masked(rel_bias_table[:, N_HEADS:], rate, window // rate).reshape(
        N_HEADS // 2, 2 * BLOCK, 2 * BLOCK)) for window, rate in D_PATTERNS]
    router_p = jnp.pad(moe_router, ((0, 0), (0, 0), (0, LANES - N_EXPERTS))).astype(BF16)
    wd_d = ffn_w_down.astype(BF16)
    moe_tm = 256

    xf = x.reshape(n, D_MODEL)
    xb = xf
    for layer in range(DEPTH):
        h = matmul_ws(xb, w_in_p[layer], 1024, 1024, BF16)
        hd = matmul_ws(xb, w_in_d[layer], 1024, 512, F32)
        h3 = h.reshape(batch, seq, IN_WIDTH_P)
        ga = swa_attention(h3, bias_a_rows, sink_rows[layer], batch=batch, seq=seq)
        q_b, k_b, v_b = mla_up(h, mla_q_norm[layer][None], mla_kv_norm[layer][None], wq_t[layer], wq_s[layer],
                               wk_t[layer], wv[layer], psw, cos_t, sin_t, seq=seq, tm=512)
        gb = mla_attention(q_b.reshape(batch, seq, -1), k_b.reshape(batch, seq, -1),
                           v_b.reshape(batch, seq, -1), batch=batch, seq=seq, tq=tq, heads=4)
        gc = stick_breaking_attention(h3, tri, batch=batch, seq=seq, tq=tq, pairs=2)
        gd = dilated_attention(hd.reshape(batch, seq, IN_WIDTH_D), biases_d, batch=batch, seq=seq)
        i = layer // 2
        router = router_p[i] if layer % 2 == 1 else jnp.zeros((D_MODEL, LANES), BF16)
        xf, xb, logits = out_proj_ln(ga.reshape(n, -1), gb.reshape(n, -1), gc.reshape(n, -1), gd.reshape(n, -1),
                                     xf, mix_g_p[layer], w_o_p[layer], ln1_g[layer][None], ln1_b[layer][None],
                                     router, tm=256)
        if layer % 2 == 0:
            hmid = ffn_gate_up(xb, ffn_w_gate[i], ffn_w_up[i], tm=512, tf=512)
            xf, xb = ffn_down_ln(hmid, wd_d[i], xf, ln2_g[layer][None], ln2_b[layer][None], tm=512, tk=512)
        else:
            pos, src, gate_rows, tile_expert, tile_valid = _route(logits[:, :N_EXPERTS], moe_tm)
            xs = jnp.take(xb, src, axis=0)
            hs = moe_gate_up(tile_expert, tile_valid, xs, moe_w_gate[i], moe_w_up[i], tm=moe_tm, tf=512)
            ys = moe_down(tile_expert, tile_valid, hs, moe_w_down[i], gate_rows, tm=moe_tm, tn=512)
            y0 = jnp.take(ys, pos[:, 0], axis=0)
            y1 = jnp.take(ys, pos[:, 1], axis=0)
            xf, xb = combine_ln(xf, y0, y1, ln2_g[layer][None], ln2_b[layer][None], tm=512)
    return xf.reshape(batch, seq, D_MODEL)
```

```python
import functools
import math

import jax
import jax.numpy as jnp
from jax import lax
from jax.experimental import pallas as pl
from jax.experimental.pallas import tpu as pltpu

D_MODEL = 2048
DEPTH = 4
HEAD_DIM = 64
N_HEADS = 8
GROUP_WIDTH = N_HEADS * HEAD_DIM
BLOCK = 128
A_KV_HEADS = 2
A_WINDOW = 128
B_NOPE_DIM = 64
B_ROPE_DIM = 32
B_Q_LORA = 384
B_KV_LORA = 256
ROPE_THETA = 10000.0
D_PATTERNS = ((128, 1), (512, 4), (2048, 16))
NUM_BUCKETS = 32
T5_MAX_DISTANCE = 2048
D_FF = 5632
N_EXPERTS = 8
TOP_K = 2
NEG_INF = -1e30
LN_EPS = 1e-5
RMS_EPS = 1e-6
ALPHA = (2 * DEPTH) ** 0.25

LANES = 128
VMEM_LIMIT = 56 * 1024 * 1024

COL_QA, COL_QC, COL_KC, COL_CQ, COL_KA = 0, 512, 1024, 1536, 1920
COL_VC, COL_CKV, COL_VA, COL_KR = 2048, 2560, 2816, 2944
IN_WIDTH_P = 3072
IN_WIDTH_D = 3 * GROUP_WIDTH
A_HEAD_ORDER = (0, 4, 1, 5, 2, 6, 3, 7)
ROPE_LANE0 = 64
EXP_UNDERFLOW = 105.0
QK_SCALE = HEAD_DIM ** -0.5

BF16 = jnp.bfloat16
F32 = jnp.float32


def _params(*sem):
    return pltpu.CompilerParams(dimension_semantics=sem, vmem_limit_bytes=VMEM_LIMIT)


def _dot(a, b):
    return jnp.dot(a, b, preferred_element_type=F32)


def _dot_nt(a, b):
    return lax.dot_general(a, b, (((1,), (1,)), ((), ())), preferred_element_type=F32)


def _lo_lanes(rows):
    return lax.broadcasted_iota(jnp.int32, (rows, LANES), 1) < HEAD_DIM


def _split_pair(t, lo):
    zero = jnp.zeros_like(t)
    return jnp.concatenate([jnp.where(lo, t, zero), jnp.where(lo, zero, t)], axis=0)


def _matmul_kernel(a_ref, w_ref, o_ref):
    o_ref[...] = _dot(a_ref[...].astype(BF16), w_ref[...]).astype(o_ref.dtype)


def matmul_ws(a, w, tm, tn, out_dtype):
    m, k = a.shape
    n = w.shape[1]
    return pl.pallas_call(
        _matmul_kernel,
        grid=(n // tn, m // tm),
        in_specs=[pl.BlockSpec((tm, k), lambda j, i: (i, 0)),
                  pl.BlockSpec((k, tn), lambda j, i: (0, j))],
        out_specs=pl.BlockSpec((tm, tn), lambda j, i: (i, j)),
        out_shape=jax.ShapeDtypeStruct((m, n), out_dtype),
        compiler_params=_params("arbitrary", "arbitrary"),
        name="matmul_ws",
    )(a, w)


def _swa_kernel(q_ref, kp_ref, kc_ref, vp_ref, vc_ref, bias_ref, sink_ref, o_ref):
    lo = _lo_lanes(BLOCK)
    kb = jnp.concatenate([kp_ref[0], kc_ref[0]], axis=0)
    vb = jnp.concatenate([vp_ref[0], vc_ref[0]], axis=0)
    which = jnp.where(pl.program_id(1) == 0, 1, 0)
    for c in range(GROUP_WIDTH // LANES):
        rows = slice(2 * c * BLOCK, 2 * (c + 1) * BLOCK)
        qs = _split_pair(q_ref[0, :, c * LANES:(c + 1) * LANES], lo) * QK_SCALE
        s = _dot_nt(qs, kb) + bias_ref[which, rows, :]
        sink = sink_ref[rows, :]
        m = jnp.maximum(jnp.max(s, axis=1, keepdims=True), sink)
        p = jnp.exp(s - jnp.concatenate([m, m], axis=1))
        den = jnp.sum(p, axis=1, keepdims=True) + jnp.exp(sink - m)
        o = _dot(p.astype(BF16), vb) / den
        o_ref[0, :, c * LANES:(c + 1) * LANES] = jnp.where(lo, o[:BLOCK], o[BLOCK:]).astype(o_ref.dtype)


def swa_attention(h3, bias_rows, sink_rows, *, batch, seq):
    nb = seq // BLOCK
    kb, vb = COL_KA // LANES, COL_VA // LANES
    return pl.pallas_call(
        _swa_kernel,
        grid=(batch, nb),
        in_specs=[pl.BlockSpec((1, BLOCK, GROUP_WIDTH), lambda b, n: (b, n, COL_QA // GROUP_WIDTH)),
                  pl.BlockSpec((1, BLOCK, LANES), lambda b, n: (b, jnp.maximum(n - 1, 0), kb)),
                  pl.BlockSpec((1, BLOCK, LANES), lambda b, n: (b, n, kb)),
                  pl.BlockSpec((1, BLOCK, LANES), lambda b, n: (b, jnp.maximum(n - 1, 0), vb)),
                  pl.BlockSpec((1, BLOCK, LANES), lambda b, n: (b, n, vb)),
                  pl.BlockSpec((2, N_HEADS * BLOCK, 2 * BLOCK), lambda b, n: (0, 0, 0)),
                  pl.BlockSpec((N_HEADS * BLOCK, LANES), lambda b, n: (0, 0))],
        out_specs=pl.BlockSpec((1, BLOCK, GROUP_WIDTH), lambda b, n: (b, n, 0)),
        out_shape=jax.ShapeDtypeStruct((batch, seq, GROUP_WIDTH), BF16),
        compiler_params=_params("arbitrary", "arbitrary"),
        name="swa_attn",
    )(h3, h3, h3, h3, h3, bias_rows, sink_rows)


def _dilated_kernel(q_ref, k_ref, v_ref, b0_ref, b1_ref, b2_ref, o_ref, osc, lsc):
    seq = q_ref.shape[1]
    lo = _lo_lanes(BLOCK)
    for p,((_, rate), b_ref) in enumerate(zip(D_PATTERNS, (b0_ref, b1_ref, b2_ref))):
        nb = seq // (rate * BLOCK)

        def unit(u, carry, p=p, rate=rate, b_ref=b_ref, nb=nb):
            n = u // rate
            res = u - n * rate
            rows = pl.ds(n * (BLOCK * rate) + res, BLOCK, stride=rate)
            qs = _split_pair((q_ref[0, rows, :] * QK_SCALE).astype(BF16), lo)
            kc = k_ref[0, rows, :].astype(BF16)
            vc = v_ref[0, rows, :].astype(BF16)
            if nb > 1:
                prow = pl.ds(jnp.maximum(n - 1, 0) * (BLOCK * rate) + res, BLOCK, stride=rate)
                kb = jnp.concatenate([k_ref[0, prow, :].astype(BF16), kc], axis=0)
                vb = jnp.concatenate([v_ref[0, prow, :].astype(BF16), vc], axis=0)
                s = _dot_nt(qs, kb) + b_ref[jnp.where(n == 0, 1, 0), 0]
            else:
                vb = vc
                s = _dot_nt(qs, kc) + b_ref[0, 0, :, BLOCK:2 * BLOCK]
            m = jnp.max(s, axis=1, keepdims=True)
            e = jnp.exp(s - m)
            den = jnp.sum(e, axis=1, keepdims=True)
            o = _dot(e.astype(BF16), vb) / den
            lse = jnp.broadcast_to(m + jnp.log(den), (2 * BLOCK, LANES))
            osc[p, rows, :] = jnp.where(lo, o[:BLOCK], o[BLOCK:])
            lsc[p, rows, :] = jnp.where(lo, lse[:BLOCK], lse[BLOCK:])
            return carry

        lax.fori_loop(0, nb * rate, unit, 0, unroll=8)

    chunk = 2 * BLOCK

    def merge(t, carry):
        r = pl.ds(pl.multiple_of(t * chunk, chunk), chunk)
        l0, l1, l2 = lsc[0, r, :], lsc[1, r, :], lsc[2, r, :]
        m = jnp.maximum(jnp.maximum(l0, l1), l2)
        e0, e1, e2 = jnp.exp(l0 - m), jnp.exp(l1 - m), jnp.exp(l2 - m)
        o = (e0 * osc[0, r, :] + e1 * osc[1, r, :] + e2 * osc[2, r, :]) / (e0 + e1 + e2)
        o_ref[0, r, :] = o.astype(o_ref.dtype)
        return carry

    lax.fori_loop(0, seq // chunk, merge, 0)


def dilated_attention(hd3, biases, *, batch, seq):
    pairs = N_HEADS // 2
    bspec = pl.BlockSpec((2, 1, 2 * BLOCK, 2 * BLOCK), lambda b, p: (0, p, 0, 0))
    return pl.pallas_call(
        _dilated_kernel,
        grid=(batch, pairs),
        in_specs=[pl.BlockSpec((1, seq, LANES), lambda b, p: (b, 0, p)),
                  pl.BlockSpec((1, seq, LANES), lambda b, p: (b, 0, pairs + p)),
                  pl.BlockSpec((1, seq, LANES), lambda b, p: (b, 0, 2 * pairs + p)),
                  bspec, bspec, bspec],
        out_specs=pl.BlockSpec((1, seq, LANES), lambda b, p: (b, 0, p)),
        out_shape=jax.ShapeDtypeStruct((batch, seq, GROUP_WIDTH), BF16),
        scratch_shapes=[pltpu.VMEM((len(D_PATTERNS), seq, LANES), F32),
                        pltpu.VMEM((len(D_PATTERNS), seq, LANES), F32)],
        compiler_params=_params("arbitrary", "arbitrary"),
        name="dilated_attn",
    )(hd3, hd3, hd3, *biases)


def _mla_up_kernel(cq_ref, ckv_ref, kr_ref, qn_ref, kvn_ref, wq_ref, wqs_ref, wk_ref, wv_ref, psw_ref,
                   cos_ref, sin_ref, q_ref, k_ref, v_ref, *, scale):
    def rms(x_ref, g_ref):
        x = x_ref[...].astype(F32)
        return (x * lax.rsqrt(jnp.mean(x * x, axis=-1, keepdims=True) + RMS_EPS) * g_ref[...]).astype(BF16)

    xq = rms(cq_ref, qn_ref)
    xkv = rms(ckv_ref, kvn_ref)
    cos = cos_ref[...]
    sin = sin_ref[...]
    t = _dot(xq, wq_ref[...])
    ts = _dot(xq, wqs_ref[...])
    kn = _dot(xkv, wk_ref[...])
    kr = kr_ref[...]
    rk = kr.astype(F32) * cos + _dot(kr, psw_ref[...]) * sin
    for h in range(N_HEADS):
        sl = slice(h * LANES, (h + 1) * LANES)
        q_ref[:, sl] = ((t[:, sl] * cos + ts[:, sl] * sin) * scale).astype(BF16)
        k_ref[:, sl] = (kn[:, sl] + rk).astype(BF16)
    v_ref[...] = _dot(xkv, wv_ref[...]).astype(BF16)


def mla_up(h, q_norm, kv_norm, wq, wqs, wk, wv, psw, cos_t, sin_t, *, seq, tm):
    n = h.shape[0]
    w = N_HEADS * LANES
    const = lambda shape: pl.BlockSpec(shape, lambda i: (0, 0))
    spt = seq // tm
    return pl.pallas_call(
        functools.partial(_mla_up_kernel, scale=(B_NOPE_DIM + B_ROPE_DIM) ** -0.5),
        grid=(n // tm,),
        in_specs=[pl.BlockSpec((tm, B_Q_LORA), lambda i: (i, COL_CQ // B_Q_LORA)),
                  pl.BlockSpec((tm, B_KV_LORA), lambda i: (i, COL_CKV // B_KV_LORA)),
                  pl.BlockSpec((tm, LANES), lambda i: (i, COL_KR // LANES)),
                  const((1, B_Q_LORA)), const((1, B_KV_LORA)),
                  const((B_Q_LORA, w)), const((B_Q_LORA, w)), const((B_KV_LORA, w)),
                  const((B_KV_LORA, GROUP_WIDTH)), const((LANES, LANES)),
                  pl.BlockSpec((tm, LANES), lambda i: (i % spt, 0)),
                  pl.BlockSpec((tm, LANES), lambda i: (i % spt, 0))],
        out_specs=[pl.BlockSpec((tm, w), lambda i: (i, 0)),
                   pl.BlockSpec((tm, w), lambda i: (i, 0)),
                   pl.BlockSpec((tm, GROUP_WIDTH), lambda i: (i, 0))],
        out_shape=[jax.ShapeDtypeStruct((n, w), BF16), jax.ShapeDtypeStruct((n, w), BF16),
                   jax.ShapeDtypeStruct((n, GROUP_WIDTH), BF16)],
        compiler_params=_params("arbitrary"),
        name="mla_up",
    )(h, h, h, q_norm, kv_norm, wq, wqs, wk, wv, psw, cos_t, sin_t)


def _mla_attn_kernel(q_ref, k_ref, v_ref, o_ref, *, tq, heads):
    i = pl.program_id(2)
    lo = _lo_lanes(tq)
    row = lax.broadcasted_iota(jnp.int32, (tq, tq), 0)
    col = lax.broadcasted_iota(jnp.int32, (tq, tq), 1)
    causal = col <= row

    def step(j, carry, masked):
        start = pl.multiple_of(j * tq, tq)
        out = []
        for hh in range(heads):
            m, l, acc = carry[3 * hh:3 * hh + 3]
            q = q_ref[0, :, hh * LANES:(hh + 1) * LANES]
            s = _dot_nt(q, k_ref[0, pl.ds(start, tq), hh * LANES:(hh + 1) * LANES])
            if masked:
                s = jnp.where(causal, s, NEG_INF)
            m_new = jnp.maximum(m, jnp.max(s, axis=1, keepdims=True))
            p = jnp.exp(s - m_new)
            a = jnp.exp(m - m_new)
            pv = _dot(p.astype(BF16), v_ref[0, pl.ds(start, tq), (hh // 2) * LANES:(hh // 2 + 1) * LANES])
            out += [m_new, a * l + jnp.sum(p, axis=1, keepdims=True), a * acc + pv]
        return tuple(out)

    init = (jnp.full((tq, 1), NEG_INF, F32), jnp.zeros((tq, 1), F32), jnp.zeros((tq, LANES), F32)) * heads
    carry = lax.fori_loop(0, i, functools.partial(step, masked=False), init)
    carry = step(i, carry, True)
    for pp in range(heads // 2):
        (_, l0, a0), (_, l1, a1) = carry[6 * pp:6 * pp + 3], carry[6 * pp + 3:6 * pp + 6]
        o_ref[0, :, pp * LANES:(pp + 1) * LANES] = jnp.where(lo, a0 / l0, a1 / l1).astype(o_ref.dtype)


def mla_attention(q, k, v, *, batch, seq, tq, heads):
    pairs = heads // 2
    return pl.pallas_call(
        functools.partial(_mla_attn_kernel, tq=tq, heads=heads),
        grid=(batch, N_HEADS // heads, seq // tq),
        in_specs=[pl.BlockSpec((1, tq, heads * LANES), lambda b, g, i: (b, i, g)),
                  pl.BlockSpec((1, seq, heads * LANES), lambda b, g, i: (b, 0, g)),
                  pl.BlockSpec((1, seq, pairs * LANES), lambda b, g, i: (b, 0, g))],
        out_specs=pl.BlockSpec((1, tq, pairs * LANES), lambda b, g, i: (b, i, g)),
        out_shape=jax.ShapeDtypeStruct((batch, seq, GROUP_WIDTH), BF16),
        compiler_params=_params("arbitrary", "arbitrary", "arbitrary"),
        name="mla_attn",
    )(q, k, v)


def _stick_kernel(q_ref, k_ref, v_ref, tri_ref, o_ref, *, tq, pairs):
    i = pl.program_id(2)
    lo = _lo_lanes(tq)
    row = lax.broadcasted_iota(jnp.int32, (2 * tq, tq), 0)
    col = lax.broadcasted_iota(jnp.int32, (2 * tq, tq), 1)
    strict = col < jnp.where(row >= tq, row - tq, row)
    tri = tri_ref[...]
    qs = [_split_pair(q_ref[0, :, pp * LANES:(pp + 1) * LANES], lo) * QK_SCALE for pp in range(pairs)]

    def chunk(pp, j, run, masked):
        start = pl.multiple_of(j * tq, tq)
        z = _dot_nt(qs[pp], k_ref[0, pl.ds(start, tq), pp * LANES:(pp + 1) * LANES])
        soft = jnp.log(1.0 + jnp.exp(-jnp.abs(z)))
        keep = -(jnp.maximum(z, 0.0) + soft)
        hit = jnp.minimum(z, 0.0) - soft
        if masked:
            keep = jnp.where(strict, keep, 0.0)
        hi = keep.astype(BF16)
        rest = (keep - hi.astype(F32)).astype(BF16)
        after = _dot(hi, tri) + _dot(rest, tri)
        a = jnp.exp(hit + after)
        if masked:
            a = jnp.where(strict, a, 0.0)
        contrib = jnp.exp(run) * _dot(a.astype(BF16), v_ref[0, pl.ds(start, tq), pp * LANES:(pp + 1) * LANES])
        return contrib, run + jnp.sum(keep, axis=1, keepdims=True)

    state = []
    for pp in range(pairs):
        acc, run = chunk(pp, i, jnp.zeros((2 * tq, 1), F32), True)
        state += [run, acc]

    def cond(carry):
        t = carry[0]
        live = carry[1]
        for pp in range(1, pairs):
            live = jnp.maximum(live, carry[1 + 2 * pp])
        return jnp.logical_and(t < i, jnp.max(live) > -EXP_UNDERFLOW)

    def body(carry):
        t = carry[0]
        out = [t + 1]
        for pp in range(pairs):
            contrib, run = chunk(pp, i - 1 - t, carry[1 + 2 * pp], False)
            out += [run, carry[2 + 2 * pp] + contrib]
        return tuple(out)

    final = lax.while_loop(cond, body, (jnp.int32(0), *state))
    for pp in range(pairs):
        acc = final[2 + 2 * pp]
        o_ref[0, :, pp * LANES:(pp + 1) * LANES] = jnp.where(lo, acc[:tq], acc[tq:]).astype(o_ref.dtype)


def stick_breaking_attention(h3, tri, *, batch, seq, tq, pairs):
    w = pairs * LANES
    return pl.pallas_call(
        functools.partial(_stick_kernel, tq=tq, pairs=pairs),
        grid=(batch, GROUP_WIDTH // w, seq // tq),
        in_specs=[pl.BlockSpec((1, tq, w), lambda b, g, i: (b, i, COL_QC // w + g)),
                  pl.BlockSpec((1, seq, w), lambda b, g, i: (b, 0, COL_KC // w + g)),
                  pl.BlockSpec((1, seq, w), lambda b, g, i: (b, 0, COL_VC // w + g)),
                  pl.BlockSpec((tq, tq), lambda b, g, i: (0, 0))],
        out_specs=pl.BlockSpec((1, tq, w), lambda b, g, i: (b, i, g)),
        out_shape=jax.ShapeDtypeStruct((batch, seq, GROUP_WIDTH), BF16),
        compiler_params=_params("arbitrary", "arbitrary", "arbitrary"),
        name="stick_breaking",
    )(h3, h3, h3, tri)


def _layer_norm(r, g, b):
    mu = jnp.mean(r, axis=-1, keepdims=True)
    d = r - mu
    var = jnp.mean(d * d, axis=-1, keepdims=True)
    return d * lax.rsqrt(var + LN_EPS) * g + b


def _out_proj_kernel(ga_ref, gb_ref, gc_ref, gd_ref, x_ref, mixg_ref, wo_ref, g_ref, b_ref, rt_ref,
                     xo_ref, xb_ref, lg_ref):
    mixed = None
    for gi, grp in enumerate((ga_ref, gb_ref, gc_ref, gd_ref)):
        xg = grp[...].astype(F32)
        y = xg * lax.rsqrt(jnp.mean(xg * xg, axis=-1, keepdims=True) + RMS_EPS) * mixg_ref[gi:gi + 1, :]
        part = _dot(y.astype(BF16), wo_ref[gi * GROUP_WIDTH:(gi + 1) * GROUP_WIDTH, :])
        mixed = part if mixed is None else mixed + part
    x1 = _layer_norm(ALPHA * x_ref[...].astype(F32) + mixed, g_ref[...], b_ref[...])
    xo_ref[...] = x1
    xb = x1.astype(BF16)
    xb_ref[...] = xb
    lg_ref[...] = _dot(xb, rt_ref[...])


def out_proj_ln(ga, gb, gc, gd, x, mix_g, wo, ln_g, ln_b, router, *, tm):
    n = x.shape[0]
    row = lambda w: pl.BlockSpec((tm, w), lambda i: (i, 0))
    const = lambda shape: pl.BlockSpec(shape, lambda i: (0, 0))
    return pl.pallas_call(
        _out_proj_kernel,
        grid=(n // tm,),
        in_specs=[row(GROUP_WIDTH)] * 4 + [row(D_MODEL), const((4, GROUP_WIDTH)), const((D_MODEL, D_MODEL)),
                                          const((1, D_MODEL)), const((1, D_MODEL)), const((D_MODEL, LANES))],
        out_specs=[row(D_MODEL), row(D_MODEL), row(LANES)],
        out_shape=[jax.ShapeDtypeStruct((n, D_MODEL), F32), jax.ShapeDtypeStruct((n, D_MODEL), BF16),
                   jax.ShapeDtypeStruct((n, LANES), F32)],
        compiler_params=_params("arbitrary"),
        name="out_proj_ln",
    )(ga, gb, gc, gd, x, mix_g, wo, ln_g, ln_b, router)


def _silu_mul(g, u):
    return g * (1.0 / (1.0 + jnp.exp(-g))) * u


def _gate_up_kernel(x_ref, wg_ref, wu_ref, o_ref, wgb_ref, wub_ref):
    @pl.when(pl.program_id(1) == 0)
    def _():
        wgb_ref[...] = wg_ref[...].astype(BF16)
        wub_ref[...] = wu_ref[...].astype(BF16)

    x = x_ref[...]
    o_ref[...] = _silu_mul(_dot(x, wgb_ref[...]), _dot(x, wub_ref[...])).astype(o_ref.dtype)


def ffn_gate_up(xb, wg, wu, *, tm, tf):
    n = xb.shape[0]
    f = wg.shape[1]
    return pl.pallas_call(
        _gate_up_kernel,
        grid=(f // tf, n // tm),
        in_specs=[pl.BlockSpec((tm, D_MODEL), lambda j, i: (i, 0)),
                  pl.BlockSpec((D_MODEL, tf), lambda j, i: (0, j)),
                  pl.BlockSpec((D_MODEL, tf), lambda j, i: (0, j))],
        out_specs=pl.BlockSpec((tm, tf), lambda j, i: (i, j)),
        out_shape=jax.ShapeDtypeStruct((n, f), BF16),
        scratch_shapes=[pltpu.VMEM((D_MODEL, tf), BF16), pltpu.VMEM((D_MODEL, tf), BF16)],
        compiler_params=_params("arbitrary", "arbitrary"),
        name="ffn_gate_up",
    )(xb, wg, wu)


def _down_ln_kernel(h_ref, wd_ref, x_ref, g_ref, b_ref, xo_ref, xb_ref, acc_ref):
    kk = pl.program_id(1)

    @pl.when(kk == 0)
    def _():
        acc_ref[...] = jnp.zeros_like(acc_ref)

    acc_ref[...] += _dot(h_ref[...], wd_ref[...])

    @pl.when(kk == pl.num_programs(1) - 1)
    def _():
        x2 = _layer_norm(ALPHA * x_ref[...] + acc_ref[...], g_ref[...], b_ref[...])
        xo_ref[...] = x2
        xb_ref[...] = x2.astype(BF16)


def ffn_down_ln(hmid, wd, x, ln_g, ln_b, *, tm, tk):
    n, f = hmid.shape
    return pl.pallas_call(
        _down_ln_kernel,
        grid=(n // tm, f // tk),
        in_specs=[pl.BlockSpec((tm, tk), lambda i, k: (i, k)),
                  pl.BlockSpec((tk, D_MODEL), lambda i, k: (k, 0)),
                  pl.BlockSpec((tm, D_MODEL), lambda i, k: (i, 0)),
                  pl.BlockSpec((1, D_MODEL), lambda i, k: (0, 0)),
                  pl.BlockSpec((1, D_MODEL), lambda i, k: (0, 0))],
        out_specs=[pl.BlockSpec((tm, D_MODEL), lambda i, k: (i, 0)),
                   pl.BlockSpec((tm, D_MODEL), lambda i, k: (i, 0))],
        out_shape=[jax.ShapeDtypeStruct((n, D_MODEL), F32), jax.ShapeDtypeStruct((n, D_MODEL), BF16)],
        scratch_shapes=[pltpu.VMEM((tm, D_MODEL), F32)],
        compiler_params=_params("arbitrary", "arbitrary"),
        name="ffn_down_ln",
    )(hmid, wd, x, ln_g, ln_b)


def _new_expert(te_ref, t):
    return jnp.logical_or(t == 0, te_ref[t] != te_ref[jnp.maximum(t - 1, 0)])


def _moe_gate_up_kernel(te_ref, tv_ref, x_ref, wg_ref, wu_ref, o_ref, wgb_ref, wub_ref):
    t = pl.program_id(1)

    @pl.when(_new_expert(te_ref, t))
    def _():
        wgb_ref[...] = wg_ref[0].astype(BF16)
        wub_ref[...] = wu_ref[0].astype(BF16)

    @pl.when(tv_ref[t] > 0)
    def _():
        x = x_ref[...]
        o_ref[...] = _silu_mul(_dot(x, wgb_ref[...]), _dot(x, wub_ref[...])).astype(o_ref.dtype)

    @pl.when(tv_ref[t] == 0)
    def _():
        o_ref[...] = jnp.zeros_like(o_ref)


def moe_gate_up(tile_expert, tile_valid, xs, wg, wu, *, tm, tf):
    p = xs.shape[0]
    f = wg.shape[2]
    grid_spec = pltpu.PrefetchScalarGridSpec(
        num_scalar_prefetch=2,
        grid=(f // tf, p // tm),
        in_specs=[pl.BlockSpec((tm, D_MODEL), lambda j, t, te, tv: (t, 0)),
                  pl.BlockSpec((1, D_MODEL, tf), lambda j, t, te, tv: (te[t], 0, j)),
                  pl.BlockSpec((1, D_MODEL, tf), lambda j, t, te, tv: (te[t], 0, j))],
        out_specs=pl.BlockSpec((tm, tf), lambda j, t, te, tv: (t, j)),
        scratch_shapes=[pltpu.VMEM((D_MODEL, tf), BF16), pltpu.VMEM((D_MODEL, tf), BF16)],
    )
    return pl.pallas_call(
        _moe_gate_up_kernel,
        grid_spec=grid_spec,
        out_shape=jax.ShapeDtypeStruct((p, f), BF16),
        compiler_params=_params("arbitrary", "arbitrary"),
        name="moe_gate_up",
    )(tile_expert, tile_valid, xs, wg, wu)


def _moe_down_kernel(te_ref, tv_ref, h_ref, wd_ref, o_ref, wdb_ref):
    t = pl.program_id(1)

    @pl.when(_new_expert(te_ref, t))
    def _():
        wdb_ref[...] = wd_ref[0].astype(BF16)

    @pl.when(tv_ref[t] > 0)
    def _():
        o_ref[...] = _dot(h_ref[...], wdb_ref[...])

    @pl.when(tv_ref[t] == 0)
    def _():
        o_ref[...] = jnp.zeros_like(o_ref)


def moe_down(tile_expert, tile_valid, hs, wd, *, tm, tn):
    p, f = hs.shape
    grid_spec = pltpu.PrefetchScalarGridSpec(
        num_scalar_prefetch=2,
        grid=(D_MODEL // tn, p // tm),
        in_specs=[pl.BlockSpec((tm, f), lambda j, t, te, tv: (t, 0)),
                  pl.BlockSpec((1, f, tn), lambda j, t, te, tv: (te[t], 0, j))],
        out_specs=pl.BlockSpec((tm, tn), lambda j, t, te, tv: (t, j)),
        scratch_shapes=[pltpu.VMEM((f, tn), BF16)],
    )
    return pl.pallas_call(
        _moe_down_kernel,
        grid_spec=grid_spec,
        out_shape=jax.ShapeDtypeStruct((p, D_MODEL), F32),
        compiler_params=_params("arbitrary", "arbitrary"),
        name="moe_down",
    )(tile_expert, tile_valid, hs, wd)


def _combine_ln_kernel(x_ref, y0_ref, y1_ref, gate_ref, g_ref, b_ref, xo_ref, xb_ref):
    f = gate_ref[:, 0:1] * y0_ref[...] + gate_ref[:, 1:2] * y1_ref[...]
    x2 = _layer_norm(ALPHA * x_ref[...] + f, g_ref[...], b_ref[...])
    xo_ref[...] = x2
    xb_ref[...] = x2.astype(BF16)


def combine_ln(x, y0, y1, gates, ln_g, ln_b, *, tm):
    n = x.shape[0]
    row = pl.BlockSpec((tm, D_MODEL), lambda i: (i, 0))
    const = pl.BlockSpec((1, D_MODEL), lambda i: (0, 0))
    return pl.pallas_call(
        _combine_ln_kernel,
        grid=(n // tm,),
        in_specs=[row, row, row, pl.BlockSpec((tm, TOP_K), lambda i: (i, 0)), const, const],
        out_specs=[row, row],
        out_shape=[jax.ShapeDtypeStruct((n, D_MODEL), F32), jax.ShapeDtypeStruct((n, D_MODEL), BF16)],
        compiler_params=_params("arbitrary"),
        name="combine_ln",
    )(x, y0, y1, gates, ln_g, ln_b)


def _t5_bucket(dist):
    max_exact = NUM_BUCKETS // 2
    d = jnp.maximum(dist, 1).astype(F32)
    large = max_exact + (jnp.log(d / max_exact) / math.log(T5_MAX_DISTANCE / max_exact)
                         * (NUM_BUCKETS - max_exact)).astype(jnp.int32)
    large = jnp.minimum(large, NUM_BUCKETS - 1)
    return jnp.where(dist < max_exact, dist, large)


def _band_bias_masked(table, stride, max_dist):
    dist = jnp.arange(BLOCK)[:, None] + BLOCK - jnp.arange(2 * BLOCK)[None, :]
    bias = jnp.transpose(table[_t5_bucket(jnp.maximum(dist, 0) * stride)], (2, 0, 1)).astype(F32)
    valid = (dist >= 0) & (dist <= max_dist)
    return jnp.where(valid[None], bias, NEG_INF)


def _rope_lane_tables(seq):
    half = B_ROPE_DIM // 2
    inv = ROPE_THETA ** (-jnp.arange(0, B_ROPE_DIM, 2, dtype=F32) / B_ROPE_DIM)
    ang = jnp.arange(seq, dtype=F32)[:, None] * inv[None, :]
    cos, sin = jnp.cos(ang), jnp.sin(ang)
    ones = jnp.ones((seq, ROPE_LANE0), F32)
    zeros = jnp.zeros((seq, ROPE_LANE0), F32)
    pad = jnp.zeros((seq, LANES - ROPE_LANE0 - 2 * half), F32)
    cos_t = jnp.concatenate([ones, cos, cos, pad], axis=1)
    sin_t = jnp.concatenate([zeros, -sin, sin, pad], axis=1)
    return cos_t, sin_t


def _permute_heads(t, axis, order):
    parts = [lax.slice_in_dim(t, hh * HEAD_DIM, (hh + 1) * HEAD_DIM, axis=axis) for hh in order]
    return jnp.concatenate(parts, axis=axis)


IN_SEGMENTS = (("qa", 512), ("ka", 128), ("va", 128), ("cq", 384), ("ckv", 256), ("kr", 32),
               ("qc", 512), ("kc", 512), ("vc", 512), ("qd", 512), ("kd", 512), ("vd", 512))
IN_WIDTH = sum(size for _, size in IN_SEGMENTS)


def _relayout_w_in_kernel(w_ref, main_ref, d_ref):
    w = w_ref[0]
    rows = w.shape[0]
    seg = {}
    start = 0
    for name, size in IN_SEGMENTS:
        seg[name] = w[:, start:start + size]
        start += size
    qa = jnp.concatenate([seg["qa"][:, hh * HEAD_DIM:(hh + 1) * HEAD_DIM] for hh in A_HEAD_ORDER], axis=1)
    kr = jnp.concatenate([jnp.zeros((rows, ROPE_LANE0), F32), seg["kr"],
                          jnp.zeros((rows, LANES - ROPE_LANE0 - B_ROPE_DIM), F32)], axis=1)
    main = jnp.concatenate([qa, seg["qc"], seg["kc"], seg["cq"], seg["ka"], seg["vc"], seg["ckv"], seg["va"], kr],
                           axis=1)
    main_ref[0] = main.astype(BF16)
    d_ref[0] = jnp.concatenate([seg["qd"], seg["kd"], seg["vd"]], axis=1).astype(BF16)


def _relayout_w_in(w_in, *, tr):
    d = w_in.shape[0]
    return pl.pallas_call(
        _relayout_w_in_kernel,
        grid=(d, D_MODEL // tr),
        in_specs=[pl.BlockSpec((1, tr, IN_WIDTH), lambda l, i: (l, i, 0))],
        out_specs=[pl.BlockSpec((1, tr, IN_WIDTH_P), lambda l, i: (l, i, 0)),
                   pl.BlockSpec((1, tr, IN_WIDTH_D), lambda l, i: (l, i, 0))],
        out_shape=[jax.ShapeDtypeStruct((d, D_MODEL, IN_WIDTH_P), BF16),
                   jax.ShapeDtypeStruct((d, D_MODEL, IN_WIDTH_D), BF16)],
        compiler_params=_params("arbitrary", "arbitrary"),
        name="relayout_w_in",
    )(w_in)


def _relayout_mla(w_uq, w_ukv):
    d = w_uq.shape[0]
    half = B_ROPE_DIM // 2
    wq = w_uq.astype(BF16).reshape(d, B_Q_LORA, N_HEADS, B_NOPE_DIM + B_ROPE_DIM)
    nope, r1, r2 = wq[..., :B_NOPE_DIM], wq[..., B_NOPE_DIM:B_NOPE_DIM + half], wq[..., B_NOPE_DIM + half:]
    z32 = jnp.zeros(wq.shape[:3] + (LANES - B_NOPE_DIM - B_ROPE_DIM,), BF16)
    z64 = jnp.zeros(wq.shape[:3] + (B_NOPE_DIM,), BF16)
    wq_t = jnp.concatenate([nope, r1, r2, z32], axis=-1).reshape(d, B_Q_LORA, N_HEADS * LANES)
    wq_s = jnp.concatenate([z64, r2, r1, z32], axis=-1).reshape(d, B_Q_LORA, N_HEADS * LANES)
    wkv = w_ukv.astype(BF16).reshape(d, B_KV_LORA, N_HEADS, 2 * HEAD_DIM)
    zk = jnp.zeros(wkv.shape[:3] + (LANES - B_NOPE_DIM,), BF16)
    wk_t = jnp.concatenate([wkv[..., :B_NOPE_DIM], zk], axis=-1).reshape(d, B_KV_LORA, N_HEADS * LANES)
    wv = wkv[..., B_NOPE_DIM:].reshape(d, B_KV_LORA, GROUP_WIDTH)
    return wq_t, wq_s, wk_t, wv


def _rope_swap_matrix():
    half = B_ROPE_DIM // 2
    src = jnp.arange(LANES)[:, None]
    dst = jnp.arange(LANES)[None, :]
    first = (dst >= ROPE_LANE0) & (dst < ROPE_LANE0 + half) & (src == dst + half)
    second = (dst >= ROPE_LANE0 + half) & (dst < ROPE_LANE0 + 2 * half) & (src == dst - half)
    return (first | second).astype(BF16)


def _route(logits, tm):
    n = logits.shape[0]
    top_logits, top_idx = lax.top_k(logits, TOP_K)
    gates = jax.nn.softmax(top_logits, axis=-1)
    onehot = jax.nn.one_hot(top_idx, N_EXPERTS, dtype=jnp.int32)
    member = jnp.sum(onehot, axis=1)
    rank = jnp.cumsum(member, axis=0) - member
    counts = jnp.sum(member, axis=0)
    padded = ((counts + tm - 1) // tm) * tm
    ends = jnp.cumsum(padded)
    starts = ends - padded
    pos = jnp.sum(onehot * (starts[None, None, :] + rank[:, None, :]), axis=-1)
    n_rows = TOP_K * n + N_EXPERTS * tm
    n_tiles = n_rows // tm
    src = jnp.zeros((n_rows,), jnp.int32).at[pos.reshape(-1)].set(
        jnp.repeat(jnp.arange(n, dtype=jnp.int32), TOP_K))
    tile_start = jnp.arange(n_tiles, dtype=jnp.int32) * tm
    tile_valid = (tile_start < ends[-1]).astype(jnp.int32)
    tile_expert = jnp.minimum(jnp.searchsorted(ends, tile_start, side="right"), N_EXPERTS - 1).astype(jnp.int32)
    last_expert = tile_expert[jnp.maximum(ends[-1] // tm - 1, 0)]
    tile_expert = jnp.where(tile_valid > 0, tile_expert, last_expert)
    return pos, gates, src, tile_expert, tile_valid


def kernel(x, w_in, w_o, mla_q_norm, mla_kv_norm, mla_w_uq, mla_w_ukv, attn_sinks, rel_bias_table, mix_norm_g,
           ln1_g, ln1_b, ln2_g, ln2_b, ffn_w_gate, ffn_w_up, ffn_w_down, moe_router, moe_w_gate, moe_w_up,
           moe_w_down):
    batch, seq, _ = x.shape
    n = batch * seq

    w_in_p, w_in_d = _relayout_w_in(w_in, tr=256)
    w_o_p = jnp.concatenate([_permute_heads(w_o[:, :GROUP_WIDTH], 1, A_HEAD_ORDER), w_o[:, GROUP_WIDTH:]],
                            axis=1).astype(BF16)
    mix_g_p = jnp.concatenate([_permute_heads(mix_norm_g[:, :1], 2, A_HEAD_ORDER), mix_norm_g[:, 1:]], axis=1)
    wq_t, wq_s, wk_t, wv = _relayout_mla(mla_w_uq, mla_w_ukv)
    psw = _rope_swap_matrix()
    cos_t, sin_t = _rope_lane_tables(seq)
    tq = 256
    tri = (jnp.arange(tq)[:, None] > jnp.arange(tq)[None, :]).astype(BF16)
    order = list(A_HEAD_ORDER)
    def with_first_block_variant(b):
        prev = jnp.arange(2 * BLOCK) < BLOCK
        return jnp.stack([b, jnp.where(prev, NEG_INF, b)], axis=0)

    bias_a = _band_bias_masked(rel_bias_table[:, :N_HEADS], 1, A_WINDOW - 1)
    bias_a_rows = with_first_block_variant(jnp.concatenate([bias_a[hh] for hh in order], axis=0))
    sink_rows = jnp.concatenate([jnp.broadcast_to(attn_sinks[:, hh, None, None], (DEPTH, BLOCK, LANES))
                                 for hh in order], axis=1)
    biases_d = [with_first_block_variant(_band_bias_masked(rel_bias_table[:, N_HEADS:], rate, window // rate).reshape(
        N_HEADS // 2, 2 * BLOCK, 2 * BLOCK)) for window, rate in D_PATTERNS]
    router_p = jnp.pad(moe_router, ((0, 0), (0, 0), (0, LANES - N_EXPERTS))).astype(BF16)
    wd_d = ffn_w_down.astype(BF16)
    moe_tm = 512

    xf = x.reshape(n, D_MODEL)
    xb = xf
    for layer in range(DEPTH):
        h = matmul_ws(xb, w_in_p[layer], 1024, 1024, BF16)
        hd = matmul_ws(xb, w_in_d[layer], 1024, 512, F32)
        h3 = h.reshape(batch, seq, IN_WIDTH_P)
        ga = swa_attention(h3, bias_a_rows, sink_rows[layer], batch=batch, seq=seq)
        q_b, k_b, v_b = mla_up(h, mla_q_norm[layer][None], mla_kv_norm[layer][None], wq_t[layer], wq_s[layer],
                               wk_t[layer], wv[layer], psw, cos_t, sin_t, seq=seq, tm=512)
        gb = mla_attention(q_b.reshape(batch, seq, -1), k_b.reshape(batch, seq, -1),
                           v_b.reshape(batch, seq, -1), batch=batch, seq=seq, tq=tq, heads=4)
        gc = stick_breaking_attention(h3, tri, batch=batch, seq=seq, tq=tq, pairs=2)
        gd = dilated_attention(hd.reshape(batch, seq, IN_WIDTH_D), biases_d, batch=batch, seq=seq)
        i = layer // 2
        router = router_p[i] if layer % 2 == 1 else jnp.zeros((D_MODEL, LANES), BF16)
        xf, xb, logits = out_proj_ln(ga.reshape(n, -1), gb.reshape(n, -1), gc.reshape(n, -1), gd.reshape(n, -1),
                                     xf, mix_g_p[layer], w_o_p[layer], ln1_g[layer][None], ln1_b[layer][None],
                                     router, tm=256)
        if layer % 2 == 0:
            hmid = ffn_gate_up(xb, ffn_w_gate[i], ffn_w_up[i], tm=512, tf=512)
            xf, xb = ffn_down_ln(hmid, wd_d[i], xf, ln2_g[layer][None], ln2_b[layer][None], tm=512, tk=512)
        else:
            pos, gates, src, tile_expert, tile_valid = _route(logits[:, :N_EXPERTS], moe_tm)
            xs = jnp.take(xf, src, axis=0, mode="clip").astype(BF16)
            hs = moe_gate_up(tile_expert, tile_valid, xs, moe_w_gate[i], moe_w_up[i], tm=moe_tm, tf=512)
            ys = moe_down(tile_expert, tile_valid, hs, moe_w_down[i], tm=moe_tm, tn=512)
            y0 = jnp.take(ys, pos[:, 0], axis=0, mode="clip")
            y1 = jnp.take(ys, pos[:, 1], axis=0, mode="clip")
            xf, xb = combine_ln(xf, y0, y1, gates, ln2_g[layer][None], ln2_b[layer][None], tm=512)
    return xf.reshape(batch, seq, D_MODEL)
```

```python
import functools
import math

import jax
import jax.numpy as jnp
from jax import lax
from jax.experimental import pallas as pl
from jax.experimental.pallas import tpu as pltpu

D_MODEL = 2048
DEPTH = 4
HEAD_DIM = 64
N_HEADS = 8
GROUP_WIDTH = N_HEADS * HEAD_DIM
BLOCK = 128
A_KV_HEADS = 2
A_WINDOW = 128
B_NOPE_DIM = 64
B_ROPE_DIM = 32
B_Q_LORA = 384
B_KV_LORA = 256
ROPE_THETA = 10000.0
D_PATTERNS = ((128, 1), (512, 4), (2048, 16))
NUM_BUCKETS = 32
T5_MAX_DISTANCE = 2048
D_FF = 5632
N_EXPERTS = 8
TOP_K = 2
NEG_INF = -1e30
LN_EPS = 1e-5
RMS_EPS = 1e-6
ALPHA = (2 * DEPTH) ** 0.25

LANES = 128
VMEM_LIMIT = 56 * 1024 * 1024

COL_QA, COL_QC, COL_KC, COL_CQ, COL_KA = 0, 512, 1024, 1536, 1920
COL_VC, COL_CKV, COL_VA, COL_KR = 2048, 2560, 2816, 2944
IN_WIDTH_P = 3072
IN_WIDTH_D = 3 * GROUP_WIDTH
A_HEAD_ORDER = (0, 4, 1, 5, 2, 6, 3, 7)
ROPE_LANE0 = 64
EXP_UNDERFLOW = 105.0
QK_SCALE = HEAD_DIM ** -0.5

BF16 = jnp.bfloat16
F32 = jnp.float32


def _params(*sem):
    return pltpu.CompilerParams(dimension_semantics=sem, vmem_limit_bytes=VMEM_LIMIT)


def _dot(a, b):
    return jnp.dot(a, b, preferred_element_type=F32)


def _dot_nt(a, b):
    return lax.dot_general(a, b, (((1,), (1,)), ((), ())), preferred_element_type=F32)


def _lo_lanes(rows):
    return lax.broadcasted_iota(jnp.int32, (rows, LANES), 1) < HEAD_DIM


def _split_pair(t, lo):
    zero = jnp.zeros_like(t)
    return jnp.concatenate([jnp.where(lo, t, zero), jnp.where(lo, zero, t)], axis=0)


def _matmul_kernel(a_ref, w_ref, o_ref):
    o_ref[...] = _dot(a_ref[...].astype(BF16), w_ref[0]).astype(o_ref.dtype)


def matmul_ws(a, w, layer, tm, tn, out_dtype):
    m, k = a.shape
    n = w.shape[2]
    return pl.pallas_call(
        _matmul_kernel,
        grid=(n // tn, m // tm),
        in_specs=[pl.BlockSpec((tm, k), lambda j, i: (i, 0)),
                  pl.BlockSpec((1, k, tn), lambda j, i: (layer, 0, j))],
        out_specs=pl.BlockSpec((tm, tn), lambda j, i: (i, j)),
        out_shape=jax.ShapeDtypeStruct((m, n), out_dtype),
        compiler_params=_params("arbitrary", "arbitrary"),
        name="matmul_ws",
    )(a, w)


def _swa_kernel(q_ref, kp_ref, kc_ref, vp_ref, vc_ref, bias_ref, sink_ref, o_ref):
    lo = _lo_lanes(BLOCK)
    kb = jnp.concatenate([kp_ref[0], kc_ref[0]], axis=0)
    vb = jnp.concatenate([vp_ref[0], vc_ref[0]], axis=0)
    which = jnp.where(pl.program_id(1) == 0, 1, 0)
    for c in range(GROUP_WIDTH // LANES):
        rows = slice(2 * c * BLOCK, 2 * (c + 1) * BLOCK)
        qs = _split_pair(q_ref[0, :, c * LANES:(c + 1) * LANES], lo) * QK_SCALE
        s = _dot_nt(qs, kb) + bias_ref[which, rows, :]
        sink = sink_ref[rows, :]
        m = jnp.maximum(jnp.max(s, axis=1, keepdims=True), sink)
        p = jnp.exp(s - jnp.concatenate([m, m], axis=1))
        den = jnp.sum(p, axis=1, keepdims=True) + jnp.exp(sink - m)
        o = _dot(p.astype(BF16), vb) / den
        o_ref[0, :, c * LANES:(c + 1) * LANES] = jnp.where(lo, o[:BLOCK], o[BLOCK:]).astype(o_ref.dtype)


def swa_attention(h3, bias_rows, sink_rows, *, batch, seq):
    nb = seq // BLOCK
    kb, vb = COL_KA // LANES, COL_VA // LANES
    return pl.pallas_call(
        _swa_kernel,
        grid=(batch, nb),
        in_specs=[pl.BlockSpec((1, BLOCK, GROUP_WIDTH), lambda b, n: (b, n, COL_QA // GROUP_WIDTH)),
                  pl.BlockSpec((1, BLOCK, LANES), lambda b, n: (b, jnp.maximum(n - 1, 0), kb)),
                  pl.BlockSpec((1, BLOCK, LANES), lambda b, n: (b, n, kb)),
                  pl.BlockSpec((1, BLOCK, LANES), lambda b, n: (b, jnp.maximum(n - 1, 0), vb)),
                  pl.BlockSpec((1, BLOCK, LANES), lambda b, n: (b, n, vb)),
                  pl.BlockSpec((2, N_HEADS * BLOCK, 2 * BLOCK), lambda b, n: (0, 0, 0)),
                  pl.BlockSpec((N_HEADS * BLOCK, LANES), lambda b, n: (0, 0))],
        out_specs=pl.BlockSpec((1, BLOCK, GROUP_WIDTH), lambda b, n: (b, n, 0)),
        out_shape=jax.ShapeDtypeStruct((batch, seq, GROUP_WIDTH), BF16),
        compiler_params=_params("arbitrary", "arbitrary"),
        name="swa_attn",
    )(h3, h3, h3, h3, h3, bias_rows, sink_rows)


def _dilated_kernel(q_ref, k_ref, v_ref, b0_ref, b1_ref, b2_ref, o_ref, osc, lsc):
    seq = q_ref.shape[1]
    lo = _lo_lanes(BLOCK)
    for p,((_, rate), b_ref) in enumerate(zip(D_PATTERNS, (b0_ref, b1_ref, b2_ref))):
        nb = seq // (rate * BLOCK)

        def unit(u, carry, p=p, rate=rate, b_ref=b_ref, nb=nb):
            n = u // rate
            res = u - n * rate
            rows = pl.ds(n * (BLOCK * rate) + res, BLOCK, stride=rate)
            qs = _split_pair((q_ref[0, rows, :] * QK_SCALE).astype(BF16), lo)
            kc = k_ref[0, rows, :].astype(BF16)
            vc = v_ref[0, rows, :].astype(BF16)
            if nb > 1:
                prow = pl.ds(jnp.maximum(n - 1, 0) * (BLOCK * rate) + res, BLOCK, stride=rate)
                kb = jnp.concatenate([k_ref[0, prow, :].astype(BF16), kc], axis=0)
                vb = jnp.concatenate([v_ref[0, prow, :].astype(BF16), vc], axis=0)
                s = _dot_nt(qs, kb) + b_ref[jnp.where(n == 0, 1, 0), 0]
            else:
                vb = vc
                s = _dot_nt(qs, kc) + b_ref[0, 0, :, BLOCK:2 * BLOCK]
            m = jnp.max(s, axis=1, keepdims=True)
            e = jnp.exp(s - m)
            den = jnp.sum(e, axis=1, keepdims=True)
            o = _dot(e.astype(BF16), vb) / den
            lse = jnp.broadcast_to(m + jnp.log(den), (2 * BLOCK, LANES))
            osc[p, rows, :] = jnp.where(lo, o[:BLOCK], o[BLOCK:])
            lsc[p, rows, :] = jnp.where(lo, lse[:BLOCK], lse[BLOCK:])
            return carry

        lax.fori_loop(0, nb * rate, unit, 0, unroll=8)

    chunk = 2 * BLOCK

    def merge(t, carry):
        r = pl.ds(pl.multiple_of(t * chunk, chunk), chunk)
        l0, l1, l2 = lsc[0, r, :], lsc[1, r, :], lsc[2, r, :]
        m = jnp.maximum(jnp.maximum(l0, l1), l2)
        e0, e1, e2 = jnp.exp(l0 - m), jnp.exp(l1 - m), jnp.exp(l2 - m)
        o = (e0 * osc[0, r, :] + e1 * osc[1, r, :] + e2 * osc[2, r, :]) / (e0 + e1 + e2)
        o_ref[0, r, :] = o.astype(o_ref.dtype)
        return carry

    lax.fori_loop(0, seq // chunk, merge, 0)


def dilated_attention(hd3, biases, *, batch, seq):
    pairs = N_HEADS // 2
    bspec = pl.BlockSpec((2, 1, 2 * BLOCK, 2 * BLOCK), lambda b, p: (0, p, 0, 0))
    return pl.pallas_call(
        _dilated_kernel,
        grid=(batch, pairs),
        in_specs=[pl.BlockSpec((1, seq, LANES), lambda b, p: (b, 0, p)),
                  pl.BlockSpec((1, seq, LANES), lambda b, p: (b, 0, pairs + p)),
                  pl.BlockSpec((1, seq, LANES), lambda b, p: (b, 0, 2 * pairs + p)),
                  bspec, bspec, bspec],
        out_specs=pl.BlockSpec((1, seq, LANES), lambda b, p: (b, 0, p)),
        out_shape=jax.ShapeDtypeStruct((batch, seq, GROUP_WIDTH), BF16),
        scratch_shapes=[pltpu.VMEM((len(D_PATTERNS), seq, LANES), F32),
                        pltpu.VMEM((len(D_PATTERNS), seq, LANES), F32)],
        compiler_params=_params("arbitrary", "arbitrary"),
        name="dilated_attn",
    )(hd3, hd3, hd3, *biases)


def _mla_up_kernel(cq_ref, ckv_ref, kr_ref, qn_ref, kvn_ref, wq_ref, wqs_ref, wk_ref, wv_ref, psw_ref,
                   cos_ref, sin_ref, q_ref, k_ref, v_ref, *, scale):
    def rms(x_ref, g_ref):
        x = x_ref[...].astype(F32)
        return (x * lax.rsqrt(jnp.mean(x * x, axis=-1, keepdims=True) + RMS_EPS) * g_ref[...]).astype(BF16)

    xq = rms(cq_ref, qn_ref)
    xkv = rms(ckv_ref, kvn_ref)
    cos = cos_ref[...]
    sin = sin_ref[...]
    t = _dot(xq, wq_ref[...])
    ts = _dot(xq, wqs_ref[...])
    kn = _dot(xkv, wk_ref[...])
    kr = kr_ref[...]
    rk = kr.astype(F32) * cos + _dot(kr, psw_ref[...]) * sin
    for h in range(N_HEADS):
        sl = slice(h * LANES, (h + 1) * LANES)
        q_ref[:, sl] = ((t[:, sl] * cos + ts[:, sl] * sin) * scale).astype(BF16)
        k_ref[:, sl] = (kn[:, sl] + rk).astype(BF16)
    v_ref[...] = _dot(xkv, wv_ref[...]).astype(BF16)


def mla_up(h, q_norm, kv_norm, wq, wqs, wk, wv, psw, cos_t, sin_t, *, seq, tm):
    n = h.shape[0]
    w = N_HEADS * LANES
    const = lambda shape: pl.BlockSpec(shape, lambda i: (0, 0))
    spt = seq // tm
    return pl.pallas_call(
        functools.partial(_mla_up_kernel, scale=(B_NOPE_DIM + B_ROPE_DIM) ** -0.5),
        grid=(n // tm,),
        in_specs=[pl.BlockSpec((tm, B_Q_LORA), lambda i: (i, COL_CQ // B_Q_LORA)),
                  pl.BlockSpec((tm, B_KV_LORA), lambda i: (i, COL_CKV // B_KV_LORA)),
                  pl.BlockSpec((tm, LANES), lambda i: (i, COL_KR // LANES)),
                  const((1, B_Q_LORA)), const((1, B_KV_LORA)),
                  const((B_Q_LORA, w)), const((B_Q_LORA, w)), const((B_KV_LORA, w)),
                  const((B_KV_LORA, GROUP_WIDTH)), const((LANES, LANES)),
                  pl.BlockSpec((tm, LANES), lambda i: (i % spt, 0)),
                  pl.BlockSpec((tm, LANES), lambda i: (i % spt, 0))],
        out_specs=[pl.BlockSpec((tm, w), lambda i: (i, 0)),
                   pl.BlockSpec((tm, w), lambda i: (i, 0)),
                   pl.BlockSpec((tm, GROUP_WIDTH), lambda i: (i, 0))],
        out_shape=[jax.ShapeDtypeStruct((n, w), BF16), jax.ShapeDtypeStruct((n, w), BF16),
                   jax.ShapeDtypeStruct((n, GROUP_WIDTH), BF16)],
        compiler_params=_params("arbitrary"),
        name="mla_up",
    )(h, h, h, q_norm, kv_norm, wq, wqs, wk, wv, psw, cos_t, sin_t)


def _mla_attn_kernel(q_ref, k_ref, v_ref, o_ref, *, tq, heads):
    i = pl.program_id(2)
    lo = _lo_lanes(tq)
    row = lax.broadcasted_iota(jnp.int32, (tq, tq), 0)
    col = lax.broadcasted_iota(jnp.int32, (tq, tq), 1)
    causal = col <= row

    def step(j, carry, masked):
        start = pl.multiple_of(j * tq, tq)
        out = []
        for hh in range(heads):
            m, l, acc = carry[3 * hh:3 * hh + 3]
            q = q_ref[0, :, hh * LANES:(hh + 1) * LANES]
            s = _dot_nt(q, k_ref[0, pl.ds(start, tq), hh * LANES:(hh + 1) * LANES])
            if masked:
                s = jnp.where(causal, s, NEG_INF)
            m_new = jnp.maximum(m, jnp.max(s, axis=1, keepdims=True))
            p = jnp.exp(s - m_new)
            a = jnp.exp(m - m_new)
            pv = _dot(p.astype(BF16), v_ref[0, pl.ds(start, tq), (hh // 2) * LANES:(hh // 2 + 1) * LANES])
            out += [m_new, a * l + jnp.sum(p, axis=1, keepdims=True), a * acc + pv]
        return tuple(out)

    init = (jnp.full((tq, 1), NEG_INF, F32), jnp.zeros((tq, 1), F32), jnp.zeros((tq, LANES), F32)) * heads
    carry = lax.fori_loop(0, i, functools.partial(step, masked=False), init)
    carry = step(i, carry, True)
    for pp in range(heads // 2):
        (_, l0, a0), (_, l1, a1) = carry[6 * pp:6 * pp + 3], carry[6 * pp + 3:6 * pp + 6]
        o_ref[0, :, pp * LANES:(pp + 1) * LANES] = jnp.where(lo, a0 / l0, a1 / l1).astype(o_ref.dtype)


def mla_attention(q, k, v, *, batch, seq, tq, heads):
    pairs = heads // 2
    return pl.pallas_call(
        functools.partial(_mla_attn_kernel, tq=tq, heads=heads),
        grid=(batch, N_HEADS // heads, seq // tq),
        in_specs=[pl.BlockSpec((1, tq, heads * LANES), lambda b, g, i: (b, i, g)),
                  pl.BlockSpec((1, seq, heads * LANES), lambda b, g, i: (b, 0, g)),
                  pl.BlockSpec((1, seq, pairs * LANES), lambda b, g, i: (b, 0, g))],
        out_specs=pl.BlockSpec((1, tq, pairs * LANES), lambda b, g, i: (b, i, g)),
        out_shape=jax.ShapeDtypeStruct((batch, seq, GROUP_WIDTH), BF16),
        compiler_params=_params("arbitrary", "arbitrary", "arbitrary"),
        name="mla_attn",
    )(q, k, v)


def _stick_kernel(q_ref, k_ref, v_ref, tri_ref, o_ref, *, tq, pairs):
    i = pl.program_id(2)
    lo = _lo_lanes(tq)
    row = lax.broadcasted_iota(jnp.int32, (2 * tq, tq), 0)
    col = lax.broadcasted_iota(jnp.int32, (2 * tq, tq), 1)
    strict = col < jnp.where(row >= tq, row - tq, row)
    tri = tri_ref[...]
    qs = [_split_pair(q_ref[0, :, pp * LANES:(pp + 1) * LANES], lo) * QK_SCALE for pp in range(pairs)]

    def chunk(pp, j, run, masked):
        start = pl.multiple_of(j * tq, tq)
        z = _dot_nt(qs[pp], k_ref[0, pl.ds(start, tq), pp * LANES:(pp + 1) * LANES])
        soft = jnp.log(1.0 + jnp.exp(-jnp.abs(z)))
        keep = -(jnp.maximum(z, 0.0) + soft)
        hit = jnp.minimum(z, 0.0) - soft
        if masked:
            keep = jnp.where(strict, keep, 0.0)
        hi = keep.astype(BF16)
        rest = (keep - hi.astype(F32)).astype(BF16)
        after = _dot(hi, tri) + _dot(rest, tri)
        a = jnp.exp(hit + after)
        if masked:
            a = jnp.where(strict, a, 0.0)
        contrib = jnp.exp(run) * _dot(a.astype(BF16), v_ref[0, pl.ds(start, tq), pp * LANES:(pp + 1) * LANES])
        return contrib, run + jnp.sum(keep, axis=1, keepdims=True)

    state = []
    for pp in range(pairs):
        acc, run = chunk(pp, i, jnp.zeros((2 * tq, 1), F32), True)
        state += [run, acc]

    def cond(carry):
        t = carry[0]
        live = carry[1]
        for pp in range(1, pairs):
            live = jnp.maximum(live, carry[1 + 2 * pp])
        return jnp.logical_and(t < i, jnp.max(live) > -EXP_UNDERFLOW)

    def body(carry):
        t = carry[0]
        out = [t + 1]
        for pp in range(pairs):
            contrib, run = chunk(pp, i - 1 - t, carry[1 + 2 * pp], False)
            out += [run, carry[2 + 2 * pp] + contrib]
        return tuple(out)

    final = lax.while_loop(cond, body, (jnp.int32(0), *state))
    for pp in range(pairs):
        acc = final[2 + 2 * pp]
        o_ref[0, :, pp * LANES:(pp + 1) * LANES] = jnp.where(lo, acc[:tq], acc[tq:]).astype(o_ref.dtype)


def stick_breaking_attention(h3, tri, *, batch, seq, tq, pairs):
    w = pairs * LANES
    return pl.pallas_call(
        functools.partial(_stick_kernel, tq=tq, pairs=pairs),
        grid=(batch, GROUP_WIDTH // w, seq // tq),
        in_specs=[pl.BlockSpec((1, tq, w), lambda b, g, i: (b, i, COL_QC // w + g)),
                  pl.BlockSpec((1, seq, w), lambda b, g, i: (b, 0, COL_KC // w + g)),
                  pl.BlockSpec((1, seq, w), lambda b, g, i: (b, 0, COL_VC // w + g)),
                  pl.BlockSpec((tq, tq), lambda b, g, i: (0, 0))],
        out_specs=pl.BlockSpec((1, tq, w), lambda b, g, i: (b, i, g)),
        out_shape=jax.ShapeDtypeStruct((batch, seq, GROUP_WIDTH), BF16),
        compiler_params=_params("arbitrary", "arbitrary", "arbitrary"),
        name="stick_breaking",
    )(h3, h3, h3, tri)


def _layer_norm(r, g, b):
    mu = jnp.mean(r, axis=-1, keepdims=True)
    d = r - mu
    var = jnp.mean(d * d, axis=-1, keepdims=True)
    return d * lax.rsqrt(var + LN_EPS) * g + b


def _out_proj_kernel(ga_ref, gb_ref, gc_ref, gd_ref, x_ref, mixg_ref, wo_ref, g_ref, b_ref, rt_ref,
                     xo_ref, xb_ref, lg_ref):
    mixed = None
    for gi, grp in enumerate((ga_ref, gb_ref, gc_ref, gd_ref)):
        xg = grp[...].astype(F32)
        y = xg * lax.rsqrt(jnp.mean(xg * xg, axis=-1, keepdims=True) + RMS_EPS) * mixg_ref[gi:gi + 1, :]
        part = _dot(y.astype(BF16), wo_ref[0, gi * GROUP_WIDTH:(gi + 1) * GROUP_WIDTH, :])
        mixed = part if mixed is None else mixed + part
    x1 = _layer_norm(ALPHA * x_ref[...].astype(F32) + mixed, g_ref[...], b_ref[...])
    xo_ref[...] = x1
    xb = x1.astype(BF16)
    xb_ref[...] = xb
    lg_ref[...] = _dot(xb, rt_ref[...])


def out_proj_ln(ga, gb, gc, gd, x, mix_g, wo, layer, ln_g, ln_b, router, *, tm):
    n = x.shape[0]
    row = lambda w: pl.BlockSpec((tm, w), lambda i: (i, 0))
    const = lambda shape: pl.BlockSpec(shape, lambda i: (0, 0))
    return pl.pallas_call(
        _out_proj_kernel,
        grid=(n // tm,),
        in_specs=[row(GROUP_WIDTH)] * 4 + [row(D_MODEL), const((4, GROUP_WIDTH)),
                                          pl.BlockSpec((1, D_MODEL, D_MODEL), lambda i: (layer, 0, 0)),
                                          const((1, D_MODEL)), const((1, D_MODEL)), const((D_MODEL, LANES))],
        out_specs=[row(D_MODEL), row(D_MODEL), row(LANES)],
        out_shape=[jax.ShapeDtypeStruct((n, D_MODEL), F32), jax.ShapeDtypeStruct((n, D_MODEL), BF16),
                   jax.ShapeDtypeStruct((n, LANES), F32)],
        compiler_params=_params("arbitrary"),
        name="out_proj_ln",
    )(ga, gb, gc, gd, x, mix_g, wo, ln_g, ln_b, router)


def _silu_mul(g, u):
    return g * (1.0 / (1.0 + jnp.exp(-g))) * u


def _gate_up_kernel(x_ref, wg_ref, wu_ref, o_ref, wgb_ref, wub_ref):
    @pl.when(pl.program_id(1) == 0)
    def _():
        wgb_ref[...] = wg_ref[0].astype(BF16)
        wub_ref[...] = wu_ref[0].astype(BF16)

    x = x_ref[...]
    o_ref[...] = _silu_mul(_dot(x, wgb_ref[...]), _dot(x, wub_ref[...])).astype(o_ref.dtype)


def ffn_gate_up(xb, wg, wu, li, *, tm, tf):
    n = xb.shape[0]
    f = wg.shape[2]
    return pl.pallas_call(
        _gate_up_kernel,
        grid=(f // tf, n // tm),
        in_specs=[pl.BlockSpec((tm, D_MODEL), lambda j, i: (i, 0)),
                  pl.BlockSpec((1, D_MODEL, tf), lambda j, i: (li, 0, j)),
                  pl.BlockSpec((1, D_MODEL, tf), lambda j, i: (li, 0, j))],
        out_specs=pl.BlockSpec((tm, tf), lambda j, i: (i, j)),
        out_shape=jax.ShapeDtypeStruct((n, f), BF16),
        scratch_shapes=[pltpu.VMEM((D_MODEL, tf), BF16), pltpu.VMEM((D_MODEL, tf), BF16)],
        compiler_params=_params("arbitrary", "arbitrary"),
        name="ffn_gate_up",
    )(xb, wg, wu)


def _down_ln_kernel(h_ref, wd_ref, x_ref, g_ref, b_ref, xo_ref, xb_ref, acc_ref):
    kk = pl.program_id(1)

    @pl.when(kk == 0)
    def _():
        acc_ref[...] = jnp.zeros_like(acc_ref)

    acc_ref[...] += _dot(h_ref[...], wd_ref[0])

    @pl.when(kk == pl.num_programs(1) - 1)
    def _():
        x2 = _layer_norm(ALPHA * x_ref[...] + acc_ref[...], g_ref[...], b_ref[...])
        xo_ref[...] = x2
        xb_ref[...] = x2.astype(BF16)


def ffn_down_ln(hmid, wd, li, x, ln_g, ln_b, *, tm, tk):
    n, f = hmid.shape
    return pl.pallas_call(
        _down_ln_kernel,
        grid=(n // tm, f // tk),
        in_specs=[pl.BlockSpec((tm, tk), lambda i, k: (i, k)),
                  pl.BlockSpec((1, tk, D_MODEL), lambda i, k: (li, k, 0)),
                  pl.BlockSpec((tm, D_MODEL), lambda i, k: (i, 0)),
                  pl.BlockSpec((1, D_MODEL), lambda i, k: (0, 0)),
                  pl.BlockSpec((1, D_MODEL), lambda i, k: (0, 0))],
        out_specs=[pl.BlockSpec((tm, D_MODEL), lambda i, k: (i, 0)),
                   pl.BlockSpec((tm, D_MODEL), lambda i, k: (i, 0))],
        out_shape=[jax.ShapeDtypeStruct((n, D_MODEL), F32), jax.ShapeDtypeStruct((n, D_MODEL), BF16)],
        scratch_shapes=[pltpu.VMEM((tm, D_MODEL), F32)],
        compiler_params=_params("arbitrary", "arbitrary"),
        name="ffn_down_ln",
    )(hmid, wd, x, ln_g, ln_b)


def _new_expert(te_ref, t):
    return jnp.logical_or(t == 0, te_ref[t] != te_ref[jnp.maximum(t - 1, 0)])


def _moe_gate_up_kernel(te_ref, tv_ref, x_ref, wg_ref, wu_ref, o_ref, wgb_ref, wub_ref):
    t = pl.program_id(1)

    @pl.when(_new_expert(te_ref, t))
    def _():
        wgb_ref[...] = wg_ref[0, 0].astype(BF16)
        wub_ref[...] = wu_ref[0, 0].astype(BF16)

    @pl.when(tv_ref[t] > 0)
    def _():
        x = x_ref[...]
        o_ref[...] = _silu_mul(_dot(x, wgb_ref[...]), _dot(x, wub_ref[...])).astype(o_ref.dtype)

    @pl.when(tv_ref[t] == 0)
    def _():
        o_ref[...] = jnp.zeros_like(o_ref)


def moe_gate_up(tile_expert, tile_valid, xs, wg, wu, li, *, tm, tf):
    p = xs.shape[0]
    f = wg.shape[3]
    grid_spec = pltpu.PrefetchScalarGridSpec(
        num_scalar_prefetch=2,
        grid=(f // tf, p // tm),
        in_specs=[pl.BlockSpec((tm, D_MODEL), lambda j, t, te, tv: (t, 0)),
                  pl.BlockSpec((1, 1, D_MODEL, tf), lambda j, t, te, tv: (li, te[t], 0, j)),
                  pl.BlockSpec((1, 1, D_MODEL, tf), lambda j, t, te, tv: (li, te[t], 0, j))],
        out_specs=pl.BlockSpec((tm, tf), lambda j, t, te, tv: (t, j)),
        scratch_shapes=[pltpu.VMEM((D_MODEL, tf), BF16), pltpu.VMEM((D_MODEL, tf), BF16)],
    )
    return pl.pallas_call(
        _moe_gate_up_kernel,
        grid_spec=grid_spec,
        out_shape=jax.ShapeDtypeStruct((p, f), BF16),
        compiler_params=_params("arbitrary", "arbitrary"),
        name="moe_gate_up",
    )(tile_expert, tile_valid, xs, wg, wu)


def _moe_down_kernel(te_ref, tv_ref, h_ref, wd_ref, o_ref, wdb_ref):
    t = pl.program_id(1)

    @pl.when(_new_expert(te_ref, t))
    def _():
        wdb_ref[...] = wd_ref[0, 0].astype(BF16)

    @pl.when(tv_ref[t] > 0)
    def _():
        o_ref[...] = _dot(h_ref[...], wdb_ref[...])

    @pl.when(tv_ref[t] == 0)
    def _():
        o_ref[...] = jnp.zeros_like(o_ref)


def moe_down(tile_expert, tile_valid, hs, wd, li, *, tm, tn):
    p, f = hs.shape
    grid_spec = pltpu.PrefetchScalarGridSpec(
        num_scalar_prefetch=2,
        grid=(D_MODEL // tn, p // tm),
        in_specs=[pl.BlockSpec((tm, f), lambda j, t, te, tv: (t, 0)),
                  pl.BlockSpec((1, 1, f, tn), lambda j, t, te, tv: (li, te[t], 0, j))],
        out_specs=pl.BlockSpec((tm, tn), lambda j, t, te, tv: (t, j)),
        scratch_shapes=[pltpu.VMEM((f, tn), BF16)],
    )
    return pl.pallas_call(
        _moe_down_kernel,
        grid_spec=grid_spec,
        out_shape=jax.ShapeDtypeStruct((p, D_MODEL), F32),
        compiler_params=_params("arbitrary", "arbitrary"),
        name="moe_down",
    )(tile_expert, tile_valid, hs, wd)


def _combine_ln_kernel(x_ref, y0_ref, y1_ref, gate_ref, g_ref, b_ref, xo_ref, xb_ref):
    f = gate_ref[:, 0:1] * y0_ref[...] + gate_ref[:, 1:2] * y1_ref[...]
    x2 = _layer_norm(ALPHA * x_ref[...] + f, g_ref[...], b_ref[...])
    xo_ref[...] = x2
    xb_ref[...] = x2.astype(BF16)


def combine_ln(x, y0, y1, gates, ln_g, ln_b, *, tm):
    n = x.shape[0]
    row = pl.BlockSpec((tm, D_MODEL), lambda i: (i, 0))
    const = pl.BlockSpec((1, D_MODEL), lambda i: (0, 0))
    return pl.pallas_call(
        _combine_ln_kernel,
        grid=(n // tm,),
        in_specs=[row, row, row, pl.BlockSpec((tm, TOP_K), lambda i: (i, 0)), const, const],
        out_specs=[row, row],
        out_shape=[jax.ShapeDtypeStruct((n, D_MODEL), F32), jax.ShapeDtypeStruct((n, D_MODEL), BF16)],
        compiler_params=_params("arbitrary"),
        name="combine_ln",
    )(x, y0, y1, gates, ln_g, ln_b)


def _t5_bucket(dist):
    max_exact = NUM_BUCKETS // 2
    d = jnp.maximum(dist, 1).astype(F32)
    large = max_exact + (jnp.log(d / max_exact) / math.log(T5_MAX_DISTANCE / max_exact)
                         * (NUM_BUCKETS - max_exact)).astype(jnp.int32)
    large = jnp.minimum(large, NUM_BUCKETS - 1)
    return jnp.where(dist < max_exact, dist, large)


def _band_bias_masked(table, stride, max_dist):
    dist = jnp.arange(BLOCK)[:, None] + BLOCK - jnp.arange(2 * BLOCK)[None, :]
    bias = jnp.transpose(table[_t5_bucket(jnp.maximum(dist, 0) * stride)], (2, 0, 1)).astype(F32)
    valid = (dist >= 0) & (dist <= max_dist)
    return jnp.where(valid[None], bias, NEG_INF)


def _rope_lane_tables(seq):
    half = B_ROPE_DIM // 2
    inv = ROPE_THETA ** (-jnp.arange(0, B_ROPE_DIM, 2, dtype=F32) / B_ROPE_DIM)
    ang = jnp.arange(seq, dtype=F32)[:, None] * inv[None, :]
    cos, sin = jnp.cos(ang), jnp.sin(ang)
    ones = jnp.ones((seq, ROPE_LANE0), F32)
    zeros = jnp.zeros((seq, ROPE_LANE0), F32)
    pad = jnp.zeros((seq, LANES - ROPE_LANE0 - 2 * half), F32)
    cos_t = jnp.concatenate([ones, cos, cos, pad], axis=1)
    sin_t = jnp.concatenate([zeros, -sin, sin, pad], axis=1)
    return cos_t, sin_t


def _permute_heads(t, axis, order):
    parts = [lax.slice_in_dim(t, hh * HEAD_DIM, (hh + 1) * HEAD_DIM, axis=axis) for hh in order]
    return jnp.concatenate(parts, axis=axis)


IN_SEGMENTS = (("qa", 512), ("ka", 128), ("va", 128), ("cq", 384), ("ckv", 256), ("kr", 32),
               ("qc", 512), ("kc", 512), ("vc", 512), ("qd", 512), ("kd", 512), ("vd", 512))
IN_WIDTH = sum(size for _, size in IN_SEGMENTS)


def _relayout_w_in_kernel(w_ref, main_ref, d_ref):
    w = w_ref[0]
    rows = w.shape[0]
    seg = {}
    start = 0
    for name, size in IN_SEGMENTS:
        seg[name] = w[:, start:start + size]
        start += size
    qa = jnp.concatenate([seg["qa"][:, hh * HEAD_DIM:(hh + 1) * HEAD_DIM] for hh in A_HEAD_ORDER], axis=1)
    kr = jnp.concatenate([jnp.zeros((rows, ROPE_LANE0), F32), seg["kr"],
                          jnp.zeros((rows, LANES - ROPE_LANE0 - B_ROPE_DIM), F32)], axis=1)
    main = jnp.concatenate([qa, seg["qc"], seg["kc"], seg["cq"], seg["ka"], seg["vc"], seg["ckv"], seg["va"], kr],
                           axis=1)
    main_ref[0] = main.astype(BF16)
    d_ref[0] = jnp.concatenate([seg["qd"], seg["kd"], seg["vd"]], axis=1).astype(BF16)


def _relayout_w_in(w_in, *, tr):
    d = w_in.shape[0]
    return pl.pallas_call(
        _relayout_w_in_kernel,
        grid=(d, D_MODEL // tr),
        in_specs=[pl.BlockSpec((1, tr, IN_WIDTH), lambda l, i: (l, i, 0))],
        out_specs=[pl.BlockSpec((1, tr, IN_WIDTH_P), lambda l, i: (l, i, 0)),
                   pl.BlockSpec((1, tr, IN_WIDTH_D), lambda l, i: (l, i, 0))],
        out_shape=[jax.ShapeDtypeStruct((d, D_MODEL, IN_WIDTH_P), BF16),
                   jax.ShapeDtypeStruct((d, D_MODEL, IN_WIDTH_D), BF16)],
        compiler_params=_params("arbitrary", "arbitrary"),
        name="relayout_w_in",
    )(w_in)


def _relayout_mla(w_uq, w_ukv):
    d = w_uq.shape[0]
    half = B_ROPE_DIM // 2
    wq = w_uq.astype(BF16).reshape(d, B_Q_LORA, N_HEADS, B_NOPE_DIM + B_ROPE_DIM)
    nope, r1, r2 = wq[..., :B_NOPE_DIM], wq[..., B_NOPE_DIM:B_NOPE_DIM + half], wq[..., B_NOPE_DIM + half:]
    z32 = jnp.zeros(wq.shape[:3] + (LANES - B_NOPE_DIM - B_ROPE_DIM,), BF16)
    z64 = jnp.zeros(wq.shape[:3] + (B_NOPE_DIM,), BF16)
    wq_t = jnp.concatenate([nope, r1, r2, z32], axis=-1).reshape(d, B_Q_LORA, N_HEADS * LANES)
    wq_s = jnp.concatenate([z64, r2, r1, z32], axis=-1).reshape(d, B_Q_LORA, N_HEADS * LANES)
    wkv = w_ukv.astype(BF16).reshape(d, B_KV_LORA, N_HEADS, 2 * HEAD_DIM)
    zk = jnp.zeros(wkv.shape[:3] + (LANES - B_NOPE_DIM,), BF16)
    wk_t = jnp.concatenate([wkv[..., :B_NOPE_DIM], zk], axis=-1).reshape(d, B_KV_LORA, N_HEADS * LANES)
    wv = wkv[..., B_NOPE_DIM:].reshape(d, B_KV_LORA, GROUP_WIDTH)
    return wq_t, wq_s, wk_t, wv


def _rope_swap_matrix():
    half = B_ROPE_DIM // 2
    src = jnp.arange(LANES)[:, None]
    dst = jnp.arange(LANES)[None, :]
    first = (dst >= ROPE_LANE0) & (dst < ROPE_LANE0 + half) & (src == dst + half)
    second = (dst >= ROPE_LANE0 + half) & (dst < ROPE_LANE0 + 2 * half) & (src == dst - half)
    return (first | second).astype(BF16)


def _route(logits, tm):
    n = logits.shape[0]
    top_logits, top_idx = lax.top_k(logits, TOP_K)
    gates = jax.nn.softmax(top_logits, axis=-1)
    onehot = jax.nn.one_hot(top_idx, N_EXPERTS, dtype=jnp.int32)
    member = jnp.sum(onehot, axis=1)
    rank = jnp.cumsum(member, axis=0) - member
    counts = jnp.sum(member, axis=0)
    padded = ((counts + tm - 1) // tm) * tm
    ends = jnp.cumsum(padded)
    starts = ends - padded
    pos = jnp.sum(onehot * (starts[None, None, :] + rank[:, None, :]), axis=-1)
    n_rows = TOP_K * n + N_EXPERTS * tm
    n_tiles = n_rows // tm
    src = (jnp.arange(n_rows, dtype=jnp.int32) % n).at[pos.reshape(-1)].set(
        jnp.repeat(jnp.arange(n, dtype=jnp.int32), TOP_K))
    tile_start = jnp.arange(n_tiles, dtype=jnp.int32) * tm
    tile_valid = (tile_start < ends[-1]).astype(jnp.int32)
    tile_expert = jnp.minimum(jnp.searchsorted(ends, tile_start, side="right"), N_EXPERTS - 1).astype(jnp.int32)
    last_expert = tile_expert[jnp.maximum(ends[-1] // tm - 1, 0)]
    tile_expert = jnp.where(tile_valid > 0, tile_expert, last_expert)
    return pos, gates, src, tile_expert, tile_valid


def kernel(x, w_in, w_o, mla_q_norm, mla_kv_norm, mla_w_uq, mla_w_ukv, attn_sinks, rel_bias_table, mix_norm_g,
           ln1_g, ln1_b, ln2_g, ln2_b, ffn_w_gate, ffn_w_up, ffn_w_down, moe_router, moe_w_gate, moe_w_up,
           moe_w_down):
    batch, seq, _ = x.shape
    n = batch * seq

    w_in_p, w_in_d = _relayout_w_in(w_in, tr=256)
    w_o_p = jnp.concatenate([_permute_heads(w_o[:, :GROUP_WIDTH], 1, A_HEAD_ORDER), w_o[:, GROUP_WIDTH:]],
                            axis=1).astype(BF16)
    mix_g_p = jnp.concatenate([_permute_heads(mix_norm_g[:, :1], 2, A_HEAD_ORDER), mix_norm_g[:, 1:]], axis=1)
    wq_t, wq_s, wk_t, wv = _relayout_mla(mla_w_uq, mla_w_ukv)
    psw = _rope_swap_matrix()
    cos_t, sin_t = _rope_lane_tables(seq)
    tq = 256
    tri = (jnp.arange(tq)[:, None] > jnp.arange(tq)[None, :]).astype(BF16)
    order = list(A_HEAD_ORDER)
    def with_first_block_variant(b):
        prev = jnp.arange(2 * BLOCK) < BLOCK
        return jnp.stack([b, jnp.where(prev, NEG_INF, b)], axis=0)

    bias_a = _band_bias_masked(rel_bias_table[:, :N_HEADS], 1, A_WINDOW - 1)
    bias_a_rows = with_first_block_variant(jnp.concatenate([bias_a[hh] for hh in order], axis=0))
    sink_rows = jnp.concatenate([jnp.broadcast_to(attn_sinks[:, hh, None, None], (DEPTH, BLOCK, LANES))
                                 for hh in order], axis=1)
    biases_d = [with_first_block_variant(_band_bias_masked(rel_bias_table[:, N_HEADS:], rate, window // rate).reshape(
        N_HEADS // 2, 2 * BLOCK, 2 * BLOCK)) for window, rate in D_PATTERNS]
    router_p = jnp.pad(moe_router, ((0, 0), (0, 0), (0, LANES - N_EXPERTS))).astype(BF16)
    wd_d = ffn_w_down.astype(BF16)
    moe_tm = 512

    xf = x.reshape(n, D_MODEL)
    xb = xf
    for layer in range(DEPTH):
        h = matmul_ws(xb, w_in_p, layer, 1024, 1024, BF16)
        hd = matmul_ws(xb, w_in_d, layer, 1024, 512, F32)
        h3 = h.reshape(batch, seq, IN_WIDTH_P)
        ga = swa_attention(h3, bias_a_rows, sink_rows[layer], batch=batch, seq=seq)
        q_b, k_b, v_b = mla_up(h, mla_q_norm[layer][None], mla_kv_norm[layer][None], wq_t[layer], wq_s[layer],
                               wk_t[layer], wv[layer], psw, cos_t, sin_t, seq=seq, tm=512)
        gb = mla_attention(q_b.reshape(batch, seq, -1), k_b.reshape(batch, seq, -1),
                           v_b.reshape(batch, seq, -1), batch=batch, seq=seq, tq=tq, heads=4)
        gc = stick_breaking_attention(h3, tri, batch=batch, seq=seq, tq=tq, pairs=2)
        gd = dilated_attention(hd.reshape(batch, seq, IN_WIDTH_D), biases_d, batch=batch, seq=seq)
        i = layer // 2
        router = router_p[i] if layer % 2 == 1 else jnp.zeros((D_MODEL, LANES), BF16)
        xf, xb, logits = out_proj_ln(ga.reshape(n, -1), gb.reshape(n, -1), gc.reshape(n, -1), gd.reshape(n, -1),
                                     xf, mix_g_p[layer], w_o_p, layer, ln1_g[layer][None], ln1_b[layer][None],
                                     router, tm=256)
        if layer % 2 == 0:
            hmid = ffn_gate_up(xb, ffn_w_gate, ffn_w_up, i, tm=512, tf=512)
            xf, xb = ffn_down_ln(hmid, wd_d, i, xf, ln2_g[layer][None], ln2_b[layer][None], tm=512, tk=512)
        else:
            pos, gates, src, tile_expert, tile_valid = _route(logits[:, :N_EXPERTS], moe_tm)
            xs = jnp.take(xf, src, axis=0, mode="clip").astype(BF16)
            hs = moe_gate_up(tile_expert, tile_valid, xs, moe_w_gate, moe_w_up, i, tm=moe_tm, tf=512)
            ys = moe_down(tile_expert, tile_valid, hs, moe_w_down, i, tm=moe_tm, tn=512)
            y0 = jnp.take(ys, pos[:, 0], axis=0, mode="clip")
            y1 = jnp.take(ys, pos[:, 1], axis=0, mode="clip")
            xf, xb = combine_ln(xf, y0, y1, gates, ln2_g[layer][None], ln2_b[layer][None], tm=512)
    return xf.reshape(batch, seq, D_MODEL)
```

```python
import functools
import math

import jax
import jax.numpy as jnp
from jax import lax
from jax.experimental import pallas as pl
from jax.experimental.pallas import tpu as pltpu

D_MODEL = 2048
DEPTH = 4
HEAD_DIM = 64
N_HEADS = 8
GROUP_WIDTH = N_HEADS * HEAD_DIM
BLOCK = 128
A_KV_HEADS = 2
A_WINDOW = 128
B_NOPE_DIM = 64
B_ROPE_DIM = 32
B_Q_LORA = 384
B_KV_LORA = 256
ROPE_THETA = 10000.0
D_PATTERNS = ((128, 1), (512, 4), (2048, 16))
NUM_BUCKETS = 32
T5_MAX_DISTANCE = 2048
D_FF = 5632
N_EXPERTS = 8
TOP_K = 2
NEG_INF = -1e30
LN_EPS = 1e-5
RMS_EPS = 1e-6
ALPHA = (2 * DEPTH) ** 0.25

LANES = 128
VMEM_LIMIT = 56 * 1024 * 1024

COL_QA, COL_QC, COL_KC, COL_CQ, COL_KA = 0, 512, 1024, 1536, 1920
COL_VC, COL_CKV, COL_VA, COL_KR = 2048, 2560, 2816, 2944
IN_WIDTH_P = 3072
IN_WIDTH_D = 3 * GROUP_WIDTH
A_HEAD_ORDER = (0, 4, 1, 5, 2, 6, 3, 7)
ROPE_LANE0 = 64
EXP_UNDERFLOW = 105.0
QK_SCALE = HEAD_DIM ** -0.5

BF16 = jnp.bfloat16
F32 = jnp.float32


def _params(*sem):
    return pltpu.CompilerParams(dimension_semantics=sem, vmem_limit_bytes=VMEM_LIMIT)


def _dot(a, b):
    return jnp.dot(a, b, preferred_element_type=F32)


def _dot_nt(a, b):
    return lax.dot_general(a, b, (((1,), (1,)), ((), ())), preferred_element_type=F32)


def _lo_lanes(rows):
    return lax.broadcasted_iota(jnp.int32, (rows, LANES), 1) < HEAD_DIM


def _split_pair(t, lo):
    zero = jnp.zeros_like(t)
    return jnp.concatenate([jnp.where(lo, t, zero), jnp.where(lo, zero, t)], axis=0)


def _matmul_kernel(a_ref, w_ref, o_ref):
    o_ref[...] = _dot(a_ref[...].astype(BF16), w_ref[0]).astype(o_ref.dtype)


def matmul_ws(a, w, layer, tm, tn, out_dtype):
    m, k = a.shape
    n = w.shape[2]
    return pl.pallas_call(
        _matmul_kernel,
        grid=(n // tn, m // tm),
        in_specs=[pl.BlockSpec((tm, k), lambda j, i: (i, 0)),
                  pl.BlockSpec((1, k, tn), lambda j, i: (layer, 0, j))],
        out_specs=pl.BlockSpec((tm, tn), lambda j, i: (i, j)),
        out_shape=jax.ShapeDtypeStruct((m, n), out_dtype),
        compiler_params=_params("arbitrary", "arbitrary"),
        name="matmul_ws",
    )(a, w)


def _swa_kernel(q_ref, kp_ref, kc_ref, vp_ref, vc_ref, bias_ref, sink_ref, o_ref):
    lo = _lo_lanes(BLOCK)
    kb = jnp.concatenate([kp_ref[0], kc_ref[0]], axis=0)
    vb = jnp.concatenate([vp_ref[0], vc_ref[0]], axis=0)
    which = jnp.where(pl.program_id(1) == 0, 1, 0)
    for c in range(GROUP_WIDTH // LANES):
        rows = slice(2 * c * BLOCK, 2 * (c + 1) * BLOCK)
        qs = _split_pair(q_ref[0, :, c * LANES:(c + 1) * LANES], lo) * QK_SCALE
        s = _dot_nt(qs, kb) + bias_ref[which, rows, :]
        sink = sink_ref[rows, :]
        m = jnp.maximum(jnp.max(s, axis=1, keepdims=True), sink)
        p = jnp.exp(s - jnp.concatenate([m, m], axis=1))
        den = jnp.sum(p, axis=1, keepdims=True) + jnp.exp(sink - m)
        o = _dot(p.astype(BF16), vb) / den
        o_ref[0, :, c * LANES:(c + 1) * LANES] = jnp.where(lo, o[:BLOCK], o[BLOCK:]).astype(o_ref.dtype)


def swa_attention(h3, bias_rows, sink_rows, *, batch, seq):
    nb = seq // BLOCK
    kb, vb = COL_KA // LANES, COL_VA // LANES
    return pl.pallas_call(
        _swa_kernel,
        grid=(batch, nb),
        in_specs=[pl.BlockSpec((1, BLOCK, GROUP_WIDTH), lambda b, n: (b, n, COL_QA // GROUP_WIDTH)),
                  pl.BlockSpec((1, BLOCK, LANES), lambda b, n: (b, jnp.maximum(n - 1, 0), kb)),
                  pl.BlockSpec((1, BLOCK, LANES), lambda b, n: (b, n, kb)),
                  pl.BlockSpec((1, BLOCK, LANES), lambda b, n: (b, jnp.maximum(n - 1, 0), vb)),
                  pl.BlockSpec((1, BLOCK, LANES), lambda b, n: (b, n, vb)),
                  pl.BlockSpec((2, N_HEADS * BLOCK, 2 * BLOCK), lambda b, n: (0, 0, 0)),
                  pl.BlockSpec((N_HEADS * BLOCK, LANES), lambda b, n: (0, 0))],
        out_specs=pl.BlockSpec((1, BLOCK, GROUP_WIDTH), lambda b, n: (b, n, 0)),
        out_shape=jax.ShapeDtypeStruct((batch, seq, GROUP_WIDTH), BF16),
        compiler_params=_params("arbitrary", "arbitrary"),
        name="swa_attn",
    )(h3, h3, h3, h3, h3, bias_rows, sink_rows)


def _dilated_kernel(q_ref, k_ref, v_ref, b0_ref, b1_ref, b2_ref, o_ref, osc, lsc):
    seq = q_ref.shape[1]
    lo = _lo_lanes(BLOCK)
    for p,((_, rate), b_ref) in enumerate(zip(D_PATTERNS, (b0_ref, b1_ref, b2_ref))):
        nb = seq // (rate * BLOCK)

        def unit(u, carry, p=p, rate=rate, b_ref=b_ref, nb=nb):
            n = u // rate
            res = u - n * rate
            rows = pl.ds(n * (BLOCK * rate) + res, BLOCK, stride=rate)
            qs = _split_pair((q_ref[0, rows, :] * QK_SCALE).astype(BF16), lo)
            kc = k_ref[0, rows, :].astype(BF16)
            vc = v_ref[0, rows, :].astype(BF16)
            if nb > 1:
                prow = pl.ds(jnp.maximum(n - 1, 0) * (BLOCK * rate) + res, BLOCK, stride=rate)
                kb = jnp.concatenate([k_ref[0, prow, :].astype(BF16), kc], axis=0)
                vb = jnp.concatenate([v_ref[0, prow, :].astype(BF16), vc], axis=0)
                s = _dot_nt(qs, kb) + b_ref[jnp.where(n == 0, 1, 0), 0]
            else:
                vb = vc
                s = _dot_nt(qs, kc) + b_ref[0, 0, :, BLOCK:2 * BLOCK]
            m = jnp.max(s, axis=1, keepdims=True)
            e = jnp.exp(s - m)
            den = jnp.sum(e, axis=1, keepdims=True)
            o = _dot(e.astype(BF16), vb) / den
            lse = jnp.broadcast_to(m + jnp.log(den), (2 * BLOCK, LANES))
            osc[p, rows, :] = jnp.where(lo, o[:BLOCK], o[BLOCK:])
            lsc[p, rows, :] = jnp.where(lo, lse[:BLOCK], lse[BLOCK:])
            return carry

        lax.fori_loop(0, nb * rate, unit, 0, unroll=8)

    chunk = 2 * BLOCK

    def merge(t, carry):
        r = pl.ds(pl.multiple_of(t * chunk, chunk), chunk)
        l0, l1, l2 = lsc[0, r, :], lsc[1, r, :], lsc[2, r, :]
        m = jnp.maximum(jnp.maximum(l0, l1), l2)
        e0, e1, e2 = jnp.exp(l0 - m), jnp.exp(l1 - m), jnp.exp(l2 - m)
        o = (e0 * osc[0, r, :] + e1 * osc[1, r, :] + e2 * osc[2, r, :]) / (e0 + e1 + e2)
        o_ref[0, r, :] = o.astype(o_ref.dtype)
        return carry

    lax.fori_loop(0, seq // chunk, merge, 0)


def dilated_attention(hd3, biases, *, batch, seq):
    pairs = N_HEADS // 2
    bspec = pl.BlockSpec((2, 1, 2 * BLOCK, 2 * BLOCK), lambda b, p: (0, p, 0, 0))
    return pl.pallas_call(
        _dilated_kernel,
        grid=(batch, pairs),
        in_specs=[pl.BlockSpec((1, seq, LANES), lambda b, p: (b, 0, p)),
                  pl.BlockSpec((1, seq, LANES), lambda b, p: (b, 0, pairs + p)),
                  pl.BlockSpec((1, seq, LANES), lambda b, p: (b, 0, 2 * pairs + p)),
                  bspec, bspec, bspec],
        out_specs=pl.BlockSpec((1, seq, LANES), lambda b, p: (b, 0, p)),
        out_shape=jax.ShapeDtypeStruct((batch, seq, GROUP_WIDTH), BF16),
        scratch_shapes=[pltpu.VMEM((len(D_PATTERNS), seq, LANES), F32),
                        pltpu.VMEM((len(D_PATTERNS), seq, LANES), F32)],
        compiler_params=_params("arbitrary", "arbitrary"),
        name="dilated_attn",
    )(hd3, hd3, hd3, *biases)


def _mla_up_kernel(cq_ref, ckv_ref, kr_ref, qn_ref, kvn_ref, wq_ref, wqs_ref, wk_ref, wv_ref, psw_ref,
                   cos_ref, sin_ref, q_ref, k_ref, v_ref, *, scale):
    def rms(x_ref, g_ref):
        x = x_ref[...].astype(F32)
        return (x * lax.rsqrt(jnp.mean(x * x, axis=-1, keepdims=True) + RMS_EPS) * g_ref[...]).astype(BF16)

    xq = rms(cq_ref, qn_ref)
    xkv = rms(ckv_ref, kvn_ref)
    cos = cos_ref[...]
    sin = sin_ref[...]
    t = _dot(xq, wq_ref[...])
    ts = _dot(xq, wqs_ref[...])
    kn = _dot(xkv, wk_ref[...])
    kr = kr_ref[...]
    rk = kr.astype(F32) * cos + _dot(kr, psw_ref[...]) * sin
    for h in range(N_HEADS):
        sl = slice(h * LANES, (h + 1) * LANES)
        q_ref[:, sl] = ((t[:, sl] * cos + ts[:, sl] * sin) * scale).astype(BF16)
        k_ref[:, sl] = (kn[:, sl] + rk).astype(BF16)
    v_ref[...] = _dot(xkv, wv_ref[...]).astype(BF16)


def mla_up(h, q_norm, kv_norm, wq, wqs, wk, wv, psw, cos_t, sin_t, *, seq, tm):
    n = h.shape[0]
    w = N_HEADS * LANES
    const = lambda shape: pl.BlockSpec(shape, lambda i: (0, 0))
    spt = seq // tm
    return pl.pallas_call(
        functools.partial(_mla_up_kernel, scale=(B_NOPE_DIM + B_ROPE_DIM) ** -0.5 * math.log2(math.e)),
        grid=(n // tm,),
        in_specs=[pl.BlockSpec((tm, B_Q_LORA), lambda i: (i, COL_CQ // B_Q_LORA)),
                  pl.BlockSpec((tm, B_KV_LORA), lambda i: (i, COL_CKV // B_KV_LORA)),
                  pl.BlockSpec((tm, LANES), lambda i: (i, COL_KR // LANES)),
                  const((1, B_Q_LORA)), const((1, B_KV_LORA)),
                  const((B_Q_LORA, w)), const((B_Q_LORA, w)), const((B_KV_LORA, w)),
                  const((B_KV_LORA, GROUP_WIDTH)), const((LANES, LANES)),
                  pl.BlockSpec((tm, LANES), lambda i: (i % spt, 0)),
                  pl.BlockSpec((tm, LANES), lambda i: (i % spt, 0))],
        out_specs=[pl.BlockSpec((tm, w), lambda i: (i, 0)),
                   pl.BlockSpec((tm, w), lambda i: (i, 0)),
                   pl.BlockSpec((tm, GROUP_WIDTH), lambda i: (i, 0))],
        out_shape=[jax.ShapeDtypeStruct((n, w), BF16), jax.ShapeDtypeStruct((n, w), BF16),
                   jax.ShapeDtypeStruct((n, GROUP_WIDTH), BF16)],
        compiler_params=_params("arbitrary"),
        name="mla_up",
    )(h, h, h, q_norm, kv_norm, wq, wqs, wk, wv, psw, cos_t, sin_t)


def _mla_attn_kernel(q_ref, k_ref, v_ref, o_ref, *, tq, heads):
    i = pl.program_id(2)
    lo = _lo_lanes(tq)
    row = lax.broadcasted_iota(jnp.int32, (tq, tq), 0)
    col = lax.broadcasted_iota(jnp.int32, (tq, tq), 1)
    causal = col <= row

    def logits(j):
        start = pl.multiple_of(j * tq, tq)
        return tuple(_dot_nt(q_ref[0, :, hh * LANES:(hh + 1) * LANES],
                             k_ref[0, pl.ds(start, tq), hh * LANES:(hh + 1) * LANES]) for hh in range(heads))

    def update(j, stats, s_all, masked):
        start = pl.multiple_of(j * tq, tq)
        out = []
        for hh in range(heads):
            m, l, acc = stats[3 * hh:3 * hh + 3]
            s = s_all[hh]
            if masked:
                s = jnp.where(causal, s, NEG_INF)
            m_new = jnp.maximum(m, jnp.max(s, axis=1, keepdims=True))
            p = jnp.exp2(s - m_new)
            a = jnp.exp2(m - m_new)
            pv = _dot(p.astype(BF16), v_ref[0, pl.ds(start, tq), (hh // 2) * LANES:(hh // 2 + 1) * LANES])
            part = functools.reduce(jnp.add, [p[:, c * LANES:(c + 1) * LANES] for c in range(tq // LANES)])
            out += [m_new, a * l + part, a * acc + pv]
        return tuple(out)

    def body(j, carry):
        s_next = logits(j + 1)
        return update(j, carry[0], carry[1], False), s_next

    init = (jnp.full((tq, 1), NEG_INF, F32), jnp.zeros((tq, LANES), F32), jnp.zeros((tq, LANES), F32)) * heads
    stats, s_diag = lax.fori_loop(0, i, body, (init, logits(0)))
    stats = update(i, stats, s_diag, True)
    for pp in range(heads // 2):
        (_, l0, a0), (_, l1, a1) = stats[6 * pp:6 * pp + 3], stats[6 * pp + 3:6 * pp + 6]
        o0 = a0 / jnp.sum(l0, axis=1, keepdims=True)
        o1 = a1 / jnp.sum(l1, axis=1, keepdims=True)
        o_ref[0, :, pp * LANES:(pp + 1) * LANES] = jnp.where(lo, o0, o1).astype(o_ref.dtype)


def mla_attention(q, k, v, *, batch, seq, tq, heads):
    pairs = heads // 2
    return pl.pallas_call(
        functools.partial(_mla_attn_kernel, tq=tq, heads=heads),
        grid=(batch, N_HEADS // heads, seq // tq),
        in_specs=[pl.BlockSpec((1, tq, heads * LANES), lambda b, g, i: (b, i, g)),
                  pl.BlockSpec((1, seq, heads * LANES), lambda b, g, i: (b, 0, g)),
                  pl.BlockSpec((1, seq, pairs * LANES), lambda b, g, i: (b, 0, g))],
        out_specs=pl.BlockSpec((1, tq, pairs * LANES), lambda b, g, i: (b, i, g)),
        out_shape=jax.ShapeDtypeStruct((batch, seq, GROUP_WIDTH), BF16),
        compiler_params=_params("arbitrary", "arbitrary", "arbitrary"),
        name="mla_attn",
    )(q, k, v)


def _stick_kernel(q_ref, k_ref, v_ref, tri_ref, o_ref, *, tq, pairs):
    i = pl.program_id(2)
    lo = _lo_lanes(tq)
    row = lax.broadcasted_iota(jnp.int32, (2 * tq, tq), 0)
    col = lax.broadcasted_iota(jnp.int32, (2 * tq, tq), 1)
    strict = col < jnp.where(row >= tq, row - tq, row)
    tri = tri_ref[...]
    qs = [_split_pair(q_ref[0, :, pp * LANES:(pp + 1) * LANES], lo) * QK_SCALE for pp in range(pairs)]

    def chunk(pp, j, run, masked):
        start = pl.multiple_of(j * tq, tq)
        z = _dot_nt(qs[pp], k_ref[0, pl.ds(start, tq), pp * LANES:(pp + 1) * LANES])
        soft = jnp.log(1.0 + jnp.exp(-jnp.abs(z)))
        keep = -(jnp.maximum(z, 0.0) + soft)
        hit = jnp.minimum(z, 0.0) - soft
        if masked:
            keep = jnp.where(strict, keep, 0.0)
        hi = keep.astype(BF16)
        rest = (keep - hi.astype(F32)).astype(BF16)
        after = _dot(hi, tri) + _dot(rest, tri)
        a = jnp.exp(hit + after)
        if masked:
            a = jnp.where(strict, a, 0.0)
        contrib = jnp.exp(run) * _dot(a.astype(BF16), v_ref[0, pl.ds(start, tq), pp * LANES:(pp + 1) * LANES])
        return contrib, run + jnp.sum(keep, axis=1, keepdims=True)

    state = []
    has_prev = i > 0
    for pp in range(pairs):
        acc, run = chunk(pp, i, jnp.zeros((2 * tq, 1), F32), True)
        contrib, run_prev = chunk(pp, jnp.maximum(i - 1, 0), run, False)
        state += [jnp.where(has_prev, run_prev, run), acc + jnp.where(has_prev, contrib, 0.0)]

    def cond(carry):
        t = carry[0]
        live = carry[1]
        for pp in range(1, pairs):
            live = jnp.maximum(live, carry[1 + 2 * pp])
        return jnp.logical_and(t < i, jnp.max(live) > -EXP_UNDERFLOW)

    def body(carry):
        t = carry[0]
        out = [t + 1]
        for pp in range(pairs):
            contrib, run = chunk(pp, i - 1 - t, carry[1 + 2 * pp], False)
            out += [run, carry[2 + 2 * pp] + contrib]
        return tuple(out)

    final = lax.while_loop(cond, body, (jnp.int32(1), *state))
    for pp in range(pairs):
        acc = final[2 + 2 * pp]
        o_ref[0, :, pp * LANES:(pp + 1) * LANES] = jnp.where(lo, acc[:tq], acc[tq:]).astype(o_ref.dtype)


def stick_breaking_attention(h3, tri, *, batch, seq, tq, pairs):
    w = pairs * LANES
    return pl.pallas_call(
        functools.partial(_stick_kernel, tq=tq, pairs=pairs),
        grid=(batch, GROUP_WIDTH // w, seq // tq),
        in_specs=[pl.BlockSpec((1, tq, w), lambda b, g, i: (b, i, COL_QC // w + g)),
                  pl.BlockSpec((1, seq, w), lambda b, g, i: (b, 0, COL_KC // w + g)),
                  pl.BlockSpec((1, seq, w), lambda b, g, i: (b, 0, COL_VC // w + g)),
                  pl.BlockSpec((tq, tq), lambda b, g, i: (0, 0))],
        out_specs=pl.BlockSpec((1, tq, w), lambda b, g, i: (b, i, g)),
        out_shape=jax.ShapeDtypeStruct((batch, seq, GROUP_WIDTH), BF16),
        compiler_params=_params("arbitrary", "arbitrary", "arbitrary"),
        name="stick_breaking",
    )(h3, h3, h3, tri)


def _layer_norm(r, g, b):
    mu = jnp.mean(r, axis=-1, keepdims=True)
    d = r - mu
    var = jnp.mean(d * d, axis=-1, keepdims=True)
    return d * lax.rsqrt(var + LN_EPS) * g + b


def _out_proj_kernel(ga_ref, gb_ref, gc_ref, gd_ref, x_ref, mixg_ref, wo_ref, g_ref, b_ref, rt_ref,
                     xo_ref, xb_ref, lg_ref):
    mixed = None
    for gi, grp in enumerate((ga_ref, gb_ref, gc_ref, gd_ref)):
        xg = grp[...].astype(F32)
        y = xg * lax.rsqrt(jnp.mean(xg * xg, axis=-1, keepdims=True) + RMS_EPS) * mixg_ref[gi:gi + 1, :]
        part = _dot(y.astype(BF16), wo_ref[0, gi * GROUP_WIDTH:(gi + 1) * GROUP_WIDTH, :])
        mixed = part if mixed is None else mixed + part
    x1 = _layer_norm(ALPHA * x_ref[...].astype(F32) + mixed, g_ref[...], b_ref[...])
    xo_ref[...] = x1
    xb = x1.astype(BF16)
    xb_ref[...] = xb
    lg_ref[...] = _dot(xb, rt_ref[...])


def out_proj_ln(ga, gb, gc, gd, x, mix_g, wo, layer, ln_g, ln_b, router, *, tm):
    n = x.shape[0]
    row = lambda w: pl.BlockSpec((tm, w), lambda i: (i, 0))
    const = lambda shape: pl.BlockSpec(shape, lambda i: (0, 0))
    return pl.pallas_call(
        _out_proj_kernel,
        grid=(n // tm,),
        in_specs=[row(GROUP_WIDTH)] * 4 + [row(D_MODEL), const((4, GROUP_WIDTH)),
                                          pl.BlockSpec((1, D_MODEL, D_MODEL), lambda i: (layer, 0, 0)),
                                          const((1, D_MODEL)), const((1, D_MODEL)), const((D_MODEL, LANES))],
        out_specs=[row(D_MODEL), row(D_MODEL), row(LANES)],
        out_shape=[jax.ShapeDtypeStruct((n, D_MODEL), F32), jax.ShapeDtypeStruct((n, D_MODEL), BF16),
                   jax.ShapeDtypeStruct((n, LANES), F32)],
        compiler_params=_params("arbitrary"),
        name="out_proj_ln",
    )(ga, gb, gc, gd, x, mix_g, wo, ln_g, ln_b, router)


def _silu_mul(g, u):
    return g * (1.0 / (1.0 + jnp.exp(-g))) * u


def _gate_up_kernel(x_ref, wg_ref, wu_ref, o_ref, wgb_ref, wub_ref):
    @pl.when(pl.program_id(1) == 0)
    def _():
        wgb_ref[...] = wg_ref[0].astype(BF16)
        wub_ref[...] = wu_ref[0].astype(BF16)

    x = x_ref[...]
    o_ref[...] = _silu_mul(_dot(x, wgb_ref[...]), _dot(x, wub_ref[...])).astype(o_ref.dtype)


def ffn_gate_up(xb, wg, wu, li, *, tm, tf):
    n = xb.shape[0]
    f = wg.shape[2]
    return pl.pallas_call(
        _gate_up_kernel,
        grid=(f // tf, n // tm),
        in_specs=[pl.BlockSpec((tm, D_MODEL), lambda j, i: (i, 0)),
                  pl.BlockSpec((1, D_MODEL, tf), lambda j, i: (li, 0, j)),
                  pl.BlockSpec((1, D_MODEL, tf), lambda j, i: (li, 0, j))],
        out_specs=pl.BlockSpec((tm, tf), lambda j, i: (i, j)),
        out_shape=jax.ShapeDtypeStruct((n, f), BF16),
        scratch_shapes=[pltpu.VMEM((D_MODEL, tf), BF16), pltpu.VMEM((D_MODEL, tf), BF16)],
        compiler_params=_params("arbitrary", "arbitrary"),
        name="ffn_gate_up",
    )(xb, wg, wu)


def _down_ln_kernel(h_ref, wd_ref, x_ref, g_ref, b_ref, xo_ref, xb_ref, acc_ref):
    kk = pl.program_id(1)

    @pl.when(kk == 0)
    def _():
        acc_ref[...] = jnp.zeros_like(acc_ref)

    acc_ref[...] += _dot(h_ref[...], wd_ref[0])

    @pl.when(kk == pl.num_programs(1) - 1)
    def _():
        x2 = _layer_norm(ALPHA * x_ref[...] + acc_ref[...], g_ref[...], b_ref[...])
        xo_ref[...] = x2
        xb_ref[...] = x2.astype(BF16)


def ffn_down_ln(hmid, wd, li, x, ln_g, ln_b, *, tm, tk):
    n, f = hmid.shape
    return pl.pallas_call(
        _down_ln_kernel,
        grid=(n // tm, f // tk),
        in_specs=[pl.BlockSpec((tm, tk), lambda i, k: (i, k)),
                  pl.BlockSpec((1, tk, D_MODEL), lambda i, k: (li, k, 0)),
                  pl.BlockSpec((tm, D_MODEL), lambda i, k: (i, 0)),
                  pl.BlockSpec((1, D_MODEL), lambda i, k: (0, 0)),
                  pl.BlockSpec((1, D_MODEL), lambda i, k: (0, 0))],
        out_specs=[pl.BlockSpec((tm, D_MODEL), lambda i, k: (i, 0)),
                   pl.BlockSpec((tm, D_MODEL), lambda i, k: (i, 0))],
        out_shape=[jax.ShapeDtypeStruct((n, D_MODEL), F32), jax.ShapeDtypeStruct((n, D_MODEL), BF16)],
        scratch_shapes=[pltpu.VMEM((tm, D_MODEL), F32)],
        compiler_params=_params("arbitrary", "arbitrary"),
        name="ffn_down_ln",
    )(hmid, wd, x, ln_g, ln_b)


def _new_expert(te_ref, t):
    return jnp.logical_or(t == 0, te_ref[t] != te_ref[jnp.maximum(t - 1, 0)])


def _moe_gate_up_kernel(te_ref, tv_ref, x_ref, wg_ref, wu_ref, o_ref, wgb_ref, wub_ref):
    t = pl.program_id(1)

    @pl.when(_new_expert(te_ref, t))
    def _():
        wgb_ref[...] = wg_ref[0, 0].astype(BF16)
        wub_ref[...] = wu_ref[0, 0].astype(BF16)

    @pl.when(tv_ref[t] > 0)
    def _():
        x = x_ref[...]
        o_ref[...] = _silu_mul(_dot(x, wgb_ref[...]), _dot(x, wub_ref[...])).astype(o_ref.dtype)

    @pl.when(tv_ref[t] == 0)
    def _():
        o_ref[...] = jnp.zeros_like(o_ref)


def moe_gate_up(tile_expert, tile_valid, xs, wg, wu, li, *, tm, tf):
    p = xs.shape[0]
    f = wg.shape[3]
    grid_spec = pltpu.PrefetchScalarGridSpec(
        num_scalar_prefetch=2,
        grid=(f // tf, p // tm),
        in_specs=[pl.BlockSpec((tm, D_MODEL), lambda j, t, te, tv: (t, 0)),
                  pl.BlockSpec((1, 1, D_MODEL, tf), lambda j, t, te, tv: (li, te[t], 0, j)),
                  pl.BlockSpec((1, 1, D_MODEL, tf), lambda j, t, te, tv: (li, te[t], 0, j))],
        out_specs=pl.BlockSpec((tm, tf), lambda j, t, te, tv: (t, j)),
        scratch_shapes=[pltpu.VMEM((D_MODEL, tf), BF16), pltpu.VMEM((D_MODEL, tf), BF16)],
    )
    return pl.pallas_call(
        _moe_gate_up_kernel,
        grid_spec=grid_spec,
        out_shape=jax.ShapeDtypeStruct((p, f), BF16),
        compiler_params=_params("arbitrary", "arbitrary"),
        name="moe_gate_up",
    )(tile_expert, tile_valid, xs, wg, wu)


def _moe_down_kernel(te_ref, tv_ref, h_ref, wd_ref, o_ref, wdb_ref):
    t = pl.program_id(1)

    @pl.when(_new_expert(te_ref, t))
    def _():
        wdb_ref[...] = wd_ref[0, 0].astype(BF16)

    @pl.when(tv_ref[t] > 0)
    def _():
        o_ref[...] = _dot(h_ref[...], wdb_ref[...])

    @pl.when(tv_ref[t] == 0)
    def _():
        o_ref[...] = jnp.zeros_like(o_ref)


def moe_down(tile_expert, tile_valid, hs, wd, li, *, tm, tn):
    p, f = hs.shape
    grid_spec = pltpu.PrefetchScalarGridSpec(
        num_scalar_prefetch=2,
        grid=(D_MODEL // tn, p // tm),
        in_specs=[pl.BlockSpec((tm, f), lambda j, t, te, tv: (t, 0)),
                  pl.BlockSpec((1, 1, f, tn), lambda j, t, te, tv: (li, te[t], 0, j))],
        out_specs=pl.BlockSpec((tm, tn), lambda j, t, te, tv: (t, j)),
        scratch_shapes=[pltpu.VMEM((f, tn), BF16)],
    )
    return pl.pallas_call(
        _moe_down_kernel,
        grid_spec=grid_spec,
        out_shape=jax.ShapeDtypeStruct((p, D_MODEL), F32),
        compiler_params=_params("arbitrary", "arbitrary"),
        name="moe_down",
    )(tile_expert, tile_valid, hs, wd)


def _combine_ln_kernel(x_ref, y0_ref, y1_ref, gate_ref, g_ref, b_ref, xo_ref, xb_ref):
    f = gate_ref[:, 0:1] * y0_ref[...] + gate_ref[:, 1:2] * y1_ref[...]
    x2 = _layer_norm(ALPHA * x_ref[...] + f, g_ref[...], b_ref[...])
    xo_ref[...] = x2
    xb_ref[...] = x2.astype(BF16)


def combine_ln(x, y0, y1, gates, ln_g, ln_b, *, tm):
    n = x.shape[0]
    row = pl.BlockSpec((tm, D_MODEL), lambda i: (i, 0))
    const = pl.BlockSpec((1, D_MODEL), lambda i: (0, 0))
    return pl.pallas_call(
        _combine_ln_kernel,
        grid=(n // tm,),
        in_specs=[row, row, row, pl.BlockSpec((tm, TOP_K), lambda i: (i, 0)), const, const],
        out_specs=[row, row],
        out_shape=[jax.ShapeDtypeStruct((n, D_MODEL), F32), jax.ShapeDtypeStruct((n, D_MODEL), BF16)],
        compiler_params=_params("arbitrary"),
        name="combine_ln",
    )(x, y0, y1, gates, ln_g, ln_b)


def _t5_bucket(dist):
    max_exact = NUM_BUCKETS // 2
    d = jnp.maximum(dist, 1).astype(F32)
    large = max_exact + (jnp.log(d / max_exact) / math.log(T5_MAX_DISTANCE / max_exact)
                         * (NUM_BUCKETS - max_exact)).astype(jnp.int32)
    large = jnp.minimum(large, NUM_BUCKETS - 1)
    return jnp.where(dist < max_exact, dist, large)


def _band_bias_masked(table, stride, max_dist):
    dist = jnp.arange(BLOCK)[:, None] + BLOCK - jnp.arange(2 * BLOCK)[None, :]
    bias = jnp.transpose(table[_t5_bucket(jnp.maximum(dist, 0) * stride)], (2, 0, 1)).astype(F32)
    valid = (dist >= 0) & (dist <= max_dist)
    return jnp.where(valid[None], bias, NEG_INF)


def _rope_lane_tables(seq):
    half = B_ROPE_DIM // 2
    inv = ROPE_THETA ** (-jnp.arange(0, B_ROPE_DIM, 2, dtype=F32) / B_ROPE_DIM)
    ang = jnp.arange(seq, dtype=F32)[:, None] * inv[None, :]
    cos, sin = jnp.cos(ang), jnp.sin(ang)
    ones = jnp.ones((seq, ROPE_LANE0), F32)
    zeros = jnp.zeros((seq, ROPE_LANE0), F32)
    pad = jnp.zeros((seq, LANES - ROPE_LANE0 - 2 * half), F32)
    cos_t = jnp.concatenate([ones, cos, cos, pad], axis=1)
    sin_t = jnp.concatenate([zeros, -sin, sin, pad], axis=1)
    return cos_t, sin_t


def _permute_heads(t, axis, order):
    parts = [lax.slice_in_dim(t, hh * HEAD_DIM, (hh + 1) * HEAD_DIM, axis=axis) for hh in order]
    return jnp.concatenate(parts, axis=axis)


IN_SEGMENTS = (("qa", 512), ("ka", 128), ("va", 128), ("cq", 384), ("ckv", 256), ("kr", 32),
               ("qc", 512), ("kc", 512), ("vc", 512), ("qd", 512), ("kd", 512), ("vd", 512))
IN_WIDTH = sum(size for _, size in IN_SEGMENTS)


def _relayout_w_in_kernel(w_ref, main_ref, d_ref):
    w = w_ref[0]
    rows = w.shape[0]
    seg = {}
    start = 0
    for name, size in IN_SEGMENTS:
        seg[name] = w[:, start:start + size]
        start += size
    qa = jnp.concatenate([seg["qa"][:, hh * HEAD_DIM:(hh + 1) * HEAD_DIM] for hh in A_HEAD_ORDER], axis=1)
    kr = jnp.concatenate([jnp.zeros((rows, ROPE_LANE0), F32), seg["kr"],
                          jnp.zeros((rows, LANES - ROPE_LANE0 - B_ROPE_DIM), F32)], axis=1)
    main = jnp.concatenate([qa, seg["qc"], seg["kc"], seg["cq"], seg["ka"], seg["vc"], seg["ckv"], seg["va"], kr],
                           axis=1)
    main_ref[0] = main.astype(BF16)
    d_ref[0] = jnp.concatenate([seg["qd"], seg["kd"], seg["vd"]], axis=1).astype(BF16)


def _relayout_w_in(w_in, *, tr):
    d = w_in.shape[0]
    return pl.pallas_call(
        _relayout_w_in_kernel,
        grid=(d, D_MODEL // tr),
        in_specs=[pl.BlockSpec((1, tr, IN_WIDTH), lambda l, i: (l, i, 0))],
        out_specs=[pl.BlockSpec((1, tr, IN_WIDTH_P), lambda l, i: (l, i, 0)),
                   pl.BlockSpec((1, tr, IN_WIDTH_D), lambda l, i: (l, i, 0))],
        out_shape=[jax.ShapeDtypeStruct((d, D_MODEL, IN_WIDTH_P), BF16),
                   jax.ShapeDtypeStruct((d, D_MODEL, IN_WIDTH_D), BF16)],
        compiler_params=_params("arbitrary", "arbitrary"),
        name="relayout_w_in",
    )(w_in)


def _relayout_mla(w_uq, w_ukv):
    d = w_uq.shape[0]
    half = B_ROPE_DIM // 2
    wq = w_uq.astype(BF16).reshape(d, B_Q_LORA, N_HEADS, B_NOPE_DIM + B_ROPE_DIM)
    nope, r1, r2 = wq[..., :B_NOPE_DIM], wq[..., B_NOPE_DIM:B_NOPE_DIM + half], wq[..., B_NOPE_DIM + half:]
    z32 = jnp.zeros(wq.shape[:3] + (LANES - B_NOPE_DIM - B_ROPE_DIM,), BF16)
    z64 = jnp.zeros(wq.shape[:3] + (B_NOPE_DIM,), BF16)
    wq_t = jnp.concatenate([nope, r1, r2, z32], axis=-1).reshape(d, B_Q_LORA, N_HEADS * LANES)
    wq_s = jnp.concatenate([z64, r2, r1, z32], axis=-1).reshape(d, B_Q_LORA, N_HEADS * LANES)
    wkv = w_ukv.astype(BF16).reshape(d, B_KV_LORA, N_HEADS, 2 * HEAD_DIM)
    zk = jnp.zeros(wkv.shape[:3] + (LANES - B_NOPE_DIM,), BF16)
    wk_t = jnp.concatenate([wkv[..., :B_NOPE_DIM], zk], axis=-1).reshape(d, B_KV_LORA, N_HEADS * LANES)
    wv = wkv[..., B_NOPE_DIM:].reshape(d, B_KV_LORA, GROUP_WIDTH)
    return wq_t, wq_s, wk_t, wv


def _rope_swap_matrix():
    half = B_ROPE_DIM // 2
    src = jnp.arange(LANES)[:, None]
    dst = jnp.arange(LANES)[None, :]
    first = (dst >= ROPE_LANE0) & (dst < ROPE_LANE0 + half) & (src == dst + half)
    second = (dst >= ROPE_LANE0 + half) & (dst < ROPE_LANE0 + 2 * half) & (src == dst - half)
    return (first | second).astype(BF16)


def _route(logits, tm):
    n = logits.shape[0]
    top_logits, top_idx = lax.top_k(logits, TOP_K)
    gates = jax.nn.softmax(top_logits, axis=-1)
    onehot = jax.nn.one_hot(top_idx, N_EXPERTS, dtype=jnp.int32)
    member = jnp.sum(onehot, axis=1)
    rank = jnp.cumsum(member, axis=0) - member
    counts = jnp.sum(member, axis=0)
    padded = ((counts + tm - 1) // tm) * tm
    ends = jnp.cumsum(padded)
    starts = ends - padded
    pos = jnp.sum(onehot * (starts[None, None, :] + rank[:, None, :]), axis=-1)
    n_rows = TOP_K * n + N_EXPERTS * tm
    n_tiles = n_rows // tm
    src = (jnp.arange(n_rows, dtype=jnp.int32) % n).at[pos.reshape(-1)].set(
        jnp.repeat(jnp.arange(n, dtype=jnp.int32), TOP_K))
    tile_start = jnp.arange(n_tiles, dtype=jnp.int32) * tm
    tile_valid = (tile_start < ends[-1]).astype(jnp.int32)
    tile_expert = jnp.minimum(jnp.searchsorted(ends, tile_start, side="right"), N_EXPERTS - 1).astype(jnp.int32)
    last_expert = tile_expert[jnp.maximum(ends[-1] // tm - 1, 0)]
    tile_expert = jnp.where(tile_valid > 0, tile_expert, last_expert)
    return pos, gates, src, tile_expert, tile_valid


def kernel(x, w_in, w_o, mla_q_norm, mla_kv_norm, mla_w_uq, mla_w_ukv, attn_sinks, rel_bias_table, mix_norm_g,
           ln1_g, ln1_b, ln2_g, ln2_b, ffn_w_gate, ffn_w_up, ffn_w_down, moe_router, moe_w_gate, moe_w_up,
           moe_w_down):
    batch, seq, _ = x.shape
    n = batch * seq

    w_in_p, w_in_d = _relayout_w_in(w_in, tr=256)
    w_o_p = jnp.concatenate([_permute_heads(w_o[:, :GROUP_WIDTH], 1, A_HEAD_ORDER), w_o[:, GROUP_WIDTH:]],
                            axis=1).astype(BF16)
    mix_g_p = jnp.concatenate([_permute_heads(mix_norm_g[:, :1], 2, A_HEAD_ORDER), mix_norm_g[:, 1:]], axis=1)
    wq_t, wq_s, wk_t, wv = _relayout_mla(mla_w_uq, mla_w_ukv)
    psw = _rope_swap_matrix()
    cos_t, sin_t = _rope_lane_tables(seq)
    tq = 256
    tri = (jnp.arange(tq)[:, None] > jnp.arange(tq)[None, :]).astype(BF16)
    order = list(A_HEAD_ORDER)
    def with_first_block_variant(b):
        prev = jnp.arange(2 * BLOCK) < BLOCK
        return jnp.stack([b, jnp.where(prev, NEG_INF, b)], axis=0)

    bias_a = _band_bias_masked(rel_bias_table[:, :N_HEADS], 1, A_WINDOW - 1)
    bias_a_rows = with_first_block_variant(jnp.concatenate([bias_a[hh] for hh in order], axis=0))
    sink_rows = jnp.concatenate([jnp.broadcast_to(attn_sinks[:, hh, None, None], (DEPTH, BLOCK, LANES))
                                 for hh in order], axis=1)
    biases_d = [with_first_block_variant(_band_bias_masked(rel_bias_table[:, N_HEADS:], rate, window // rate).reshape(
        N_HEADS // 2, 2 * BLOCK, 2 * BLOCK)) for window, rate in D_PATTERNS]
    router_p = jnp.pad(moe_router, ((0, 0), (0, 0), (0, LANES - N_EXPERTS))).astype(BF16)
    wd_d = ffn_w_down.astype(BF16)
    moe_tm = 512

    xf = x.reshape(n, D_MODEL)
    xb = xf
    for layer in range(DEPTH):
        h = matmul_ws(xb, w_in_p, layer, 1024, 1024, BF16)
        hd = matmul_ws(xb, w_in_d, layer, 1024, 512, F32)
        h3 = h.reshape(batch, seq, IN_WIDTH_P)
        ga = swa_attention(h3, bias_a_rows, sink_rows[layer], batch=batch, seq=seq)
        q_b, k_b, v_b = mla_up(h, mla_q_norm[layer][None], mla_kv_norm[layer][None], wq_t[layer], wq_s[layer],
                               wk_t[layer], wv[layer], psw, cos_t, sin_t, seq=seq, tm=512)
        gb = mla_attention(q_b.reshape(batch, seq, -1), k_b.reshape(batch, seq, -1),
                           v_b.reshape(batch, seq, -1), batch=batch, seq=seq, tq=tq, heads=4)
        gc = stick_breaking_attention(h3, tri, batch=batch, seq=seq, tq=tq, pairs=2)
        gd = dilated_attention(hd.reshape(batch, seq, IN_WIDTH_D), biases_d, batch=batch, seq=seq)
        i = layer // 2
        router = router_p[i] if layer % 2 == 1 else jnp.zeros((D_MODEL, LANES), BF16)
        xf, xb, logits = out_proj_ln(ga.reshape(n, -1), gb.reshape(n, -1), gc.reshape(n, -1), gd.reshape(n, -1),
                                     xf, mix_g_p[layer], w_o_p, layer, ln1_g[layer][None], ln1_b[layer][None],
                                     router, tm=256)
        if layer % 2 == 0:
            hmid = ffn_gate_up(xb, ffn_w_gate, ffn_w_up, i, tm=512, tf=512)
            xf, xb = ffn_down_ln(hmid, wd_d, i, xf, ln2_g[layer][None], ln2_b[layer][None], tm=512, tk=512)
        else:
            pos, gates, src, tile_expert, tile_valid = _route(logits[:, :N_EXPERTS], moe_tm)
            xs = jnp.take(xf, src, axis=0, mode="clip").astype(BF16)
            hs = moe_gate_up(tile_expert, tile_valid, xs, moe_w_gate, moe_w_up, i, tm=moe_tm, tf=512)
            ys = moe_down(tile_expert, tile_valid, hs, moe_w_down, i, tm=moe_tm, tn=512)
            y0 = jnp.take(ys, pos[:, 0], axis=0, mode="clip")
            y1 = jnp.take(ys, pos[:, 1], axis=0, mode="clip")
            xf, xb = combine_ln(xf, y0, y1, gates, ln2_g[layer][None], ln2_b[layer][None], tm=512)
    return xf.reshape(batch, seq, D_MODEL)
```

```python
import functools
import math

import jax
import jax.numpy as jnp
from jax import lax
from jax.experimental import pallas as pl
from jax.experimental.pallas import tpu as pltpu

D_MODEL = 2048
DEPTH = 4
HEAD_DIM = 64
N_HEADS = 8
GROUP_WIDTH = N_HEADS * HEAD_DIM
BLOCK = 128
A_KV_HEADS = 2
A_WINDOW = 128
B_NOPE_DIM = 64
B_ROPE_DIM = 32
B_Q_LORA = 384
B_KV_LORA = 256
ROPE_THETA = 10000.0
D_PATTERNS = ((128, 1), (512, 4), (2048, 16))
NUM_BUCKETS = 32
T5_MAX_DISTANCE = 2048
D_FF = 5632
N_EXPERTS = 8
TOP_K = 2
NEG_INF = -1e30
LN_EPS = 1e-5
RMS_EPS = 1e-6
ALPHA = (2 * DEPTH) ** 0.25

LANES = 128
VMEM_LIMIT = 56 * 1024 * 1024

COL_QA, COL_QC, COL_KC, COL_CQ, COL_KA = 0, 512, 1024, 1536, 1920
COL_VC, COL_CKV, COL_VA, COL_KR = 2048, 2560, 2816, 2944
IN_WIDTH_P = 3072
IN_WIDTH_D = 3 * GROUP_WIDTH
A_HEAD_ORDER = (0, 4, 1, 5, 2, 6, 3, 7)
ROPE_LANE0 = 64
EXP_UNDERFLOW = 105.0
QK_SCALE = HEAD_DIM ** -0.5

BF16 = jnp.bfloat16
F32 = jnp.float32


def _params(*sem):
    return pltpu.CompilerParams(dimension_semantics=sem, vmem_limit_bytes=VMEM_LIMIT)


def _dot(a, b):
    return jnp.dot(a, b, preferred_element_type=F32)


def _dot_nt(a, b):
    return lax.dot_general(a, b, (((1,), (1,)), ((), ())), preferred_element_type=F32)


def _lo_lanes(rows):
    return lax.broadcasted_iota(jnp.int32, (rows, LANES), 1) < HEAD_DIM


def _split_pair(t, lo):
    zero = jnp.zeros_like(t)
    return jnp.concatenate([jnp.where(lo, t, zero), jnp.where(lo, zero, t)], axis=0)


def _matmul_kernel(a_ref, w_ref, o_ref):
    o_ref[...] = _dot(a_ref[...].astype(BF16), w_ref[0]).astype(o_ref.dtype)


def matmul_ws(a, w, layer, tm, tn, out_dtype):
    m, k = a.shape
    n = w.shape[2]
    return pl.pallas_call(
        _matmul_kernel,
        grid=(n // tn, m // tm),
        in_specs=[pl.BlockSpec((tm, k), lambda j, i: (i, 0)),
                  pl.BlockSpec((1, k, tn), lambda j, i: (layer, 0, j))],
        out_specs=pl.BlockSpec((tm, tn), lambda j, i: (i, j)),
        out_shape=jax.ShapeDtypeStruct((m, n), out_dtype),
        compiler_params=_params("arbitrary", "arbitrary"),
        name="matmul_ws",
    )(a, w)


def _swa_kernel(q_ref, kp_ref, kc_ref, vp_ref, vc_ref, bias_ref, sink_ref, o_ref):
    lo = _lo_lanes(BLOCK)
    kb = jnp.concatenate([kp_ref[0], kc_ref[0]], axis=0)
    vb = jnp.concatenate([vp_ref[0], vc_ref[0]], axis=0)
    which = jnp.where(pl.program_id(1) == 0, 1, 0)
    for c in range(GROUP_WIDTH // LANES):
        rows = slice(2 * c * BLOCK, 2 * (c + 1) * BLOCK)
        qs = _split_pair(q_ref[0, :, c * LANES:(c + 1) * LANES], lo) * QK_SCALE
        s = _dot_nt(qs, kb) + bias_ref[which, rows, :]
        sink = sink_ref[rows, :]
        m = jnp.maximum(jnp.max(s, axis=1, keepdims=True), sink)
        p = jnp.exp(s - jnp.concatenate([m, m], axis=1))
        den = jnp.sum(p, axis=1, keepdims=True) + jnp.exp(sink - m)
        o = _dot(p.astype(BF16), vb) / den
        o_ref[0, :, c * LANES:(c + 1) * LANES] = jnp.where(lo, o[:BLOCK], o[BLOCK:]).astype(o_ref.dtype)


def swa_attention(h3, bias_rows, sink_rows, *, batch, seq):
    nb = seq // BLOCK
    kb, vb = COL_KA // LANES, COL_VA // LANES
    return pl.pallas_call(
        _swa_kernel,
        grid=(batch, nb),
        in_specs=[pl.BlockSpec((1, BLOCK, GROUP_WIDTH), lambda b, n: (b, n, COL_QA // GROUP_WIDTH)),
                  pl.BlockSpec((1, BLOCK, LANES), lambda b, n: (b, jnp.maximum(n - 1, 0), kb)),
                  pl.BlockSpec((1, BLOCK, LANES), lambda b, n: (b, n, kb)),
                  pl.BlockSpec((1, BLOCK, LANES), lambda b, n: (b, jnp.maximum(n - 1, 0), vb)),
                  pl.BlockSpec((1, BLOCK, LANES), lambda b, n: (b, n, vb)),
                  pl.BlockSpec((2, N_HEADS * BLOCK, 2 * BLOCK), lambda b, n: (0, 0, 0)),
                  pl.BlockSpec((N_HEADS * BLOCK, LANES), lambda b, n: (0, 0))],
        out_specs=pl.BlockSpec((1, BLOCK, GROUP_WIDTH), lambda b, n: (b, n, 0)),
        out_shape=jax.ShapeDtypeStruct((batch, seq, GROUP_WIDTH), BF16),
        compiler_params=_params("arbitrary", "arbitrary"),
        name="swa_attn",
    )(h3, h3, h3, h3, h3, bias_rows, sink_rows)


def _dilated_kernel(q_ref, k_ref, v_ref, b0_ref, b1_ref, b2_ref, o_ref, osc, lsc):
    seq = q_ref.shape[1]
    lo = _lo_lanes(BLOCK)
    for p,((_, rate), b_ref) in enumerate(zip(D_PATTERNS, (b0_ref, b1_ref, b2_ref))):
        nb = seq // (rate * BLOCK)

        def unit(u, carry, p=p, rate=rate, b_ref=b_ref, nb=nb):
            n = u // rate
            res = u - n * rate
            rows = pl.ds(n * (BLOCK * rate) + res, BLOCK, stride=rate)
            qs = _split_pair((q_ref[0, rows, :] * QK_SCALE).astype(BF16), lo)
            kc = k_ref[0, rows, :].astype(BF16)
            vc = v_ref[0, rows, :].astype(BF16)
            if nb > 1:
                prow = pl.ds(jnp.maximum(n - 1, 0) * (BLOCK * rate) + res, BLOCK, stride=rate)
                kb = jnp.concatenate([k_ref[0, prow, :].astype(BF16), kc], axis=0)
                vb = jnp.concatenate([v_ref[0, prow, :].astype(BF16), vc], axis=0)
                s = _dot_nt(qs, kb) + b_ref[jnp.where(n == 0, 1, 0), 0]
            else:
                vb = vc
                s = _dot_nt(qs, kc) + b_ref[0, 0, :, BLOCK:2 * BLOCK]
            m = jnp.max(s, axis=1, keepdims=True)
            e = jnp.exp(s - m)
            den = jnp.sum(e, axis=1, keepdims=True)
            o = _dot(e.astype(BF16), vb) / den
            lse = jnp.broadcast_to(m + jnp.log(den), (2 * BLOCK, LANES))
            osc[p, rows, :] = jnp.where(lo, o[:BLOCK], o[BLOCK:])
            lsc[p, rows, :] = jnp.where(lo, lse[:BLOCK], lse[BLOCK:])
            return carry

        lax.fori_loop(0, nb * rate, unit, 0, unroll=8)

    chunk = 2 * BLOCK

    def merge(t, carry):
        r = pl.ds(pl.multiple_of(t * chunk, chunk), chunk)
        l0, l1, l2 = lsc[0, r, :], lsc[1, r, :], lsc[2, r, :]
        m = jnp.maximum(jnp.maximum(l0, l1), l2)
        e0, e1, e2 = jnp.exp(l0 - m), jnp.exp(l1 - m), jnp.exp(l2 - m)
        o = (e0 * osc[0, r, :] + e1 * osc[1, r, :] + e2 * osc[2, r, :]) / (e0 + e1 + e2)
        o_ref[0, r, :] = o.astype(o_ref.dtype)
        return carry

    lax.fori_loop(0, seq // chunk, merge, 0)


def dilated_attention(hd3, biases, *, batch, seq):
    pairs = N_HEADS // 2
    bspec = pl.BlockSpec((2, 1, 2 * BLOCK, 2 * BLOCK), lambda b, p: (0, p, 0, 0))
    return pl.pallas_call(
        _dilated_kernel,
        grid=(batch, pairs),
        in_specs=[pl.BlockSpec((1, seq, LANES), lambda b, p: (b, 0, p)),
                  pl.BlockSpec((1, seq, LANES), lambda b, p: (b, 0, pairs + p)),
                  pl.BlockSpec((1, seq, LANES), lambda b, p: (b, 0, 2 * pairs + p)),
                  bspec, bspec, bspec],
        out_specs=pl.BlockSpec((1, seq, LANES), lambda b, p: (b, 0, p)),
        out_shape=jax.ShapeDtypeStruct((batch, seq, GROUP_WIDTH), BF16),
        scratch_shapes=[pltpu.VMEM((len(D_PATTERNS), seq, LANES), F32),
                        pltpu.VMEM((len(D_PATTERNS), seq, LANES), F32)],
        compiler_params=_params("arbitrary", "arbitrary"),
        name="dilated_attn",
    )(hd3, hd3, hd3, *biases)


def _mla_up_kernel(cq_ref, ckv_ref, kr_ref, qn_ref, kvn_ref, wq_ref, wqs_ref, wk_ref, wv_ref, psw_ref,
                   cos_ref, sin_ref, q_ref, k_ref, v_ref, *, scale):
    def rms(x_ref, g_ref):
        x = x_ref[...].astype(F32)
        return (x * lax.rsqrt(jnp.mean(x * x, axis=-1, keepdims=True) + RMS_EPS) * g_ref[...]).astype(BF16)

    xq = rms(cq_ref, qn_ref)
    xkv = rms(ckv_ref, kvn_ref)
    cos = cos_ref[...]
    sin = sin_ref[...]
    t = _dot(xq, wq_ref[...])
    ts = _dot(xq, wqs_ref[...])
    kn = _dot(xkv, wk_ref[...])
    kr = kr_ref[...]
    rk = kr.astype(F32) * cos + _dot(kr, psw_ref[...]) * sin
    for h in range(N_HEADS):
        sl = slice(h * LANES, (h + 1) * LANES)
        q_ref[:, sl] = ((t[:, sl] * cos + ts[:, sl] * sin) * scale).astype(BF16)
        k_ref[:, sl] = (kn[:, sl] + rk).astype(BF16)
    v_ref[...] = _dot(xkv, wv_ref[...]).astype(BF16)


def mla_up(h, q_norm, kv_norm, wq, wqs, wk, wv, psw, cos_t, sin_t, *, seq, tm):
    n = h.shape[0]
    w = N_HEADS * LANES
    const = lambda shape: pl.BlockSpec(shape, lambda i: (0, 0))
    spt = seq // tm
    return pl.pallas_call(
        functools.partial(_mla_up_kernel, scale=(B_NOPE_DIM + B_ROPE_DIM) ** -0.5 * math.log2(math.e)),
        grid=(n // tm,),
        in_specs=[pl.BlockSpec((tm, B_Q_LORA), lambda i: (i, COL_CQ // B_Q_LORA)),
                  pl.BlockSpec((tm, B_KV_LORA), lambda i: (i, COL_CKV // B_KV_LORA)),
                  pl.BlockSpec((tm, LANES), lambda i: (i, COL_KR // LANES)),
                  const((1, B_Q_LORA)), const((1, B_KV_LORA)),
                  const((B_Q_LORA, w)), const((B_Q_LORA, w)), const((B_KV_LORA, w)),
                  const((B_KV_LORA, GROUP_WIDTH)), const((LANES, LANES)),
                  pl.BlockSpec((tm, LANES), lambda i: (i % spt, 0)),
                  pl.BlockSpec((tm, LANES), lambda i: (i % spt, 0))],
        out_specs=[pl.BlockSpec((tm, w), lambda i: (i, 0)),
                   pl.BlockSpec((tm, w), lambda i: (i, 0)),
                   pl.BlockSpec((tm, GROUP_WIDTH), lambda i: (i, 0))],
        out_shape=[jax.ShapeDtypeStruct((n, w), BF16), jax.ShapeDtypeStruct((n, w), BF16),
                   jax.ShapeDtypeStruct((n, GROUP_WIDTH), BF16)],
        compiler_params=_params("arbitrary"),
        name="mla_up",
    )(h, h, h, q_norm, kv_norm, wq, wqs, wk, wv, psw, cos_t, sin_t)


def _mla_attn_kernel(q_ref, k_ref, v_ref, o_ref, *, tq, heads):
    i = pl.program_id(2)
    lo = _lo_lanes(tq)
    row = lax.broadcasted_iota(jnp.int32, (tq, tq), 0)
    col = lax.broadcasted_iota(jnp.int32, (tq, tq), 1)
    causal = col <= row

    def logits(j):
        start = pl.multiple_of(j * tq, tq)
        return tuple(_dot_nt(q_ref[0, :, hh * LANES:(hh + 1) * LANES],
                             k_ref[0, pl.ds(start, tq), hh * LANES:(hh + 1) * LANES]) for hh in range(heads))

    def update(j, stats, s_all, masked):
        start = pl.multiple_of(j * tq, tq)
        out = []
        for hh in range(heads):
            m, l, acc = stats[3 * hh:3 * hh + 3]
            s = s_all[hh]
            if masked:
                s = jnp.where(causal, s, NEG_INF)
            m_new = jnp.maximum(m, jnp.max(s, axis=1, keepdims=True))
            p = jnp.exp2(s - m_new)
            a = jnp.exp2(m - m_new)
            pv = _dot(p.astype(BF16), v_ref[0, pl.ds(start, tq), (hh // 2) * LANES:(hh // 2 + 1) * LANES])
            part = functools.reduce(jnp.add, [p[:, c * LANES:(c + 1) * LANES] for c in range(tq // LANES)])
            out += [m_new, a * l + part, a * acc + pv]
        return tuple(out)

    def body(j, carry):
        s_next = logits(j + 1)
        return update(j, carry[0], carry[1], False), s_next

    init = (jnp.full((tq, 1), NEG_INF, F32), jnp.zeros((tq, LANES), F32), jnp.zeros((tq, LANES), F32)) * heads
    stats, s_diag = lax.fori_loop(0, i, body, (init, logits(0)))
    stats = update(i, stats, s_diag, True)
    for pp in range(heads // 2):
        (_, l0, a0), (_, l1, a1) = stats[6 * pp:6 * pp + 3], stats[6 * pp + 3:6 * pp + 6]
        o0 = a0 / jnp.sum(l0, axis=1, keepdims=True)
        o1 = a1 / jnp.sum(l1, axis=1, keepdims=True)
        o_ref[0, :, pp * LANES:(pp + 1) * LANES] = jnp.where(lo, o0, o1).astype(o_ref.dtype)


def mla_attention(q, k, v, *, batch, seq, tq, heads):
    pairs = heads // 2
    return pl.pallas_call(
        functools.partial(_mla_attn_kernel, tq=tq, heads=heads),
        grid=(batch, N_HEADS // heads, seq // tq),
        in_specs=[pl.BlockSpec((1, tq, heads * LANES), lambda b, g, i: (b, i, g)),
                  pl.BlockSpec((1, seq, heads * LANES), lambda b, g, i: (b, 0, g)),
                  pl.BlockSpec((1, seq, pairs * LANES), lambda b, g, i: (b, 0, g))],
        out_specs=pl.BlockSpec((1, tq, pairs * LANES), lambda b, g, i: (b, i, g)),
        out_shape=jax.ShapeDtypeStruct((batch, seq, GROUP_WIDTH), BF16),
        compiler_params=_params("arbitrary", "arbitrary", "arbitrary"),
        name="mla_attn",
    )(q, k, v)


def _stick_kernel(q_ref, k_ref, v_ref, tri_ref, o_ref, *, tq, pairs):
    i = pl.program_id(2)
    lo = _lo_lanes(tq)
    row = lax.broadcasted_iota(jnp.int32, (2 * tq, tq), 0)
    col = lax.broadcasted_iota(jnp.int32, (2 * tq, tq), 1)
    strict = col < jnp.where(row >= tq, row - tq, row)
    tri = tri_ref[...]
    qs = [_split_pair(q_ref[0, :, pp * LANES:(pp + 1) * LANES], lo) * QK_SCALE for pp in range(pairs)]

    def chunk(pp, j, run, masked):
        start = pl.multiple_of(j * tq, tq)
        z = _dot_nt(qs[pp], k_ref[0, pl.ds(start, tq), pp * LANES:(pp + 1) * LANES])
        soft = jnp.log(1.0 + jnp.exp(-jnp.abs(z)))
        keep = -(jnp.maximum(z, 0.0) + soft)
        hit = jnp.minimum(z, 0.0) - soft
        if masked:
            keep = jnp.where(strict, keep, 0.0)
        hi = keep.astype(BF16)
        rest = (keep - hi.astype(F32)).astype(BF16)
        after = _dot(hi, tri) + _dot(rest, tri)
        a = jnp.exp(hit + after)
        if masked:
            a = jnp.where(strict, a, 0.0)
        contrib = jnp.exp(run) * _dot(a.astype(BF16), v_ref[0, pl.ds(start, tq), pp * LANES:(pp + 1) * LANES])
        return contrib, run + jnp.sum(keep, axis=1, keepdims=True)

    state = []
    has_prev = i > 0
    for pp in range(pairs):
        acc, run = chunk(pp, i, jnp.zeros((2 * tq, 1), F32), True)
        contrib, run_prev = chunk(pp, jnp.maximum(i - 1, 0), run, False)
        state += [jnp.where(has_prev, run_prev, run), acc + jnp.where(has_prev, contrib, 0.0)]

    def cond(carry):
        t = carry[0]
        live = carry[1]
        for pp in range(1, pairs):
            live = jnp.maximum(live, carry[1 + 2 * pp])
        return jnp.logical_and(t < i, jnp.max(live) > -EXP_UNDERFLOW)

    def body(carry):
        t = carry[0]
        out = [t + 1]
        for pp in range(pairs):
            contrib, run = chunk(pp, i - 1 - t, carry[1 + 2 * pp], False)
            out += [run, carry[2 + 2 * pp] + contrib]
        return tuple(out)

    final = lax.while_loop(cond, body, (jnp.int32(1), *state))
    for pp in range(pairs):
        acc = final[2 + 2 * pp]
        o_ref[0, :, pp * LANES:(pp + 1) * LANES] = jnp.where(lo, acc[:tq], acc[tq:]).astype(o_ref.dtype)


def stick_breaking_attention(h3, tri, *, batch, seq, tq, pairs):
    w = pairs * LANES
    return pl.pallas_call(
        functools.partial(_stick_kernel, tq=tq, pairs=pairs),
        grid=(batch, GROUP_WIDTH // w, seq // tq),
        in_specs=[pl.BlockSpec((1, tq, w), lambda b, g, i: (b, i, COL_QC // w + g)),
                  pl.BlockSpec((1, seq, w), lambda b, g, i: (b, 0, COL_KC // w + g)),
                  pl.BlockSpec((1, seq, w), lambda b, g, i: (b, 0, COL_VC // w + g)),
                  pl.BlockSpec((tq, tq), lambda b, g, i: (0, 0))],
        out_specs=pl.BlockSpec((1, tq, w), lambda b, g, i: (b, i, g)),
        out_shape=jax.ShapeDtypeStruct((batch, seq, GROUP_WIDTH), BF16),
        compiler_params=_params("arbitrary", "arbitrary", "arbitrary"),
        name="stick_breaking",
    )(h3, h3, h3, tri)


def _layer_norm(r, g, b):
    mu = jnp.mean(r, axis=-1, keepdims=True)
    d = r - mu
    var = jnp.mean(d * d, axis=-1, keepdims=True)
    return d * lax.rsqrt(var + LN_EPS) * g + b


def _out_proj_kernel(ga_ref, gb_ref, gc_ref, gd_ref, x_ref, mixg_ref, wo_ref, g_ref, b_ref, rt_ref,
                     xo_ref, xb_ref, lg_ref):
    mixed = None
    for gi, grp in enumerate((ga_ref, gb_ref, gc_ref, gd_ref)):
        xg = grp[...].astype(F32)
        y = xg * lax.rsqrt(jnp.mean(xg * xg, axis=-1, keepdims=True) + RMS_EPS) * mixg_ref[gi:gi + 1, :]
        part = _dot(y.astype(BF16), wo_ref[0, gi * GROUP_WIDTH:(gi + 1) * GROUP_WIDTH, :])
        mixed = part if mixed is None else mixed + part
    x1 = _layer_norm(ALPHA * x_ref[...].astype(F32) + mixed, g_ref[...], b_ref[...])
    xo_ref[...] = x1
    xb = x1.astype(BF16)
    xb_ref[...] = xb
    lg_ref[...] = _dot(xb, rt_ref[...])


def out_proj_ln(ga, gb, gc, gd, x, mix_g, wo, layer, ln_g, ln_b, router, *, tm):
    n = x.shape[0]
    row = lambda w: pl.BlockSpec((tm, w), lambda i: (i, 0))
    const = lambda shape: pl.BlockSpec(shape, lambda i: (0, 0))
    return pl.pallas_call(
        _out_proj_kernel,
        grid=(n // tm,),
        in_specs=[row(GROUP_WIDTH)] * 4 + [row(D_MODEL), const((4, GROUP_WIDTH)),
                                          pl.BlockSpec((1, D_MODEL, D_MODEL), lambda i: (layer, 0, 0)),
                                          const((1, D_MODEL)), const((1, D_MODEL)), const((D_MODEL, LANES))],
        out_specs=[row(D_MODEL), row(D_MODEL), row(LANES)],
        out_shape=[jax.ShapeDtypeStruct((n, D_MODEL), F32), jax.ShapeDtypeStruct((n, D_MODEL), BF16),
                   jax.ShapeDtypeStruct((n, LANES), F32)],
        compiler_params=_params("arbitrary"),
        name="out_proj_ln",
    )(ga, gb, gc, gd, x, mix_g, wo, ln_g, ln_b, router)


def _silu_mul(g, u):
    return g * (1.0 / (1.0 + jnp.exp(-g))) * u


def _gate_up_kernel(x_ref, wg_ref, wu_ref, o_ref, wgb_ref, wub_ref):
    @pl.when(pl.program_id(1) == 0)
    def _():
        wgb_ref[...] = wg_ref[0].astype(BF16)
        wub_ref[...] = wu_ref[0].astype(BF16)

    x = x_ref[...]
    o_ref[...] = _silu_mul(_dot(x, wgb_ref[...]), _dot(x, wub_ref[...])).astype(o_ref.dtype)


def ffn_gate_up(xb, wg, wu, li, *, tm, tf):
    n = xb.shape[0]
    f = wg.shape[2]
    return pl.pallas_call(
        _gate_up_kernel,
        grid=(f // tf, n // tm),
        in_specs=[pl.BlockSpec((tm, D_MODEL), lambda j, i: (i, 0)),
                  pl.BlockSpec((1, D_MODEL, tf), lambda j, i: (li, 0, j)),
                  pl.BlockSpec((1, D_MODEL, tf), lambda j, i: (li, 0, j))],
        out_specs=pl.BlockSpec((tm, tf), lambda j, i: (i, j)),
        out_shape=jax.ShapeDtypeStruct((n, f), BF16),
        scratch_shapes=[pltpu.VMEM((D_MODEL, tf), BF16), pltpu.VMEM((D_MODEL, tf), BF16)],
        compiler_params=_params("arbitrary", "arbitrary"),
        name="ffn_gate_up",
    )(xb, wg, wu)


def _down_ln_kernel(h_ref, wd_ref, x_ref, g_ref, b_ref, xo_ref, xb_ref, acc_ref):
    kk = pl.program_id(1)

    @pl.when(kk == 0)
    def _():
        acc_ref[...] = jnp.zeros_like(acc_ref)

    acc_ref[...] += _dot(h_ref[...], wd_ref[0])

    @pl.when(kk == pl.num_programs(1) - 1)
    def _():
        x2 = _layer_norm(ALPHA * x_ref[...] + acc_ref[...], g_ref[...], b_ref[...])
        xo_ref[...] = x2
        xb_ref[...] = x2.astype(BF16)


def ffn_down_ln(hmid, wd, li, x, ln_g, ln_b, *, tm, tk):
    n, f = hmid.shape
    return pl.pallas_call(
        _down_ln_kernel,
        grid=(n // tm, f // tk),
        in_specs=[pl.BlockSpec((tm, tk), lambda i, k: (i, k)),
                  pl.BlockSpec((1, tk, D_MODEL), lambda i, k: (li, k, 0)),
                  pl.BlockSpec((tm, D_MODEL), lambda i, k: (i, 0)),
                  pl.BlockSpec((1, D_MODEL), lambda i, k: (0, 0)),
                  pl.BlockSpec((1, D_MODEL), lambda i, k: (0, 0))],
        out_specs=[pl.BlockSpec((tm, D_MODEL), lambda i, k: (i, 0)),
                   pl.BlockSpec((tm, D_MODEL), lambda i, k: (i, 0))],
        out_shape=[jax.ShapeDtypeStruct((n, D_MODEL), F32), jax.ShapeDtypeStruct((n, D_MODEL), BF16)],
        scratch_shapes=[pltpu.VMEM((tm, D_MODEL), F32)],
        compiler_params=_params("arbitrary", "arbitrary"),
        name="ffn_down_ln",
    )(hmid, wd, x, ln_g, ln_b)


def _new_expert(te_ref, t):
    return jnp.logical_or(t == 0, te_ref[t] != te_ref[jnp.maximum(t - 1, 0)])


def _by_valid_rows(valid, tm, rows):
    half = tm // 2

    @pl.when(valid > half)
    def _():
        rows(tm)

    @pl.when(jnp.logical_and(valid > 0, valid <= half))
    def _():
        rows(half)

    @pl.when(valid == 0)
    def _():
        rows(0)


def _moe_gate_up_kernel(te_ref, tv_ref, x_ref, wg_ref, wu_ref, o_ref, wgb_ref, wub_ref):
    t = pl.program_id(1)

    @pl.when(_new_expert(te_ref, t))
    def _():
        wgb_ref[...] = wg_ref[0, 0].astype(BF16)
        wub_ref[...] = wu_ref[0, 0].astype(BF16)

    def rows(n):
        if n:
            x = x_ref[0:n, :]
            o_ref[0:n, :] = _silu_mul(_dot(x, wgb_ref[...]), _dot(x, wub_ref[...])).astype(o_ref.dtype)
        if n < o_ref.shape[0]:
            o_ref[n:, :] = jnp.zeros((o_ref.shape[0] - n, o_ref.shape[1]), o_ref.dtype)

    _by_valid_rows(tv_ref[t], x_ref.shape[0], rows)


def moe_gate_up(tile_expert, tile_rows, xs, wg, wu, li, *, tm, tf):
    p = xs.shape[0]
    f = wg.shape[3]
    grid_spec = pltpu.PrefetchScalarGridSpec(
        num_scalar_prefetch=2,
        grid=(f // tf, p // tm),
        in_specs=[pl.BlockSpec((tm, D_MODEL), lambda j, t, te, tv: (t, 0)),
                  pl.BlockSpec((1, 1, D_MODEL, tf), lambda j, t, te, tv: (li, te[t], 0, j)),
                  pl.BlockSpec((1, 1, D_MODEL, tf), lambda j, t, te, tv: (li, te[t], 0, j))],
        out_specs=pl.BlockSpec((tm, tf), lambda j, t, te, tv: (t, j)),
        scratch_shapes=[pltpu.VMEM((D_MODEL, tf), BF16), pltpu.VMEM((D_MODEL, tf), BF16)],
    )
    return pl.pallas_call(
        _moe_gate_up_kernel,
        grid_spec=grid_spec,
        out_shape=jax.ShapeDtypeStruct((p, f), BF16),
        compiler_params=_params("arbitrary", "arbitrary"),
        name="moe_gate_up",
    )(tile_expert, tile_rows, xs, wg, wu)


def _moe_down_kernel(te_ref, tv_ref, h_ref, wd_ref, o_ref, wdb_ref):
    t = pl.program_id(1)

    @pl.when(_new_expert(te_ref, t))
    def _():
        wdb_ref[...] = wd_ref[0, 0].astype(BF16)

    def rows(n):
        if n:
            o_ref[0:n, :] = _dot(h_ref[0:n, :], wdb_ref[...])
        if n < o_ref.shape[0]:
            o_ref[n:, :] = jnp.zeros((o_ref.shape[0] - n, o_ref.shape[1]), o_ref.dtype)

    _by_valid_rows(tv_ref[t], h_ref.shape[0], rows)


def moe_down(tile_expert, tile_rows, hs, wd, li, *, tm, tn):
    p, f = hs.shape
    grid_spec = pltpu.PrefetchScalarGridSpec(
        num_scalar_prefetch=2,
        grid=(D_MODEL // tn, p // tm),
        in_specs=[pl.BlockSpec((tm, f), lambda j, t, te, tv: (t, 0)),
                  pl.BlockSpec((1, 1, f, tn), lambda j, t, te, tv: (li, te[t], 0, j))],
        out_specs=pl.BlockSpec((tm, tn), lambda j, t, te, tv: (t, j)),
        scratch_shapes=[pltpu.VMEM((f, tn), BF16)],
    )
    return pl.pallas_call(
        _moe_down_kernel,
        grid_spec=grid_spec,
        out_shape=jax.ShapeDtypeStruct((p, D_MODEL), F32),
        compiler_params=_params("arbitrary", "arbitrary"),
        name="moe_down",
    )(tile_expert, tile_rows, hs, wd)


def _combine_ln_kernel(x_ref, y0_ref, y1_ref, gate_ref, g_ref, b_ref, xo_ref, xb_ref):
    f = gate_ref[:, 0:1] * y0_ref[...] + gate_ref[:, 1:2] * y1_ref[...]
    x2 = _layer_norm(ALPHA * x_ref[...] + f, g_ref[...], b_ref[...])
    xo_ref[...] = x2
    xb_ref[...] = x2.astype(BF16)


def combine_ln(x, y0, y1, gates, ln_g, ln_b, *, tm):
    n = x.shape[0]
    row = pl.BlockSpec((tm, D_MODEL), lambda i: (i, 0))
    const = pl.BlockSpec((1, D_MODEL), lambda i: (0, 0))
    return pl.pallas_call(
        _combine_ln_kernel,
        grid=(n // tm,),
        in_specs=[row, row, row, pl.BlockSpec((tm, TOP_K), lambda i: (i, 0)), const, const],
        out_specs=[row, row],
        out_shape=[jax.ShapeDtypeStruct((n, D_MODEL), F32), jax.ShapeDtypeStruct((n, D_MODEL), BF16)],
        compiler_params=_params("arbitrary"),
        name="combine_ln",
    )(x, y0, y1, gates, ln_g, ln_b)


def _t5_bucket(dist):
    max_exact = NUM_BUCKETS // 2
    d = jnp.maximum(dist, 1).astype(F32)
    large = max_exact + (jnp.log(d / max_exact) / math.log(T5_MAX_DISTANCE / max_exact)
                         * (NUM_BUCKETS - max_exact)).astype(jnp.int32)
    large = jnp.minimum(large, NUM_BUCKETS - 1)
    return jnp.where(dist < max_exact, dist, large)


def _band_bias_masked(table, stride, max_dist):
    dist = jnp.arange(BLOCK)[:, None] + BLOCK - jnp.arange(2 * BLOCK)[None, :]
    onehot = jax.nn.one_hot(_t5_bucket(jnp.maximum(dist, 0) * stride), NUM_BUCKETS, dtype=F32)
    bias = jnp.einsum("qkb,bh->hqk", onehot, table.astype(F32), precision=lax.Precision.HIGHEST)
    valid = (dist >= 0) & (dist <= max_dist)
    return jnp.where(valid[None], bias, NEG_INF)


def _rope_lane_tables(seq):
    half = B_ROPE_DIM // 2
    inv = ROPE_THETA ** (-jnp.arange(0, B_ROPE_DIM, 2, dtype=F32) / B_ROPE_DIM)
    ang = jnp.arange(seq, dtype=F32)[:, None] * inv[None, :]
    cos, sin = jnp.cos(ang), jnp.sin(ang)
    ones = jnp.ones((seq, ROPE_LANE0), F32)
    zeros = jnp.zeros((seq, ROPE_LANE0), F32)
    pad = jnp.zeros((seq, LANES - ROPE_LANE0 - 2 * half), F32)
    cos_t = jnp.concatenate([ones, cos, cos, pad], axis=1)
    sin_t = jnp.concatenate([zeros, -sin, sin, pad], axis=1)
    return cos_t, sin_t


def _permute_heads(t, axis, order):
    parts = [lax.slice_in_dim(t, hh * HEAD_DIM, (hh + 1) * HEAD_DIM, axis=axis) for hh in order]
    return jnp.concatenate(parts, axis=axis)


IN_SEGMENTS = (("qa", 512), ("ka", 128), ("va", 128), ("cq", 384), ("ckv", 256), ("kr", 32),
               ("qc", 512), ("kc", 512), ("vc", 512), ("qd", 512), ("kd", 512), ("vd", 512))
IN_WIDTH = sum(size for _, size in IN_SEGMENTS)


def _relayout_w_in_kernel(w_ref, main_ref, d_ref):
    w = w_ref[0]
    rows = w.shape[0]
    seg = {}
    start = 0
    for name, size in IN_SEGMENTS:
        seg[name] = w[:, start:start + size]
        start += size
    qa = jnp.concatenate([seg["qa"][:, hh * HEAD_DIM:(hh + 1) * HEAD_DIM] for hh in A_HEAD_ORDER], axis=1)
    kr = jnp.concatenate([jnp.zeros((rows, ROPE_LANE0), F32), seg["kr"],
                          jnp.zeros((rows, LANES - ROPE_LANE0 - B_ROPE_DIM), F32)], axis=1)
    main = jnp.concatenate([qa, seg["qc"], seg["kc"], seg["cq"], seg["ka"], seg["vc"], seg["ckv"], seg["va"], kr],
                           axis=1)
    main_ref[0] = main.astype(BF16)
    d_ref[0] = jnp.concatenate([seg["qd"], seg["kd"], seg["vd"]], axis=1).astype(BF16)


def _relayout_w_in(w_in, *, tr):
    d = w_in.shape[0]
    return pl.pallas_call(
        _relayout_w_in_kernel,
        grid=(d, D_MODEL // tr),
        in_specs=[pl.BlockSpec((1, tr, IN_WIDTH), lambda l, i: (l, i, 0))],
        out_specs=[pl.BlockSpec((1, tr, IN_WIDTH_P), lambda l, i: (l, i, 0)),
                   pl.BlockSpec((1, tr, IN_WIDTH_D), lambda l, i: (l, i, 0))],
        out_shape=[jax.ShapeDtypeStruct((d, D_MODEL, IN_WIDTH_P), BF16),
                   jax.ShapeDtypeStruct((d, D_MODEL, IN_WIDTH_D), BF16)],
        compiler_params=_params("arbitrary", "arbitrary"),
        name="relayout_w_in",
    )(w_in)


def _relayout_mla(w_uq, w_ukv):
    d = w_uq.shape[0]
    half = B_ROPE_DIM // 2
    wq = w_uq.astype(BF16).reshape(d, B_Q_LORA, N_HEADS, B_NOPE_DIM + B_ROPE_DIM)
    nope, r1, r2 = wq[..., :B_NOPE_DIM], wq[..., B_NOPE_DIM:B_NOPE_DIM + half], wq[..., B_NOPE_DIM + half:]
    z32 = jnp.zeros(wq.shape[:3] + (LANES - B_NOPE_DIM - B_ROPE_DIM,), BF16)
    z64 = jnp.zeros(wq.shape[:3] + (B_NOPE_DIM,), BF16)
    wq_t = jnp.concatenate([nope, r1, r2, z32], axis=-1).reshape(d, B_Q_LORA, N_HEADS * LANES)
    wq_s = jnp.concatenate([z64, r2, r1, z32], axis=-1).reshape(d, B_Q_LORA, N_HEADS * LANES)
    wkv = w_ukv.astype(BF16).reshape(d, B_KV_LORA, N_HEADS, 2 * HEAD_DIM)
    zk = jnp.zeros(wkv.shape[:3] + (LANES - B_NOPE_DIM,), BF16)
    wk_t = jnp.concatenate([wkv[..., :B_NOPE_DIM], zk], axis=-1).reshape(d, B_KV_LORA, N_HEADS * LANES)
    wv = wkv[..., B_NOPE_DIM:].reshape(d, B_KV_LORA, GROUP_WIDTH)
    return wq_t, wq_s, wk_t, wv


def _rope_swap_matrix():
    half = B_ROPE_DIM // 2
    src = jnp.arange(LANES)[:, None]
    dst = jnp.arange(LANES)[None, :]
    first = (dst >= ROPE_LANE0) & (dst < ROPE_LANE0 + half) & (src == dst + half)
    second = (dst >= ROPE_LANE0 + half) & (dst < ROPE_LANE0 + 2 * half) & (src == dst - half)
    return (first | second).astype(BF16)


def _route(logits, tm):
    n = logits.shape[0]
    top_logits, top_idx = lax.top_k(logits, TOP_K)
    gates = jax.nn.softmax(top_logits, axis=-1)
    onehot = jax.nn.one_hot(top_idx, N_EXPERTS, dtype=jnp.int32)
    member = jnp.sum(onehot, axis=1)
    rank = jnp.cumsum(member, axis=0) - member
    counts = jnp.sum(member, axis=0)
    padded = ((counts + tm - 1) // tm) * tm
    ends = jnp.cumsum(padded)
    starts = ends - padded
    pos = jnp.sum(onehot * (starts[None, None, :] + rank[:, None, :]), axis=-1)
    n_rows = TOP_K * n + N_EXPERTS * tm
    n_tiles = n_rows // tm
    src = (jnp.arange(n_rows, dtype=jnp.int32) % n).at[pos.reshape(-1)].set(
        jnp.repeat(jnp.arange(n, dtype=jnp.int32), TOP_K))
    tile_start = jnp.arange(n_tiles, dtype=jnp.int32) * tm
    tile_expert = jnp.minimum(jnp.sum((tile_start[:, None] >= ends[None, :]).astype(jnp.int32), axis=1),
                              N_EXPERTS - 1)
    onehot_te = jax.nn.one_hot(tile_expert, N_EXPERTS, dtype=jnp.int32)
    group_end = jnp.sum(onehot_te * (starts + counts)[None, :], axis=1)
    tile_rows = jnp.clip(group_end - tile_start, 0, tm).astype(jnp.int32)
    return pos, gates, src, tile_expert.astype(jnp.int32), tile_rows


def kernel(x, w_in, w_o, mla_q_norm, mla_kv_norm, mla_w_uq, mla_w_ukv, attn_sinks, rel_bias_table, mix_norm_g,
           ln1_g, ln1_b, ln2_g, ln2_b, ffn_w_gate, ffn_w_up, ffn_w_down, moe_router, moe_w_gate, moe_w_up,
           moe_w_down):
    batch, seq, _ = x.shape
    n = batch * seq

    w_in_p, w_in_d = _relayout_w_in(w_in, tr=256)
    w_o_p = jnp.concatenate([_permute_heads(w_o[:, :GROUP_WIDTH], 1, A_HEAD_ORDER), w_o[:, GROUP_WIDTH:]],
                            axis=1).astype(BF16)
    mix_g_p = jnp.concatenate([_permute_heads(mix_norm_g[:, :1], 2, A_HEAD_ORDER), mix_norm_g[:, 1:]], axis=1)
    wq_t, wq_s, wk_t, wv = _relayout_mla(mla_w_uq, mla_w_ukv)
    psw = _rope_swap_matrix()
    cos_t, sin_t = _rope_lane_tables(seq)
    tq = 256
    tri = (jnp.arange(tq)[:, None] > jnp.arange(tq)[None, :]).astype(BF16)
    order = list(A_HEAD_ORDER)
    def with_first_block_variant(b):
        prev = jnp.arange(2 * BLOCK) < BLOCK
        return jnp.stack([b, jnp.where(prev, NEG_INF, b)], axis=0)

    bias_a = _band_bias_masked(rel_bias_table[:, :N_HEADS], 1, A_WINDOW - 1)
    bias_a_rows = with_first_block_variant(jnp.concatenate([bias_a[hh] for hh in order], axis=0))
    sink_rows = jnp.concatenate([jnp.broadcast_to(attn_sinks[:, hh, None, None], (DEPTH, BLOCK, LANES))
                                 for hh in order], axis=1)
    biases_d = [with_first_block_variant(_band_bias_masked(rel_bias_table[:, N_HEADS:], rate, window // rate).reshape(
        N_HEADS // 2, 2 * BLOCK, 2 * BLOCK)) for window, rate in D_PATTERNS]
    router_p = jnp.pad(moe_router, ((0, 0), (0, 0), (0, LANES - N_EXPERTS))).astype(BF16)
    wd_d = ffn_w_down.astype(BF16)
    moe_tm = 512

    xf = x.reshape(n, D_MODEL)
    xb = xf
    for layer in range(DEPTH):
        h = matmul_ws(xb, w_in_p, layer, 1024, 1024, BF16)
        hd = matmul_ws(xb, w_in_d, layer, 1024, 512, F32)
        h3 = h.reshape(batch, seq, IN_WIDTH_P)
        ga = swa_attention(h3, bias_a_rows, sink_rows[layer], batch=batch, seq=seq)
        q_b, k_b, v_b = mla_up(h, mla_q_norm[layer][None], mla_kv_norm[layer][None], wq_t[layer], wq_s[layer],
                               wk_t[layer], wv[layer], psw, cos_t, sin_t, seq=seq, tm=512)
        gb = mla_attention(q_b.reshape(batch, seq, -1), k_b.reshape(batch, seq, -1),
                           v_b.reshape(batch, seq, -1), batch=batch, seq=seq, tq=tq, heads=4)
        gc = stick_breaking_attention(h3, tri, batch=batch, seq=seq, tq=tq, pairs=2)
        gd = dilated_attention(hd.reshape(batch, seq, IN_WIDTH_D), biases_d, batch=batch, seq=seq)
        i = layer // 2
        router = router_p[i] if layer % 2 == 1 else jnp.zeros((D_MODEL, LANES), BF16)
        xf, xb, logits = out_proj_ln(ga.reshape(n, -1), gb.reshape(n, -1), gc.reshape(n, -1), gd.reshape(n, -1),
                                     xf, mix_g_p[layer], w_o_p, layer, ln1_g[layer][None], ln1_b[layer][None],
                                     router, tm=256)
        if layer % 2 == 0:
            hmid = ffn_gate_up(xb, ffn_w_gate, ffn_w_up, i, tm=512, tf=512)
            xf, xb = ffn_down_ln(hmid, wd_d, i, xf, ln2_g[layer][None], ln2_b[layer][None], tm=512, tk=512)
        else:
            pos, gates, src, tile_expert, tile_rows = _route(logits[:, :N_EXPERTS], moe_tm)
            xs = jnp.take(xf, src, axis=0, mode="clip").astype(BF16)
            hs = moe_gate_up(tile_expert, tile_rows, xs, moe_w_gate, moe_w_up, i, tm=moe_tm, tf=512)
            ys = moe_down(tile_expert, tile_rows, hs, moe_w_down, i, tm=moe_tm, tn=512)
            y0 = jnp.take(ys, pos[:, 0], axis=0, mode="clip")
            y1 = jnp.take(ys, pos[:, 1], axis=0, mode="clip")
            xf, xb = combine_ln(xf, y0, y1, gates, ln2_g[layer][None], ln2_b[layer][None], tm=512)
    return xf.reshape(batch, seq, D_MODEL)
```

```python
import functools
import math

import jax
import jax.numpy as jnp
from jax import lax
from jax.experimental import pallas as pl
from jax.experimental.pallas import tpu as pltpu

D_MODEL = 2048
DEPTH = 4
HEAD_DIM = 64
N_HEADS = 8
GROUP_WIDTH = N_HEADS * HEAD_DIM
BLOCK = 128
A_KV_HEADS = 2
A_WINDOW = 128
B_NOPE_DIM = 64
B_ROPE_DIM = 32
B_Q_LORA = 384
B_KV_LORA = 256
ROPE_THETA = 10000.0
D_PATTERNS = ((128, 1), (512, 4), (2048, 16))
NUM_BUCKETS = 32
T5_MAX_DISTANCE = 2048
D_FF = 5632
N_EXPERTS = 8
TOP_K = 2
NEG_INF = -1e30
LN_EPS = 1e-5
RMS_EPS = 1e-6
ALPHA = (2 * DEPTH) ** 0.25

LANES = 128
VMEM_LIMIT = 56 * 1024 * 1024

COL_QA, COL_QC, COL_KC, COL_CQ, COL_KA = 0, 512, 1024, 1536, 1920
COL_VC, COL_CKV, COL_VA, COL_KR = 2048, 2560, 2816, 2944
IN_WIDTH_P = 3072
IN_WIDTH_D = 3 * GROUP_WIDTH
A_HEAD_ORDER = (0, 4, 1, 5, 2, 6, 3, 7)
ROPE_LANE0 = 64
EXP_UNDERFLOW = 105.0
QK_SCALE = HEAD_DIM ** -0.5

BF16 = jnp.bfloat16
F32 = jnp.float32


def _params(*sem):
    return pltpu.CompilerParams(dimension_semantics=sem, vmem_limit_bytes=VMEM_LIMIT)


def _dot(a, b):
    return jnp.dot(a, b, preferred_element_type=F32)


def _dot_nt(a, b):
    return lax.dot_general(a, b, (((1,), (1,)), ((), ())), preferred_element_type=F32)


def _lo_lanes(rows):
    return lax.broadcasted_iota(jnp.int32, (rows, LANES), 1) < HEAD_DIM


def _split_pair(t, lo):
    zero = jnp.zeros_like(t)
    return jnp.concatenate([jnp.where(lo, t, zero), jnp.where(lo, zero, t)], axis=0)


def _matmul_kernel(a_ref, w_ref, o_ref):
    o_ref[...] = _dot(a_ref[...].astype(BF16), w_ref[0]).astype(o_ref.dtype)


def matmul_ws(a, w, layer, tm, tn, out_dtype):
    m, k = a.shape
    n = w.shape[2]
    return pl.pallas_call(
        _matmul_kernel,
        grid=(n // tn, m // tm),
        in_specs=[pl.BlockSpec((tm, k), lambda j, i: (i, 0)),
                  pl.BlockSpec((1, k, tn), lambda j, i: (layer, 0, j))],
        out_specs=pl.BlockSpec((tm, tn), lambda j, i: (i, j)),
        out_shape=jax.ShapeDtypeStruct((m, n), out_dtype),
        compiler_params=_params("arbitrary", "arbitrary"),
        name="matmul_ws",
    )(a, w)


def _swa_kernel(q_ref, kp_ref, kc_ref, vp_ref, vc_ref, bias_ref, sink_ref, o_ref):
    lo = _lo_lanes(BLOCK)
    kb = jnp.concatenate([kp_ref[0], kc_ref[0]], axis=0)
    vb = jnp.concatenate([vp_ref[0], vc_ref[0]], axis=0)
    which = jnp.where(pl.program_id(1) == 0, 1, 0)
    for c in range(GROUP_WIDTH // LANES):
        rows = slice(2 * c * BLOCK, 2 * (c + 1) * BLOCK)
        qs = _split_pair(q_ref[0, :, c * LANES:(c + 1) * LANES], lo) * QK_SCALE
        s = _dot_nt(qs, kb) + bias_ref[which, rows, :]
        sink = sink_ref[rows, :]
        m = jnp.maximum(jnp.max(s, axis=1, keepdims=True), sink)
        p = jnp.exp(s - jnp.concatenate([m, m], axis=1))
        den = jnp.sum(p, axis=1, keepdims=True) + jnp.exp(sink - m)
        o = _dot(p.astype(BF16), vb) / den
        o_ref[0, :, c * LANES:(c + 1) * LANES] = jnp.where(lo, o[:BLOCK], o[BLOCK:]).astype(o_ref.dtype)


def swa_attention(h3, bias_rows, sink_rows, *, batch, seq):
    nb = seq // BLOCK
    kb, vb = COL_KA // LANES, COL_VA // LANES
    return pl.pallas_call(
        _swa_kernel,
        grid=(batch, nb),
        in_specs=[pl.BlockSpec((1, BLOCK, GROUP_WIDTH), lambda b, n: (b, n, COL_QA // GROUP_WIDTH)),
                  pl.BlockSpec((1, BLOCK, LANES), lambda b, n: (b, jnp.maximum(n - 1, 0), kb)),
                  pl.BlockSpec((1, BLOCK, LANES), lambda b, n: (b, n, kb)),
                  pl.BlockSpec((1, BLOCK, LANES), lambda b, n: (b, jnp.maximum(n - 1, 0), vb)),
                  pl.BlockSpec((1, BLOCK, LANES), lambda b, n: (b, n, vb)),
                  pl.BlockSpec((2, N_HEADS * BLOCK, 2 * BLOCK), lambda b, n: (0, 0, 0)),
                  pl.BlockSpec((N_HEADS * BLOCK, LANES), lambda b, n: (0, 0))],
        out_specs=pl.BlockSpec((1, BLOCK, GROUP_WIDTH), lambda b, n: (b, n, 0)),
        out_shape=jax.ShapeDtypeStruct((batch, seq, GROUP_WIDTH), BF16),
        compiler_params=_params("arbitrary", "arbitrary"),
        name="swa_attn",
    )(h3, h3, h3, h3, h3, bias_rows, sink_rows)


def _dilated_kernel(q_ref, k_ref, v_ref, b0_ref, b1_ref, b2_ref, o_ref, osc, lsc):
    seq = q_ref.shape[1]
    lo = _lo_lanes(BLOCK)
    for p,((_, rate), b_ref) in enumerate(zip(D_PATTERNS, (b0_ref, b1_ref, b2_ref))):
        nb = seq // (rate * BLOCK)

        def unit(u, carry, p=p, rate=rate, b_ref=b_ref, nb=nb):
            n = u // rate
            res = u - n * rate
            rows = pl.ds(n * (BLOCK * rate) + res, BLOCK, stride=rate)
            qs = _split_pair((q_ref[0, rows, :] * QK_SCALE).astype(BF16), lo)
            kc = k_ref[0, rows, :].astype(BF16)
            vc = v_ref[0, rows, :].astype(BF16)
            if nb > 1:
                prow = pl.ds(jnp.maximum(n - 1, 0) * (BLOCK * rate) + res, BLOCK, stride=rate)
                kb = jnp.concatenate([k_ref[0, prow, :].astype(BF16), kc], axis=0)
                vb = jnp.concatenate([v_ref[0, prow, :].astype(BF16), vc], axis=0)
                s = _dot_nt(qs, kb) + b_ref[jnp.where(n == 0, 1, 0), 0]
            else:
                vb = vc
                s = _dot_nt(qs, kc) + b_ref[0, 0, :, BLOCK:2 * BLOCK]
            m = jnp.max(s, axis=1, keepdims=True)
            e = jnp.exp(s - m)
            den = jnp.sum(e, axis=1, keepdims=True)
            o = _dot(e.astype(BF16), vb) / den
            lse = jnp.broadcast_to(m + jnp.log(den), (2 * BLOCK, LANES))
            osc[p, rows, :] = jnp.where(lo, o[:BLOCK], o[BLOCK:])
            lsc[p, rows, :] = jnp.where(lo, lse[:BLOCK], lse[BLOCK:])
            return carry

        lax.fori_loop(0, nb * rate, unit, 0, unroll=8)

    chunk = 2 * BLOCK

    def merge(t, carry):
        r = pl.ds(pl.multiple_of(t * chunk, chunk), chunk)
        l0, l1, l2 = lsc[0, r, :], lsc[1, r, :], lsc[2, r, :]
        m = jnp.maximum(jnp.maximum(l0, l1), l2)
        e0, e1, e2 = jnp.exp(l0 - m), jnp.exp(l1 - m), jnp.exp(l2 - m)
        o = (e0 * osc[0, r, :] + e1 * osc[1, r, :] + e2 * osc[2, r, :]) / (e0 + e1 + e2)
        o_ref[0, r, :] = o.astype(o_ref.dtype)
        return carry

    lax.fori_loop(0, seq // chunk, merge, 0)


def dilated_attention(hd3, biases, *, batch, seq):
    pairs = N_HEADS // 2
    bspec = pl.BlockSpec((2, 1, 2 * BLOCK, 2 * BLOCK), lambda b, p: (0, p, 0, 0))
    return pl.pallas_call(
        _dilated_kernel,
        grid=(batch, pairs),
        in_specs=[pl.BlockSpec((1, seq, LANES), lambda b, p: (b, 0, p)),
                  pl.BlockSpec((1, seq, LANES), lambda b, p: (b, 0, pairs + p)),
                  pl.BlockSpec((1, seq, LANES), lambda b, p: (b, 0, 2 * pairs + p)),
                  bspec, bspec, bspec],
        out_specs=pl.BlockSpec((1, seq, LANES), lambda b, p: (b, 0, p)),
        out_shape=jax.ShapeDtypeStruct((batch, seq, GROUP_WIDTH), BF16),
        scratch_shapes=[pltpu.VMEM((len(D_PATTERNS), seq, LANES), F32),
                        pltpu.VMEM((len(D_PATTERNS), seq, LANES), F32)],
        compiler_params=_params("arbitrary", "arbitrary"),
        name="dilated_attn",
    )(hd3, hd3, hd3, *biases)


def _mla_up_kernel(cq_ref, ckv_ref, kr_ref, qn_ref, kvn_ref, wq_ref, wqs_ref, wk_ref, wv_ref, psw_ref,
                   cos_ref, sin_ref, q_ref, k_ref, v_ref, *, scale):
    def rms(x_ref, g_ref):
        x = x_ref[...].astype(F32)
        return (x * lax.rsqrt(jnp.mean(x * x, axis=-1, keepdims=True) + RMS_EPS) * g_ref[...]).astype(BF16)

    xq = rms(cq_ref, qn_ref)
    xkv = rms(ckv_ref, kvn_ref)
    cos = cos_ref[...]
    sin = sin_ref[...]
    t = _dot(xq, wq_ref[...])
    ts = _dot(xq, wqs_ref[...])
    kn = _dot(xkv, wk_ref[...])
    kr = kr_ref[...]
    rk = kr.astype(F32) * cos + _dot(kr, psw_ref[...]) * sin
    for h in range(N_HEADS):
        sl = slice(h * LANES, (h + 1) * LANES)
        q_ref[:, sl] = ((t[:, sl] * cos + ts[:, sl] * sin) * scale).astype(BF16)
        k_ref[:, sl] = (kn[:, sl] + rk).astype(BF16)
    v_ref[...] = _dot(xkv, wv_ref[...]).astype(BF16)


def mla_up(h, q_norm, kv_norm, wq, wqs, wk, wv, psw, cos_t, sin_t, *, seq, tm):
    n = h.shape[0]
    w = N_HEADS * LANES
    const = lambda shape: pl.BlockSpec(shape, lambda i: (0, 0))
    spt = seq // tm
    return pl.pallas_call(
        functools.partial(_mla_up_kernel, scale=(B_NOPE_DIM + B_ROPE_DIM) ** -0.5 * math.log2(math.e)),
        grid=(n // tm,),
        in_specs=[pl.BlockSpec((tm, B_Q_LORA), lambda i: (i, COL_CQ // B_Q_LORA)),
                  pl.BlockSpec((tm, B_KV_LORA), lambda i: (i, COL_CKV // B_KV_LORA)),
                  pl.BlockSpec((tm, LANES), lambda i: (i, COL_KR // LANES)),
                  const((1, B_Q_LORA)), const((1, B_KV_LORA)),
                  const((B_Q_LORA, w)), const((B_Q_LORA, w)), const((B_KV_LORA, w)),
                  const((B_KV_LORA, GROUP_WIDTH)), const((LANES, LANES)),
                  pl.BlockSpec((tm, LANES), lambda i: (i % spt, 0)),
                  pl.BlockSpec((tm, LANES), lambda i: (i % spt, 0))],
        out_specs=[pl.BlockSpec((tm, w), lambda i: (i, 0)),
                   pl.BlockSpec((tm, w), lambda i: (i, 0)),
                   pl.BlockSpec((tm, GROUP_WIDTH), lambda i: (i, 0))],
        out_shape=[jax.ShapeDtypeStruct((n, w), BF16), jax.ShapeDtypeStruct((n, w), BF16),
                   jax.ShapeDtypeStruct((n, GROUP_WIDTH), BF16)],
        compiler_params=_params("arbitrary"),
        name="mla_up",
    )(h, h, h, q_norm, kv_norm, wq, wqs, wk, wv, psw, cos_t, sin_t)


def _mla_attn_kernel(q_ref, k_ref, v_ref, o_ref, *, tq, heads):
    i = pl.program_id(2)
    lo = _lo_lanes(tq)
    row = lax.broadcasted_iota(jnp.int32, (tq, tq), 0)
    col = lax.broadcasted_iota(jnp.int32, (tq, tq), 1)
    causal = col <= row

    def logits(j):
        start = pl.multiple_of(j * tq, tq)
        return tuple(_dot_nt(q_ref[0, :, hh * LANES:(hh + 1) * LANES],
                             k_ref[0, pl.ds(start, tq), hh * LANES:(hh + 1) * LANES]) for hh in range(heads))

    def update(j, stats, s_all, masked):
        start = pl.multiple_of(j * tq, tq)
        out = []
        for hh in range(heads):
            m, l, acc = stats[3 * hh:3 * hh + 3]
            s = s_all[hh]
            if masked:
                s = jnp.where(causal, s, NEG_INF)
            m_new = jnp.maximum(m, jnp.max(s, axis=1, keepdims=True))
            p = jnp.exp2(s - m_new)
            a = jnp.exp2(m - m_new)
            pv = _dot(p.astype(BF16), v_ref[0, pl.ds(start, tq), (hh // 2) * LANES:(hh // 2 + 1) * LANES])
            part = functools.reduce(jnp.add, [p[:, c * LANES:(c + 1) * LANES] for c in range(tq // LANES)])
            out += [m_new, a * l + part, a * acc + pv]
        return tuple(out)

    def body(j, carry):
        s_next = logits(j + 1)
        return update(j, carry[0], carry[1], False), s_next

    init = (jnp.full((tq, 1), NEG_INF, F32), jnp.zeros((tq, LANES), F32), jnp.zeros((tq, LANES), F32)) * heads
    stats, s_diag = lax.fori_loop(0, i, body, (init, logits(0)))
    stats = update(i, stats, s_diag, True)
    for pp in range(heads // 2):
        (_, l0, a0), (_, l1, a1) = stats[6 * pp:6 * pp + 3], stats[6 * pp + 3:6 * pp + 6]
        o0 = a0 / jnp.sum(l0, axis=1, keepdims=True)
        o1 = a1 / jnp.sum(l1, axis=1, keepdims=True)
        o_ref[0, :, pp * LANES:(pp + 1) * LANES] = jnp.where(lo, o0, o1).astype(o_ref.dtype)


def mla_attention(q, k, v, *, batch, seq, tq, heads):
    pairs = heads // 2
    return pl.pallas_call(
        functools.partial(_mla_attn_kernel, tq=tq, heads=heads),
        grid=(batch, N_HEADS // heads, seq // tq),
        in_specs=[pl.BlockSpec((1, tq, heads * LANES), lambda b, g, i: (b, i, g)),
                  pl.BlockSpec((1, seq, heads * LANES), lambda b, g, i: (b, 0, g)),
                  pl.BlockSpec((1, seq, pairs * LANES), lambda b, g, i: (b, 0, g))],
        out_specs=pl.BlockSpec((1, tq, pairs * LANES), lambda b, g, i: (b, i, g)),
        out_shape=jax.ShapeDtypeStruct((batch, seq, GROUP_WIDTH), BF16),
        compiler_params=_params("arbitrary", "arbitrary", "arbitrary"),
        name="mla_attn",
    )(q, k, v)


def _stick_kernel(q_ref, k_ref, v_ref, tri_ref, o_ref, *, tq, pairs):
    i = pl.program_id(2)
    lo = _lo_lanes(tq)
    row = lax.broadcasted_iota(jnp.int32, (2 * tq, tq), 0)
    col = lax.broadcasted_iota(jnp.int32, (2 * tq, tq), 1)
    strict = col < jnp.where(row >= tq, row - tq, row)
    tri = tri_ref[...]
    qs = [_split_pair(q_ref[0, :, pp * LANES:(pp + 1) * LANES], lo) * QK_SCALE for pp in range(pairs)]

    def chunk(pp, j, run, masked):
        start = pl.multiple_of(j * tq, tq)
        z = _dot_nt(qs[pp], k_ref[0, pl.ds(start, tq), pp * LANES:(pp + 1) * LANES])
        soft = jnp.log(1.0 + jnp.exp(-jnp.abs(z)))
        keep = -(jnp.maximum(z, 0.0) + soft)
        hit = jnp.minimum(z, 0.0) - soft
        if masked:
            keep = jnp.where(strict, keep, 0.0)
        hi = keep.astype(BF16)
        rest = (keep - hi.astype(F32)).astype(BF16)
        after = _dot(hi, tri) + _dot(rest, tri)
        a = jnp.exp(hit + after)
        if masked:
            a = jnp.where(strict, a, 0.0)
        contrib = jnp.exp(run) * _dot(a.astype(BF16), v_ref[0, pl.ds(start, tq), pp * LANES:(pp + 1) * LANES])
        return contrib, run + jnp.sum(keep, axis=1, keepdims=True)

    state = []
    has_prev = i > 0
    for pp in range(pairs):
        acc, run = chunk(pp, i, jnp.zeros((2 * tq, 1), F32), True)
        contrib, run_prev = chunk(pp, jnp.maximum(i - 1, 0), run, False)
        state += [jnp.where(has_prev, run_prev, run), acc + jnp.where(has_prev, contrib, 0.0)]

    def cond(carry):
        t = carry[0]
        live = carry[1]
        for pp in range(1, pairs):
            live = jnp.maximum(live, carry[1 + 2 * pp])
        return jnp.logical_and(t < i, jnp.max(live) > -EXP_UNDERFLOW)

    def body(carry):
        t = carry[0]
        out = [t + 1]
        for pp in range(pairs):
            contrib, run = chunk(pp, i - 1 - t, carry[1 + 2 * pp], False)
            out += [run, carry[2 + 2 * pp] + contrib]
        return tuple(out)

    final = lax.while_loop(cond, body, (jnp.int32(1), *state))
    for pp in range(pairs):
        acc = final[2 + 2 * pp]
        o_ref[0, :, pp * LANES:(pp + 1) * LANES] = jnp.where(lo, acc[:tq], acc[tq:]).astype(o_ref.dtype)


def stick_breaking_attention(h3, tri, *, batch, seq, tq, pairs):
    w = pairs * LANES
    return pl.pallas_call(
        functools.partial(_stick_kernel, tq=tq, pairs=pairs),
        grid=(batch, GROUP_WIDTH // w, seq // tq),
        in_specs=[pl.BlockSpec((1, tq, w), lambda b, g, i: (b, i, COL_QC // w + g)),
                  pl.BlockSpec((1, seq, w), lambda b, g, i: (b, 0, COL_KC // w + g)),
                  pl.BlockSpec((1, seq, w), lambda b, g, i: (b, 0, COL_VC // w + g)),
                  pl.BlockSpec((tq, tq), lambda b, g, i: (0, 0))],
        out_specs=pl.BlockSpec((1, tq, w), lambda b, g, i: (b, i, g)),
        out_shape=jax.ShapeDtypeStruct((batch, seq, GROUP_WIDTH), BF16),
        compiler_params=_params("arbitrary", "arbitrary", "arbitrary"),
        name="stick_breaking",
    )(h3, h3, h3, tri)


def _layer_norm(r, g, b):
    mu = jnp.mean(r, axis=-1, keepdims=True)
    d = r - mu
    var = jnp.mean(d * d, axis=-1, keepdims=True)
    return d * lax.rsqrt(var + LN_EPS) * g + b


def _out_proj_kernel(ga_ref, gb_ref, gc_ref, gd_ref, x_ref, mixg_ref, wo_ref, g_ref, b_ref, rt_ref,
                     xo_ref, xb_ref, lg_ref):
    mixed = None
    for gi, grp in enumerate((ga_ref, gb_ref, gc_ref, gd_ref)):
        xg = grp[...].astype(F32)
        y = xg * lax.rsqrt(jnp.mean(xg * xg, axis=-1, keepdims=True) + RMS_EPS) * mixg_ref[gi:gi + 1, :]
        part = _dot(y.astype(BF16), wo_ref[0, gi * GROUP_WIDTH:(gi + 1) * GROUP_WIDTH, :])
        mixed = part if mixed is None else mixed + part
    x1 = _layer_norm(ALPHA * x_ref[...].astype(F32) + mixed, g_ref[...], b_ref[...])
    xo_ref[...] = x1
    xb = x1.astype(BF16)
    xb_ref[...] = xb
    lg_ref[...] = _dot(xb, rt_ref[...])


def out_proj_ln(ga, gb, gc, gd, x, mix_g, wo, layer, ln_g, ln_b, router, *, tm):
    n = x.shape[0]
    row = lambda w: pl.BlockSpec((tm, w), lambda i: (i, 0))
    const = lambda shape: pl.BlockSpec(shape, lambda i: (0, 0))
    return pl.pallas_call(
        _out_proj_kernel,
        grid=(n // tm,),
        in_specs=[row(GROUP_WIDTH)] * 4 + [row(D_MODEL), const((4, GROUP_WIDTH)),
                                          pl.BlockSpec((1, D_MODEL, D_MODEL), lambda i: (layer, 0, 0)),
                                          const((1, D_MODEL)), const((1, D_MODEL)), const((D_MODEL, LANES))],
        out_specs=[row(D_MODEL), row(D_MODEL), row(LANES)],
        out_shape=[jax.ShapeDtypeStruct((n, D_MODEL), F32), jax.ShapeDtypeStruct((n, D_MODEL), BF16),
                   jax.ShapeDtypeStruct((n, LANES), F32)],
        compiler_params=_params("arbitrary"),
        name="out_proj_ln",
    )(ga, gb, gc, gd, x, mix_g, wo, ln_g, ln_b, router)


def _silu_mul(g, u):
    return g * (1.0 / (1.0 + jnp.exp(-g))) * u


def _gate_up_kernel(x_ref, wg_ref, wu_ref, o_ref, wgb_ref, wub_ref):
    @pl.when(pl.program_id(1) == 0)
    def _():
        wgb_ref[...] = wg_ref[0].astype(BF16)
        wub_ref[...] = wu_ref[0].astype(BF16)

    x = x_ref[...]
    o_ref[...] = _silu_mul(_dot(x, wgb_ref[...]), _dot(x, wub_ref[...])).astype(o_ref.dtype)


def ffn_gate_up(xb, wg, wu, li, *, tm, tf):
    n = xb.shape[0]
    f = wg.shape[2]
    return pl.pallas_call(
        _gate_up_kernel,
        grid=(f // tf, n // tm),
        in_specs=[pl.BlockSpec((tm, D_MODEL), lambda j, i: (i, 0)),
                  pl.BlockSpec((1, D_MODEL, tf), lambda j, i: (li, 0, j)),
                  pl.BlockSpec((1, D_MODEL, tf), lambda j, i: (li, 0, j))],
        out_specs=pl.BlockSpec((tm, tf), lambda j, i: (i, j)),
        out_shape=jax.ShapeDtypeStruct((n, f), BF16),
        scratch_shapes=[pltpu.VMEM((D_MODEL, tf), BF16), pltpu.VMEM((D_MODEL, tf), BF16)],
        compiler_params=_params("arbitrary", "arbitrary"),
        name="ffn_gate_up",
    )(xb, wg, wu)


def _down_ln_kernel(h_ref, wd_ref, x_ref, g_ref, b_ref, xo_ref, xb_ref, acc_ref):
    kk = pl.program_id(1)

    @pl.when(kk == 0)
    def _():
        acc_ref[...] = jnp.zeros_like(acc_ref)

    acc_ref[...] += _dot(h_ref[...], wd_ref[0])

    @pl.when(kk == pl.num_programs(1) - 1)
    def _():
        x2 = _layer_norm(ALPHA * x_ref[...] + acc_ref[...], g_ref[...], b_ref[...])
        xo_ref[...] = x2
        xb_ref[...] = x2.astype(BF16)


def ffn_down_ln(hmid, wd, li, x, ln_g, ln_b, *, tm, tk):
    n, f = hmid.shape
    return pl.pallas_call(
        _down_ln_kernel,
        grid=(n // tm, f // tk),
        in_specs=[pl.BlockSpec((tm, tk), lambda i, k: (i, k)),
                  pl.BlockSpec((1, tk, D_MODEL), lambda i, k: (li, k, 0)),
                  pl.BlockSpec((tm, D_MODEL), lambda i, k: (i, 0)),
                  pl.BlockSpec((1, D_MODEL), lambda i, k: (0, 0)),
                  pl.BlockSpec((1, D_MODEL), lambda i, k: (0, 0))],
        out_specs=[pl.BlockSpec((tm, D_MODEL), lambda i, k: (i, 0)),
                   pl.BlockSpec((tm, D_MODEL), lambda i, k: (i, 0))],
        out_shape=[jax.ShapeDtypeStruct((n, D_MODEL), F32), jax.ShapeDtypeStruct((n, D_MODEL), BF16)],
        scratch_shapes=[pltpu.VMEM((tm, D_MODEL), F32)],
        compiler_params=_params("arbitrary", "arbitrary"),
        name="ffn_down_ln",
    )(hmid, wd, x, ln_g, ln_b)


def _cast_rows_kernel(tr_ref, x_ref, o_ref):
    @pl.when(tr_ref[pl.program_id(0)] > 0)
    def _():
        o_ref[...] = x_ref[...].astype(o_ref.dtype)

    @pl.when(tr_ref[pl.program_id(0)] == 0)
    def _():
        o_ref[...] = jnp.zeros_like(o_ref)


def cast_rows_bf16(tile_rows, xs, *, tm):
    p, d = xs.shape
    grid_spec = pltpu.PrefetchScalarGridSpec(
        num_scalar_prefetch=1,
        grid=(p // tm,),
        in_specs=[pl.BlockSpec((tm, d), lambda t, tr: (jnp.where(tr[t] > 0, t, 0), 0))],
        out_specs=pl.BlockSpec((tm, d), lambda t, tr: (t, 0)),
    )
    return pl.pallas_call(
        _cast_rows_kernel,
        grid_spec=grid_spec,
        out_shape=jax.ShapeDtypeStruct((p, d), BF16),
        compiler_params=_params("arbitrary"),
        name="cast_rows",
    )(tile_rows, xs)


def _new_expert(te_ref, t):
    return jnp.logical_or(t == 0, te_ref[t] != te_ref[jnp.maximum(t - 1, 0)])


MOE_ROW_STEP = 256


def _by_valid_rows(valid, tm, rows):
    for n in range(0, tm + 1, MOE_ROW_STEP):
        @pl.when(jnp.logical_and(valid > n - MOE_ROW_STEP, valid <= n))
        def _(n=n):
            rows(n)


def _moe_gate_up_kernel(te_ref, tv_ref, x_ref, wg_ref, wu_ref, o_ref, wgb_ref, wub_ref):
    t = pl.program_id(1)

    @pl.when(_new_expert(te_ref, t))
    def _():
        wgb_ref[...] = wg_ref[0, 0].astype(BF16)
        wub_ref[...] = wu_ref[0, 0].astype(BF16)

    def rows(n):
        if n:
            x = x_ref[0:n, :]
            o_ref[0:n, :] = _silu_mul(_dot(x, wgb_ref[...]), _dot(x, wub_ref[...])).astype(o_ref.dtype)
        if n < o_ref.shape[0]:
            o_ref[n:, :] = jnp.zeros((o_ref.shape[0] - n, o_ref.shape[1]), o_ref.dtype)

    _by_valid_rows(tv_ref[t], x_ref.shape[0], rows)


def moe_gate_up(tile_expert, tile_rows, xs, wg, wu, li, *, tm, tf):
    p = xs.shape[0]
    f = wg.shape[3]
    grid_spec = pltpu.PrefetchScalarGridSpec(
        num_scalar_prefetch=2,
        grid=(f // tf, p // tm),
        in_specs=[pl.BlockSpec((tm, D_MODEL), lambda j, t, te, tv: (t, 0)),
                  pl.BlockSpec((1, 1, D_MODEL, tf), lambda j, t, te, tv: (li, te[t], 0, j)),
                  pl.BlockSpec((1, 1, D_MODEL, tf), lambda j, t, te, tv: (li, te[t], 0, j))],
        out_specs=pl.BlockSpec((tm, tf), lambda j, t, te, tv: (t, j)),
        scratch_shapes=[pltpu.VMEM((D_MODEL, tf), BF16), pltpu.VMEM((D_MODEL, tf), BF16)],
    )
    return pl.pallas_call(
        _moe_gate_up_kernel,
        grid_spec=grid_spec,
        out_shape=jax.ShapeDtypeStruct((p, f), BF16),
        compiler_params=_params("arbitrary", "arbitrary"),
        name="moe_gate_up",
    )(tile_expert, tile_rows, xs, wg, wu)


def _moe_down_kernel(te_ref, tv_ref, h_ref, wd_ref, o_ref, wdb_ref):
    t = pl.program_id(1)

    @pl.when(_new_expert(te_ref, t))
    def _():
        wdb_ref[...] = wd_ref[0, 0].astype(BF16)

    def rows(n):
        if n:
            o_ref[0:n, :] = _dot(h_ref[0:n, :], wdb_ref[...])
        if n < o_ref.shape[0]:
            o_ref[n:, :] = jnp.zeros((o_ref.shape[0] - n, o_ref.shape[1]), o_ref.dtype)

    _by_valid_rows(tv_ref[t], h_ref.shape[0], rows)


def moe_down(tile_expert, tile_rows, hs, wd, li, *, tm, tn):
    p, f = hs.shape
    grid_spec = pltpu.PrefetchScalarGridSpec(
        num_scalar_prefetch=2,
        grid=(D_MODEL // tn, p // tm),
        in_specs=[pl.BlockSpec((tm, f), lambda j, t, te, tv: (t, 0)),
                  pl.BlockSpec((1, 1, f, tn), lambda j, t, te, tv: (li, te[t], 0, j))],
        out_specs=pl.BlockSpec((tm, tn), lambda j, t, te, tv: (t, j)),
        scratch_shapes=[pltpu.VMEM((f, tn), BF16)],
    )
    return pl.pallas_call(
        _moe_down_kernel,
        grid_spec=grid_spec,
        out_shape=jax.ShapeDtypeStruct((p, D_MODEL), F32),
        compiler_params=_params("arbitrary", "arbitrary"),
        name="moe_down",
    )(tile_expert, tile_rows, hs, wd)


def _combine_ln_kernel(x_ref, y0_ref, y1_ref, gate_ref, g_ref, b_ref, xo_ref, xb_ref):
    f = gate_ref[:, 0:1] * y0_ref[...] + gate_ref[:, 1:2] * y1_ref[...]
    x2 = _layer_norm(ALPHA * x_ref[...] + f, g_ref[...], b_ref[...])
    xo_ref[...] = x2
    xb_ref[...] = x2.astype(BF16)


def combine_ln(x, y0, y1, gates, ln_g, ln_b, *, tm):
    n = x.shape[0]
    row = pl.BlockSpec((tm, D_MODEL), lambda i: (i, 0))
    const = pl.BlockSpec((1, D_MODEL), lambda i: (0, 0))
    return pl.pallas_call(
        _combine_ln_kernel,
        grid=(n // tm,),
        in_specs=[row, row, row, pl.BlockSpec((tm, TOP_K), lambda i: (i, 0)), const, const],
        out_specs=[row, row],
        out_shape=[jax.ShapeDtypeStruct((n, D_MODEL), F32), jax.ShapeDtypeStruct((n, D_MODEL), BF16)],
        compiler_params=_params("arbitrary"),
        name="combine_ln",
    )(x, y0, y1, gates, ln_g, ln_b)


def _t5_bucket(dist):
    max_exact = NUM_BUCKETS // 2
    d = jnp.maximum(dist, 1).astype(F32)
    large = max_exact + (jnp.log(d / max_exact) / math.log(T5_MAX_DISTANCE / max_exact)
                         * (NUM_BUCKETS - max_exact)).astype(jnp.int32)
    large = jnp.minimum(large, NUM_BUCKETS - 1)
    return jnp.where(dist < max_exact, dist, large)


def _band_bias_masked(table, stride, max_dist):
    dist = jnp.arange(BLOCK)[:, None] + BLOCK - jnp.arange(2 * BLOCK)[None, :]
    onehot = jax.nn.one_hot(_t5_bucket(jnp.maximum(dist, 0) * stride), NUM_BUCKETS, dtype=F32)
    bias = jnp.einsum("qkb,bh->hqk", onehot, table.astype(F32), precision=lax.Precision.HIGHEST)
    valid = (dist >= 0) & (dist <= max_dist)
    return jnp.where(valid[None], bias, NEG_INF)


def _rope_lane_tables(seq):
    half = B_ROPE_DIM // 2
    inv = ROPE_THETA ** (-jnp.arange(0, B_ROPE_DIM, 2, dtype=F32) / B_ROPE_DIM)
    ang = jnp.arange(seq, dtype=F32)[:, None] * inv[None, :]
    cos, sin = jnp.cos(ang), jnp.sin(ang)
    ones = jnp.ones((seq, ROPE_LANE0), F32)
    zeros = jnp.zeros((seq, ROPE_LANE0), F32)
    pad = jnp.zeros((seq, LANES - ROPE_LANE0 - 2 * half), F32)
    cos_t = jnp.concatenate([ones, cos, cos, pad], axis=1)
    sin_t = jnp.concatenate([zeros, -sin, sin, pad], axis=1)
    return cos_t, sin_t


def _permute_heads(t, axis, order):
    parts = [lax.slice_in_dim(t, hh * HEAD_DIM, (hh + 1) * HEAD_DIM, axis=axis) for hh in order]
    return jnp.concatenate(parts, axis=axis)


IN_SEGMENTS = (("qa", 512), ("ka", 128), ("va", 128), ("cq", 384), ("ckv", 256), ("kr", 32),
               ("qc", 512), ("kc", 512), ("vc", 512), ("qd", 512), ("kd", 512), ("vd", 512))
IN_WIDTH = sum(size for _, size in IN_SEGMENTS)


def _relayout_w_in_kernel(w_ref, main_ref, d_ref):
    w = w_ref[0]
    rows = w.shape[0]
    seg = {}
    start = 0
    for name, size in IN_SEGMENTS:
        seg[name] = w[:, start:start + size]
        start += size
    qa = jnp.concatenate([seg["qa"][:, hh * HEAD_DIM:(hh + 1) * HEAD_DIM] for hh in A_HEAD_ORDER], axis=1)
    kr = jnp.concatenate([jnp.zeros((rows, ROPE_LANE0), F32), seg["kr"],
                          jnp.zeros((rows, LANES - ROPE_LANE0 - B_ROPE_DIM), F32)], axis=1)
    main = jnp.concatenate([qa, seg["qc"], seg["kc"], seg["cq"], seg["ka"], seg["vc"], seg["ckv"], seg["va"], kr],
                           axis=1)
    main_ref[0] = main.astype(BF16)
    d_ref[0] = jnp.concatenate([seg["qd"], seg["kd"], seg["vd"]], axis=1).astype(BF16)


def _relayout_w_in(w_in, *, tr):
    d = w_in.shape[0]
    return pl.pallas_call(
        _relayout_w_in_kernel,
        grid=(d, D_MODEL // tr),
        in_specs=[pl.BlockSpec((1, tr, IN_WIDTH), lambda l, i: (l, i, 0))],
        out_specs=[pl.BlockSpec((1, tr, IN_WIDTH_P), lambda l, i: (l, i, 0)),
                   pl.BlockSpec((1, tr, IN_WIDTH_D), lambda l, i: (l, i, 0))],
        out_shape=[jax.ShapeDtypeStruct((d, D_MODEL, IN_WIDTH_P), BF16),
                   jax.ShapeDtypeStruct((d, D_MODEL, IN_WIDTH_D), BF16)],
        compiler_params=_params("arbitrary", "arbitrary"),
        name="relayout_w_in",
    )(w_in)


def _relayout_mla(w_uq, w_ukv):
    d = w_uq.shape[0]
    half = B_ROPE_DIM // 2
    wq = w_uq.astype(BF16).reshape(d, B_Q_LORA, N_HEADS, B_NOPE_DIM + B_ROPE_DIM)
    nope, r1, r2 = wq[..., :B_NOPE_DIM], wq[..., B_NOPE_DIM:B_NOPE_DIM + half], wq[..., B_NOPE_DIM + half:]
    z32 = jnp.zeros(wq.shape[:3] + (LANES - B_NOPE_DIM - B_ROPE_DIM,), BF16)
    z64 = jnp.zeros(wq.shape[:3] + (B_NOPE_DIM,), BF16)
    wq_t = jnp.concatenate([nope, r1, r2, z32], axis=-1).reshape(d, B_Q_LORA, N_HEADS * LANES)
    wq_s = jnp.concatenate([z64, r2, r1, z32], axis=-1).reshape(d, B_Q_LORA, N_HEADS * LANES)
    wkv = w_ukv.astype(BF16).reshape(d, B_KV_LORA, N_HEADS, 2 * HEAD_DIM)
    zk = jnp.zeros(wkv.shape[:3] + (LANES - B_NOPE_DIM,), BF16)
    wk_t = jnp.concatenate([wkv[..., :B_NOPE_DIM], zk], axis=-1).reshape(d, B_KV_LORA, N_HEADS * LANES)
    wv = wkv[..., B_NOPE_DIM:].reshape(d, B_KV_LORA, GROUP_WIDTH)
    return wq_t, wq_s, wk_t, wv


def _rope_swap_matrix():
    half = B_ROPE_DIM // 2
    src = jnp.arange(LANES)[:, None]
    dst = jnp.arange(LANES)[None, :]
    first = (dst >= ROPE_LANE0) & (dst < ROPE_LANE0 + half) & (src == dst + half)
    second = (dst >= ROPE_LANE0 + half) & (dst < ROPE_LANE0 + 2 * half) & (src == dst - half)
    return (first | second).astype(BF16)


def _route(logits, tm):
    n = logits.shape[0]
    top_logits, top_idx = lax.top_k(logits, TOP_K)
    gates = jax.nn.softmax(top_logits, axis=-1)
    onehot = jax.nn.one_hot(top_idx, N_EXPERTS, dtype=jnp.int32)
    member = jnp.sum(onehot, axis=1)
    rank = jnp.cumsum(member, axis=0) - member
    counts = jnp.sum(member, axis=0)
    padded = ((counts + tm - 1) // tm) * tm
    ends = jnp.cumsum(padded)
    starts = ends - padded
    pos = jnp.sum(onehot * (starts[None, None, :] + rank[:, None, :]), axis=-1)
    n_rows = TOP_K * n + N_EXPERTS * tm
    src = (jnp.arange(n_rows, dtype=jnp.int32) % n).at[pos.reshape(-1)].set(
        jnp.repeat(jnp.arange(n, dtype=jnp.int32), TOP_K))

    def tile_tables(tile):
        tile_start = jnp.arange(n_rows // tile, dtype=jnp.int32) * tile
        tile_expert = jnp.minimum(jnp.sum((tile_start[:, None] >= ends[None, :]).astype(jnp.int32), axis=1),
                                  N_EXPERTS - 1)
        onehot_te = jax.nn.one_hot(tile_expert, N_EXPERTS, dtype=jnp.int32)
        group_end = jnp.sum(onehot_te * (starts + counts)[None, :], axis=1)
        return tile_expert.astype(jnp.int32), jnp.clip(group_end - tile_start, 0, tile).astype(jnp.int32)

    return pos, gates, src, tile_tables


def kernel(x, w_in, w_o, mla_q_norm, mla_kv_norm, mla_w_uq, mla_w_ukv, attn_sinks, rel_bias_table, mix_norm_g,
           ln1_g, ln1_b, ln2_g, ln2_b, ffn_w_gate, ffn_w_up, ffn_w_down, moe_router, moe_w_gate, moe_w_up,
           moe_w_down):
    batch, seq, _ = x.shape
    n = batch * seq

    w_in_p, w_in_d = _relayout_w_in(w_in, tr=256)
    w_o_p = jnp.concatenate([_permute_heads(w_o[:, :GROUP_WIDTH], 1, A_HEAD_ORDER), w_o[:, GROUP_WIDTH:]],
                            axis=1).astype(BF16)
    mix_g_p = jnp.concatenate([_permute_heads(mix_norm_g[:, :1], 2, A_HEAD_ORDER), mix_norm_g[:, 1:]], axis=1)
    wq_t, wq_s, wk_t, wv = _relayout_mla(mla_w_uq, mla_w_ukv)
    psw = _rope_swap_matrix()
    cos_t, sin_t = _rope_lane_tables(seq)
    tq = 256
    tri = (jnp.arange(tq)[:, None] > jnp.arange(tq)[None, :]).astype(BF16)
    order = list(A_HEAD_ORDER)
    def with_first_block_variant(b):
        prev = jnp.arange(2 * BLOCK) < BLOCK
        return jnp.stack([b, jnp.where(prev, NEG_INF, b)], axis=0)

    bias_a = _band_bias_masked(rel_bias_table[:, :N_HEADS], 1, A_WINDOW - 1)
    bias_a_rows = with_first_block_variant(jnp.concatenate([bias_a[hh] for hh in order], axis=0))
    sink_rows = jnp.concatenate([jnp.broadcast_to(attn_sinks[:, hh, None, None], (DEPTH, BLOCK, LANES))
                                 for hh in order], axis=1)
    biases_d = [with_first_block_variant(_band_bias_masked(rel_bias_table[:, N_HEADS:], rate, window // rate).reshape(
        N_HEADS // 2, 2 * BLOCK, 2 * BLOCK)) for window, rate in D_PATTERNS]
    router_p = jnp.pad(moe_router, ((0, 0), (0, 0), (0, LANES - N_EXPERTS))).astype(BF16)
    wd_d = ffn_w_down.astype(BF16)
    moe_tm = 1024

    xf = x.reshape(n, D_MODEL)
    xb = xf
    for layer in range(DEPTH):
        h = matmul_ws(xb, w_in_p, layer, 1024, 1024, BF16)
        hd = matmul_ws(xb, w_in_d, layer, 1024, 512, F32)
        h3 = h.reshape(batch, seq, IN_WIDTH_P)
        ga = swa_attention(h3, bias_a_rows, sink_rows[layer], batch=batch, seq=seq)
        q_b, k_b, v_b = mla_up(h, mla_q_norm[layer][None], mla_kv_norm[layer][None], wq_t[layer], wq_s[layer],
                               wk_t[layer], wv[layer], psw, cos_t, sin_t, seq=seq, tm=512)
        gb = mla_attention(q_b.reshape(batch, seq, -1), k_b.reshape(batch, seq, -1),
                           v_b.reshape(batch, seq, -1), batch=batch, seq=seq, tq=tq, heads=4)
        gc = stick_breaking_attention(h3, tri, batch=batch, seq=seq, tq=tq, pairs=2)
        gd = dilated_attention(hd.reshape(batch, seq, IN_WIDTH_D), biases_d, batch=batch, seq=seq)
        i = layer // 2
        router = router_p[i] if layer % 2 == 1 else jnp.zeros((D_MODEL, LANES), BF16)
        xf, xb, logits = out_proj_ln(ga.reshape(n, -1), gb.reshape(n, -1), gc.reshape(n, -1), gd.reshape(n, -1),
                                     xf, mix_g_p[layer], w_o_p, layer, ln1_g[layer][None], ln1_b[layer][None],
                                     router, tm=256)
        if layer % 2 == 0:
            hmid = ffn_gate_up(xb, ffn_w_gate, ffn_w_up, i, tm=1024, tf=512)
            xf, xb = ffn_down_ln(hmid, wd_d, i, xf, ln2_g[layer][None], ln2_b[layer][None], tm=512, tk=512)
        else:
            pos, gates, src, tile_tables = _route(logits[:, :N_EXPERTS], moe_tm)
            te_up, tr_up = tile_tables(moe_tm)
            te_dn, tr_dn = tile_tables(moe_tm // 2)
            xs = cast_rows_bf16(tr_up, jnp.take(xf, src, axis=0, mode="clip"), tm=moe_tm)
            hs = moe_gate_up(te_up, tr_up, xs, moe_w_gate, moe_w_up, i, tm=moe_tm, tf=512)
            ys = moe_down(te_dn, tr_dn, hs, moe_w_down, i, tm=moe_tm // 2, tn=512)
            y0 = jnp.take(ys, pos[:, 0], axis=0, mode="clip")
            y1 = jnp.take(ys, pos[:, 1], axis=0, mode="clip")
            xf, xb = combine_ln(xf, y0, y1, gates, ln2_g[layer][None], ln2_b[layer][None], tm=512)
    return xf.reshape(batch, seq, D_MODEL)
```

```python
import functools
import math

import jax
import jax.numpy as jnp
from jax import lax
from jax.experimental import pallas as pl
from jax.experimental.pallas import tpu as pltpu

D_MODEL = 2048
DEPTH = 4
HEAD_DIM = 64
N_HEADS = 8
GROUP_WIDTH = N_HEADS * HEAD_DIM
BLOCK = 128
A_KV_HEADS = 2
A_WINDOW = 128
B_NOPE_DIM = 64
B_ROPE_DIM = 32
B_Q_LORA = 384
B_KV_LORA = 256
ROPE_THETA = 10000.0
D_PATTERNS = ((128, 1), (512, 4), (2048, 16))
NUM_BUCKETS = 32
T5_MAX_DISTANCE = 2048
D_FF = 5632
N_EXPERTS = 8
TOP_K = 2
NEG_INF = -1e30
LN_EPS = 1e-5
RMS_EPS = 1e-6
ALPHA = (2 * DEPTH) ** 0.25

LANES = 128
VMEM_LIMIT = 56 * 1024 * 1024

COL_QA, COL_QC, COL_KC, COL_CQ, COL_KA = 0, 512, 1024, 1536, 1920
COL_VC, COL_CKV, COL_VA, COL_KR = 2048, 2560, 2816, 2944
IN_WIDTH_P = 3072
IN_WIDTH_D = 3 * GROUP_WIDTH
A_HEAD_ORDER = (0, 4, 1, 5, 2, 6, 3, 7)
ROPE_LANE0 = 64
LOG2E = math.log2(math.e)
EXP2_UNDERFLOW = 151.0
QK_SCALE = HEAD_DIM ** -0.5 * LOG2E

BF16 = jnp.bfloat16
F32 = jnp.float32


def _params(*sem):
    return pltpu.CompilerParams(dimension_semantics=sem, vmem_limit_bytes=VMEM_LIMIT)


def _dot(a, b):
    return jnp.dot(a, b, preferred_element_type=F32)


def _dot_nt(a, b):
    return lax.dot_general(a, b, (((1,), (1,)), ((), ())), preferred_element_type=F32)


def _lo_lanes(rows):
    return lax.broadcasted_iota(jnp.int32, (rows, LANES), 1) < HEAD_DIM


def _split_pair(t, lo):
    zero = jnp.zeros_like(t)
    return jnp.concatenate([jnp.where(lo, t, zero), jnp.where(lo, zero, t)], axis=0)


def _matmul_kernel(a_ref, w_ref, o_ref):
    o_ref[...] = _dot(a_ref[...].astype(BF16), w_ref[0]).astype(o_ref.dtype)


def matmul_ws(a, w, layer, tm, tn, out_dtype):
    m, k = a.shape
    n = w.shape[2]
    return pl.pallas_call(
        _matmul_kernel,
        grid=(n // tn, m // tm),
        in_specs=[pl.BlockSpec((tm, k), lambda j, i: (i, 0)),
                  pl.BlockSpec((1, k, tn), lambda j, i: (layer, 0, j))],
        out_specs=pl.BlockSpec((tm, tn), lambda j, i: (i, j)),
        out_shape=jax.ShapeDtypeStruct((m, n), out_dtype),
        compiler_params=_params("arbitrary", "arbitrary"),
        name="matmul_ws",
    )(a, w)


def _swa_kernel(q_ref, kp_ref, kc_ref, vp_ref, vc_ref, bias_ref, sink_ref, o_ref):
    lo = _lo_lanes(BLOCK)
    kb = jnp.concatenate([kp_ref[0], kc_ref[0]], axis=0)
    vb = jnp.concatenate([vp_ref[0], vc_ref[0]], axis=0)
    which = jnp.where(pl.program_id(1) == 0, 1, 0)
    for c in range(GROUP_WIDTH // LANES):
        rows = slice(2 * c * BLOCK, 2 * (c + 1) * BLOCK)
        qs = _split_pair(q_ref[0, :, c * LANES:(c + 1) * LANES], lo) * QK_SCALE
        s = _dot_nt(qs, kb) + bias_ref[which, rows, :]
        sink = sink_ref[rows, :]
        m = jnp.maximum(jnp.max(s, axis=1, keepdims=True), sink)
        p = jnp.exp2(s - jnp.concatenate([m, m], axis=1))
        den = jnp.sum(p, axis=1, keepdims=True) + jnp.exp2(sink - m)
        o = _dot(p.astype(BF16), vb) / den
        o_ref[0, :, c * LANES:(c + 1) * LANES] = jnp.where(lo, o[:BLOCK], o[BLOCK:]).astype(o_ref.dtype)


def swa_attention(h3, bias_rows, sink_rows, *, batch, seq):
    nb = seq // BLOCK
    kb, vb = COL_KA // LANES, COL_VA // LANES
    return pl.pallas_call(
        _swa_kernel,
        grid=(batch, nb),
        in_specs=[pl.BlockSpec((1, BLOCK, GROUP_WIDTH), lambda b, n: (b, n, COL_QA // GROUP_WIDTH)),
                  pl.BlockSpec((1, BLOCK, LANES), lambda b, n: (b, jnp.maximum(n - 1, 0), kb)),
                  pl.BlockSpec((1, BLOCK, LANES), lambda b, n: (b, n, kb)),
                  pl.BlockSpec((1, BLOCK, LANES), lambda b, n: (b, jnp.maximum(n - 1, 0), vb)),
                  pl.BlockSpec((1, BLOCK, LANES), lambda b, n: (b, n, vb)),
                  pl.BlockSpec((2, N_HEADS * BLOCK, 2 * BLOCK), lambda b, n: (0, 0, 0)),
                  pl.BlockSpec((N_HEADS * BLOCK, LANES), lambda b, n: (0, 0))],
        out_specs=pl.BlockSpec((1, BLOCK, GROUP_WIDTH), lambda b, n: (b, n, 0)),
        out_shape=jax.ShapeDtypeStruct((batch, seq, GROUP_WIDTH), BF16),
        compiler_params=_params("arbitrary", "arbitrary"),
        name="swa_attn",
    )(h3, h3, h3, h3, h3, bias_rows, sink_rows)


def _dilated_kernel(q_ref, k_ref, v_ref, b0_ref, b1_ref, b2_ref, o_ref, osc, lsc):
    seq = q_ref.shape[1]
    lo = _lo_lanes(BLOCK)
    for p,((_, rate), b_ref) in enumerate(zip(D_PATTERNS, (b0_ref, b1_ref, b2_ref))):
        nb = seq // (rate * BLOCK)

        def unit(u, carry, p=p, rate=rate, b_ref=b_ref, nb=nb):
            n = u // rate
            res = u - n * rate
            rows = pl.ds(n * (BLOCK * rate) + res, BLOCK, stride=rate)
            qs = _split_pair((q_ref[0, rows, :] * QK_SCALE).astype(BF16), lo)
            kc = k_ref[0, rows, :].astype(BF16)
            vc = v_ref[0, rows, :].astype(BF16)
            if nb > 1:
                prow = pl.ds(jnp.maximum(n - 1, 0) * (BLOCK * rate) + res, BLOCK, stride=rate)
                kb = jnp.concatenate([k_ref[0, prow, :].astype(BF16), kc], axis=0)
                vb = jnp.concatenate([v_ref[0, prow, :].astype(BF16), vc], axis=0)
                s = _dot_nt(qs, kb) + b_ref[jnp.where(n == 0, 1, 0), 0]
            else:
                vb = vc
                s = _dot_nt(qs, kc) + b_ref[0, 0, :, BLOCK:2 * BLOCK]
            m = jnp.max(s, axis=1, keepdims=True)
            e = jnp.exp2(s - m)
            den = jnp.sum(e, axis=1, keepdims=True)
            o = _dot(e.astype(BF16), vb) / den
            lse = jnp.broadcast_to(m + jnp.log2(den), (2 * BLOCK, LANES))
            osc[p, rows, :] = jnp.where(lo, o[:BLOCK], o[BLOCK:])
            lsc[p, rows, :] = jnp.where(lo, lse[:BLOCK], lse[BLOCK:])
            return carry

        lax.fori_loop(0, nb * rate, unit, 0, unroll=16)

    chunk = 2 * BLOCK

    def merge(t, carry):
        r = pl.ds(pl.multiple_of(t * chunk, chunk), chunk)
        l0, l1, l2 = lsc[0, r, :], lsc[1, r, :], lsc[2, r, :]
        m = jnp.maximum(jnp.maximum(l0, l1), l2)
        e0, e1, e2 = jnp.exp2(l0 - m), jnp.exp2(l1 - m), jnp.exp2(l2 - m)
        o = (e0 * osc[0, r, :] + e1 * osc[1, r, :] + e2 * osc[2, r, :]) / (e0 + e1 + e2)
        o_ref[0, r, :] = o.astype(o_ref.dtype)
        return carry

    lax.fori_loop(0, seq // chunk, merge, 0)


def dilated_attention(hd3, biases, *, batch, seq):
    pairs = N_HEADS // 2
    bspec = pl.BlockSpec((2, 1, 2 * BLOCK, 2 * BLOCK), lambda b, p: (0, p, 0, 0))
    return pl.pallas_call(
        _dilated_kernel,
        grid=(batch, pairs),
        in_specs=[pl.BlockSpec((1, seq, LANES), lambda b, p: (b, 0, p)),
                  pl.BlockSpec((1, seq, LANES), lambda b, p: (b, 0, pairs + p)),
                  pl.BlockSpec((1, seq, LANES), lambda b, p: (b, 0, 2 * pairs + p)),
                  bspec, bspec, bspec],
        out_specs=pl.BlockSpec((1, seq, LANES), lambda b, p: (b, 0, p)),
        out_shape=jax.ShapeDtypeStruct((batch, seq, GROUP_WIDTH), BF16),
        scratch_shapes=[pltpu.VMEM((len(D_PATTERNS), seq, LANES), F32),
                        pltpu.VMEM((len(D_PATTERNS), seq, LANES), F32)],
        compiler_params=_params("arbitrary", "arbitrary"),
        name="dilated_attn",
    )(hd3, hd3, hd3, *biases)


def _mla_up_kernel(cq_ref, ckv_ref, kr_ref, qn_ref, kvn_ref, wq_ref, wqs_ref, wk_ref, wv_ref, psw_ref,
                   cos_ref, sin_ref, q_ref, k_ref, v_ref, *, scale):
    def rms(x_ref, g_ref):
        x = x_ref[...].astype(F32)
        return (x * lax.rsqrt(jnp.mean(x * x, axis=-1, keepdims=True) + RMS_EPS) * g_ref[...]).astype(BF16)

    xq = rms(cq_ref, qn_ref)
    xkv = rms(ckv_ref, kvn_ref)
    cos = cos_ref[...]
    sin = sin_ref[...]
    t = _dot(xq, wq_ref[...])
    ts = _dot(xq, wqs_ref[...])
    kn = _dot(xkv, wk_ref[...])
    kr = kr_ref[...]
    rk = kr.astype(F32) * cos + _dot(kr, psw_ref[...]) * sin
    for h in range(N_HEADS):
        sl = slice(h * LANES, (h + 1) * LANES)
        q_ref[:, sl] = ((t[:, sl] * cos + ts[:, sl] * sin) * scale).astype(BF16)
        k_ref[:, sl] = (kn[:, sl] + rk).astype(BF16)
    v_ref[...] = _dot(xkv, wv_ref[...]).astype(BF16)


def mla_up(h, q_norm, kv_norm, wq, wqs, wk, wv, psw, cos_t, sin_t, *, seq, tm):
    n = h.shape[0]
    w = N_HEADS * LANES
    const = lambda shape: pl.BlockSpec(shape, lambda i: (0, 0))
    spt = seq // tm
    return pl.pallas_call(
        functools.partial(_mla_up_kernel, scale=(B_NOPE_DIM + B_ROPE_DIM) ** -0.5 * math.log2(math.e)),
        grid=(n // tm,),
        in_specs=[pl.BlockSpec((tm, B_Q_LORA), lambda i: (i, COL_CQ // B_Q_LORA)),
                  pl.BlockSpec((tm, B_KV_LORA), lambda i: (i, COL_CKV // B_KV_LORA)),
                  pl.BlockSpec((tm, LANES), lambda i: (i, COL_KR // LANES)),
                  const((1, B_Q_LORA)), const((1, B_KV_LORA)),
                  const((B_Q_LORA, w)), const((B_Q_LORA, w)), const((B_KV_LORA, w)),
                  const((B_KV_LORA, GROUP_WIDTH)), const((LANES, LANES)),
                  pl.BlockSpec((tm, LANES), lambda i: (i % spt, 0)),
                  pl.BlockSpec((tm, LANES), lambda i: (i % spt, 0))],
        out_specs=[pl.BlockSpec((tm, w), lambda i: (i, 0)),
                   pl.BlockSpec((tm, w), lambda i: (i, 0)),
                   pl.BlockSpec((tm, GROUP_WIDTH), lambda i: (i, 0))],
        out_shape=[jax.ShapeDtypeStruct((n, w), BF16), jax.ShapeDtypeStruct((n, w), BF16),
                   jax.ShapeDtypeStruct((n, GROUP_WIDTH), BF16)],
        compiler_params=_params("arbitrary"),
        name="mla_up",
    )(h, h, h, q_norm, kv_norm, wq, wqs, wk, wv, psw, cos_t, sin_t)


def _mla_attn_kernel(q_ref, k_ref, v_ref, o_ref, *, tq, heads):
    i = pl.program_id(2)
    lo = _lo_lanes(tq)
    row = lax.broadcasted_iota(jnp.int32, (tq, tq), 0)
    col = lax.broadcasted_iota(jnp.int32, (tq, tq), 1)
    causal = col <= row

    def logits(j):
        start = pl.multiple_of(j * tq, tq)
        return tuple(_dot_nt(q_ref[0, :, hh * LANES:(hh + 1) * LANES],
                             k_ref[0, pl.ds(start, tq), hh * LANES:(hh + 1) * LANES]) for hh in range(heads))

    def update(j, stats, s_all, masked):
        start = pl.multiple_of(j * tq, tq)
        out = []
        for hh in range(heads):
            m, l, acc = stats[3 * hh:3 * hh + 3]
            s = s_all[hh]
            if masked:
                s = jnp.where(causal, s, NEG_INF)
            m_new = jnp.maximum(m, jnp.max(s, axis=1, keepdims=True))
            p = jnp.exp2(s - m_new)
            a = jnp.exp2(m - m_new)
            pv = _dot(p.astype(BF16), v_ref[0, pl.ds(start, tq), (hh // 2) * LANES:(hh // 2 + 1) * LANES])
            part = functools.reduce(jnp.add, [p[:, c * LANES:(c + 1) * LANES] for c in range(tq // LANES)])
            out += [m_new, a * l + part, a * acc + pv]
        return tuple(out)

    def body(j, carry):
        s_next = logits(j + 1)
        return update(j, carry[0], carry[1], False), s_next

    init = (jnp.full((tq, 1), NEG_INF, F32), jnp.zeros((tq, LANES), F32), jnp.zeros((tq, LANES), F32)) * heads
    stats, s_diag = lax.fori_loop(0, i, body, (init, logits(0)))
    stats = update(i, stats, s_diag, True)
    for pp in range(heads // 2):
        (_, l0, a0), (_, l1, a1) = stats[6 * pp:6 * pp + 3], stats[6 * pp + 3:6 * pp + 6]
        o0 = a0 / jnp.sum(l0, axis=1, keepdims=True)
        o1 = a1 / jnp.sum(l1, axis=1, keepdims=True)
        o_ref[0, :, pp * LANES:(pp + 1) * LANES] = jnp.where(lo, o0, o1).astype(o_ref.dtype)


def mla_attention(q, k, v, *, batch, seq, tq, heads):
    pairs = heads // 2
    return pl.pallas_call(
        functools.partial(_mla_attn_kernel, tq=tq, heads=heads),
        grid=(batch, N_HEADS // heads, seq // tq),
        in_specs=[pl.BlockSpec((1, tq, heads * LANES), lambda b, g, i: (b, i, g)),
                  pl.BlockSpec((1, seq, heads * LANES), lambda b, g, i: (b, 0, g)),
                  pl.BlockSpec((1, seq, pairs * LANES), lambda b, g, i: (b, 0, g))],
        out_specs=pl.BlockSpec((1, tq, pairs * LANES), lambda b, g, i: (b, i, g)),
        out_shape=jax.ShapeDtypeStruct((batch, seq, GROUP_WIDTH), BF16),
        compiler_params=_params("arbitrary", "arbitrary", "arbitrary"),
        name="mla_attn",
    )(q, k, v)


def _stick_kernel(q_ref, k_ref, v_ref, tri_ref, o_ref, *, tq, pairs):
    i = pl.program_id(2)
    lo = _lo_lanes(tq)
    row = lax.broadcasted_iota(jnp.int32, (2 * tq, tq), 0)
    col = lax.broadcasted_iota(jnp.int32, (2 * tq, tq), 1)
    strict = col < jnp.where(row >= tq, row - tq, row)
    tri = tri_ref[...]
    qs = [_split_pair(q_ref[0, :, pp * LANES:(pp + 1) * LANES], lo) * QK_SCALE for pp in range(pairs)]

    def chunk(pp, j, run, masked):
        start = pl.multiple_of(j * tq, tq)
        z = _dot_nt(qs[pp], k_ref[0, pl.ds(start, tq), pp * LANES:(pp + 1) * LANES])
        soft = jnp.log2(1.0 + jnp.exp2(-jnp.abs(z)))
        keep = -(jnp.maximum(z, 0.0) + soft)
        hit = jnp.minimum(z, 0.0) - soft
        if masked:
            keep = jnp.where(strict, keep, 0.0)
        hi = keep.astype(BF16)
        rest = (keep - hi.astype(F32)).astype(BF16)
        after = _dot(hi, tri) + _dot(rest, tri)
        a = jnp.exp2(hit + after)
        if masked:
            a = jnp.where(strict, a, 0.0)
        contrib = jnp.exp2(run) * _dot(a.astype(BF16), v_ref[0, pl.ds(start, tq), pp * LANES:(pp + 1) * LANES])
        return contrib, run + jnp.sum(keep, axis=1, keepdims=True)

    state = []
    has_prev = i > 0
    for pp in range(pairs):
        acc, run = chunk(pp, i, jnp.zeros((2 * tq, 1), F32), True)
        contrib, run_prev = chunk(pp, jnp.maximum(i - 1, 0), run, False)
        state += [jnp.where(has_prev, run_prev, run), acc + jnp.where(has_prev, contrib, 0.0)]

    def cond(carry):
        t = carry[0]
        live = carry[1]
        for pp in range(1, pairs):
            live = jnp.maximum(live, carry[1 + 2 * pp])
        return jnp.logical_and(t < i, jnp.max(live) > -EXP2_UNDERFLOW)

    def body(carry):
        t = carry[0]
        out = [t + 1]
        for pp in range(pairs):
            contrib, run = chunk(pp, i - 1 - t, carry[1 + 2 * pp], False)
            out += [run, carry[2 + 2 * pp] + contrib]
        return tuple(out)

    final = lax.while_loop(cond, body, (jnp.int32(1), *state))
    for pp in range(pairs):
        acc = final[2 + 2 * pp]
        o_ref[0, :, pp * LANES:(pp + 1) * LANES] = jnp.where(lo, acc[:tq], acc[tq:]).astype(o_ref.dtype)


def stick_breaking_attention(h3, tri, *, batch, seq, tq, pairs):
    w = pairs * LANES
    return pl.pallas_call(
        functools.partial(_stick_kernel, tq=tq, pairs=pairs),
        grid=(batch, GROUP_WIDTH // w, seq // tq),
        in_specs=[pl.BlockSpec((1, tq, w), lambda b, g, i: (b, i, COL_QC // w + g)),
                  pl.BlockSpec((1, seq, w), lambda b, g, i: (b, 0, COL_KC // w + g)),
                  pl.BlockSpec((1, seq, w), lambda b, g, i: (b, 0, COL_VC // w + g)),
                  pl.BlockSpec((tq, tq), lambda b, g, i: (0, 0))],
        out_specs=pl.BlockSpec((1, tq, w), lambda b, g, i: (b, i, g)),
        out_shape=jax.ShapeDtypeStruct((batch, seq, GROUP_WIDTH), BF16),
        compiler_params=_params("arbitrary", "arbitrary", "arbitrary"),
        name="stick_breaking",
    )(h3, h3, h3, tri)


def _layer_norm(r, g, b):
    mu = jnp.mean(r, axis=-1, keepdims=True)
    d = r - mu
    var = jnp.mean(d * d, axis=-1, keepdims=True)
    return d * lax.rsqrt(var + LN_EPS) * g + b


def _out_proj_kernel(ga_ref, gb_ref, gc_ref, gd_ref, x_ref, mixg_ref, wo_ref, g_ref, b_ref, rt_ref,
                     xo_ref, xb_ref, lg_ref):
    mixed = None
    for gi, grp in enumerate((ga_ref, gb_ref, gc_ref, gd_ref)):
        xg = grp[...].astype(F32)
        y = xg * lax.rsqrt(jnp.mean(xg * xg, axis=-1, keepdims=True) + RMS_EPS) * mixg_ref[gi:gi + 1, :]
        part = _dot(y.astype(BF16), wo_ref[0, gi * GROUP_WIDTH:(gi + 1) * GROUP_WIDTH, :])
        mixed = part if mixed is None else mixed + part
    x1 = _layer_norm(ALPHA * x_ref[...].astype(F32) + mixed, g_ref[...], b_ref[...])
    xo_ref[...] = x1
    xb = x1.astype(BF16)
    xb_ref[...] = xb
    lg_ref[...] = _dot(xb, rt_ref[...])


def out_proj_ln(ga, gb, gc, gd, x, mix_g, wo, layer, ln_g, ln_b, router, *, tm):
    n = x.shape[0]
    row = lambda w: pl.BlockSpec((tm, w), lambda i: (i, 0))
    const = lambda shape: pl.BlockSpec(shape, lambda i: (0, 0))
    return pl.pallas_call(
        _out_proj_kernel,
        grid=(n // tm,),
        in_specs=[row(GROUP_WIDTH)] * 4 + [row(D_MODEL), const((4, GROUP_WIDTH)),
                                          pl.BlockSpec((1, D_MODEL, D_MODEL), lambda i: (layer, 0, 0)),
                                          const((1, D_MODEL)), const((1, D_MODEL)), const((D_MODEL, LANES))],
        out_specs=[row(D_MODEL), row(D_MODEL), row(LANES)],
        out_shape=[jax.ShapeDtypeStruct((n, D_MODEL), F32), jax.ShapeDtypeStruct((n, D_MODEL), BF16),
                   jax.ShapeDtypeStruct((n, LANES), F32)],
        compiler_params=_params("arbitrary"),
        name="out_proj_ln",
    )(ga, gb, gc, gd, x, mix_g, wo, ln_g, ln_b, router)


def _silu_mul(g, u):
    return g * (1.0 / (1.0 + jnp.exp(-g))) * u


def _gate_up_kernel(x_ref, wg_ref, wu_ref, o_ref, wgb_ref, wub_ref):
    @pl.when(pl.program_id(1) == 0)
    def _():
        wgb_ref[...] = wg_ref[0].astype(BF16)
        wub_ref[...] = wu_ref[0].astype(BF16)

    x = x_ref[...]
    o_ref[...] = _silu_mul(_dot(x, wgb_ref[...]), _dot(x, wub_ref[...])).astype(o_ref.dtype)


def ffn_gate_up(xb, wg, wu, li, *, tm, tf):
    n = xb.shape[0]
    f = wg.shape[2]
    return pl.pallas_call(
        _gate_up_kernel,
        grid=(f // tf, n // tm),
        in_specs=[pl.BlockSpec((tm, D_MODEL), lambda j, i: (i, 0)),
                  pl.BlockSpec((1, D_MODEL, tf), lambda j, i: (li, 0, j)),
                  pl.BlockSpec((1, D_MODEL, tf), lambda j, i: (li, 0, j))],
        out_specs=pl.BlockSpec((tm, tf), lambda j, i: (i, j)),
        out_shape=jax.ShapeDtypeStruct((n, f), BF16),
        scratch_shapes=[pltpu.VMEM((D_MODEL, tf), BF16), pltpu.VMEM((D_MODEL, tf), BF16)],
        compiler_params=_params("arbitrary", "arbitrary"),
        name="ffn_gate_up",
    )(xb, wg, wu)


def _down_ln_kernel(h_ref, wd_ref, x_ref, g_ref, b_ref, xo_ref, xb_ref, acc_ref):
    kk = pl.program_id(1)

    @pl.when(kk == 0)
    def _():
        acc_ref[...] = jnp.zeros_like(acc_ref)

    acc_ref[...] += _dot(h_ref[...], wd_ref[0])

    @pl.when(kk == pl.num_programs(1) - 1)
    def _():
        x2 = _layer_norm(ALPHA * x_ref[...] + acc_ref[...], g_ref[...], b_ref[...])
        xo_ref[...] = x2
        xb_ref[...] = x2.astype(BF16)


def ffn_down_ln(hmid, wd, li, x, ln_g, ln_b, *, tm, tk):
    n, f = hmid.shape
    return pl.pallas_call(
        _down_ln_kernel,
        grid=(n // tm, f // tk),
        in_specs=[pl.BlockSpec((tm, tk), lambda i, k: (i, k)),
                  pl.BlockSpec((1, tk, D_MODEL), lambda i, k: (li, k, 0)),
                  pl.BlockSpec((tm, D_MODEL), lambda i, k: (i, 0)),
                  pl.BlockSpec((1, D_MODEL), lambda i, k: (0, 0)),
                  pl.BlockSpec((1, D_MODEL), lambda i, k: (0, 0))],
        out_specs=[pl.BlockSpec((tm, D_MODEL), lambda i, k: (i, 0)),
                   pl.BlockSpec((tm, D_MODEL), lambda i, k: (i, 0))],
        out_shape=[jax.ShapeDtypeStruct((n, D_MODEL), F32), jax.ShapeDtypeStruct((n, D_MODEL), BF16)],
        scratch_shapes=[pltpu.VMEM((tm, D_MODEL), F32)],
        compiler_params=_params("arbitrary", "arbitrary"),
        name="ffn_down_ln",
    )(hmid, wd, x, ln_g, ln_b)


def _cast_rows_kernel(tr_ref, x_ref, o_ref):
    @pl.when(tr_ref[pl.program_id(0)] > 0)
    def _():
        o_ref[...] = x_ref[...].astype(o_ref.dtype)

    @pl.when(tr_ref[pl.program_id(0)] == 0)
    def _():
        o_ref[...] = jnp.zeros_like(o_ref)


def cast_rows_bf16(tile_rows, xs, *, tm):
    p, d = xs.shape
    grid_spec = pltpu.PrefetchScalarGridSpec(
        num_scalar_prefetch=1,
        grid=(p // tm,),
        in_specs=[pl.BlockSpec((tm, d), lambda t, tr: (jnp.where(tr[t] > 0, t, 0), 0))],
        out_specs=pl.BlockSpec((tm, d), lambda t, tr: (t, 0)),
    )
    return pl.pallas_call(
        _cast_rows_kernel,
        grid_spec=grid_spec,
        out_shape=jax.ShapeDtypeStruct((p, d), BF16),
        compiler_params=_params("arbitrary"),
        name="cast_rows",
    )(tile_rows, xs)


def _new_expert(te_ref, t):
    return jnp.logical_or(t == 0, te_ref[t] != te_ref[jnp.maximum(t - 1, 0)])


MOE_ROW_STEP = 256


def _by_valid_rows(valid, tm, rows):
    for n in range(0, tm + 1, MOE_ROW_STEP):
        @pl.when(jnp.logical_and(valid > n - MOE_ROW_STEP, valid <= n))
        def _(n=n):
            rows(n)


def _moe_gate_up_kernel(te_ref, tv_ref, x_ref, wg_ref, wu_ref, o_ref, wgb_ref, wub_ref):
    t = pl.program_id(1)

    @pl.when(_new_expert(te_ref, t))
    def _():
        wgb_ref[...] = wg_ref[0, 0].astype(BF16)
        wub_ref[...] = wu_ref[0, 0].astype(BF16)

    def rows(n):
        if n:
            x = x_ref[0:n, :]
            o_ref[0:n, :] = _silu_mul(_dot(x, wgb_ref[...]), _dot(x, wub_ref[...])).astype(o_ref.dtype)
        if n < o_ref.shape[0]:
            o_ref[n:, :] = jnp.zeros((o_ref.shape[0] - n, o_ref.shape[1]), o_ref.dtype)

    _by_valid_rows(tv_ref[t], x_ref.shape[0], rows)


def moe_gate_up(tile_expert, tile_rows, xs, wg, wu, li, *, tm, tf):
    p = xs.shape[0]
    f = wg.shape[3]
    grid_spec = pltpu.PrefetchScalarGridSpec(
        num_scalar_prefetch=2,
        grid=(f // tf, p // tm),
        in_specs=[pl.BlockSpec((tm, D_MODEL), lambda j, t, te, tv: (jnp.where(tv[t] > 0, t, 0), 0)),
                  pl.BlockSpec((1, 1, D_MODEL, tf), lambda j, t, te, tv: (li, te[t], 0, j)),
                  pl.BlockSpec((1, 1, D_MODEL, tf), lambda j, t, te, tv: (li, te[t], 0, j))],
        out_specs=pl.BlockSpec((tm, tf), lambda j, t, te, tv: (t, j)),
        scratch_shapes=[pltpu.VMEM((D_MODEL, tf), BF16), pltpu.VMEM((D_MODEL, tf), BF16)],
    )
    return pl.pallas_call(
        _moe_gate_up_kernel,
        grid_spec=grid_spec,
        out_shape=jax.ShapeDtypeStruct((p, f), BF16),
        compiler_params=_params("arbitrary", "arbitrary"),
        name="moe_gate_up",
    )(tile_expert, tile_rows, xs, wg, wu)


def _moe_down_kernel(te_ref, tv_ref, h_ref, wd_ref, o_ref, wdb_ref):
    t = pl.program_id(1)

    @pl.when(_new_expert(te_ref, t))
    def _():
        wdb_ref[...] = wd_ref[0, 0].astype(BF16)

    def rows(n):
        if n:
            o_ref[0:n, :] = _dot(h_ref[0:n, :], wdb_ref[...])
        if n < o_ref.shape[0]:
            o_ref[n:, :] = jnp.zeros((o_ref.shape[0] - n, o_ref.shape[1]), o_ref.dtype)

    _by_valid_rows(tv_ref[t], h_ref.shape[0], rows)


def moe_down(tile_expert, tile_rows, hs, wd, li, *, tm, tn):
    p, f = hs.shape
    grid_spec = pltpu.PrefetchScalarGridSpec(
        num_scalar_prefetch=2,
        grid=(D_MODEL // tn, p // tm),
        in_specs=[pl.BlockSpec((tm, f), lambda j, t, te, tv: (jnp.where(tv[t] > 0, t, 0), 0)),
                  pl.BlockSpec((1, 1, f, tn), lambda j, t, te, tv: (li, te[t], 0, j))],
        out_specs=pl.BlockSpec((tm, tn), lambda j, t, te, tv: (t, j)),
        scratch_shapes=[pltpu.VMEM((f, tn), BF16)],
    )
    return pl.pallas_call(
        _moe_down_kernel,
        grid_spec=grid_spec,
        out_shape=jax.ShapeDtypeStruct((p, D_MODEL), F32),
        compiler_params=_params("arbitrary", "arbitrary"),
        name="moe_down",
    )(tile_expert, tile_rows, hs, wd)


def _combine_ln_kernel(x_ref, y0_ref, y1_ref, gate_ref, g_ref, b_ref, xo_ref, xb_ref):
    f = gate_ref[:, 0:1] * y0_ref[...] + gate_ref[:, 1:2] * y1_ref[...]
    x2 = _layer_norm(ALPHA * x_ref[...] + f, g_ref[...], b_ref[...])
    xo_ref[...] = x2
    xb_ref[...] = x2.astype(BF16)


def combine_ln(x, y0, y1, gates, ln_g, ln_b, *, tm):
    n = x.shape[0]
    row = pl.BlockSpec((tm, D_MODEL), lambda i: (i, 0))
    const = pl.BlockSpec((1, D_MODEL), lambda i: (0, 0))
    return pl.pallas_call(
        _combine_ln_kernel,
        grid=(n // tm,),
        in_specs=[row, row, row, pl.BlockSpec((tm, TOP_K), lambda i: (i, 0)), const, const],
        out_specs=[row, row],
        out_shape=[jax.ShapeDtypeStruct((n, D_MODEL), F32), jax.ShapeDtypeStruct((n, D_MODEL), BF16)],
        compiler_params=_params("arbitrary"),
        name="combine_ln",
    )(x, y0, y1, gates, ln_g, ln_b)


def _t5_bucket(dist):
    max_exact = NUM_BUCKETS // 2
    d = jnp.maximum(dist, 1).astype(F32)
    large = max_exact + (jnp.log(d / max_exact) / math.log(T5_MAX_DISTANCE / max_exact)
                         * (NUM_BUCKETS - max_exact)).astype(jnp.int32)
    large = jnp.minimum(large, NUM_BUCKETS - 1)
    return jnp.where(dist < max_exact, dist, large)


def _band_bias_masked(table, stride, max_dist):
    dist = jnp.arange(BLOCK)[:, None] + BLOCK - jnp.arange(2 * BLOCK)[None, :]
    onehot = jax.nn.one_hot(_t5_bucket(jnp.maximum(dist, 0) * stride), NUM_BUCKETS, dtype=F32)
    bias = jnp.einsum("qkb,bh->hqk", onehot, table.astype(F32), precision=lax.Precision.HIGHEST)
    valid = (dist >= 0) & (dist <= max_dist)
    return jnp.where(valid[None], bias, NEG_INF)


def _rope_lane_tables(seq):
    half = B_ROPE_DIM // 2
    inv = ROPE_THETA ** (-jnp.arange(0, B_ROPE_DIM, 2, dtype=F32) / B_ROPE_DIM)
    ang = jnp.arange(seq, dtype=F32)[:, None] * inv[None, :]
    cos, sin = jnp.cos(ang), jnp.sin(ang)
    ones = jnp.ones((seq, ROPE_LANE0), F32)
    zeros = jnp.zeros((seq, ROPE_LANE0), F32)
    pad = jnp.zeros((seq, LANES - ROPE_LANE0 - 2 * half), F32)
    cos_t = jnp.concatenate([ones, cos, cos, pad], axis=1)
    sin_t = jnp.concatenate([zeros, -sin, sin, pad], axis=1)
    return cos_t, sin_t


def _permute_heads(t, axis, order):
    parts = [lax.slice_in_dim(t, hh * HEAD_DIM, (hh + 1) * HEAD_DIM, axis=axis) for hh in order]
    return jnp.concatenate(parts, axis=axis)


IN_SEGMENTS = (("qa", 512), ("ka", 128), ("va", 128), ("cq", 384), ("ckv", 256), ("kr", 32),
               ("qc", 512), ("kc", 512), ("vc", 512), ("qd", 512), ("kd", 512), ("vd", 512))
IN_WIDTH = sum(size for _, size in IN_SEGMENTS)


def _relayout_w_in_kernel(w_ref, main_ref, d_ref):
    w = w_ref[0]
    rows = w.shape[0]
    seg = {}
    start = 0
    for name, size in IN_SEGMENTS:
        seg[name] = w[:, start:start + size]
        start += size
    qa = jnp.concatenate([seg["qa"][:, hh * HEAD_DIM:(hh + 1) * HEAD_DIM] for hh in A_HEAD_ORDER], axis=1)
    kr = jnp.concatenate([jnp.zeros((rows, ROPE_LANE0), F32), seg["kr"],
                          jnp.zeros((rows, LANES - ROPE_LANE0 - B_ROPE_DIM), F32)], axis=1)
    main = jnp.concatenate([qa, seg["qc"], seg["kc"], seg["cq"], seg["ka"], seg["vc"], seg["ckv"], seg["va"], kr],
                           axis=1)
    main_ref[0] = main.astype(BF16)
    d_ref[0] = jnp.concatenate([seg["qd"], seg["kd"], seg["vd"]], axis=1).astype(BF16)


def _relayout_w_in(w_in, *, tr):
    d = w_in.shape[0]
    return pl.pallas_call(
        _relayout_w_in_kernel,
        grid=(d, D_MODEL // tr),
        in_specs=[pl.BlockSpec((1, tr, IN_WIDTH), lambda l, i: (l, i, 0))],
        out_specs=[pl.BlockSpec((1, tr, IN_WIDTH_P), lambda l, i: (l, i, 0)),
                   pl.BlockSpec((1, tr, IN_WIDTH_D), lambda l, i: (l, i, 0))],
        out_shape=[jax.ShapeDtypeStruct((d, D_MODEL, IN_WIDTH_P), BF16),
                   jax.ShapeDtypeStruct((d, D_MODEL, IN_WIDTH_D), BF16)],
        compiler_params=_params("arbitrary", "arbitrary"),
        name="relayout_w_in",
    )(w_in)


def _relayout_mla(w_uq, w_ukv):
    d = w_uq.shape[0]
    half = B_ROPE_DIM // 2
    wq = w_uq.astype(BF16).reshape(d, B_Q_LORA, N_HEADS, B_NOPE_DIM + B_ROPE_DIM)
    nope, r1, r2 = wq[..., :B_NOPE_DIM], wq[..., B_NOPE_DIM:B_NOPE_DIM + half], wq[..., B_NOPE_DIM + half:]
    z32 = jnp.zeros(wq.shape[:3] + (LANES - B_NOPE_DIM - B_ROPE_DIM,), BF16)
    z64 = jnp.zeros(wq.shape[:3] + (B_NOPE_DIM,), BF16)
    wq_t = jnp.concatenate([nope, r1, r2, z32], axis=-1).reshape(d, B_Q_LORA, N_HEADS * LANES)
    wq_s = jnp.concatenate([z64, r2, r1, z32], axis=-1).reshape(d, B_Q_LORA, N_HEADS * LANES)
    wkv = w_ukv.astype(BF16).reshape(d, B_KV_LORA, N_HEADS, 2 * HEAD_DIM)
    zk = jnp.zeros(wkv.shape[:3] + (LANES - B_NOPE_DIM,), BF16)
    wk_t = jnp.concatenate([wkv[..., :B_NOPE_DIM], zk], axis=-1).reshape(d, B_KV_LORA, N_HEADS * LANES)
    wv = wkv[..., B_NOPE_DIM:].reshape(d, B_KV_LORA, GROUP_WIDTH)
    return wq_t, wq_s, wk_t, wv


def _rope_swap_matrix():
    half = B_ROPE_DIM // 2
    src = jnp.arange(LANES)[:, None]
    dst = jnp.arange(LANES)[None, :]
    first = (dst >= ROPE_LANE0) & (dst < ROPE_LANE0 + half) & (src == dst + half)
    second = (dst >= ROPE_LANE0 + half) & (dst < ROPE_LANE0 + 2 * half) & (src == dst - half)
    return (first | second).astype(BF16)


def _route(logits, tm):
    n = logits.shape[0]
    top_logits, top_idx = lax.top_k(logits, TOP_K)
    gates = jax.nn.softmax(top_logits, axis=-1)
    onehot = jax.nn.one_hot(top_idx, N_EXPERTS, dtype=jnp.int32)
    member = jnp.sum(onehot, axis=1)
    rank = jnp.cumsum(member, axis=0) - member
    counts = jnp.sum(member, axis=0)
    padded = ((counts + tm - 1) // tm) * tm
    ends = jnp.cumsum(padded)
    starts = ends - padded
    pos = jnp.sum(onehot * (starts[None, None, :] + rank[:, None, :]), axis=-1)
    n_rows = TOP_K * n + N_EXPERTS * tm
    src = (jnp.arange(n_rows, dtype=jnp.int32) % n).at[pos.reshape(-1)].set(
        jnp.repeat(jnp.arange(n, dtype=jnp.int32), TOP_K))

    def tile_tables(tile):
        tile_start = jnp.arange(n_rows // tile, dtype=jnp.int32) * tile
        tile_expert = jnp.minimum(jnp.sum((tile_start[:, None] >= ends[None, :]).astype(jnp.int32), axis=1),
                                  N_EXPERTS - 1)
        onehot_te = jax.nn.one_hot(tile_expert, N_EXPERTS, dtype=jnp.int32)
        group_end = jnp.sum(onehot_te * (starts + counts)[None, :], axis=1)
        return tile_expert.astype(jnp.int32), jnp.clip(group_end - tile_start, 0, tile).astype(jnp.int32)

    return pos, gates, src, tile_tables


def kernel(x, w_in, w_o, mla_q_norm, mla_kv_norm, mla_w_uq, mla_w_ukv, attn_sinks, rel_bias_table, mix_norm_g,
           ln1_g, ln1_b, ln2_g, ln2_b, ffn_w_gate, ffn_w_up, ffn_w_down, moe_router, moe_w_gate, moe_w_up,
           moe_w_down):
    batch, seq, _ = x.shape
    n = batch * seq

    w_in_p, w_in_d = _relayout_w_in(w_in, tr=256)
    w_o_p = jnp.concatenate([_permute_heads(w_o[:, :GROUP_WIDTH], 1, A_HEAD_ORDER), w_o[:, GROUP_WIDTH:]],
                            axis=1).astype(BF16)
    mix_g_p = jnp.concatenate([_permute_heads(mix_norm_g[:, :1], 2, A_HEAD_ORDER), mix_norm_g[:, 1:]], axis=1)
    wq_t, wq_s, wk_t, wv = _relayout_mla(mla_w_uq, mla_w_ukv)
    psw = _rope_swap_matrix()
    cos_t, sin_t = _rope_lane_tables(seq)
    tq = 256
    tri = (jnp.arange(tq)[:, None] > jnp.arange(tq)[None, :]).astype(BF16)
    order = list(A_HEAD_ORDER)
    def with_first_block_variant(b):
        prev = jnp.arange(2 * BLOCK) < BLOCK
        return jnp.stack([b, jnp.where(prev, NEG_INF, b)], axis=0) * LOG2E

    bias_a = _band_bias_masked(rel_bias_table[:, :N_HEADS], 1, A_WINDOW - 1)
    bias_a_rows = with_first_block_variant(jnp.concatenate([bias_a[hh] for hh in order], axis=0))
    sink_rows = jnp.concatenate([jnp.broadcast_to(attn_sinks[:, hh, None, None] * LOG2E, (DEPTH, BLOCK, LANES))
                                 for hh in order], axis=1)
    biases_d = [with_first_block_variant(_band_bias_masked(rel_bias_table[:, N_HEADS:], rate, window // rate).reshape(
        N_HEADS // 2, 2 * BLOCK, 2 * BLOCK)) for window, rate in D_PATTERNS]
    router_p = jnp.pad(moe_router, ((0, 0), (0, 0), (0, LANES - N_EXPERTS))).astype(BF16)
    wd_d = ffn_w_down.astype(BF16)
    moe_tm = 1024

    xf = x.reshape(n, D_MODEL)
    xb = xf
    for layer in range(DEPTH):
        h = matmul_ws(xb, w_in_p, layer, 1024, 1024, BF16)
        hd = matmul_ws(xb, w_in_d, layer, 1024, 512, F32)
        h3 = h.reshape(batch, seq, IN_WIDTH_P)
        ga = swa_attention(h3, bias_a_rows, sink_rows[layer], batch=batch, seq=seq)
        q_b, k_b, v_b = mla_up(h, mla_q_norm[layer][None], mla_kv_norm[layer][None], wq_t[layer], wq_s[layer],
                               wk_t[layer], wv[layer], psw, cos_t, sin_t, seq=seq, tm=512)
        gb = mla_attention(q_b.reshape(batch, seq, -1), k_b.reshape(batch, seq, -1),
                           v_b.reshape(batch, seq, -1), batch=batch, seq=seq, tq=tq, heads=4)
        gc = stick_breaking_attention(h3, tri, batch=batch, seq=seq, tq=tq, pairs=2)
        gd = dilated_attention(hd.reshape(batch, seq, IN_WIDTH_D), biases_d, batch=batch, seq=seq)
        i = layer // 2
        router = router_p[i] if layer % 2 == 1 else jnp.zeros((D_MODEL, LANES), BF16)
        xf, xb, logits = out_proj_ln(ga.reshape(n, -1), gb.reshape(n, -1), gc.reshape(n, -1), gd.reshape(n, -1),
                                     xf, mix_g_p[layer], w_o_p, layer, ln1_g[layer][None], ln1_b[layer][None],
                                     router, tm=256)
        if layer % 2 == 0:
            hmid = ffn_gate_up(xb, ffn_w_gate, ffn_w_up, i, tm=1024, tf=512)
            xf, xb = ffn_down_ln(hmid, wd_d, i, xf, ln2_g[layer][None], ln2_b[layer][None], tm=512, tk=512)
        else:
            pos, gates, src, tile_tables = _route(logits[:, :N_EXPERTS], moe_tm)
            te_up, tr_up = tile_tables(moe_tm)
            te_dn, tr_dn = tile_tables(moe_tm // 2)
            xs = cast_rows_bf16(tr_up, jnp.take(xf, src, axis=0, mode="clip"), tm=moe_tm)
            hs = moe_gate_up(te_up, tr_up, xs, moe_w_gate, moe_w_up, i, tm=moe_tm, tf=512)
            ys = moe_down(te_dn, tr_dn, hs, moe_w_down, i, tm=moe_tm // 2, tn=512)
            y0 = jnp.take(ys, pos[:, 0], axis=0, mode="clip")
            y1 = jnp.take(ys, pos[:, 1], axis=0, mode="clip")
            xf, xb = combine_ln(xf, y0, y1, gates, ln2_g[layer][None], ln2_b[layer][None], tm=512)
    return xf.reshape(batch, seq, D_MODEL)
```

```python
import functools
import math

import jax
import jax.numpy as jnp
from jax import lax
from jax.experimental import pallas as pl
from jax.experimental.pallas import tpu as pltpu

D_MODEL = 2048
DEPTH = 4
HEAD_DIM = 64
N_HEADS = 8
GROUP_WIDTH = N_HEADS * HEAD_DIM
BLOCK = 128
A_KV_HEADS = 2
A_WINDOW = 128
B_NOPE_DIM = 64
B_ROPE_DIM = 32
B_Q_LORA = 384
B_KV_LORA = 256
ROPE_THETA = 10000.0
D_PATTERNS = ((128, 1), (512, 4), (2048, 16))
NUM_BUCKETS = 32
T5_MAX_DISTANCE = 2048
D_FF = 5632
N_EXPERTS = 8
TOP_K = 2
NEG_INF = -1e30
LN_EPS = 1e-5
RMS_EPS = 1e-6
ALPHA = (2 * DEPTH) ** 0.25

LANES = 128
VMEM_LIMIT = 56 * 1024 * 1024

COL_QA, COL_QC, COL_KC, COL_CQ, COL_KA = 0, 512, 1024, 1536, 1920
COL_VC, COL_CKV, COL_VA, COL_KR = 2048, 2560, 2816, 2944
IN_WIDTH_P = 3072
IN_WIDTH_D = 3 * GROUP_WIDTH
A_HEAD_ORDER = (0, 4, 1, 5, 2, 6, 3, 7)
ROPE_LANE0 = 64
LOG2E = math.log2(math.e)
EXP2_UNDERFLOW = 151.0
QK_SCALE = HEAD_DIM ** -0.5 * LOG2E

BF16 = jnp.bfloat16
F32 = jnp.float32


def _params(*sem):
    return pltpu.CompilerParams(dimension_semantics=sem, vmem_limit_bytes=VMEM_LIMIT)


def _dot(a, b):
    return jnp.dot(a, b, preferred_element_type=F32)


def _dot_nt(a, b):
    return lax.dot_general(a, b, (((1,), (1,)), ((), ())), preferred_element_type=F32)


def _lo_lanes(rows):
    return lax.broadcasted_iota(jnp.int32, (rows, LANES), 1) < HEAD_DIM


def _split_pair(t, lo):
    zero = jnp.zeros_like(t)
    return jnp.concatenate([jnp.where(lo, t, zero), jnp.where(lo, zero, t)], axis=0)


def _matmul_kernel(a_ref, w_ref, o_ref):
    o_ref[...] = _dot(a_ref[...].astype(BF16), w_ref[0]).astype(o_ref.dtype)


def matmul_ws(a, w, layer, tm, tn, out_dtype):
    m, k = a.shape
    n = w.shape[2]
    return pl.pallas_call(
        _matmul_kernel,
        grid=(n // tn, m // tm),
        in_specs=[pl.BlockSpec((tm, k), lambda j, i: (i, 0)),
                  pl.BlockSpec((1, k, tn), lambda j, i: (layer, 0, j))],
        out_specs=pl.BlockSpec((tm, tn), lambda j, i: (i, j)),
        out_shape=jax.ShapeDtypeStruct((m, n), out_dtype),
        compiler_params=_params("arbitrary", "arbitrary"),
        name="matmul_ws",
    )(a, w)


def _swa_kernel(q_ref, kp_ref, kc_ref, vp_ref, vc_ref, bias_ref, sink_ref, o_ref):
    lo = _lo_lanes(BLOCK)
    kb = jnp.concatenate([kp_ref[0], kc_ref[0]], axis=0)
    vb = jnp.concatenate([vp_ref[0], vc_ref[0]], axis=0)
    which = jnp.where(pl.program_id(1) == 0, 1, 0)
    for c in range(GROUP_WIDTH // LANES):
        rows = slice(2 * c * BLOCK, 2 * (c + 1) * BLOCK)
        qs = _split_pair(q_ref[0, :, c * LANES:(c + 1) * LANES], lo) * QK_SCALE
        s = _dot_nt(qs, kb) + bias_ref[which, rows, :]
        sink = sink_ref[rows, :]
        m = jnp.maximum(jnp.max(s, axis=1, keepdims=True), sink)
        p = jnp.exp2(s - jnp.concatenate([m, m], axis=1))
        den = jnp.sum(p, axis=1, keepdims=True) + jnp.exp2(sink - m)
        o = _dot(p.astype(BF16), vb) / den
        o_ref[0, :, c * LANES:(c + 1) * LANES] = jnp.where(lo, o[:BLOCK], o[BLOCK:]).astype(o_ref.dtype)


def swa_attention(h3, bias_rows, sink_rows, *, batch, seq):
    nb = seq // BLOCK
    kb, vb = COL_KA // LANES, COL_VA // LANES
    return pl.pallas_call(
        _swa_kernel,
        grid=(batch, nb),
        in_specs=[pl.BlockSpec((1, BLOCK, GROUP_WIDTH), lambda b, n: (b, n, COL_QA // GROUP_WIDTH)),
                  pl.BlockSpec((1, BLOCK, LANES), lambda b, n: (b, jnp.maximum(n - 1, 0), kb)),
                  pl.BlockSpec((1, BLOCK, LANES), lambda b, n: (b, n, kb)),
                  pl.BlockSpec((1, BLOCK, LANES), lambda b, n: (b, jnp.maximum(n - 1, 0), vb)),
                  pl.BlockSpec((1, BLOCK, LANES), lambda b, n: (b, n, vb)),
                  pl.BlockSpec((2, N_HEADS * BLOCK, 2 * BLOCK), lambda b, n: (0, 0, 0)),
                  pl.BlockSpec((N_HEADS * BLOCK, LANES), lambda b, n: (0, 0))],
        out_specs=pl.BlockSpec((1, BLOCK, GROUP_WIDTH), lambda b, n: (b, n, 0)),
        out_shape=jax.ShapeDtypeStruct((batch, seq, GROUP_WIDTH), BF16),
        compiler_params=_params("arbitrary", "arbitrary"),
        name="swa_attn",
    )(h3, h3, h3, h3, h3, bias_rows, sink_rows)


def _dilated_kernel(q_ref, k_ref, v_ref, b0_ref, b1_ref, b2_ref, o_ref, osc, lsc):
    seq = q_ref.shape[1]
    lo = _lo_lanes(BLOCK)
    for p,((_, rate), b_ref) in enumerate(zip(D_PATTERNS, (b0_ref, b1_ref, b2_ref))):
        nb = seq // (rate * BLOCK)

        def unit(u, carry, p=p, rate=rate, b_ref=b_ref, nb=nb):
            n = u // rate
            res = u - n * rate
            rows = pl.ds(n * (BLOCK * rate) + res, BLOCK, stride=rate)
            qs = _split_pair((q_ref[0, rows, :] * QK_SCALE).astype(BF16), lo)
            kc = k_ref[0, rows, :].astype(BF16)
            vc = v_ref[0, rows, :].astype(BF16)
            if nb > 1:
                prow = pl.ds(jnp.maximum(n - 1, 0) * (BLOCK * rate) + res, BLOCK, stride=rate)
                kb = jnp.concatenate([k_ref[0, prow, :].astype(BF16), kc], axis=0)
                vb = jnp.concatenate([v_ref[0, prow, :].astype(BF16), vc], axis=0)
                s = _dot_nt(qs, kb) + b_ref[jnp.where(n == 0, 1, 0), 0]
            else:
                vb = vc
                s = _dot_nt(qs, kc) + b_ref[0, 0, :, BLOCK:2 * BLOCK]
            m = jnp.max(s, axis=1, keepdims=True)
            e = jnp.exp2(s - m)
            den = jnp.sum(e, axis=1, keepdims=True)
            o = _dot(e.astype(BF16), vb) / den
            lse = jnp.broadcast_to(m + jnp.log2(den), (2 * BLOCK, LANES))
            osc[p, rows, :] = jnp.where(lo, o[:BLOCK], o[BLOCK:])
            lsc[p, rows, :] = jnp.where(lo, lse[:BLOCK], lse[BLOCK:])
            return carry

        lax.fori_loop(0, nb * rate, unit, 0, unroll=16)

    chunk = 2 * BLOCK

    def merge(t, carry):
        r = pl.ds(pl.multiple_of(t * chunk, chunk), chunk)
        l0, l1, l2 = lsc[0, r, :], lsc[1, r, :], lsc[2, r, :]
        m = jnp.maximum(jnp.maximum(l0, l1), l2)
        e0, e1, e2 = jnp.exp2(l0 - m), jnp.exp2(l1 - m), jnp.exp2(l2 - m)
        o = (e0 * osc[0, r, :] + e1 * osc[1, r, :] + e2 * osc[2, r, :]) / (e0 + e1 + e2)
        o_ref[0, r, :] = o.astype(o_ref.dtype)
        return carry

    lax.fori_loop(0, seq // chunk, merge, 0)


def dilated_attention(hd3, biases, *, batch, seq):
    pairs = N_HEADS // 2
    bspec = pl.BlockSpec((2, 1, 2 * BLOCK, 2 * BLOCK), lambda b, p: (0, p, 0, 0))
    return pl.pallas_call(
        _dilated_kernel,
        grid=(batch, pairs),
        in_specs=[pl.BlockSpec((1, seq, LANES), lambda b, p: (b, 0, p)),
                  pl.BlockSpec((1, seq, LANES), lambda b, p: (b, 0, pairs + p)),
                  pl.BlockSpec((1, seq, LANES), lambda b, p: (b, 0, 2 * pairs + p)),
                  bspec, bspec, bspec],
        out_specs=pl.BlockSpec((1, seq, LANES), lambda b, p: (b, 0, p)),
        out_shape=jax.ShapeDtypeStruct((batch, seq, GROUP_WIDTH), BF16),
        scratch_shapes=[pltpu.VMEM((len(D_PATTERNS), seq, LANES), F32),
                        pltpu.VMEM((len(D_PATTERNS), seq, LANES), F32)],
        compiler_params=_params("arbitrary", "arbitrary"),
        name="dilated_attn",
    )(hd3, hd3, hd3, *biases)


def _mla_up_kernel(cq_ref, ckv_ref, kr_ref, qn_ref, kvn_ref, wq_ref, wqs_ref, wk_ref, wv_ref, psw_ref,
                   cos_ref, sin_ref, q_ref, k_ref, v_ref, *, scale):
    def rms(x_ref, g_ref):
        x = x_ref[...].astype(F32)
        return (x * lax.rsqrt(jnp.mean(x * x, axis=-1, keepdims=True) + RMS_EPS) * g_ref[...]).astype(BF16)

    xq = rms(cq_ref, qn_ref)
    xkv = rms(ckv_ref, kvn_ref)
    cos = cos_ref[...]
    sin = sin_ref[...]
    t = _dot(xq, wq_ref[...])
    ts = _dot(xq, wqs_ref[...])
    kn = _dot(xkv, wk_ref[...])
    kr = kr_ref[...]
    rk = kr.astype(F32) * cos + _dot(kr, psw_ref[...]) * sin
    for h in range(N_HEADS):
        sl = slice(h * LANES, (h + 1) * LANES)
        q_ref[:, sl] = ((t[:, sl] * cos + ts[:, sl] * sin) * scale).astype(BF16)
        k_ref[:, sl] = (kn[:, sl] + rk).astype(BF16)
    v_ref[...] = _dot(xkv, wv_ref[...]).astype(BF16)


def mla_up(h, q_norm, kv_norm, wq, wqs, wk, wv, psw, cos_t, sin_t, *, seq, tm):
    n = h.shape[0]
    w = N_HEADS * LANES
    const = lambda shape: pl.BlockSpec(shape, lambda i: (0, 0))
    spt = seq // tm
    return pl.pallas_call(
        functools.partial(_mla_up_kernel, scale=(B_NOPE_DIM + B_ROPE_DIM) ** -0.5 * math.log2(math.e)),
        grid=(n // tm,),
        in_specs=[pl.BlockSpec((tm, B_Q_LORA), lambda i: (i, COL_CQ // B_Q_LORA)),
                  pl.BlockSpec((tm, B_KV_LORA), lambda i: (i, COL_CKV // B_KV_LORA)),
                  pl.BlockSpec((tm, LANES), lambda i: (i, COL_KR // LANES)),
                  const((1, B_Q_LORA)), const((1, B_KV_LORA)),
                  const((B_Q_LORA, w)), const((B_Q_LORA, w)), const((B_KV_LORA, w)),
                  const((B_KV_LORA, GROUP_WIDTH)), const((LANES, LANES)),
                  pl.BlockSpec((tm, LANES), lambda i: (i % spt, 0)),
                  pl.BlockSpec((tm, LANES), lambda i: (i % spt, 0))],
        out_specs=[pl.BlockSpec((tm, w), lambda i: (i, 0)),
                   pl.BlockSpec((tm, w), lambda i: (i, 0)),
                   pl.BlockSpec((tm, GROUP_WIDTH), lambda i: (i, 0))],
        out_shape=[jax.ShapeDtypeStruct((n, w), BF16), jax.ShapeDtypeStruct((n, w), BF16),
                   jax.ShapeDtypeStruct((n, GROUP_WIDTH), BF16)],
        compiler_params=_params("arbitrary"),
        name="mla_up",
    )(h, h, h, q_norm, kv_norm, wq, wqs, wk, wv, psw, cos_t, sin_t)


def _mla_attn_kernel(q_ref, k_ref, v_ref, o_ref, *, tq, heads):
    i = pl.program_id(2)
    lo = _lo_lanes(tq)
    row = lax.broadcasted_iota(jnp.int32, (tq, tq), 0)
    col = lax.broadcasted_iota(jnp.int32, (tq, tq), 1)
    causal = col <= row

    def logits(j):
        start = pl.multiple_of(j * tq, tq)
        return tuple(_dot_nt(q_ref[0, :, hh * LANES:(hh + 1) * LANES],
                             k_ref[0, pl.ds(start, tq), hh * LANES:(hh + 1) * LANES]) for hh in range(heads))

    def update(j, stats, s_all, masked):
        start = pl.multiple_of(j * tq, tq)
        out = []
        for hh in range(heads):
            m, l, acc = stats[3 * hh:3 * hh + 3]
            s = s_all[hh]
            if masked:
                s = jnp.where(causal, s, NEG_INF)
            m_new = jnp.maximum(m, jnp.max(s, axis=1, keepdims=True))
            p = jnp.exp2(s - m_new)
            a = jnp.exp2(m - m_new)
            pv = _dot(p.astype(BF16), v_ref[0, pl.ds(start, tq), (hh // 2) * LANES:(hh // 2 + 1) * LANES])
            part = functools.reduce(jnp.add, [p[:, c * LANES:(c + 1) * LANES] for c in range(tq // LANES)])
            out += [m_new, a * l + part, a * acc + pv]
        return tuple(out)

    def body(j, carry):
        s_next = logits(j + 1)
        return update(j, carry[0], carry[1], False), s_next

    init = (jnp.full((tq, 1), NEG_INF, F32), jnp.zeros((tq, LANES), F32), jnp.zeros((tq, LANES), F32)) * heads
    stats, s_diag = lax.fori_loop(0, i, body, (init, logits(0)))
    stats = update(i, stats, s_diag, True)
    for pp in range(heads // 2):
        (_, l0, a0), (_, l1, a1) = stats[6 * pp:6 * pp + 3], stats[6 * pp + 3:6 * pp + 6]
        o0 = a0 / jnp.sum(l0, axis=1, keepdims=True)
        o1 = a1 / jnp.sum(l1, axis=1, keepdims=True)
        o_ref[0, :, pp * LANES:(pp + 1) * LANES] = jnp.where(lo, o0, o1).astype(o_ref.dtype)


def mla_attention(q, k, v, *, batch, seq, tq, heads):
    pairs = heads // 2
    return pl.pallas_call(
        functools.partial(_mla_attn_kernel, tq=tq, heads=heads),
        grid=(batch, N_HEADS // heads, seq // tq),
        in_specs=[pl.BlockSpec((1, tq, heads * LANES), lambda b, g, i: (b, i, g)),
                  pl.BlockSpec((1, seq, heads * LANES), lambda b, g, i: (b, 0, g)),
                  pl.BlockSpec((1, seq, pairs * LANES), lambda b, g, i: (b, 0, g))],
        out_specs=pl.BlockSpec((1, tq, pairs * LANES), lambda b, g, i: (b, i, g)),
        out_shape=jax.ShapeDtypeStruct((batch, seq, GROUP_WIDTH), BF16),
        compiler_params=_params("arbitrary", "arbitrary", "arbitrary"),
        name="mla_attn",
    )(q, k, v)


def _stick_kernel(q_ref, k_ref, v_ref, tri_ref, o_ref, *, tq, pairs):
    i = pl.program_id(2)
    lo = _lo_lanes(tq)
    row = lax.broadcasted_iota(jnp.int32, (2 * tq, tq), 0)
    col = lax.broadcasted_iota(jnp.int32, (2 * tq, tq), 1)
    strict = col < jnp.where(row >= tq, row - tq, row)
    tri = tri_ref[...]
    qs = [_split_pair(q_ref[0, :, pp * LANES:(pp + 1) * LANES], lo) * QK_SCALE for pp in range(pairs)]

    def chunk(pp, j, run, masked):
        start = pl.multiple_of(j * tq, tq)
        z = _dot_nt(qs[pp], k_ref[0, pl.ds(start, tq), pp * LANES:(pp + 1) * LANES])
        soft = jnp.log2(1.0 + jnp.exp2(-jnp.abs(z)))
        keep = -(jnp.maximum(z, 0.0) + soft)
        hit = jnp.minimum(z, 0.0) - soft
        if masked:
            keep = jnp.where(strict, keep, 0.0)
        hi = keep.astype(BF16)
        rest = (keep - hi.astype(F32)).astype(BF16)
        after = _dot(hi, tri) + _dot(rest, tri)
        a = jnp.exp2(hit + after)
        if masked:
            a = jnp.where(strict, a, 0.0)
        contrib = jnp.exp2(run) * _dot(a.astype(BF16), v_ref[0, pl.ds(start, tq), pp * LANES:(pp + 1) * LANES])
        return contrib, run + jnp.sum(keep, axis=1, keepdims=True)

    state = []
    has_prev = i > 0
    for pp in range(pairs):
        acc, run = chunk(pp, i, jnp.zeros((2 * tq, 1), F32), True)
        contrib, run_prev = chunk(pp, jnp.maximum(i - 1, 0), run, False)
        state += [jnp.where(has_prev, run_prev, run), acc + jnp.where(has_prev, contrib, 0.0)]

    def cond(carry):
        t = carry[0]
        live = carry[1]
        for pp in range(1, pairs):
            live = jnp.maximum(live, carry[1 + 2 * pp])
        return jnp.logical_and(t < i, jnp.max(live) > -EXP2_UNDERFLOW)

    def body(carry):
        t = carry[0]
        out = [t + 1]
        for pp in range(pairs):
            contrib, run = chunk(pp, i - 1 - t, carry[1 + 2 * pp], False)
            out += [run, carry[2 + 2 * pp] + contrib]
        return tuple(out)

    final = lax.while_loop(cond, body, (jnp.int32(1), *state))
    for pp in range(pairs):
        acc = final[2 + 2 * pp]
        o_ref[0, :, pp * LANES:(pp + 1) * LANES] = jnp.where(lo, acc[:tq], acc[tq:]).astype(o_ref.dtype)


def stick_breaking_attention(h3, tri, *, batch, seq, tq, pairs):
    w = pairs * LANES
    return pl.pallas_call(
        functools.partial(_stick_kernel, tq=tq, pairs=pairs),
        grid=(batch, GROUP_WIDTH // w, seq // tq),
        in_specs=[pl.BlockSpec((1, tq, w), lambda b, g, i: (b, i, COL_QC // w + g)),
                  pl.BlockSpec((1, seq, w), lambda b, g, i: (b, 0, COL_KC // w + g)),
                  pl.BlockSpec((1, seq, w), lambda b, g, i: (b, 0, COL_VC // w + g)),
                  pl.BlockSpec((tq, tq), lambda b, g, i: (0, 0))],
        out_specs=pl.BlockSpec((1, tq, w), lambda b, g, i: (b, i, g)),
        out_shape=jax.ShapeDtypeStruct((batch, seq, GROUP_WIDTH), BF16),
        compiler_params=_params("arbitrary", "arbitrary", "arbitrary"),
        name="stick_breaking",
    )(h3, h3, h3, tri)


def _layer_norm(r, g, b):
    mu = jnp.mean(r, axis=-1, keepdims=True)
    d = r - mu
    var = jnp.mean(d * d, axis=-1, keepdims=True)
    return d * lax.rsqrt(var + LN_EPS) * g + b


def _pack_bf16_pairs(xb):
    half = xb.shape[1] // 2
    lo = lax.bitcast_convert_type(xb[:, :half].astype(F32), jnp.uint32) >> 16
    hi = lax.bitcast_convert_type(xb[:, half:].astype(F32), jnp.uint32) & jnp.uint32(0xFFFF0000)
    return lax.bitcast_convert_type(hi | lo, F32)


def _unpack_bf16_pairs(xp):
    u = lax.bitcast_convert_type(xp, jnp.uint32)
    lo = lax.bitcast_convert_type(u << 16, F32).astype(BF16)
    hi = lax.bitcast_convert_type(u & jnp.uint32(0xFFFF0000), F32).astype(BF16)
    return jnp.concatenate([lo, hi], axis=1)


def _out_proj_kernel(ga_ref, gb_ref, gc_ref, gd_ref, x_ref, mixg_ref, wo_ref, g_ref, b_ref, rt_ref,
                     xo_ref, xb_ref, xp_ref, lg_ref):
    mixed = None
    for gi, grp in enumerate((ga_ref, gb_ref, gc_ref, gd_ref)):
        xg = grp[...].astype(F32)
        y = xg * lax.rsqrt(jnp.mean(xg * xg, axis=-1, keepdims=True) + RMS_EPS) * mixg_ref[gi:gi + 1, :]
        part = _dot(y.astype(BF16), wo_ref[0, gi * GROUP_WIDTH:(gi + 1) * GROUP_WIDTH, :])
        mixed = part if mixed is None else mixed + part
    x1 = _layer_norm(ALPHA * x_ref[...].astype(F32) + mixed, g_ref[...], b_ref[...])
    xo_ref[...] = x1
    xb = x1.astype(BF16)
    xb_ref[...] = xb
    xp_ref[...] = _pack_bf16_pairs(xb)
    lg_ref[...] = _dot(xb, rt_ref[...])


def out_proj_ln(ga, gb, gc, gd, x, mix_g, wo, layer, ln_g, ln_b, router, *, tm):
    n = x.shape[0]
    row = lambda w: pl.BlockSpec((tm, w), lambda i: (i, 0))
    const = lambda shape: pl.BlockSpec(shape, lambda i: (0, 0))
    return pl.pallas_call(
        _out_proj_kernel,
        grid=(n // tm,),
        in_specs=[row(GROUP_WIDTH)] * 4 + [row(D_MODEL), const((4, GROUP_WIDTH)),
                                          pl.BlockSpec((1, D_MODEL, D_MODEL), lambda i: (layer, 0, 0)),
                                          const((1, D_MODEL)), const((1, D_MODEL)), const((D_MODEL, LANES))],
        out_specs=[row(D_MODEL), row(D_MODEL), row(D_MODEL // 2), row(LANES)],
        out_shape=[jax.ShapeDtypeStruct((n, D_MODEL), F32), jax.ShapeDtypeStruct((n, D_MODEL), BF16),
                   jax.ShapeDtypeStruct((n, D_MODEL // 2), F32), jax.ShapeDtypeStruct((n, LANES), F32)],
        compiler_params=_params("arbitrary"),
        name="out_proj_ln",
    )(ga, gb, gc, gd, x, mix_g, wo, ln_g, ln_b, router)


def _silu_mul(g, u):
    return g * (1.0 / (1.0 + jnp.exp(-g))) * u


def _gate_up_kernel(x_ref, wg_ref, wu_ref, o_ref, wgb_ref, wub_ref):
    @pl.when(pl.program_id(1) == 0)
    def _():
        wgb_ref[...] = wg_ref[0].astype(BF16)
        wub_ref[...] = wu_ref[0].astype(BF16)

    x = x_ref[...]
    o_ref[...] = _silu_mul(_dot(x, wgb_ref[...]), _dot(x, wub_ref[...])).astype(o_ref.dtype)


def ffn_gate_up(xb, wg, wu, li, *, tm, tf):
    n = xb.shape[0]
    f = wg.shape[2]
    return pl.pallas_call(
        _gate_up_kernel,
        grid=(f // tf, n // tm),
        in_specs=[pl.BlockSpec((tm, D_MODEL), lambda j, i: (i, 0)),
                  pl.BlockSpec((1, D_MODEL, tf), lambda j, i: (li, 0, j)),
                  pl.BlockSpec((1, D_MODEL, tf), lambda j, i: (li, 0, j))],
        out_specs=pl.BlockSpec((tm, tf), lambda j, i: (i, j)),
        out_shape=jax.ShapeDtypeStruct((n, f), BF16),
        scratch_shapes=[pltpu.VMEM((D_MODEL, tf), BF16), pltpu.VMEM((D_MODEL, tf), BF16)],
        compiler_params=_params("arbitrary", "arbitrary"),
        name="ffn_gate_up",
    )(xb, wg, wu)


def _down_ln_kernel(h_ref, wd_ref, x_ref, g_ref, b_ref, xo_ref, xb_ref):
    kk = pl.program_id(1)
    part = _dot(h_ref[...], wd_ref[0])

    @pl.when(kk == 0)
    def _():
        xo_ref[...] = part

    @pl.when(kk > 0)
    def _():
        xo_ref[...] += part

    @pl.when(kk == pl.num_programs(1) - 1)
    def _():
        x2 = _layer_norm(ALPHA * x_ref[...] + xo_ref[...], g_ref[...], b_ref[...])
        xo_ref[...] = x2
        xb_ref[...] = x2.astype(BF16)


def ffn_down_ln(hmid, wd, li, x, ln_g, ln_b, *, tm, tk):
    n, f = hmid.shape
    return pl.pallas_call(
        _down_ln_kernel,
        grid=(n // tm, f // tk),
        in_specs=[pl.BlockSpec((tm, tk), lambda i, k: (i, k)),
                  pl.BlockSpec((1, tk, D_MODEL), lambda i, k: (li, k, 0)),
                  pl.BlockSpec((tm, D_MODEL), lambda i, k: (i, 0)),
                  pl.BlockSpec((1, D_MODEL), lambda i, k: (0, 0)),
                  pl.BlockSpec((1, D_MODEL), lambda i, k: (0, 0))],
        out_specs=[pl.BlockSpec((tm, D_MODEL), lambda i, k: (i, 0)),
                   pl.BlockSpec((tm, D_MODEL), lambda i, k: (i, 0))],
        out_shape=[jax.ShapeDtypeStruct((n, D_MODEL), F32), jax.ShapeDtypeStruct((n, D_MODEL), BF16)],
        compiler_params=_params("arbitrary", "arbitrary"),
        name="ffn_down_ln",
    )(hmid, wd, x, ln_g, ln_b)


def _new_expert(te_ref, t):
    return jnp.logical_or(t == 0, te_ref[t] != te_ref[jnp.maximum(t - 1, 0)])


MOE_ROW_STEP = 256


def _by_valid_rows(valid, tm, rows):
    for n in range(0, tm + 1, MOE_ROW_STEP):
        @pl.when(jnp.logical_and(valid > n - MOE_ROW_STEP, valid <= n))
        def _(n=n):
            rows(n)


def _moe_gate_up_kernel(te_ref, tv_ref, x_ref, wg_ref, wu_ref, o_ref, wgb_ref, wub_ref):
    t = pl.program_id(1)

    @pl.when(_new_expert(te_ref, t))
    def _():
        wgb_ref[...] = wg_ref[0, 0].astype(BF16)
        wub_ref[...] = wu_ref[0, 0].astype(BF16)

    def rows(n):
        if n:
            x = _unpack_bf16_pairs(x_ref[0:n, :])
            o_ref[0:n, :] = _silu_mul(_dot(x, wgb_ref[...]), _dot(x, wub_ref[...])).astype(o_ref.dtype)
        if n < o_ref.shape[0]:
            o_ref[n:, :] = jnp.zeros((o_ref.shape[0] - n, o_ref.shape[1]), o_ref.dtype)

    _by_valid_rows(tv_ref[t], x_ref.shape[0], rows)


def moe_gate_up(tile_expert, tile_rows, xs, wg, wu, li, *, tm, tf):
    p = xs.shape[0]
    f = wg.shape[3]
    grid_spec = pltpu.PrefetchScalarGridSpec(
        num_scalar_prefetch=2,
        grid=(f // tf, p // tm),
        in_specs=[pl.BlockSpec((tm, D_MODEL // 2), lambda j, t, te, tv: (jnp.where(tv[t] > 0, t, 0), 0)),
                  pl.BlockSpec((1, 1, D_MODEL, tf), lambda j, t, te, tv: (li, te[t], 0, j)),
                  pl.BlockSpec((1, 1, D_MODEL, tf), lambda j, t, te, tv: (li, te[t], 0, j))],
        out_specs=pl.BlockSpec((tm, tf), lambda j, t, te, tv: (t, j)),
        scratch_shapes=[pltpu.VMEM((D_MODEL, tf), BF16), pltpu.VMEM((D_MODEL, tf), BF16)],
    )
    return pl.pallas_call(
        _moe_gate_up_kernel,
        grid_spec=grid_spec,
        out_shape=jax.ShapeDtypeStruct((p, f), BF16),
        compiler_params=_params("arbitrary", "arbitrary"),
        name="moe_gate_up",
    )(tile_expert, tile_rows, xs, wg, wu)


def _moe_down_kernel(te_ref, tv_ref, h_ref, wd_ref, o_ref, wdb_ref):
    t = pl.program_id(1)

    @pl.when(_new_expert(te_ref, t))
    def _():
        wdb_ref[...] = wd_ref[0, 0].astype(BF16)

    def rows(n):
        if n:
            o_ref[0:n, :] = _dot(h_ref[0:n, :], wdb_ref[...])
        if n < o_ref.shape[0]:
            o_ref[n:, :] = jnp.zeros((o_ref.shape[0] - n, o_ref.shape[1]), o_ref.dtype)

    _by_valid_rows(tv_ref[t], h_ref.shape[0], rows)


def moe_down(tile_expert, tile_rows, hs, wd, li, *, tm, tn):
    p, f = hs.shape
    grid_spec = pltpu.PrefetchScalarGridSpec(
        num_scalar_prefetch=2,
        grid=(D_MODEL // tn, p // tm),
        in_specs=[pl.BlockSpec((tm, f), lambda j, t, te, tv: (jnp.where(tv[t] > 0, t, 0), 0)),
                  pl.BlockSpec((1, 1, f, tn), lambda j, t, te, tv: (li, te[t], 0, j))],
        out_specs=pl.BlockSpec((tm, tn), lambda j, t, te, tv: (t, j)),
        scratch_shapes=[pltpu.VMEM((f, tn), BF16)],
    )
    return pl.pallas_call(
        _moe_down_kernel,
        grid_spec=grid_spec,
        out_shape=jax.ShapeDtypeStruct((p, D_MODEL), F32),
        compiler_params=_params("arbitrary", "arbitrary"),
        name="moe_down",
    )(tile_expert, tile_rows, hs, wd)


def _combine_ln_kernel(x_ref, y0_ref, y1_ref, gate_ref, g_ref, b_ref, xo_ref, xb_ref):
    f = gate_ref[:, 0:1] * y0_ref[...] + gate_ref[:, 1:2] * y1_ref[...]
    x2 = _layer_norm(ALPHA * x_ref[...] + f, g_ref[...], b_ref[...])
    xo_ref[...] = x2
    xb_ref[...] = x2.astype(BF16)


def combine_ln(x, y0, y1, gates, ln_g, ln_b, *, tm):
    n = x.shape[0]
    row = pl.BlockSpec((tm, D_MODEL), lambda i: (i, 0))
    const = pl.BlockSpec((1, D_MODEL), lambda i: (0, 0))
    return pl.pallas_call(
        _combine_ln_kernel,
        grid=(n // tm,),
        in_specs=[row, row, row, pl.BlockSpec((tm, TOP_K), lambda i: (i, 0)), const, const],
        out_specs=[row, row],
        out_shape=[jax.ShapeDtypeStruct((n, D_MODEL), F32), jax.ShapeDtypeStruct((n, D_MODEL), BF16)],
        compiler_params=_params("arbitrary"),
        name="combine_ln",
    )(x, y0, y1, gates, ln_g, ln_b)


def _t5_bucket(dist):
    max_exact = NUM_BUCKETS // 2
    d = jnp.maximum(dist, 1).astype(F32)
    large = max_exact + (jnp.log(d / max_exact) / math.log(T5_MAX_DISTANCE / max_exact)
                         * (NUM_BUCKETS - max_exact)).astype(jnp.int32)
    large = jnp.minimum(large, NUM_BUCKETS - 1)
    return jnp.where(dist < max_exact, dist, large)


def _band_bias_masked(table, stride, max_dist):
    dist = jnp.arange(BLOCK)[:, None] + BLOCK - jnp.arange(2 * BLOCK)[None, :]
    onehot = jax.nn.one_hot(_t5_bucket(jnp.maximum(dist, 0) * stride), NUM_BUCKETS, dtype=F32)
    bias = jnp.einsum("qkb,bh->hqk", onehot, table.astype(F32), precision=lax.Precision.HIGHEST)
    valid = (dist >= 0) & (dist <= max_dist)
    return jnp.where(valid[None], bias, NEG_INF)


def _rope_lane_tables(seq):
    half = B_ROPE_DIM // 2
    inv = ROPE_THETA ** (-jnp.arange(0, B_ROPE_DIM, 2, dtype=F32) / B_ROPE_DIM)
    ang = jnp.arange(seq, dtype=F32)[:, None] * inv[None, :]
    cos, sin = jnp.cos(ang), jnp.sin(ang)
    ones = jnp.ones((seq, ROPE_LANE0), F32)
    zeros = jnp.zeros((seq, ROPE_LANE0), F32)
    pad = jnp.zeros((seq, LANES - ROPE_LANE0 - 2 * half), F32)
    cos_t = jnp.concatenate([ones, cos, cos, pad], axis=1)
    sin_t = jnp.concatenate([zeros, -sin, sin, pad], axis=1)
    return cos_t, sin_t


def _permute_heads(t, axis, order):
    parts = [lax.slice_in_dim(t, hh * HEAD_DIM, (hh + 1) * HEAD_DIM, axis=axis) for hh in order]
    return jnp.concatenate(parts, axis=axis)


IN_SEGMENTS = (("qa", 512), ("ka", 128), ("va", 128), ("cq", 384), ("ckv", 256), ("kr", 32),
               ("qc", 512), ("kc", 512), ("vc", 512), ("qd", 512), ("kd", 512), ("vd", 512))
IN_WIDTH = sum(size for _, size in IN_SEGMENTS)


def _relayout_w_in_kernel(w_ref, main_ref, d_ref):
    w = w_ref[0]
    rows = w.shape[0]
    seg = {}
    start = 0
    for name, size in IN_SEGMENTS:
        seg[name] = w[:, start:start + size]
        start += size
    qa = jnp.concatenate([seg["qa"][:, hh * HEAD_DIM:(hh + 1) * HEAD_DIM] for hh in A_HEAD_ORDER], axis=1)
    kr = jnp.concatenate([jnp.zeros((rows, ROPE_LANE0), F32), seg["kr"],
                          jnp.zeros((rows, LANES - ROPE_LANE0 - B_ROPE_DIM), F32)], axis=1)
    main = jnp.concatenate([qa, seg["qc"], seg["kc"], seg["cq"], seg["ka"], seg["vc"], seg["ckv"], seg["va"], kr],
                           axis=1)
    main_ref[0] = main.astype(BF16)
    d_ref[0] = jnp.concatenate([seg["qd"], seg["kd"], seg["vd"]], axis=1).astype(BF16)


def _relayout_w_in(w_in, *, tr):
    d = w_in.shape[0]
    return pl.pallas_call(
        _relayout_w_in_kernel,
        grid=(d, D_MODEL // tr),
        in_specs=[pl.BlockSpec((1, tr, IN_WIDTH), lambda l, i: (l, i, 0))],
        out_specs=[pl.BlockSpec((1, tr, IN_WIDTH_P), lambda l, i: (l, i, 0)),
                   pl.BlockSpec((1, tr, IN_WIDTH_D), lambda l, i: (l, i, 0))],
        out_shape=[jax.ShapeDtypeStruct((d, D_MODEL, IN_WIDTH_P), BF16),
                   jax.ShapeDtypeStruct((d, D_MODEL, IN_WIDTH_D), BF16)],
        compiler_params=_params("arbitrary", "arbitrary"),
        name="relayout_w_in",
    )(w_in)


def _relayout_mla(w_uq, w_ukv):
    d = w_uq.shape[0]
    half = B_ROPE_DIM // 2
    wq = w_uq.astype(BF16).reshape(d, B_Q_LORA, N_HEADS, B_NOPE_DIM + B_ROPE_DIM)
    nope, r1, r2 = wq[..., :B_NOPE_DIM], wq[..., B_NOPE_DIM:B_NOPE_DIM + half], wq[..., B_NOPE_DIM + half:]
    z32 = jnp.zeros(wq.shape[:3] + (LANES - B_NOPE_DIM - B_ROPE_DIM,), BF16)
    z64 = jnp.zeros(wq.shape[:3] + (B_NOPE_DIM,), BF16)
    wq_t = jnp.concatenate([nope, r1, r2, z32], axis=-1).reshape(d, B_Q_LORA, N_HEADS * LANES)
    wq_s = jnp.concatenate([z64, r2, r1, z32], axis=-1).reshape(d, B_Q_LORA, N_HEADS * LANES)
    wkv = w_ukv.astype(BF16).reshape(d, B_KV_LORA, N_HEADS, 2 * HEAD_DIM)
    zk = jnp.zeros(wkv.shape[:3] + (LANES - B_NOPE_DIM,), BF16)
    wk_t = jnp.concatenate([wkv[..., :B_NOPE_DIM], zk], axis=-1).reshape(d, B_KV_LORA, N_HEADS * LANES)
    wv = wkv[..., B_NOPE_DIM:].reshape(d, B_KV_LORA, GROUP_WIDTH)
    return wq_t, wq_s, wk_t, wv


def _rope_swap_matrix():
    half = B_ROPE_DIM // 2
    src = jnp.arange(LANES)[:, None]
    dst = jnp.arange(LANES)[None, :]
    first = (dst >= ROPE_LANE0) & (dst < ROPE_LANE0 + half) & (src == dst + half)
    second = (dst >= ROPE_LANE0 + half) & (dst < ROPE_LANE0 + 2 * half) & (src == dst - half)
    return (first | second).astype(BF16)


def _route(logits, tm):
    n = logits.shape[0]
    top_logits, top_idx = lax.top_k(logits, TOP_K)
    gates = jax.nn.softmax(top_logits, axis=-1)
    onehot = jax.nn.one_hot(top_idx, N_EXPERTS, dtype=jnp.int32)
    member = jnp.sum(onehot, axis=1)
    rank = jnp.cumsum(member, axis=0) - member
    counts = jnp.sum(member, axis=0)
    padded = ((counts + tm - 1) // tm) * tm
    ends = jnp.cumsum(padded)
    starts = ends - padded
    pos = jnp.sum(onehot * (starts[None, None, :] + rank[:, None, :]), axis=-1)
    n_rows = TOP_K * n + N_EXPERTS * tm
    src = (jnp.arange(n_rows, dtype=jnp.int32) % n).at[pos.reshape(-1)].set(
        jnp.repeat(jnp.arange(n, dtype=jnp.int32), TOP_K))

    def tile_tables(tile):
        tile_start = jnp.arange(n_rows // tile, dtype=jnp.int32) * tile
        tile_expert = jnp.minimum(jnp.sum((tile_start[:, None] >= ends[None, :]).astype(jnp.int32), axis=1),
                                  N_EXPERTS - 1)
        onehot_te = jax.nn.one_hot(tile_expert, N_EXPERTS, dtype=jnp.int32)
        group_end = jnp.sum(onehot_te * (starts + counts)[None, :], axis=1)
        return tile_expert.astype(jnp.int32), jnp.clip(group_end - tile_start, 0, tile).astype(jnp.int32)

    return pos, gates, src, tile_tables


def kernel(x, w_in, w_o, mla_q_norm, mla_kv_norm, mla_w_uq, mla_w_ukv, attn_sinks, rel_bias_table, mix_norm_g,
           ln1_g, ln1_b, ln2_g, ln2_b, ffn_w_gate, ffn_w_up, ffn_w_down, moe_router, moe_w_gate, moe_w_up,
           moe_w_down):
    batch, seq, _ = x.shape
    n = batch * seq

    w_in_p, w_in_d = _relayout_w_in(w_in, tr=256)
    w_o_p = jnp.concatenate([_permute_heads(w_o[:, :GROUP_WIDTH], 1, A_HEAD_ORDER), w_o[:, GROUP_WIDTH:]],
                            axis=1).astype(BF16)
    mix_g_p = jnp.concatenate([_permute_heads(mix_norm_g[:, :1], 2, A_HEAD_ORDER), mix_norm_g[:, 1:]], axis=1)
    wq_t, wq_s, wk_t, wv = _relayout_mla(mla_w_uq, mla_w_ukv)
    psw = _rope_swap_matrix()
    cos_t, sin_t = _rope_lane_tables(seq)
    tq = 256
    tri = (jnp.arange(tq)[:, None] > jnp.arange(tq)[None, :]).astype(BF16)
    order = list(A_HEAD_ORDER)
    def with_first_block_variant(b):
        prev = jnp.arange(2 * BLOCK) < BLOCK
        return jnp.stack([b, jnp.where(prev, NEG_INF, b)], axis=0) * LOG2E

    bias_a = _band_bias_masked(rel_bias_table[:, :N_HEADS], 1, A_WINDOW - 1)
    bias_a_rows = with_first_block_variant(jnp.concatenate([bias_a[hh] for hh in order], axis=0))
    sink_rows = jnp.concatenate([jnp.broadcast_to(attn_sinks[:, hh, None, None] * LOG2E, (DEPTH, BLOCK, LANES))
                                 for hh in order], axis=1)
    biases_d = [with_first_block_variant(_band_bias_masked(rel_bias_table[:, N_HEADS:], rate, window // rate).reshape(
        N_HEADS // 2, 2 * BLOCK, 2 * BLOCK)) for window, rate in D_PATTERNS]
    router_p = jnp.pad(moe_router, ((0, 0), (0, 0), (0, LANES - N_EXPERTS))).astype(BF16)
    wd_d = ffn_w_down.astype(BF16)
    moe_tm = 1024

    xf = x.reshape(n, D_MODEL)
    xb = xf
    for layer in range(DEPTH):
        h = matmul_ws(xb, w_in_p, layer, 1024, 1024, BF16)
        hd = matmul_ws(xb, w_in_d, layer, 1024, 512, F32)
        h3 = h.reshape(batch, seq, IN_WIDTH_P)
        ga = swa_attention(h3, bias_a_rows, sink_rows[layer], batch=batch, seq=seq)
        q_b, k_b, v_b = mla_up(h, mla_q_norm[layer][None], mla_kv_norm[layer][None], wq_t[layer], wq_s[layer],
                               wk_t[layer], wv[layer], psw, cos_t, sin_t, seq=seq, tm=512)
        gb = mla_attention(q_b.reshape(batch, seq, -1), k_b.reshape(batch, seq, -1),
                           v_b.reshape(batch, seq, -1), batch=batch, seq=seq, tq=tq, heads=4)
        gc = stick_breaking_attention(h3, tri, batch=batch, seq=seq, tq=tq, pairs=2)
        gd = dilated_attention(hd.reshape(batch, seq, IN_WIDTH_D), biases_d, batch=batch, seq=seq)
        i = layer // 2
        router = router_p[i] if layer % 2 == 1 else jnp.zeros((D_MODEL, LANES), BF16)
        xf, xb, xp, logits = out_proj_ln(ga.reshape(n, -1), gb.reshape(n, -1), gc.reshape(n, -1), gd.reshape(n, -1),
                                     xf, mix_g_p[layer], w_o_p, layer, ln1_g[layer][None], ln1_b[layer][None],
                                     router, tm=256)
        if layer % 2 == 0:
            hmid = ffn_gate_up(xb, ffn_w_gate, ffn_w_up, i, tm=1024, tf=512)
            xf, xb = ffn_down_ln(hmid, wd_d, i, xf, ln2_g[layer][None], ln2_b[layer][None], tm=512, tk=1408)
        else:
            pos, gates, src, tile_tables = _route(logits[:, :N_EXPERTS], moe_tm)
            te_up, tr_up = tile_tables(moe_tm)
            te_dn, tr_dn = tile_tables(moe_tm // 2)
            xs = jnp.take(xp, src, axis=0, mode="clip")
            hs = moe_gate_up(te_up, tr_up, xs, moe_w_gate, moe_w_up, i, tm=moe_tm, tf=512)
            ys = moe_down(te_dn, tr_dn, hs, moe_w_down, i, tm=moe_tm // 2, tn=512)
            y0 = jnp.take(ys, pos[:, 0], axis=0, mode="clip")
            y1 = jnp.take(ys, pos[:, 1], axis=0, mode="clip")
            xf, xb = combine_ln(xf, y0, y1, gates, ln2_g[layer][None], ln2_b[layer][None], tm=512)
    return xf.reshape(batch, seq, D_MODEL)
```

```python
import functools
import math

import jax
import jax.numpy as jnp
from jax import lax
from jax.experimental import pallas as pl
from jax.experimental.pallas import tpu as pltpu

D_MODEL = 2048
DEPTH = 4
HEAD_DIM = 64
N_HEADS = 8
GROUP_WIDTH = N_HEADS * HEAD_DIM
BLOCK = 128
A_KV_HEADS = 2
A_WINDOW = 128
B_NOPE_DIM = 64
B_ROPE_DIM = 32
B_Q_LORA = 384
B_KV_LORA = 256
ROPE_THETA = 10000.0
D_PATTERNS = ((128, 1), (512, 4), (2048, 16))
NUM_BUCKETS = 32
T5_MAX_DISTANCE = 2048
D_FF = 5632
N_EXPERTS = 8
TOP_K = 2
NEG_INF = -1e30
LN_EPS = 1e-5
RMS_EPS = 1e-6
ALPHA = (2 * DEPTH) ** 0.25

LANES = 128
VMEM_LIMIT = 56 * 1024 * 1024

COL_QA, COL_QC, COL_KC, COL_CQ, COL_KA = 0, 512, 1024, 1536, 1920
COL_VC, COL_CKV, COL_VA, COL_KR = 2048, 2560, 2816, 2944
IN_WIDTH_P = 3072
IN_WIDTH_D = 3 * GROUP_WIDTH
A_HEAD_ORDER = (0, 4, 1, 5, 2, 6, 3, 7)
ROPE_LANE0 = 64
LOG2E = math.log2(math.e)
EXP2_UNDERFLOW = 151.0
QK_SCALE = HEAD_DIM ** -0.5 * LOG2E

BF16 = jnp.bfloat16
F32 = jnp.float32


def _params(*sem):
    return pltpu.CompilerParams(dimension_semantics=sem, vmem_limit_bytes=VMEM_LIMIT)


def _dot(a, b):
    return jnp.dot(a, b, preferred_element_type=F32)


def _dot_nt(a, b):
    return lax.dot_general(a, b, (((1,), (1,)), ((), ())), preferred_element_type=F32)


def _lo_lanes(rows):
    return lax.broadcasted_iota(jnp.int32, (rows, LANES), 1) < HEAD_DIM


def _split_pair(t, lo):
    zero = jnp.zeros_like(t)
    return jnp.concatenate([jnp.where(lo, t, zero), jnp.where(lo, zero, t)], axis=0)


def _matmul_kernel(a_ref, w_ref, o_ref):
    o_ref[...] = _dot(a_ref[...].astype(BF16), w_ref[0]).astype(o_ref.dtype)


def matmul_ws(a, w, layer, tm, tn, out_dtype):
    m, k = a.shape
    n = w.shape[2]
    return pl.pallas_call(
        _matmul_kernel,
        grid=(n // tn, m // tm),
        in_specs=[pl.BlockSpec((tm, k), lambda j, i: (i, 0)),
                  pl.BlockSpec((1, k, tn), lambda j, i: (layer, 0, j))],
        out_specs=pl.BlockSpec((tm, tn), lambda j, i: (i, j)),
        out_shape=jax.ShapeDtypeStruct((m, n), out_dtype),
        compiler_params=_params("arbitrary", "arbitrary"),
        name="matmul_ws",
    )(a, w)


def _swa_kernel(q_ref, kp_ref, kc_ref, vp_ref, vc_ref, bias_ref, sink_ref, o_ref):
    lo = _lo_lanes(BLOCK)
    kb = jnp.concatenate([kp_ref[0], kc_ref[0]], axis=0)
    vb = jnp.concatenate([vp_ref[0], vc_ref[0]], axis=0)
    which = jnp.where(pl.program_id(1) == 0, 1, 0)
    for c in range(GROUP_WIDTH // LANES):
        rows = slice(2 * c * BLOCK, 2 * (c + 1) * BLOCK)
        qs = _split_pair(q_ref[0, :, c * LANES:(c + 1) * LANES], lo) * QK_SCALE
        s = _dot_nt(qs, kb) + bias_ref[which, rows, :]
        sink = sink_ref[rows, :]
        m = jnp.maximum(jnp.max(s, axis=1, keepdims=True), sink)
        p = jnp.exp2(s - jnp.concatenate([m, m], axis=1))
        den = jnp.sum(p, axis=1, keepdims=True) + jnp.exp2(sink - m)
        o = _dot(p.astype(BF16), vb) / den
        o_ref[0, :, c * LANES:(c + 1) * LANES] = jnp.where(lo, o[:BLOCK], o[BLOCK:]).astype(o_ref.dtype)


def swa_attention(h3, bias_rows, sink_rows, *, batch, seq):
    nb = seq // BLOCK
    kb, vb = COL_KA // LANES, COL_VA // LANES
    return pl.pallas_call(
        _swa_kernel,
        grid=(batch, nb),
        in_specs=[pl.BlockSpec((1, BLOCK, GROUP_WIDTH), lambda b, n: (b, n, COL_QA // GROUP_WIDTH)),
                  pl.BlockSpec((1, BLOCK, LANES), lambda b, n: (b, jnp.maximum(n - 1, 0), kb)),
                  pl.BlockSpec((1, BLOCK, LANES), lambda b, n: (b, n, kb)),
                  pl.BlockSpec((1, BLOCK, LANES), lambda b, n: (b, jnp.maximum(n - 1, 0), vb)),
                  pl.BlockSpec((1, BLOCK, LANES), lambda b, n: (b, n, vb)),
                  pl.BlockSpec((2, N_HEADS * BLOCK, 2 * BLOCK), lambda b, n: (0, 0, 0)),
                  pl.BlockSpec((N_HEADS * BLOCK, LANES), lambda b, n: (0, 0))],
        out_specs=pl.BlockSpec((1, BLOCK, GROUP_WIDTH), lambda b, n: (b, n, 0)),
        out_shape=jax.ShapeDtypeStruct((batch, seq, GROUP_WIDTH), BF16),
        compiler_params=_params("arbitrary", "arbitrary"),
        name="swa_attn",
    )(h3, h3, h3, h3, h3, bias_rows, sink_rows)


def _dilated_kernel(q_ref, k_ref, v_ref, b0_ref, b1_ref, b2_ref, o_ref, osc, lsc):
    seq = q_ref.shape[1]
    lo = _lo_lanes(BLOCK)
    for p,((_, rate), b_ref) in enumerate(zip(D_PATTERNS, (b0_ref, b1_ref, b2_ref))):
        nb = seq // (rate * BLOCK)

        def unit(u, carry, p=p, rate=rate, b_ref=b_ref, nb=nb):
            n = u // rate
            res = u - n * rate
            rows = pl.ds(n * (BLOCK * rate) + res, BLOCK, stride=rate)
            qs = _split_pair((q_ref[0, rows, :] * QK_SCALE).astype(BF16), lo)
            kc = k_ref[0, rows, :].astype(BF16)
            vc = v_ref[0, rows, :].astype(BF16)
            if nb > 1:
                prow = pl.ds(jnp.maximum(n - 1, 0) * (BLOCK * rate) + res, BLOCK, stride=rate)
                kb = jnp.concatenate([k_ref[0, prow, :].astype(BF16), kc], axis=0)
                vb = jnp.concatenate([v_ref[0, prow, :].astype(BF16), vc], axis=0)
                s = _dot_nt(qs, kb) + b_ref[jnp.where(n == 0, 1, 0), 0]
            else:
                vb = vc
                s = _dot_nt(qs, kc) + b_ref[0, 0, :, BLOCK:2 * BLOCK]
            m = jnp.max(s, axis=1, keepdims=True)
            e = jnp.exp2(s - m)
            den = jnp.sum(e, axis=1, keepdims=True)
            o = _dot(e.astype(BF16), vb) / den
            lse = jnp.broadcast_to(m + jnp.log2(den), (2 * BLOCK, LANES))
            osc[p, rows, :] = jnp.where(lo, o[:BLOCK], o[BLOCK:])
            lsc[p, rows, :] = jnp.where(lo, lse[:BLOCK], lse[BLOCK:])
            return carry

        lax.fori_loop(0, nb * rate, unit, 0, unroll=16)

    chunk = 2 * BLOCK

    def merge(t, carry):
        r = pl.ds(pl.multiple_of(t * chunk, chunk), chunk)
        l0, l1, l2 = lsc[0, r, :], lsc[1, r, :], lsc[2, r, :]
        m = jnp.maximum(jnp.maximum(l0, l1), l2)
        e0, e1, e2 = jnp.exp2(l0 - m), jnp.exp2(l1 - m), jnp.exp2(l2 - m)
        o = (e0 * osc[0, r, :] + e1 * osc[1, r, :] + e2 * osc[2, r, :]) / (e0 + e1 + e2)
        o_ref[0, r, :] = o.astype(o_ref.dtype)
        return carry

    lax.fori_loop(0, seq // chunk, merge, 0)


def dilated_attention(hd3, biases, *, batch, seq):
    pairs = N_HEADS // 2
    bspec = pl.BlockSpec((2, 1, 2 * BLOCK, 2 * BLOCK), lambda b, p: (0, p, 0, 0))
    return pl.pallas_call(
        _dilated_kernel,
        grid=(batch, pairs),
        in_specs=[pl.BlockSpec((1, seq, LANES), lambda b, p: (b, 0, p)),
                  pl.BlockSpec((1, seq, LANES), lambda b, p: (b, 0, pairs + p)),
                  pl.BlockSpec((1, seq, LANES), lambda b, p: (b, 0, 2 * pairs + p)),
                  bspec, bspec, bspec],
        out_specs=pl.BlockSpec((1, seq, LANES), lambda b, p: (b, 0, p)),
        out_shape=jax.ShapeDtypeStruct((batch, seq, GROUP_WIDTH), BF16),
        scratch_shapes=[pltpu.VMEM((len(D_PATTERNS), seq, LANES), F32),
                        pltpu.VMEM((len(D_PATTERNS), seq, LANES), F32)],
        compiler_params=_params("arbitrary", "arbitrary"),
        name="dilated_attn",
    )(hd3, hd3, hd3, *biases)


def _mla_up_kernel(cq_ref, ckv_ref, kr_ref, qn_ref, kvn_ref, wq_ref, wqs_ref, wk_ref, wv_ref, psw_ref,
                   cos_ref, sin_ref, q_ref, k_ref, v_ref, *, scale):
    def rms(x_ref, g_ref):
        x = x_ref[...].astype(F32)
        return (x * lax.rsqrt(jnp.mean(x * x, axis=-1, keepdims=True) + RMS_EPS) * g_ref[...]).astype(BF16)

    xq = rms(cq_ref, qn_ref)
    xkv = rms(ckv_ref, kvn_ref)
    cos = cos_ref[...]
    sin = sin_ref[...]
    t = _dot(xq, wq_ref[...])
    ts = _dot(xq, wqs_ref[...])
    kn = _dot(xkv, wk_ref[...])
    kr = kr_ref[...]
    rk = kr.astype(F32) * cos + _dot(kr, psw_ref[...]) * sin
    for h in range(N_HEADS):
        sl = slice(h * LANES, (h + 1) * LANES)
        q_ref[:, sl] = ((t[:, sl] * cos + ts[:, sl] * sin) * scale).astype(BF16)
        k_ref[:, sl] = (kn[:, sl] + rk).astype(BF16)
    v_ref[...] = _dot(xkv, wv_ref[...]).astype(BF16)


def mla_up(h, q_norm, kv_norm, wq, wqs, wk, wv, psw, cos_t, sin_t, *, seq, tm):
    n = h.shape[0]
    w = N_HEADS * LANES
    const = lambda shape: pl.BlockSpec(shape, lambda i: (0, 0))
    spt = seq // tm
    return pl.pallas_call(
        functools.partial(_mla_up_kernel, scale=(B_NOPE_DIM + B_ROPE_DIM) ** -0.5 * math.log2(math.e)),
        grid=(n // tm,),
        in_specs=[pl.BlockSpec((tm, B_Q_LORA), lambda i: (i, COL_CQ // B_Q_LORA)),
                  pl.BlockSpec((tm, B_KV_LORA), lambda i: (i, COL_CKV // B_KV_LORA)),
                  pl.BlockSpec((tm, LANES), lambda i: (i, COL_KR // LANES)),
                  const((1, B_Q_LORA)), const((1, B_KV_LORA)),
                  const((B_Q_LORA, w)), const((B_Q_LORA, w)), const((B_KV_LORA, w)),
                  const((B_KV_LORA, GROUP_WIDTH)), const((LANES, LANES)),
                  pl.BlockSpec((tm, LANES), lambda i: (i % spt, 0)),
                  pl.BlockSpec((tm, LANES), lambda i: (i % spt, 0))],
        out_specs=[pl.BlockSpec((tm, w), lambda i: (i, 0)),
                   pl.BlockSpec((tm, w), lambda i: (i, 0)),
                   pl.BlockSpec((tm, GROUP_WIDTH), lambda i: (i, 0))],
        out_shape=[jax.ShapeDtypeStruct((n, w), BF16), jax.ShapeDtypeStruct((n, w), BF16),
                   jax.ShapeDtypeStruct((n, GROUP_WIDTH), BF16)],
        compiler_params=_params("arbitrary"),
        name="mla_up",
    )(h, h, h, q_norm, kv_norm, wq, wqs, wk, wv, psw, cos_t, sin_t)


def _mla_attn_kernel(q_ref, k_ref, v_ref, o_ref, *, tq, heads):
    i = pl.program_id(2)
    lo = _lo_lanes(tq)
    row = lax.broadcasted_iota(jnp.int32, (tq, tq), 0)
    col = lax.broadcasted_iota(jnp.int32, (tq, tq), 1)
    causal = col <= row

    def logits(j):
        start = pl.multiple_of(j * tq, tq)
        return tuple(_dot_nt(q_ref[0, :, hh * LANES:(hh + 1) * LANES],
                             k_ref[0, pl.ds(start, tq), hh * LANES:(hh + 1) * LANES]) for hh in range(heads))

    def update(j, stats, s_all, masked):
        start = pl.multiple_of(j * tq, tq)
        out = []
        for hh in range(heads):
            m, l, acc = stats[3 * hh:3 * hh + 3]
            s = s_all[hh]
            if masked:
                s = jnp.where(causal, s, NEG_INF)
            m_new = jnp.maximum(m, jnp.max(s, axis=1, keepdims=True))
            p = jnp.exp2(s - m_new)
            a = jnp.exp2(m - m_new)
            pv = _dot(p.astype(BF16), v_ref[0, pl.ds(start, tq), (hh // 2) * LANES:(hh // 2 + 1) * LANES])
            part = functools.reduce(jnp.add, [p[:, c * LANES:(c + 1) * LANES] for c in range(tq // LANES)])
            out += [m_new, a * l + part, a * acc + pv]
        return tuple(out)

    def body(j, carry):
        s_next = logits(j + 1)
        return update(j, carry[0], carry[1], False), s_next

    init = (jnp.full((tq, 1), NEG_INF, F32), jnp.zeros((tq, LANES), F32), jnp.zeros((tq, LANES), F32)) * heads
    stats, s_diag = lax.fori_loop(0, i, body, (init, logits(0)))
    stats = update(i, stats, s_diag, True)
    for pp in range(heads // 2):
        (_, l0, a0), (_, l1, a1) = stats[6 * pp:6 * pp + 3], stats[6 * pp + 3:6 * pp + 6]
        o0 = a0 / jnp.sum(l0, axis=1, keepdims=True)
        o1 = a1 / jnp.sum(l1, axis=1, keepdims=True)
        o_ref[0, :, pp * LANES:(pp + 1) * LANES] = jnp.where(lo, o0, o1).astype(o_ref.dtype)


def mla_attention(q, k, v, *, batch, seq, tq, heads):
    pairs = heads // 2
    return pl.pallas_call(
        functools.partial(_mla_attn_kernel, tq=tq, heads=heads),
        grid=(batch, N_HEADS // heads, seq // tq),
        in_specs=[pl.BlockSpec((1, tq, heads * LANES), lambda b, g, i: (b, i, g)),
                  pl.BlockSpec((1, seq, heads * LANES), lambda b, g, i: (b, 0, g)),
                  pl.BlockSpec((1, seq, pairs * LANES), lambda b, g, i: (b, 0, g))],
        out_specs=pl.BlockSpec((1, tq, pairs * LANES), lambda b, g, i: (b, i, g)),
        out_shape=jax.ShapeDtypeStruct((batch, seq, GROUP_WIDTH), BF16),
        compiler_params=_params("arbitrary", "arbitrary", "arbitrary"),
        name="mla_attn",
    )(q, k, v)


def _stick_kernel(q_ref, k_ref, v_ref, tri_ref, o_ref, *, tq, pairs):
    i = pl.program_id(2)
    lo = _lo_lanes(tq)
    row = lax.broadcasted_iota(jnp.int32, (2 * tq, tq), 0)
    col = lax.broadcasted_iota(jnp.int32, (2 * tq, tq), 1)
    strict = col < jnp.where(row >= tq, row - tq, row)
    tri = tri_ref[...]
    qs = [_split_pair(q_ref[0, :, pp * LANES:(pp + 1) * LANES], lo) * QK_SCALE for pp in range(pairs)]

    def chunk(pp, j, run, masked):
        start = pl.multiple_of(j * tq, tq)
        z = _dot_nt(qs[pp], k_ref[0, pl.ds(start, tq), pp * LANES:(pp + 1) * LANES])
        soft = jnp.log2(1.0 + jnp.exp2(-jnp.abs(z)))
        keep = -(jnp.maximum(z, 0.0) + soft)
        hit = jnp.minimum(z, 0.0) - soft
        if masked:
            keep = jnp.where(strict, keep, 0.0)
        hi = keep.astype(BF16)
        rest = (keep - hi.astype(F32)).astype(BF16)
        after = _dot(hi, tri) + _dot(rest, tri)
        a = jnp.exp2(hit + after)
        if masked:
            a = jnp.where(strict, a, 0.0)
        contrib = jnp.exp2(run) * _dot(a.astype(BF16), v_ref[0, pl.ds(start, tq), pp * LANES:(pp + 1) * LANES])
        return contrib, run + jnp.sum(keep, axis=1, keepdims=True)

    state = []
    has_prev = i > 0
    for pp in range(pairs):
        acc, run = chunk(pp, i, jnp.zeros((2 * tq, 1), F32), True)
        contrib, run_prev = chunk(pp, jnp.maximum(i - 1, 0), run, False)
        state += [jnp.where(has_prev, run_prev, run), acc + jnp.where(has_prev, contrib, 0.0)]

    def cond(carry):
        t = carry[0]
        live = carry[1]
        for pp in range(1, pairs):
            live = jnp.maximum(live, carry[1 + 2 * pp])
        return jnp.logical_and(t < i, jnp.max(live) > -EXP2_UNDERFLOW)

    def body(carry):
        t = carry[0]
        out = [t + 1]
        for pp in range(pairs):
            contrib, run = chunk(pp, i - 1 - t, carry[1 + 2 * pp], False)
            out += [run, carry[2 + 2 * pp] + contrib]
        return tuple(out)

    final = lax.while_loop(cond, body, (jnp.int32(1), *state))
    for pp in range(pairs):
        acc = final[2 + 2 * pp]
        o_ref[0, :, pp * LANES:(pp + 1) * LANES] = jnp.where(lo, acc[:tq], acc[tq:]).astype(o_ref.dtype)


def stick_breaking_attention(h3, tri, *, batch, seq, tq, pairs):
    w = pairs * LANES
    return pl.pallas_call(
        functools.partial(_stick_kernel, tq=tq, pairs=pairs),
        grid=(batch, GROUP_WIDTH // w, seq // tq),
        in_specs=[pl.BlockSpec((1, tq, w), lambda b, g, i: (b, i, COL_QC // w + g)),
                  pl.BlockSpec((1, seq, w), lambda b, g, i: (b, 0, COL_KC // w + g)),
                  pl.BlockSpec((1, seq, w), lambda b, g, i: (b, 0, COL_VC // w + g)),
                  pl.BlockSpec((tq, tq), lambda b, g, i: (0, 0))],
        out_specs=pl.BlockSpec((1, tq, w), lambda b, g, i: (b, i, g)),
        out_shape=jax.ShapeDtypeStruct((batch, seq, GROUP_WIDTH), BF16),
        compiler_params=_params("arbitrary", "arbitrary", "arbitrary"),
        name="stick_breaking",
    )(h3, h3, h3, tri)


def _layer_norm(r, g, b):
    mu = jnp.mean(r, axis=-1, keepdims=True)
    d = r - mu
    var = jnp.mean(d * d, axis=-1, keepdims=True)
    return d * lax.rsqrt(var + LN_EPS) * g + b


def _pack_bf16_pairs(xb):
    half = xb.shape[1] // 2
    lo = lax.bitcast_convert_type(xb[:, :half].astype(F32), jnp.uint32) >> 16
    hi = lax.bitcast_convert_type(xb[:, half:].astype(F32), jnp.uint32) & jnp.uint32(0xFFFF0000)
    return lax.bitcast_convert_type(hi | lo, F32)


def _unpack_bf16_pairs(xp):
    u = lax.bitcast_convert_type(xp, jnp.uint32)
    lo = lax.bitcast_convert_type(u << 16, F32).astype(BF16)
    hi = lax.bitcast_convert_type(u & jnp.uint32(0xFFFF0000), F32).astype(BF16)
    return jnp.concatenate([lo, hi], axis=1)


def _out_proj_kernel(ga_ref, gb_ref, gc_ref, gd_ref, x_ref, mixg_ref, wo_ref, g_ref, b_ref, rt_ref,
                     xo_ref, xb_ref, xp_ref, lg_ref):
    mixed = None
    for gi, grp in enumerate((ga_ref, gb_ref, gc_ref, gd_ref)):
        xg = grp[...].astype(F32)
        y = xg * lax.rsqrt(jnp.mean(xg * xg, axis=-1, keepdims=True) + RMS_EPS) * mixg_ref[gi:gi + 1, :]
        part = _dot(y.astype(BF16), wo_ref[0, gi * GROUP_WIDTH:(gi + 1) * GROUP_WIDTH, :])
        mixed = part if mixed is None else mixed + part
    x1 = _layer_norm(ALPHA * x_ref[...].astype(F32) + mixed, g_ref[...], b_ref[...])
    xo_ref[...] = x1
    xb = x1.astype(BF16)
    xb_ref[...] = xb
    xp_ref[...] = _pack_bf16_pairs(xb)
    lg_ref[...] = _dot(xb, rt_ref[...])


def out_proj_ln(ga, gb, gc, gd, x, mix_g, wo, layer, ln_g, ln_b, router, *, tm):
    n = x.shape[0]
    row = lambda w: pl.BlockSpec((tm, w), lambda i: (i, 0))
    const = lambda shape: pl.BlockSpec(shape, lambda i: (0, 0))
    return pl.pallas_call(
        _out_proj_kernel,
        grid=(n // tm,),
        in_specs=[row(GROUP_WIDTH)] * 4 + [row(D_MODEL), const((4, GROUP_WIDTH)),
                                          pl.BlockSpec((1, D_MODEL, D_MODEL), lambda i: (layer, 0, 0)),
                                          const((1, D_MODEL)), const((1, D_MODEL)), const((D_MODEL, LANES))],
        out_specs=[row(D_MODEL), row(D_MODEL), row(D_MODEL // 2), row(LANES)],
        out_shape=[jax.ShapeDtypeStruct((n, D_MODEL), F32), jax.ShapeDtypeStruct((n, D_MODEL), BF16),
                   jax.ShapeDtypeStruct((n, D_MODEL // 2), F32), jax.ShapeDtypeStruct((n, LANES), F32)],
        compiler_params=_params("arbitrary"),
        name="out_proj_ln",
    )(ga, gb, gc, gd, x, mix_g, wo, ln_g, ln_b, router)


def _silu_mul(g, u):
    return g * (1.0 / (1.0 + jnp.exp(-g))) * u


def _gate_up_kernel(x_ref, wg_ref, wu_ref, o_ref, wgb_ref, wub_ref):
    @pl.when(pl.program_id(1) == 0)
    def _():
        wgb_ref[...] = wg_ref[0].astype(BF16)
        wub_ref[...] = wu_ref[0].astype(BF16)

    x = x_ref[...]
    o_ref[...] = _silu_mul(_dot(x, wgb_ref[...]), _dot(x, wub_ref[...])).astype(o_ref.dtype)


def ffn_gate_up(xb, wg, wu, li, *, tm, tf):
    n = xb.shape[0]
    f = wg.shape[2]
    return pl.pallas_call(
        _gate_up_kernel,
        grid=(f // tf, n // tm),
        in_specs=[pl.BlockSpec((tm, D_MODEL), lambda j, i: (i, 0)),
                  pl.BlockSpec((1, D_MODEL, tf), lambda j, i: (li, 0, j)),
                  pl.BlockSpec((1, D_MODEL, tf), lambda j, i: (li, 0, j))],
        out_specs=pl.BlockSpec((tm, tf), lambda j, i: (i, j)),
        out_shape=jax.ShapeDtypeStruct((n, f), BF16),
        scratch_shapes=[pltpu.VMEM((D_MODEL, tf), BF16), pltpu.VMEM((D_MODEL, tf), BF16)],
        compiler_params=_params("arbitrary", "arbitrary"),
        name="ffn_gate_up",
    )(xb, wg, wu)


def _down_ln_kernel(h_ref, wd_ref, x_ref, g_ref, b_ref, xo_ref, xb_ref):
    kk = pl.program_id(1)
    part = _dot(h_ref[...], wd_ref[0])

    @pl.when(kk == 0)
    def _():
        xo_ref[...] = part

    @pl.when(kk > 0)
    def _():
        xo_ref[...] += part

    @pl.when(kk == pl.num_programs(1) - 1)
    def _():
        x2 = _layer_norm(ALPHA * x_ref[...] + xo_ref[...], g_ref[...], b_ref[...])
        xo_ref[...] = x2
        xb_ref[...] = x2.astype(BF16)


def ffn_down_ln(hmid, wd, li, x, ln_g, ln_b, *, tm, tk):
    n, f = hmid.shape
    return pl.pallas_call(
        _down_ln_kernel,
        grid=(n // tm, f // tk),
        in_specs=[pl.BlockSpec((tm, tk), lambda i, k: (i, k)),
                  pl.BlockSpec((1, tk, D_MODEL), lambda i, k: (li, k, 0)),
                  pl.BlockSpec((tm, D_MODEL), lambda i, k: (i, 0)),
                  pl.BlockSpec((1, D_MODEL), lambda i, k: (0, 0)),
                  pl.BlockSpec((1, D_MODEL), lambda i, k: (0, 0))],
        out_specs=[pl.BlockSpec((tm, D_MODEL), lambda i, k: (i, 0)),
                   pl.BlockSpec((tm, D_MODEL), lambda i, k: (i, 0))],
        out_shape=[jax.ShapeDtypeStruct((n, D_MODEL), F32), jax.ShapeDtypeStruct((n, D_MODEL), BF16)],
        compiler_params=_params("arbitrary", "arbitrary"),
        name="ffn_down_ln",
    )(hmid, wd, x, ln_g, ln_b)


def _new_expert(te_ref, t):
    return jnp.logical_or(t == 0, te_ref[t] != te_ref[jnp.maximum(t - 1, 0)])


MOE_ROW_STEP = 256


def _by_valid_rows(valid, tm, rows):
    for n in range(0, tm + 1, MOE_ROW_STEP):
        @pl.when(jnp.logical_and(valid > n - MOE_ROW_STEP, valid <= n))
        def _(n=n):
            rows(n)


def _moe_gate_up_kernel(te_ref, tv_ref, x_ref, wg_ref, wu_ref, o_ref, wgb_ref, wub_ref):
    t = pl.program_id(1)

    @pl.when(_new_expert(te_ref, t))
    def _():
        wgb_ref[...] = wg_ref[0, 0].astype(BF16)
        wub_ref[...] = wu_ref[0, 0].astype(BF16)

    def rows(n):
        pad = o_ref.shape[0] - n
        if n:
            x = _unpack_bf16_pairs(x_ref[pad:, :])
            o_ref[pad:, :] = _silu_mul(_dot(x, wgb_ref[...]), _dot(x, wub_ref[...])).astype(o_ref.dtype)
        if pad:
            o_ref[0:pad, :] = jnp.zeros((pad, o_ref.shape[1]), o_ref.dtype)

    _by_valid_rows(tv_ref[t], x_ref.shape[0], rows)


def moe_gate_up(tile_expert, tile_rows, xs, wg, wu, li, *, tm, tf):
    p = xs.shape[0]
    f = wg.shape[3]
    grid_spec = pltpu.PrefetchScalarGridSpec(
        num_scalar_prefetch=2,
        grid=(f // tf, p // tm),
        in_specs=[pl.BlockSpec((tm, D_MODEL // 2), lambda j, t, te, tv: (jnp.where(tv[t] > 0, t, 0), 0)),
                  pl.BlockSpec((1, 1, D_MODEL, tf), lambda j, t, te, tv: (li, te[t], 0, j)),
                  pl.BlockSpec((1, 1, D_MODEL, tf), lambda j, t, te, tv: (li, te[t], 0, j))],
        out_specs=pl.BlockSpec((tm, tf), lambda j, t, te, tv: (t, j)),
        scratch_shapes=[pltpu.VMEM((D_MODEL, tf), BF16), pltpu.VMEM((D_MODEL, tf), BF16)],
    )
    return pl.pallas_call(
        _moe_gate_up_kernel,
        grid_spec=grid_spec,
        out_shape=jax.ShapeDtypeStruct((p, f), BF16),
        compiler_params=_params("arbitrary", "arbitrary"),
        name="moe_gate_up",
    )(tile_expert, tile_rows, xs, wg, wu)


def _moe_down_kernel(te_ref, tv_ref, h_ref, wd_ref, o_ref, wdb_ref):
    t = pl.program_id(1)

    @pl.when(_new_expert(te_ref, t))
    def _():
        wdb_ref[...] = wd_ref[0, 0].astype(BF16)

    def rows(n):
        pad = o_ref.shape[0] - n
        if n:
            o_ref[pad:, :] = _dot(h_ref[pad:, :], wdb_ref[...])
        if pad:
            o_ref[0:pad, :] = jnp.zeros((pad, o_ref.shape[1]), o_ref.dtype)

    _by_valid_rows(tv_ref[t], h_ref.shape[0], rows)


def moe_down(tile_expert, tile_rows, hs, wd, li, *, tm, tn):
    p, f = hs.shape
    grid_spec = pltpu.PrefetchScalarGridSpec(
        num_scalar_prefetch=2,
        grid=(D_MODEL // tn, p // tm),
        in_specs=[pl.BlockSpec((tm, f), lambda j, t, te, tv: (jnp.where(tv[t] > 0, t, 0), 0)),
                  pl.BlockSpec((1, 1, f, tn), lambda j, t, te, tv: (li, te[t], 0, j))],
        out_specs=pl.BlockSpec((tm, tn), lambda j, t, te, tv: (t, j)),
        scratch_shapes=[pltpu.VMEM((f, tn), BF16)],
    )
    return pl.pallas_call(
        _moe_down_kernel,
        grid_spec=grid_spec,
        out_shape=jax.ShapeDtypeStruct((p, D_MODEL), F32),
        compiler_params=_params("arbitrary", "arbitrary"),
        name="moe_down",
    )(tile_expert, tile_rows, hs, wd)


def _combine_ln_kernel(x_ref, y0_ref, y1_ref, gate_ref, g_ref, b_ref, xo_ref, xb_ref):
    f = gate_ref[:, 0:1] * y0_ref[...] + gate_ref[:, 1:2] * y1_ref[...]
    x2 = _layer_norm(ALPHA * x_ref[...] + f, g_ref[...], b_ref[...])
    xo_ref[...] = x2
    xb_ref[...] = x2.astype(BF16)


def combine_ln(x, y0, y1, gates, ln_g, ln_b, *, tm):
    n = x.shape[0]
    row = pl.BlockSpec((tm, D_MODEL), lambda i: (i, 0))
    const = pl.BlockSpec((1, D_MODEL), lambda i: (0, 0))
    return pl.pallas_call(
        _combine_ln_kernel,
        grid=(n // tm,),
        in_specs=[row, row, row, pl.BlockSpec((tm, TOP_K), lambda i: (i, 0)), const, const],
        out_specs=[row, row],
        out_shape=[jax.ShapeDtypeStruct((n, D_MODEL), F32), jax.ShapeDtypeStruct((n, D_MODEL), BF16)],
        compiler_params=_params("arbitrary"),
        name="combine_ln",
    )(x, y0, y1, gates, ln_g, ln_b)


def _t5_bucket(dist):
    max_exact = NUM_BUCKETS // 2
    d = jnp.maximum(dist, 1).astype(F32)
    large = max_exact + (jnp.log(d / max_exact) / math.log(T5_MAX_DISTANCE / max_exact)
                         * (NUM_BUCKETS - max_exact)).astype(jnp.int32)
    large = jnp.minimum(large, NUM_BUCKETS - 1)
    return jnp.where(dist < max_exact, dist, large)


def _band_bias_masked(table, stride, max_dist):
    dist = jnp.arange(BLOCK)[:, None] + BLOCK - jnp.arange(2 * BLOCK)[None, :]
    onehot = jax.nn.one_hot(_t5_bucket(jnp.maximum(dist, 0) * stride), NUM_BUCKETS, dtype=F32)
    bias = jnp.einsum("qkb,bh->hqk", onehot, table.astype(F32), precision=lax.Precision.HIGHEST)
    valid = (dist >= 0) & (dist <= max_dist)
    return jnp.where(valid[None], bias, NEG_INF)


def _rope_lane_tables(seq):
    half = B_ROPE_DIM // 2
    inv = ROPE_THETA ** (-jnp.arange(0, B_ROPE_DIM, 2, dtype=F32) / B_ROPE_DIM)
    ang = jnp.arange(seq, dtype=F32)[:, None] * inv[None, :]
    cos, sin = jnp.cos(ang), jnp.sin(ang)
    ones = jnp.ones((seq, ROPE_LANE0), F32)
    zeros = jnp.zeros((seq, ROPE_LANE0), F32)
    pad = jnp.zeros((seq, LANES - ROPE_LANE0 - 2 * half), F32)
    cos_t = jnp.concatenate([ones, cos, cos, pad], axis=1)
    sin_t = jnp.concatenate([zeros, -sin, sin, pad], axis=1)
    return cos_t, sin_t


def _permute_heads(t, axis, order):
    parts = [lax.slice_in_dim(t, hh * HEAD_DIM, (hh + 1) * HEAD_DIM, axis=axis) for hh in order]
    return jnp.concatenate(parts, axis=axis)


IN_SEGMENTS = (("qa", 512), ("ka", 128), ("va", 128), ("cq", 384), ("ckv", 256), ("kr", 32),
               ("qc", 512), ("kc", 512), ("vc", 512), ("qd", 512), ("kd", 512), ("vd", 512))
IN_WIDTH = sum(size for _, size in IN_SEGMENTS)


def _relayout_w_in_kernel(w_ref, main_ref, d_ref):
    w = w_ref[0]
    rows = w.shape[0]
    seg = {}
    start = 0
    for name, size in IN_SEGMENTS:
        seg[name] = w[:, start:start + size]
        start += size
    qa = jnp.concatenate([seg["qa"][:, hh * HEAD_DIM:(hh + 1) * HEAD_DIM] for hh in A_HEAD_ORDER], axis=1)
    kr = jnp.concatenate([jnp.zeros((rows, ROPE_LANE0), F32), seg["kr"],
                          jnp.zeros((rows, LANES - ROPE_LANE0 - B_ROPE_DIM), F32)], axis=1)
    main = jnp.concatenate([qa, seg["qc"], seg["kc"], seg["cq"], seg["ka"], seg["vc"], seg["ckv"], seg["va"], kr],
                           axis=1)
    main_ref[0] = main.astype(BF16)
    d_ref[0] = jnp.concatenate([seg["qd"], seg["kd"], seg["vd"]], axis=1).astype(BF16)


def _relayout_w_in(w_in, *, tr):
    d = w_in.shape[0]
    return pl.pallas_call(
        _relayout_w_in_kernel,
        grid=(d, D_MODEL // tr),
        in_specs=[pl.BlockSpec((1, tr, IN_WIDTH), lambda l, i: (l, i, 0))],
        out_specs=[pl.BlockSpec((1, tr, IN_WIDTH_P), lambda l, i: (l, i, 0)),
                   pl.BlockSpec((1, tr, IN_WIDTH_D), lambda l, i: (l, i, 0))],
        out_shape=[jax.ShapeDtypeStruct((d, D_MODEL, IN_WIDTH_P), BF16),
                   jax.ShapeDtypeStruct((d, D_MODEL, IN_WIDTH_D), BF16)],
        compiler_params=_params("arbitrary", "arbitrary"),
        name="relayout_w_in",
    )(w_in)


def _relayout_mla(w_uq, w_ukv):
    d = w_uq.shape[0]
    half = B_ROPE_DIM // 2
    wq = w_uq.astype(BF16).reshape(d, B_Q_LORA, N_HEADS, B_NOPE_DIM + B_ROPE_DIM)
    nope, r1, r2 = wq[..., :B_NOPE_DIM], wq[..., B_NOPE_DIM:B_NOPE_DIM + half], wq[..., B_NOPE_DIM + half:]
    z32 = jnp.zeros(wq.shape[:3] + (LANES - B_NOPE_DIM - B_ROPE_DIM,), BF16)
    z64 = jnp.zeros(wq.shape[:3] + (B_NOPE_DIM,), BF16)
    wq_t = jnp.concatenate([nope, r1, r2, z32], axis=-1).reshape(d, B_Q_LORA, N_HEADS * LANES)
    wq_s = jnp.concatenate([z64, r2, r1, z32], axis=-1).reshape(d, B_Q_LORA, N_HEADS * LANES)
    wkv = w_ukv.astype(BF16).reshape(d, B_KV_LORA, N_HEADS, 2 * HEAD_DIM)
    zk = jnp.zeros(wkv.shape[:3] + (LANES - B_NOPE_DIM,), BF16)
    wk_t = jnp.concatenate([wkv[..., :B_NOPE_DIM], zk], axis=-1).reshape(d, B_KV_LORA, N_HEADS * LANES)
    wv = wkv[..., B_NOPE_DIM:].reshape(d, B_KV_LORA, GROUP_WIDTH)
    return wq_t, wq_s, wk_t, wv


def _rope_swap_matrix():
    half = B_ROPE_DIM // 2
    src = jnp.arange(LANES)[:, None]
    dst = jnp.arange(LANES)[None, :]
    first = (dst >= ROPE_LANE0) & (dst < ROPE_LANE0 + half) & (src == dst + half)
    second = (dst >= ROPE_LANE0 + half) & (dst < ROPE_LANE0 + 2 * half) & (src == dst - half)
    return (first | second).astype(BF16)


def _route(logits, tm):
    n = logits.shape[0]
    top_logits, top_idx = lax.top_k(logits, TOP_K)
    gates = jax.nn.softmax(top_logits, axis=-1)
    onehot = jax.nn.one_hot(top_idx, N_EXPERTS, dtype=jnp.int32)
    member = jnp.sum(onehot, axis=1)
    rank = jnp.cumsum(member, axis=0) - member
    counts = jnp.sum(member, axis=0)
    padded = ((counts + tm - 1) // tm) * tm
    ends = jnp.cumsum(padded)
    first = ends - counts
    pos = jnp.sum(onehot * (first[None, None, :] + rank[:, None, :]), axis=-1)
    n_rows = TOP_K * n + N_EXPERTS * tm
    src = (jnp.arange(n_rows, dtype=jnp.int32) % n).at[pos.reshape(-1)].set(
        jnp.repeat(jnp.arange(n, dtype=jnp.int32), TOP_K))

    def tile_tables(tile):
        tile_start = jnp.arange(n_rows // tile, dtype=jnp.int32) * tile
        tile_expert = jnp.minimum(jnp.sum((tile_start[:, None] >= ends[None, :]).astype(jnp.int32), axis=1),
                                  N_EXPERTS - 1)
        onehot_te = jax.nn.one_hot(tile_expert, N_EXPERTS, dtype=jnp.int32)
        group_first = jnp.sum(onehot_te * first[None, :], axis=1)
        rows = jnp.clip(tile_start + tile - group_first, 0, tile)
        rows = jnp.where(tile_start < ends[-1], rows, 0)
        return tile_expert.astype(jnp.int32), rows.astype(jnp.int32)

    return pos, gates, src, tile_tables


def kernel(x, w_in, w_o, mla_q_norm, mla_kv_norm, mla_w_uq, mla_w_ukv, attn_sinks, rel_bias_table, mix_norm_g,
           ln1_g, ln1_b, ln2_g, ln2_b, ffn_w_gate, ffn_w_up, ffn_w_down, moe_router, moe_w_gate, moe_w_up,
           moe_w_down):
    batch, seq, _ = x.shape
    n = batch * seq

    w_in_p, w_in_d = _relayout_w_in(w_in, tr=256)
    w_o_p = jnp.concatenate([_permute_heads(w_o[:, :GROUP_WIDTH], 1, A_HEAD_ORDER), w_o[:, GROUP_WIDTH:]],
                            axis=1).astype(BF16)
    mix_g_p = jnp.concatenate([_permute_heads(mix_norm_g[:, :1], 2, A_HEAD_ORDER), mix_norm_g[:, 1:]], axis=1)
    wq_t, wq_s, wk_t, wv = _relayout_mla(mla_w_uq, mla_w_ukv)
    psw = _rope_swap_matrix()
    cos_t, sin_t = _rope_lane_tables(seq)
    tq = 256
    tri = (jnp.arange(tq)[:, None] > jnp.arange(tq)[None, :]).astype(BF16)
    order = list(A_HEAD_ORDER)
    def with_first_block_variant(b):
        prev = jnp.arange(2 * BLOCK) < BLOCK
        return jnp.stack([b, jnp.where(prev, NEG_INF, b)], axis=0) * LOG2E

    bias_a = _band_bias_masked(rel_bias_table[:, :N_HEADS], 1, A_WINDOW - 1)
    bias_a_rows = with_first_block_variant(jnp.concatenate([bias_a[hh] for hh in order], axis=0))
    sink_rows = jnp.concatenate([jnp.broadcast_to(attn_sinks[:, hh, None, None] * LOG2E, (DEPTH, BLOCK, LANES))
                                 for hh in order], axis=1)
    biases_d = [with_first_block_variant(_band_bias_masked(rel_bias_table[:, N_HEADS:], rate, window // rate).reshape(
        N_HEADS // 2, 2 * BLOCK, 2 * BLOCK)) for window, rate in D_PATTERNS]
    router_p = jnp.pad(moe_router, ((0, 0), (0, 0), (0, LANES - N_EXPERTS))).astype(BF16)
    wd_d = ffn_w_down.astype(BF16)
    moe_tm = 1024

    xf = x.reshape(n, D_MODEL)
    xb = xf
    for layer in range(DEPTH):
        h = matmul_ws(xb, w_in_p, layer, 1024, 1024, BF16)
        hd = matmul_ws(xb, w_in_d, layer, 1024, 512, F32)
        h3 = h.reshape(batch, seq, IN_WIDTH_P)
        ga = swa_attention(h3, bias_a_rows, sink_rows[layer], batch=batch, seq=seq)
        q_b, k_b, v_b = mla_up(h, mla_q_norm[layer][None], mla_kv_norm[layer][None], wq_t[layer], wq_s[layer],
                               wk_t[layer], wv[layer], psw, cos_t, sin_t, seq=seq, tm=512)
        gb = mla_attention(q_b.reshape(batch, seq, -1), k_b.reshape(batch, seq, -1),
                           v_b.reshape(batch, seq, -1), batch=batch, seq=seq, tq=tq, heads=4)
        gc = stick_breaking_attention(h3, tri, batch=batch, seq=seq, tq=tq, pairs=2)
        gd = dilated_attention(hd.reshape(batch, seq, IN_WIDTH_D), biases_d, batch=batch, seq=seq)
        i = layer // 2
        router = router_p[i] if layer % 2 == 1 else jnp.zeros((D_MODEL, LANES), BF16)
        xf, xb, xp, logits = out_proj_ln(ga.reshape(n, -1), gb.reshape(n, -1), gc.reshape(n, -1), gd.reshape(n, -1),
                                     xf, mix_g_p[layer], w_o_p, layer, ln1_g[layer][None], ln1_b[layer][None],
                                     router, tm=256)
        if layer % 2 == 0:
            hmid = ffn_gate_up(xb, ffn_w_gate, ffn_w_up, i, tm=1024, tf=512)
            xf, xb = ffn_down_ln(hmid, wd_d, i, xf, ln2_g[layer][None], ln2_b[layer][None], tm=512, tk=1408)
        else:
            pos, gates, src, tile_tables = _route(logits[:, :N_EXPERTS], moe_tm)
            te_up, tr_up = tile_tables(moe_tm)
            te_dn, tr_dn = tile_tables(moe_tm // 2)
            xs = jnp.take(xp, src, axis=0, mode="clip")
            hs = moe_gate_up(te_up, tr_up, xs, moe_w_gate, moe_w_up, i, tm=moe_tm, tf=512)
            ys = moe_down(te_dn, tr_dn, hs, moe_w_down, i, tm=moe_tm // 2, tn=512)
            y0 = jnp.take(ys, pos[:, 0], axis=0, mode="clip")
            y1 = jnp.take(ys, pos[:, 1], axis=0, mode="clip")
            xf, xb = combine_ln(xf, y0, y1, gates, ln2_g[layer][None], ln2_b[layer][None], tm=512)
    return xf.reshape(batch, seq, D_MODEL)
```

```python
import functools
import math

import jax
import jax.numpy as jnp
from jax import lax
from jax.experimental import pallas as pl
from jax.experimental.pallas import tpu as pltpu

D_MODEL = 2048
DEPTH = 4
HEAD_DIM = 64
N_HEADS = 8
GROUP_WIDTH = N_HEADS * HEAD_DIM
BLOCK = 128
A_KV_HEADS = 2
A_WINDOW = 128
B_NOPE_DIM = 64
B_ROPE_DIM = 32
B_Q_LORA = 384
B_KV_LORA = 256
ROPE_THETA = 10000.0
D_PATTERNS = ((128, 1), (512, 4), (2048, 16))
NUM_BUCKETS = 32
T5_MAX_DISTANCE = 2048
D_FF = 5632
N_EXPERTS = 8
TOP_K = 2
NEG_INF = -1e30
LN_EPS = 1e-5
RMS_EPS = 1e-6
ALPHA = (2 * DEPTH) ** 0.25

LANES = 128
VMEM_LIMIT = 56 * 1024 * 1024

COL_QA, COL_QC, COL_KC, COL_CQ, COL_KA = 0, 512, 1024, 1536, 1920
COL_VC, COL_CKV, COL_VA, COL_KR = 2048, 2560, 2816, 2944
IN_WIDTH_P = 3072
IN_WIDTH_D = 3 * GROUP_WIDTH
A_HEAD_ORDER = (0, 4, 1, 5, 2, 6, 3, 7)
ROPE_LANE0 = 64
LOG2E = math.log2(math.e)
EXP2_UNDERFLOW = 151.0
QK_SCALE = HEAD_DIM ** -0.5 * LOG2E

BF16 = jnp.bfloat16
F32 = jnp.float32


def _params(*sem):
    return pltpu.CompilerParams(dimension_semantics=sem, vmem_limit_bytes=VMEM_LIMIT)


def _dot(a, b):
    return jnp.dot(a, b, preferred_element_type=F32)


def _dot_nt(a, b):
    return lax.dot_general(a, b, (((1,), (1,)), ((), ())), preferred_element_type=F32)


def _lo_lanes(rows):
    return lax.broadcasted_iota(jnp.int32, (rows, LANES), 1) < HEAD_DIM


def _split_pair(t, lo):
    zero = jnp.zeros_like(t)
    return jnp.concatenate([jnp.where(lo, t, zero), jnp.where(lo, zero, t)], axis=0)


def _matmul_kernel(a_ref, w_ref, o_ref):
    o_ref[...] = _dot_nt(a_ref[...].astype(BF16), w_ref[0]).astype(o_ref.dtype)


def matmul_ws(a, w, layer, tm, tn, out_dtype):
    m, k = a.shape
    n = w.shape[1]
    return pl.pallas_call(
        _matmul_kernel,
        grid=(n // tn, m // tm),
        in_specs=[pl.BlockSpec((tm, k), lambda j, i: (i, 0)),
                  pl.BlockSpec((1, tn, k), lambda j, i: (layer, j, 0))],
        out_specs=pl.BlockSpec((tm, tn), lambda j, i: (i, j)),
        out_shape=jax.ShapeDtypeStruct((m, n), out_dtype),
        compiler_params=_params("arbitrary", "arbitrary"),
        name="matmul_ws",
    )(a, w)


def _swa_kernel(q_ref, kp_ref, kc_ref, vp_ref, vc_ref, bias_ref, sink_ref, o_ref):
    lo = _lo_lanes(BLOCK)
    kb = jnp.concatenate([kp_ref[0], kc_ref[0]], axis=0)
    vb = jnp.concatenate([vp_ref[0], vc_ref[0]], axis=0)
    which = jnp.where(pl.program_id(1) == 0, 1, 0)
    for c in range(GROUP_WIDTH // LANES):
        rows = slice(2 * c * BLOCK, 2 * (c + 1) * BLOCK)
        qs = _split_pair(q_ref[0, :, c * LANES:(c + 1) * LANES], lo) * QK_SCALE
        s = _dot_nt(qs, kb) + bias_ref[which, rows, :]
        sink = sink_ref[rows, :]
        m = jnp.maximum(jnp.max(s, axis=1, keepdims=True), sink)
        p = jnp.exp2(s - jnp.concatenate([m, m], axis=1))
        den = jnp.sum(p, axis=1, keepdims=True) + jnp.exp2(sink - m)
        o = _dot(p.astype(BF16), vb) / den
        o_ref[0, :, c * LANES:(c + 1) * LANES] = jnp.where(lo, o[:BLOCK], o[BLOCK:]).astype(o_ref.dtype)


def swa_attention(h3, bias_rows, sink_rows, *, batch, seq):
    nb = seq // BLOCK
    kb, vb = COL_KA // LANES, COL_VA // LANES
    return pl.pallas_call(
        _swa_kernel,
        grid=(batch, nb),
        in_specs=[pl.BlockSpec((1, BLOCK, GROUP_WIDTH), lambda b, n: (b, n, COL_QA // GROUP_WIDTH)),
                  pl.BlockSpec((1, BLOCK, LANES), lambda b, n: (b, jnp.maximum(n - 1, 0), kb)),
                  pl.BlockSpec((1, BLOCK, LANES), lambda b, n: (b, n, kb)),
                  pl.BlockSpec((1, BLOCK, LANES), lambda b, n: (b, jnp.maximum(n - 1, 0), vb)),
                  pl.BlockSpec((1, BLOCK, LANES), lambda b, n: (b, n, vb)),
                  pl.BlockSpec((2, N_HEADS * BLOCK, 2 * BLOCK), lambda b, n: (0, 0, 0)),
                  pl.BlockSpec((N_HEADS * BLOCK, LANES), lambda b, n: (0, 0))],
        out_specs=pl.BlockSpec((1, BLOCK, GROUP_WIDTH), lambda b, n: (b, n, 0)),
        out_shape=jax.ShapeDtypeStruct((batch, seq, GROUP_WIDTH), BF16),
        compiler_params=_params("arbitrary", "arbitrary"),
        name="swa_attn",
    )(h3, h3, h3, h3, h3, bias_rows, sink_rows)


def _dilated_kernel(q_ref, k_ref, v_ref, b0_ref, b1_ref, b2_ref, o_ref, osc, lsc):
    seq = q_ref.shape[1]
    lo = _lo_lanes(BLOCK)
    for p,((_, rate), b_ref) in enumerate(zip(D_PATTERNS, (b0_ref, b1_ref, b2_ref))):
        nb = seq // (rate * BLOCK)

        def unit(u, carry, p=p, rate=rate, b_ref=b_ref, nb=nb):
            n = u // rate
            res = u - n * rate
            rows = pl.ds(n * (BLOCK * rate) + res, BLOCK, stride=rate)
            qs = _split_pair((q_ref[0, rows, :] * QK_SCALE).astype(BF16), lo)
            kc = k_ref[0, rows, :].astype(BF16)
            vc = v_ref[0, rows, :].astype(BF16)
            if nb > 1:
                prow = pl.ds(jnp.maximum(n - 1, 0) * (BLOCK * rate) + res, BLOCK, stride=rate)
                kb = jnp.concatenate([k_ref[0, prow, :].astype(BF16), kc], axis=0)
                vb = jnp.concatenate([v_ref[0, prow, :].astype(BF16), vc], axis=0)
                s = _dot_nt(qs, kb) + b_ref[jnp.where(n == 0, 1, 0), 0]
            else:
                vb = vc
                s = _dot_nt(qs, kc) + b_ref[0, 0, :, BLOCK:2 * BLOCK]
            m = jnp.max(s, axis=1, keepdims=True)
            e = jnp.exp2(s - m)
            den = jnp.sum(e, axis=1, keepdims=True)
            o = _dot(e.astype(BF16), vb) / den
            lse = jnp.broadcast_to(m + jnp.log2(den), (2 * BLOCK, LANES))
            osc[p, rows, :] = jnp.where(lo, o[:BLOCK], o[BLOCK:])
            lsc[p, rows, :] = jnp.where(lo, lse[:BLOCK], lse[BLOCK:])
            return carry

        lax.fori_loop(0, nb * rate, unit, 0, unroll=16)

    chunk = 2 * BLOCK

    def merge(t, carry):
        r = pl.ds(pl.multiple_of(t * chunk, chunk), chunk)
        l0, l1, l2 = lsc[0, r, :], lsc[1, r, :], lsc[2, r, :]
        m = jnp.maximum(jnp.maximum(l0, l1), l2)
        e0, e1, e2 = jnp.exp2(l0 - m), jnp.exp2(l1 - m), jnp.exp2(l2 - m)
        o = (e0 * osc[0, r, :] + e1 * osc[1, r, :] + e2 * osc[2, r, :]) / (e0 + e1 + e2)
        o_ref[0, r, :] = o.astype(o_ref.dtype)
        return carry

    lax.fori_loop(0, seq // chunk, merge, 0)


def dilated_attention(hd3, biases, *, batch, seq):
    pairs = N_HEADS // 2
    bspec = pl.BlockSpec((2, 1, 2 * BLOCK, 2 * BLOCK), lambda b, p: (0, p, 0, 0))
    return pl.pallas_call(
        _dilated_kernel,
        grid=(batch, pairs),
        in_specs=[pl.BlockSpec((1, seq, LANES), lambda b, p: (b, 0, p)),
                  pl.BlockSpec((1, seq, LANES), lambda b, p: (b, 0, pairs + p)),
                  pl.BlockSpec((1, seq, LANES), lambda b, p: (b, 0, 2 * pairs + p)),
                  bspec, bspec, bspec],
        out_specs=pl.BlockSpec((1, seq, LANES), lambda b, p: (b, 0, p)),
        out_shape=jax.ShapeDtypeStruct((batch, seq, GROUP_WIDTH), BF16),
        scratch_shapes=[pltpu.VMEM((len(D_PATTERNS), seq, LANES), F32),
                        pltpu.VMEM((len(D_PATTERNS), seq, LANES), F32)],
        compiler_params=_params("arbitrary", "arbitrary"),
        name="dilated_attn",
    )(hd3, hd3, hd3, *biases)


def _mla_up_kernel(cq_ref, ckv_ref, kr_ref, qn_ref, kvn_ref, wq_ref, wqs_ref, wk_ref, wv_ref, psw_ref,
                   cos_ref, sin_ref, q_ref, k_ref, v_ref, *, scale):
    def rms(x_ref, g_ref):
        x = x_ref[...].astype(F32)
        return (x * lax.rsqrt(jnp.mean(x * x, axis=-1, keepdims=True) + RMS_EPS) * g_ref[...]).astype(BF16)

    xq = rms(cq_ref, qn_ref)
    xkv = rms(ckv_ref, kvn_ref)
    cos = cos_ref[...]
    sin = sin_ref[...]
    t = _dot(xq, wq_ref[...])
    ts = _dot(xq, wqs_ref[...])
    kn = _dot(xkv, wk_ref[...])
    kr = kr_ref[...]
    rk = kr.astype(F32) * cos + _dot(kr, psw_ref[...]) * sin
    for h in range(N_HEADS):
        sl = slice(h * LANES, (h + 1) * LANES)
        q_ref[:, sl] = ((t[:, sl] * cos + ts[:, sl] * sin) * scale).astype(BF16)
        k_ref[:, sl] = (kn[:, sl] + rk).astype(BF16)
    v_ref[...] = _dot(xkv, wv_ref[...]).astype(BF16)


def mla_up(h, q_norm, kv_norm, wq, wqs, wk, wv, psw, cos_t, sin_t, *, seq, tm):
    n = h.shape[0]
    w = N_HEADS * LANES
    const = lambda shape: pl.BlockSpec(shape, lambda i: (0, 0))
    spt = seq // tm
    return pl.pallas_call(
        functools.partial(_mla_up_kernel, scale=(B_NOPE_DIM + B_ROPE_DIM) ** -0.5 * math.log2(math.e)),
        grid=(n // tm,),
        in_specs=[pl.BlockSpec((tm, B_Q_LORA), lambda i: (i, COL_CQ // B_Q_LORA)),
                  pl.BlockSpec((tm, B_KV_LORA), lambda i: (i, COL_CKV // B_KV_LORA)),
                  pl.BlockSpec((tm, LANES), lambda i: (i, COL_KR // LANES)),
                  const((1, B_Q_LORA)), const((1, B_KV_LORA)),
                  const((B_Q_LORA, w)), const((B_Q_LORA, w)), const((B_KV_LORA, w)),
                  const((B_KV_LORA, GROUP_WIDTH)), const((LANES, LANES)),
                  pl.BlockSpec((tm, LANES), lambda i: (i % spt, 0)),
                  pl.BlockSpec((tm, LANES), lambda i: (i % spt, 0))],
        out_specs=[pl.BlockSpec((tm, w), lambda i: (i, 0)),
                   pl.BlockSpec((tm, w), lambda i: (i, 0)),
                   pl.BlockSpec((tm, GROUP_WIDTH), lambda i: (i, 0))],
        out_shape=[jax.ShapeDtypeStruct((n, w), BF16), jax.ShapeDtypeStruct((n, w), BF16),
                   jax.ShapeDtypeStruct((n, GROUP_WIDTH), BF16)],
        compiler_params=_params("arbitrary"),
        name="mla_up",
    )(h, h, h, q_norm, kv_norm, wq, wqs, wk, wv, psw, cos_t, sin_t)


def _mla_attn_kernel(q_ref, k_ref, v_ref, o_ref, *, tq, heads):
    i = pl.program_id(2)
    lo = _lo_lanes(tq)
    row = lax.broadcasted_iota(jnp.int32, (tq, tq), 0)
    col = lax.broadcasted_iota(jnp.int32, (tq, tq), 1)
    causal = col <= row

    def logits(j):
        start = pl.multiple_of(j * tq, tq)
        return tuple(_dot_nt(q_ref[0, :, hh * LANES:(hh + 1) * LANES],
                             k_ref[0, pl.ds(start, tq), hh * LANES:(hh + 1) * LANES]) for hh in range(heads))

    def update(j, stats, s_all, masked):
        start = pl.multiple_of(j * tq, tq)
        out = []
        for hh in range(heads):
            m, l, acc = stats[3 * hh:3 * hh + 3]
            s = s_all[hh]
            if masked:
                s = jnp.where(causal, s, NEG_INF)
            m_new = jnp.maximum(m, jnp.max(s, axis=1, keepdims=True))
            p = jnp.exp2(s - m_new)
            a = jnp.exp2(m - m_new)
            pv = _dot(p.astype(BF16), v_ref[0, pl.ds(start, tq), (hh // 2) * LANES:(hh // 2 + 1) * LANES])
            part = functools.reduce(jnp.add, [p[:, c * LANES:(c + 1) * LANES] for c in range(tq // LANES)])
            out += [m_new, a * l + part, a * acc + pv]
        return tuple(out)

    def body(j, carry):
        s_next = logits(j + 1)
        return update(j, carry[0], carry[1], False), s_next

    init = (jnp.full((tq, 1), NEG_INF, F32), jnp.zeros((tq, LANES), F32), jnp.zeros((tq, LANES), F32)) * heads
    stats, s_diag = lax.fori_loop(0, i, body, (init, logits(0)))
    stats = update(i, stats, s_diag, True)
    for pp in range(heads // 2):
        (_, l0, a0), (_, l1, a1) = stats[6 * pp:6 * pp + 3], stats[6 * pp + 3:6 * pp + 6]
        o0 = a0 / jnp.sum(l0, axis=1, keepdims=True)
        o1 = a1 / jnp.sum(l1, axis=1, keepdims=True)
        o_ref[0, :, pp * LANES:(pp + 1) * LANES] = jnp.where(lo, o0, o1).astype(o_ref.dtype)


def mla_attention(q, k, v, *, batch, seq, tq, heads):
    pairs = heads // 2
    return pl.pallas_call(
        functools.partial(_mla_attn_kernel, tq=tq, heads=heads),
        grid=(batch, N_HEADS // heads, seq // tq),
        in_specs=[pl.BlockSpec((1, tq, heads * LANES), lambda b, g, i: (b, i, g)),
                  pl.BlockSpec((1, seq, heads * LANES), lambda b, g, i: (b, 0, g)),
                  pl.BlockSpec((1, seq, pairs * LANES), lambda b, g, i: (b, 0, g))],
        out_specs=pl.BlockSpec((1, tq, pairs * LANES), lambda b, g, i: (b, i, g)),
        out_shape=jax.ShapeDtypeStruct((batch, seq, GROUP_WIDTH), BF16),
        compiler_params=_params("arbitrary", "arbitrary", "arbitrary"),
        name="mla_attn",
    )(q, k, v)


def _stick_kernel(q_ref, k_ref, v_ref, tri_ref, o_ref, *, tq, pairs):
    i = pl.program_id(2)
    lo = _lo_lanes(tq)
    row = lax.broadcasted_iota(jnp.int32, (2 * tq, tq), 0)
    col = lax.broadcasted_iota(jnp.int32, (2 * tq, tq), 1)
    strict = col < jnp.where(row >= tq, row - tq, row)
    tri = tri_ref[...]
    qs = [_split_pair(q_ref[0, :, pp * LANES:(pp + 1) * LANES], lo) * QK_SCALE for pp in range(pairs)]

    def chunk(pp, j, run, masked):
        start = pl.multiple_of(j * tq, tq)
        z = _dot_nt(qs[pp], k_ref[0, pl.ds(start, tq), pp * LANES:(pp + 1) * LANES])
        soft = jnp.log2(1.0 + jnp.exp2(-jnp.abs(z)))
        keep = -(jnp.maximum(z, 0.0) + soft)
        hit = jnp.minimum(z, 0.0) - soft
        if masked:
            keep = jnp.where(strict, keep, 0.0)
        hi = keep.astype(BF16)
        rest = (keep - hi.astype(F32)).astype(BF16)
        after = _dot(hi, tri) + _dot(rest, tri)
        a = jnp.exp2(hit + after)
        if masked:
            a = jnp.where(strict, a, 0.0)
        contrib = jnp.exp2(run) * _dot(a.astype(BF16), v_ref[0, pl.ds(start, tq), pp * LANES:(pp + 1) * LANES])
        return contrib, run + jnp.sum(keep, axis=1, keepdims=True)

    state = []
    has_prev = i > 0
    for pp in range(pairs):
        acc, run = chunk(pp, i, jnp.zeros((2 * tq, 1), F32), True)
        contrib, run_prev = chunk(pp, jnp.maximum(i - 1, 0), run, False)
        state += [jnp.where(has_prev, run_prev, run), acc + jnp.where(has_prev, contrib, 0.0)]

    def cond(carry):
        t = carry[0]
        live = carry[1]
        for pp in range(1, pairs):
            live = jnp.maximum(live, carry[1 + 2 * pp])
        return jnp.logical_and(t < i, jnp.max(live) > -EXP2_UNDERFLOW)

    def body(carry):
        t = carry[0]
        out = [t + 1]
        for pp in range(pairs):
            contrib, run = chunk(pp, i - 1 - t, carry[1 + 2 * pp], False)
            out += [run, carry[2 + 2 * pp] + contrib]
        return tuple(out)

    final = lax.while_loop(cond, body, (jnp.int32(1), *state))
    for pp in range(pairs):
        acc = final[2 + 2 * pp]
        o_ref[0, :, pp * LANES:(pp + 1) * LANES] = jnp.where(lo, acc[:tq], acc[tq:]).astype(o_ref.dtype)


def stick_breaking_attention(h3, tri, *, batch, seq, tq, pairs):
    w = pairs * LANES
    return pl.pallas_call(
        functools.partial(_stick_kernel, tq=tq, pairs=pairs),
        grid=(batch, GROUP_WIDTH // w, seq // tq),
        in_specs=[pl.BlockSpec((1, tq, w), lambda b, g, i: (b, i, COL_QC // w + g)),
                  pl.BlockSpec((1, seq, w), lambda b, g, i: (b, 0, COL_KC // w + g)),
                  pl.BlockSpec((1, seq, w), lambda b, g, i: (b, 0, COL_VC // w + g)),
                  pl.BlockSpec((tq, tq), lambda b, g, i: (0, 0))],
        out_specs=pl.BlockSpec((1, tq, w), lambda b, g, i: (b, i, g)),
        out_shape=jax.ShapeDtypeStruct((batch, seq, GROUP_WIDTH), BF16),
        compiler_params=_params("arbitrary", "arbitrary", "arbitrary"),
        name="stick_breaking",
    )(h3, h3, h3, tri)


def _layer_norm(r, g, b):
    mu = jnp.mean(r, axis=-1, keepdims=True)
    d = r - mu
    var = jnp.mean(d * d, axis=-1, keepdims=True)
    return d * lax.rsqrt(var + LN_EPS) * g + b


def _pack_bf16_pairs(xb):
    half = xb.shape[1] // 2
    lo = lax.bitcast_convert_type(xb[:, :half].astype(F32), jnp.uint32) >> 16
    hi = lax.bitcast_convert_type(xb[:, half:].astype(F32), jnp.uint32) & jnp.uint32(0xFFFF0000)
    return lax.bitcast_convert_type(hi | lo, F32)


def _unpack_bf16_pairs(xp):
    u = lax.bitcast_convert_type(xp, jnp.uint32)
    lo = lax.bitcast_convert_type(u << 16, F32).astype(BF16)
    hi = lax.bitcast_convert_type(u & jnp.uint32(0xFFFF0000), F32).astype(BF16)
    return jnp.concatenate([lo, hi], axis=1)


def _out_proj_kernel(ga_ref, gb_ref, gc_ref, gd_ref, x_ref, mixg_ref, wo_ref, g_ref, b_ref, rt_ref,
                     xo_ref, xb_ref, xp_ref, lg_ref):
    mixed = None
    for gi, grp in enumerate((ga_ref, gb_ref, gc_ref, gd_ref)):
        xg = grp[...].astype(F32)
        y = xg * lax.rsqrt(jnp.mean(xg * xg, axis=-1, keepdims=True) + RMS_EPS) * mixg_ref[gi:gi + 1, :]
        part = _dot(y.astype(BF16), wo_ref[0, gi * GROUP_WIDTH:(gi + 1) * GROUP_WIDTH, :])
        mixed = part if mixed is None else mixed + part
    x1 = _layer_norm(ALPHA * x_ref[...].astype(F32) + mixed, g_ref[...], b_ref[...])
    xo_ref[...] = x1
    xb = x1.astype(BF16)
    xb_ref[...] = xb
    xp_ref[...] = _pack_bf16_pairs(xb)
    lg_ref[...] = _dot(xb, rt_ref[...])


def out_proj_ln(ga, gb, gc, gd, x, mix_g, wo, layer, ln_g, ln_b, router, *, tm):
    n = x.shape[0]
    row = lambda w: pl.BlockSpec((tm, w), lambda i: (i, 0))
    const = lambda shape: pl.BlockSpec(shape, lambda i: (0, 0))
    return pl.pallas_call(
        _out_proj_kernel,
        grid=(n // tm,),
        in_specs=[row(GROUP_WIDTH)] * 4 + [row(D_MODEL), const((4, GROUP_WIDTH)),
                                          pl.BlockSpec((1, D_MODEL, D_MODEL), lambda i: (layer, 0, 0)),
                                          const((1, D_MODEL)), const((1, D_MODEL)), const((D_MODEL, LANES))],
        out_specs=[row(D_MODEL), row(D_MODEL), row(D_MODEL // 2), row(LANES)],
        out_shape=[jax.ShapeDtypeStruct((n, D_MODEL), F32), jax.ShapeDtypeStruct((n, D_MODEL), BF16),
                   jax.ShapeDtypeStruct((n, D_MODEL // 2), F32), jax.ShapeDtypeStruct((n, LANES), F32)],
        compiler_params=_params("arbitrary"),
        name="out_proj_ln",
    )(ga, gb, gc, gd, x, mix_g, wo, ln_g, ln_b, router)


def _silu_mul(g, u):
    return g * (1.0 / (1.0 + jnp.exp(-g))) * u


def _gate_up_kernel(x_ref, wg_ref, wu_ref, o_ref, wgb_ref, wub_ref):
    @pl.when(pl.program_id(1) == 0)
    def _():
        wgb_ref[...] = wg_ref[0].astype(BF16)
        wub_ref[...] = wu_ref[0].astype(BF16)

    x = x_ref[...]
    o_ref[...] = _silu_mul(_dot(x, wgb_ref[...]), _dot(x, wub_ref[...])).astype(o_ref.dtype)


def ffn_gate_up(xb, wg, wu, li, *, tm, tf):
    n = xb.shape[0]
    f = wg.shape[2]
    return pl.pallas_call(
        _gate_up_kernel,
        grid=(f // tf, n // tm),
        in_specs=[pl.BlockSpec((tm, D_MODEL), lambda j, i: (i, 0)),
                  pl.BlockSpec((1, D_MODEL, tf), lambda j, i: (li, 0, j)),
                  pl.BlockSpec((1, D_MODEL, tf), lambda j, i: (li, 0, j))],
        out_specs=pl.BlockSpec((tm, tf), lambda j, i: (i, j)),
        out_shape=jax.ShapeDtypeStruct((n, f), BF16),
        scratch_shapes=[pltpu.VMEM((D_MODEL, tf), BF16), pltpu.VMEM((D_MODEL, tf), BF16)],
        compiler_params=_params("arbitrary", "arbitrary"),
        name="ffn_gate_up",
    )(xb, wg, wu)


def _down_ln_kernel(h_ref, wd_ref, x_ref, g_ref, b_ref, xo_ref, xb_ref):
    kk = pl.program_id(1)
    part = _dot(h_ref[...], wd_ref[0])

    @pl.when(kk == 0)
    def _():
        xo_ref[...] = part

    @pl.when(kk > 0)
    def _():
        xo_ref[...] += part

    @pl.when(kk == pl.num_programs(1) - 1)
    def _():
        x2 = _layer_norm(ALPHA * x_ref[...] + xo_ref[...], g_ref[...], b_ref[...])
        xo_ref[...] = x2
        xb_ref[...] = x2.astype(BF16)


def ffn_down_ln(hmid, wd, li, x, ln_g, ln_b, *, tm, tk):
    n, f = hmid.shape
    return pl.pallas_call(
        _down_ln_kernel,
        grid=(n // tm, f // tk),
        in_specs=[pl.BlockSpec((tm, tk), lambda i, k: (i, k)),
                  pl.BlockSpec((1, tk, D_MODEL), lambda i, k: (li, k, 0)),
                  pl.BlockSpec((tm, D_MODEL), lambda i, k: (i, 0)),
                  pl.BlockSpec((1, D_MODEL), lambda i, k: (0, 0)),
                  pl.BlockSpec((1, D_MODEL), lambda i, k: (0, 0))],
        out_specs=[pl.BlockSpec((tm, D_MODEL), lambda i, k: (i, 0)),
                   pl.BlockSpec((tm, D_MODEL), lambda i, k: (i, 0))],
        out_shape=[jax.ShapeDtypeStruct((n, D_MODEL), F32), jax.ShapeDtypeStruct((n, D_MODEL), BF16)],
        compiler_params=_params("arbitrary", "arbitrary"),
        name="ffn_down_ln",
    )(hmid, wd, x, ln_g, ln_b)


def _new_expert(te_ref, t):
    return jnp.logical_or(t == 0, te_ref[t] != te_ref[jnp.maximum(t - 1, 0)])


MOE_ROW_STEP = 256


def _by_valid_rows(valid, tm, rows):
    for n in range(0, tm + 1, MOE_ROW_STEP):
        @pl.when(jnp.logical_and(valid > n - MOE_ROW_STEP, valid <= n))
        def _(n=n):
            rows(n)


def _moe_gate_up_kernel(te_ref, tv_ref, x_ref, wg_ref, wu_ref, o_ref, wgb_ref, wub_ref):
    t = pl.program_id(1)

    @pl.when(_new_expert(te_ref, t))
    def _():
        wgb_ref[...] = wg_ref[0, 0].astype(BF16)
        wub_ref[...] = wu_ref[0, 0].astype(BF16)

    def rows(n):
        pad = o_ref.shape[0] - n
        if n:
            x = _unpack_bf16_pairs(x_ref[pad:, :])
            o_ref[pad:, :] = _silu_mul(_dot(x, wgb_ref[...]), _dot(x, wub_ref[...])).astype(o_ref.dtype)
        if pad:
            o_ref[0:pad, :] = jnp.zeros((pad, o_ref.shape[1]), o_ref.dtype)

    _by_valid_rows(tv_ref[t], x_ref.shape[0], rows)


def moe_gate_up(tile_expert, tile_rows, xs, wg, wu, li, *, tm, tf):
    p = xs.shape[0]
    f = wg.shape[3]
    grid_spec = pltpu.PrefetchScalarGridSpec(
        num_scalar_prefetch=2,
        grid=(f // tf, p // tm),
        in_specs=[pl.BlockSpec((tm, D_MODEL // 2), lambda j, t, te, tv: (jnp.where(tv[t] > 0, t, 0), 0)),
                  pl.BlockSpec((1, 1, D_MODEL, tf), lambda j, t, te, tv: (li, te[t], 0, j)),
                  pl.BlockSpec((1, 1, D_MODEL, tf), lambda j, t, te, tv: (li, te[t], 0, j))],
        out_specs=pl.BlockSpec((tm, tf), lambda j, t, te, tv: (t, j)),
        scratch_shapes=[pltpu.VMEM((D_MODEL, tf), BF16), pltpu.VMEM((D_MODEL, tf), BF16)],
    )
    return pl.pallas_call(
        _moe_gate_up_kernel,
        grid_spec=grid_spec,
        out_shape=jax.ShapeDtypeStruct((p, f), BF16),
        compiler_params=_params("arbitrary", "arbitrary"),
        name="moe_gate_up",
    )(tile_expert, tile_rows, xs, wg, wu)


def _moe_down_kernel(te_ref, tv_ref, h_ref, wd_ref, o_ref, wdb_ref):
    t = pl.program_id(1)

    @pl.when(_new_expert(te_ref, t))
    def _():
        wdb_ref[...] = wd_ref[0, 0].astype(BF16)

    def rows(n):
        pad = o_ref.shape[0] - n
        if n:
            o_ref[pad:, :] = _pack_bf16_pairs(_dot(h_ref[pad:, :], wdb_ref[...]).astype(BF16))
        if pad:
            o_ref[0:pad, :] = jnp.zeros((pad, o_ref.shape[1]), o_ref.dtype)

    _by_valid_rows(tv_ref[t], h_ref.shape[0], rows)


def moe_down(tile_expert, tile_rows, hs, wd, li, *, tm, tn):
    p, f = hs.shape
    grid_spec = pltpu.PrefetchScalarGridSpec(
        num_scalar_prefetch=2,
        grid=(D_MODEL // tn, p // tm),
        in_specs=[pl.BlockSpec((tm, f), lambda j, t, te, tv: (jnp.where(tv[t] > 0, t, 0), 0)),
                  pl.BlockSpec((1, 1, f, tn), lambda j, t, te, tv: (li, te[t], 0, j))],
        out_specs=pl.BlockSpec((tm, tn // 2), lambda j, t, te, tv: (t, j)),
        scratch_shapes=[pltpu.VMEM((f, tn), BF16)],
    )
    return pl.pallas_call(
        _moe_down_kernel,
        grid_spec=grid_spec,
        out_shape=jax.ShapeDtypeStruct((p, D_MODEL // 2), F32),
        compiler_params=_params("arbitrary", "arbitrary"),
        name="moe_down",
    )(tile_expert, tile_rows, hs, wd)


def _combine_ln_kernel(x_ref, y0_ref, y1_ref, gate_ref, g_ref, b_ref, xo_ref, xb_ref, *, group):
    def unpack(y_ref):
        parts = [_unpack_bf16_pairs(y_ref[:, c:c + group]) for c in range(0, y_ref.shape[1], group)]
        return jnp.concatenate(parts, axis=1).astype(F32)

    f = gate_ref[:, 0:1] * unpack(y0_ref) + gate_ref[:, 1:2] * unpack(y1_ref)
    x2 = _layer_norm(ALPHA * x_ref[...] + f, g_ref[...], b_ref[...])
    xo_ref[...] = x2
    xb_ref[...] = x2.astype(BF16)


def combine_ln(x, y0, y1, gates, ln_g, ln_b, *, tm, group):
    n = x.shape[0]
    row = pl.BlockSpec((tm, D_MODEL), lambda i: (i, 0))
    packed = pl.BlockSpec((tm, D_MODEL // 2), lambda i: (i, 0))
    const = pl.BlockSpec((1, D_MODEL), lambda i: (0, 0))
    return pl.pallas_call(
        functools.partial(_combine_ln_kernel, group=group),
        grid=(n // tm,),
        in_specs=[row, packed, packed, pl.BlockSpec((tm, TOP_K), lambda i: (i, 0)), const, const],
        out_specs=[row, row],
        out_shape=[jax.ShapeDtypeStruct((n, D_MODEL), F32), jax.ShapeDtypeStruct((n, D_MODEL), BF16)],
        compiler_params=_params("arbitrary"),
        name="combine_ln",
    )(x, y0, y1, gates, ln_g, ln_b)


def _t5_bucket(dist):
    max_exact = NUM_BUCKETS // 2
    d = jnp.maximum(dist, 1).astype(F32)
    large = max_exact + (jnp.log(d / max_exact) / math.log(T5_MAX_DISTANCE / max_exact)
                         * (NUM_BUCKETS - max_exact)).astype(jnp.int32)
    large = jnp.minimum(large, NUM_BUCKETS - 1)
    return jnp.where(dist < max_exact, dist, large)


def _band_bias_masked(table, stride, max_dist):
    dist = jnp.arange(BLOCK)[:, None] + BLOCK - jnp.arange(2 * BLOCK)[None, :]
    onehot = jax.nn.one_hot(_t5_bucket(jnp.maximum(dist, 0) * stride), NUM_BUCKETS, dtype=F32)
    bias = jnp.einsum("qkb,bh->hqk", onehot, table.astype(F32), precision=lax.Precision.HIGHEST)
    valid = (dist >= 0) & (dist <= max_dist)
    return jnp.where(valid[None], bias, NEG_INF)


def _rope_lane_tables(seq):
    half = B_ROPE_DIM // 2
    inv = ROPE_THETA ** (-jnp.arange(0, B_ROPE_DIM, 2, dtype=F32) / B_ROPE_DIM)
    ang = jnp.arange(seq, dtype=F32)[:, None] * inv[None, :]
    cos, sin = jnp.cos(ang), jnp.sin(ang)
    ones = jnp.ones((seq, ROPE_LANE0), F32)
    zeros = jnp.zeros((seq, ROPE_LANE0), F32)
    pad = jnp.zeros((seq, LANES - ROPE_LANE0 - 2 * half), F32)
    cos_t = jnp.concatenate([ones, cos, cos, pad], axis=1)
    sin_t = jnp.concatenate([zeros, -sin, sin, pad], axis=1)
    return cos_t, sin_t


def _permute_heads(t, axis, order):
    parts = [lax.slice_in_dim(t, hh * HEAD_DIM, (hh + 1) * HEAD_DIM, axis=axis) for hh in order]
    return jnp.concatenate(parts, axis=axis)


IN_SEGMENTS = (("qa", 512), ("ka", 128), ("va", 128), ("cq", 384), ("ckv", 256), ("kr", 32),
               ("qc", 512), ("kc", 512), ("vc", 512), ("qd", 512), ("kd", 512), ("vd", 512))
IN_WIDTH = sum(size for _, size in IN_SEGMENTS)


def _relayout_w_in_kernel(w_ref, main_ref, d_ref):
    w = w_ref[0]
    cols = w.shape[1]
    seg = {}
    start = 0
    for name, size in IN_SEGMENTS:
        seg[name] = w[start:start + size, :]
        start += size
    qa = jnp.concatenate([seg["qa"][hh * HEAD_DIM:(hh + 1) * HEAD_DIM, :] for hh in A_HEAD_ORDER], axis=0)
    kr = jnp.concatenate([jnp.zeros((ROPE_LANE0, cols), F32), seg["kr"],
                          jnp.zeros((LANES - ROPE_LANE0 - B_ROPE_DIM, cols), F32)], axis=0)
    main = jnp.concatenate([qa, seg["qc"], seg["kc"], seg["cq"], seg["ka"], seg["vc"], seg["ckv"], seg["va"], kr],
                           axis=0)
    main_ref[0] = main.astype(BF16)
    d_ref[0] = jnp.concatenate([seg["qd"], seg["kd"], seg["vd"]], axis=0).astype(BF16)


def _relayout_w_in(w_in, *, tc):
    d = w_in.shape[0]
    return pl.pallas_call(
        _relayout_w_in_kernel,
        grid=(d, D_MODEL // tc),
        in_specs=[pl.BlockSpec((1, IN_WIDTH, tc), lambda l, i: (l, 0, i))],
        out_specs=[pl.BlockSpec((1, IN_WIDTH_P, tc), lambda l, i: (l, 0, i)),
                   pl.BlockSpec((1, IN_WIDTH_D, tc), lambda l, i: (l, 0, i))],
        out_shape=[jax.ShapeDtypeStruct((d, IN_WIDTH_P, D_MODEL), BF16),
                   jax.ShapeDtypeStruct((d, IN_WIDTH_D, D_MODEL), BF16)],
        compiler_params=_params("arbitrary", "arbitrary"),
        name="relayout_w_in",
    )(jnp.swapaxes(w_in, 1, 2))


def _relayout_mla(w_uq, w_ukv):
    d = w_uq.shape[0]
    half = B_ROPE_DIM // 2
    wq = w_uq.astype(BF16).reshape(d, B_Q_LORA, N_HEADS, B_NOPE_DIM + B_ROPE_DIM)
    nope, r1, r2 = wq[..., :B_NOPE_DIM], wq[..., B_NOPE_DIM:B_NOPE_DIM + half], wq[..., B_NOPE_DIM + half:]
    z32 = jnp.zeros(wq.shape[:3] + (LANES - B_NOPE_DIM - B_ROPE_DIM,), BF16)
    z64 = jnp.zeros(wq.shape[:3] + (B_NOPE_DIM,), BF16)
    wq_t = jnp.concatenate([nope, r1, r2, z32], axis=-1).reshape(d, B_Q_LORA, N_HEADS * LANES)
    wq_s = jnp.concatenate([z64, r2, r1, z32], axis=-1).reshape(d, B_Q_LORA, N_HEADS * LANES)
    wkv = w_ukv.astype(BF16).reshape(d, B_KV_LORA, N_HEADS, 2 * HEAD_DIM)
    zk = jnp.zeros(wkv.shape[:3] + (LANES - B_NOPE_DIM,), BF16)
    wk_t = jnp.concatenate([wkv[..., :B_NOPE_DIM], zk], axis=-1).reshape(d, B_KV_LORA, N_HEADS * LANES)
    wv = wkv[..., B_NOPE_DIM:].reshape(d, B_KV_LORA, GROUP_WIDTH)
    return wq_t, wq_s, wk_t, wv


def _rope_swap_matrix():
    half = B_ROPE_DIM // 2
    src = jnp.arange(LANES)[:, None]
    dst = jnp.arange(LANES)[None, :]
    first = (dst >= ROPE_LANE0) & (dst < ROPE_LANE0 + half) & (src == dst + half)
    second = (dst >= ROPE_LANE0 + half) & (dst < ROPE_LANE0 + 2 * half) & (src == dst - half)
    return (first | second).astype(BF16)


def _route(logits, tm):
    n = logits.shape[0]
    top_logits, top_idx = lax.top_k(logits, TOP_K)
    gates = jax.nn.softmax(top_logits, axis=-1)
    onehot = jax.nn.one_hot(top_idx, N_EXPERTS, dtype=jnp.int32)
    member = jnp.sum(onehot, axis=1)
    rank = jnp.cumsum(member, axis=0) - member
    counts = jnp.sum(member, axis=0)
    padded = ((counts + tm - 1) // tm) * tm
    ends = jnp.cumsum(padded)
    first = ends - counts
    pos = jnp.sum(onehot * (first[None, None, :] + rank[:, None, :]), axis=-1)
    n_rows = TOP_K * n + N_EXPERTS * tm
    src = (jnp.arange(n_rows, dtype=jnp.int32) % n).at[pos.reshape(-1)].set(
        jnp.repeat(jnp.arange(n, dtype=jnp.int32), TOP_K))

    def tile_tables(tile):
        tile_start = jnp.arange(n_rows // tile, dtype=jnp.int32) * tile
        tile_expert = jnp.minimum(jnp.sum((tile_start[:, None] >= ends[None, :]).astype(jnp.int32), axis=1),
                                  N_EXPERTS - 1)
        onehot_te = jax.nn.one_hot(tile_expert, N_EXPERTS, dtype=jnp.int32)
        group_first = jnp.sum(onehot_te * first[None, :], axis=1)
        rows = jnp.clip(tile_start + tile - group_first, 0, tile)
        rows = jnp.where(tile_start < ends[-1], rows, 0)
        return tile_expert.astype(jnp.int32), rows.astype(jnp.int32)

    return pos, gates, src, tile_tables


def kernel(x, w_in, w_o, mla_q_norm, mla_kv_norm, mla_w_uq, mla_w_ukv, attn_sinks, rel_bias_table, mix_norm_g,
           ln1_g, ln1_b, ln2_g, ln2_b, ffn_w_gate, ffn_w_up, ffn_w_down, moe_router, moe_w_gate, moe_w_up,
           moe_w_down):
    batch, seq, _ = x.shape
    n = batch * seq

    w_in_p, w_in_d = _relayout_w_in(w_in, tc=256)
    w_o_p = jnp.concatenate([_permute_heads(w_o[:, :GROUP_WIDTH], 1, A_HEAD_ORDER), w_o[:, GROUP_WIDTH:]],
                            axis=1).astype(BF16)
    mix_g_p = jnp.concatenate([_permute_heads(mix_norm_g[:, :1], 2, A_HEAD_ORDER), mix_norm_g[:, 1:]], axis=1)
    wq_t, wq_s, wk_t, wv = _relayout_mla(mla_w_uq, mla_w_ukv)
    psw = _rope_swap_matrix()
    cos_t, sin_t = _rope_lane_tables(seq)
    tq = 256
    tri = (jnp.arange(tq)[:, None] > jnp.arange(tq)[None, :]).astype(BF16)
    order = list(A_HEAD_ORDER)
    def with_first_block_variant(b):
        prev = jnp.arange(2 * BLOCK) < BLOCK
        return jnp.stack([b, jnp.where(prev, NEG_INF, b)], axis=0) * LOG2E

    bias_a = _band_bias_masked(rel_bias_table[:, :N_HEADS], 1, A_WINDOW - 1)
    bias_a_rows = with_first_block_variant(jnp.concatenate([bias_a[hh] for hh in order], axis=0))
    sink_rows = jnp.concatenate([jnp.broadcast_to(attn_sinks[:, hh, None, None] * LOG2E, (DEPTH, BLOCK, LANES))
                                 for hh in order], axis=1)
    biases_d = [with_first_block_variant(_band_bias_masked(rel_bias_table[:, N_HEADS:], rate, window // rate).reshape(
        N_HEADS // 2, 2 * BLOCK, 2 * BLOCK)) for window, rate in D_PATTERNS]
    router_p = jnp.pad(moe_router, ((0, 0), (0, 0), (0, LANES - N_EXPERTS))).astype(BF16)
    wd_d = ffn_w_down.astype(BF16)
    moe_tm = 1024

    xf = x.reshape(n, D_MODEL)
    xb = xf
    for layer in range(DEPTH):
        h = matmul_ws(xb, w_in_p, layer, 1024, 1024, BF16)
        hd = matmul_ws(xb, w_in_d, layer, 1024, 512, F32)
        h3 = h.reshape(batch, seq, IN_WIDTH_P)
        ga = swa_attention(h3, bias_a_rows, sink_rows[layer], batch=batch, seq=seq)
        q_b, k_b, v_b = mla_up(h, mla_q_norm[layer][None], mla_kv_norm[layer][None], wq_t[layer], wq_s[layer],
                               wk_t[layer], wv[layer], psw, cos_t, sin_t, seq=seq, tm=512)
        gb = mla_attention(q_b.reshape(batch, seq, -1), k_b.reshape(batch, seq, -1),
                           v_b.reshape(batch, seq, -1), batch=batch, seq=seq, tq=tq, heads=4)
        gc = stick_breaking_attention(h3, tri, batch=batch, seq=seq, tq=tq, pairs=2)
        gd = dilated_attention(hd.reshape(batch, seq, IN_WIDTH_D), biases_d, batch=batch, seq=seq)
        i = layer // 2
        router = router_p[i] if layer % 2 == 1 else jnp.zeros((D_MODEL, LANES), BF16)
        xf, xb, xp, logits = out_proj_ln(ga.reshape(n, -1), gb.reshape(n, -1), gc.reshape(n, -1), gd.reshape(n, -1),
                                     xf, mix_g_p[layer], w_o_p, layer, ln1_g[layer][None], ln1_b[layer][None],
                                     router, tm=256)
        if layer % 2 == 0:
            hmid = ffn_gate_up(xb, ffn_w_gate, ffn_w_up, i, tm=1024, tf=512)
            xf, xb = ffn_down_ln(hmid, wd_d, i, xf, ln2_g[layer][None], ln2_b[layer][None], tm=512, tk=1408)
        else:
            pos, gates, src, tile_tables = _route(logits[:, :N_EXPERTS], moe_tm)
            te_up, tr_up = tile_tables(moe_tm)
            te_dn, tr_dn = tile_tables(moe_tm // 2)
            xs = jnp.take(xp, src, axis=0, mode="clip")
            hs = moe_gate_up(te_up, tr_up, xs, moe_w_gate, moe_w_up, i, tm=moe_tm, tf=512)
            ys = moe_down(te_dn, tr_dn, hs, moe_w_down, i, tm=moe_tm // 2, tn=512)
            y0 = jnp.take(ys, pos[:, 0], axis=0, mode="clip")
            y1 = jnp.take(ys, pos[:, 1], axis=0, mode="clip")
            xf, xb = combine_ln(xf, y0, y1, gates, ln2_g[layer][None], ln2_b[layer][None], tm=512, group=256)
    return xf.reshape(batch, seq, D_MODEL)
```

```python
import functools
import math

import jax
import jax.numpy as jnp
from jax import lax
from jax.experimental import pallas as pl
from jax.experimental.pallas import tpu as pltpu

D_MODEL = 2048
DEPTH = 4
HEAD_DIM = 64
N_HEADS = 8
GROUP_WIDTH = N_HEADS * HEAD_DIM
BLOCK = 128
A_KV_HEADS = 2
A_WINDOW = 128
B_NOPE_DIM = 64
B_ROPE_DIM = 32
B_Q_LORA = 384
B_KV_LORA = 256
ROPE_THETA = 10000.0
D_PATTERNS = ((128, 1), (512, 4), (2048, 16))
NUM_BUCKETS = 32
T5_MAX_DISTANCE = 2048
D_FF = 5632
N_EXPERTS = 8
TOP_K = 2
NEG_INF = -1e30
LN_EPS = 1e-5
RMS_EPS = 1e-6
ALPHA = (2 * DEPTH) ** 0.25

LANES = 128
VMEM_LIMIT = 56 * 1024 * 1024

COL_QA, COL_QC, COL_KC, COL_CQ, COL_KA = 0, 512, 1024, 1536, 1920
COL_VC, COL_CKV, COL_VA, COL_KR = 2048, 2560, 2816, 2944
IN_WIDTH_P = 3072
IN_WIDTH_D = 3 * GROUP_WIDTH
A_HEAD_ORDER = (0, 4, 1, 5, 2, 6, 3, 7)
ROPE_LANE0 = 64
LOG2E = math.log2(math.e)
EXP2_UNDERFLOW = 151.0
QK_SCALE = HEAD_DIM ** -0.5 * LOG2E

BF16 = jnp.bfloat16
F32 = jnp.float32


def _params(*sem):
    return pltpu.CompilerParams(dimension_semantics=sem, vmem_limit_bytes=VMEM_LIMIT)


def _dot(a, b):
    return jnp.dot(a, b, preferred_element_type=F32)


def _dot_nt(a, b):
    return lax.dot_general(a, b, (((1,), (1,)), ((), ())), preferred_element_type=F32)


def _lo_lanes(rows):
    return lax.broadcasted_iota(jnp.int32, (rows, LANES), 1) < HEAD_DIM


def _split_pair(t, lo):
    zero = jnp.zeros_like(t)
    return jnp.concatenate([jnp.where(lo, t, zero), jnp.where(lo, zero, t)], axis=0)


def _matmul_kernel(a_ref, w_ref, o_ref):
    o_ref[...] = _dot_nt(a_ref[...].astype(BF16), w_ref[0]).astype(o_ref.dtype)


def matmul_ws(a, w, layer, tm, tn, out_dtype):
    m, k = a.shape
    n = w.shape[1]
    return pl.pallas_call(
        _matmul_kernel,
        grid=(n // tn, m // tm),
        in_specs=[pl.BlockSpec((tm, k), lambda j, i: (i, 0)),
                  pl.BlockSpec((1, tn, k), lambda j, i: (layer, j, 0))],
        out_specs=pl.BlockSpec((tm, tn), lambda j, i: (i, j)),
        out_shape=jax.ShapeDtypeStruct((m, n), out_dtype),
        compiler_params=_params("arbitrary", "arbitrary"),
        name="matmul_ws",
    )(a, w)


def _swa_kernel(q_ref, kp_ref, kc_ref, vp_ref, vc_ref, bias_ref, sink_ref, o_ref):
    lo = _lo_lanes(BLOCK)
    kb = jnp.concatenate([kp_ref[0], kc_ref[0]], axis=0)
    vb = jnp.concatenate([vp_ref[0], vc_ref[0]], axis=0)
    which = jnp.where(pl.program_id(1) == 0, 1, 0)
    for c in range(GROUP_WIDTH // LANES):
        rows = slice(2 * c * BLOCK, 2 * (c + 1) * BLOCK)
        qs = _split_pair(q_ref[0, :, c * LANES:(c + 1) * LANES], lo) * QK_SCALE
        s = _dot_nt(qs, kb) + bias_ref[which, rows, :]
        sink = sink_ref[rows, :]
        m = jnp.maximum(jnp.max(s, axis=1, keepdims=True), sink)
        p = jnp.exp2(s - jnp.concatenate([m, m], axis=1))
        den = jnp.sum(p, axis=1, keepdims=True) + jnp.exp2(sink - m)
        o = _dot(p.astype(BF16), vb) / den
        o_ref[0, :, c * LANES:(c + 1) * LANES] = jnp.where(lo, o[:BLOCK], o[BLOCK:]).astype(o_ref.dtype)


def swa_attention(h3, bias_rows, sink_rows, *, batch, seq):
    nb = seq // BLOCK
    kb, vb = COL_KA // LANES, COL_VA // LANES
    return pl.pallas_call(
        _swa_kernel,
        grid=(batch, nb),
        in_specs=[pl.BlockSpec((1, BLOCK, GROUP_WIDTH), lambda b, n: (b, n, COL_QA // GROUP_WIDTH)),
                  pl.BlockSpec((1, BLOCK, LANES), lambda b, n: (b, jnp.maximum(n - 1, 0), kb)),
                  pl.BlockSpec((1, BLOCK, LANES), lambda b, n: (b, n, kb)),
                  pl.BlockSpec((1, BLOCK, LANES), lambda b, n: (b, jnp.maximum(n - 1, 0), vb)),
                  pl.BlockSpec((1, BLOCK, LANES), lambda b, n: (b, n, vb)),
                  pl.BlockSpec((2, N_HEADS * BLOCK, 2 * BLOCK), lambda b, n: (0, 0, 0)),
                  pl.BlockSpec((N_HEADS * BLOCK, LANES), lambda b, n: (0, 0))],
        out_specs=pl.BlockSpec((1, BLOCK, GROUP_WIDTH), lambda b, n: (b, n, 0)),
        out_shape=jax.ShapeDtypeStruct((batch, seq, GROUP_WIDTH), BF16),
        compiler_params=_params("arbitrary", "arbitrary"),
        name="swa_attn",
    )(h3, h3, h3, h3, h3, bias_rows, sink_rows)


def _dilated_kernel(q_ref, k_ref, v_ref, b0_ref, b1_ref, b2_ref, o_ref, osc, msc, dsc):
    seq = q_ref.shape[1]
    lo = _lo_lanes(BLOCK)
    for p,((_, rate), b_ref) in enumerate(zip(D_PATTERNS, (b0_ref, b1_ref, b2_ref))):
        nb = seq // (rate * BLOCK)

        def unit(u, carry, p=p, rate=rate, b_ref=b_ref, nb=nb):
            n = u // rate
            res = u - n * rate
            rows = pl.ds(n * (BLOCK * rate) + res, BLOCK, stride=rate)
            qs = _split_pair((q_ref[0, rows, :] * QK_SCALE).astype(BF16), lo)
            kc = k_ref[0, rows, :].astype(BF16)
            vc = v_ref[0, rows, :].astype(BF16)
            if nb > 1:
                prow = pl.ds(jnp.maximum(n - 1, 0) * (BLOCK * rate) + res, BLOCK, stride=rate)
                kb = jnp.concatenate([k_ref[0, prow, :].astype(BF16), kc], axis=0)
                vb = jnp.concatenate([v_ref[0, prow, :].astype(BF16), vc], axis=0)
                s = _dot_nt(qs, kb) + b_ref[jnp.where(n == 0, 1, 0), 0]
            else:
                vb = vc
                s = _dot_nt(qs, kc) + b_ref[0, 0, :, BLOCK:2 * BLOCK]
            m = jnp.max(s, axis=1, keepdims=True)
            e = jnp.exp2(s - m)
            den = jnp.broadcast_to(jnp.sum(e, axis=1, keepdims=True), (2 * BLOCK, LANES))
            mx = jnp.broadcast_to(m, (2 * BLOCK, LANES))
            o = _dot(e.astype(BF16), vb)
            osc[p, rows, :] = jnp.where(lo, o[:BLOCK], o[BLOCK:])
            msc[p, rows, :] = jnp.where(lo, mx[:BLOCK], mx[BLOCK:])
            dsc[p, rows, :] = jnp.where(lo, den[:BLOCK], den[BLOCK:])
            return carry

        lax.fori_loop(0, nb * rate, unit, 0, unroll=16)

    chunk = 2 * BLOCK

    def merge(t, carry):
        r = pl.ds(pl.multiple_of(t * chunk, chunk), chunk)
        m0, m1, m2 = msc[0, r, :], msc[1, r, :], msc[2, r, :]
        m = jnp.maximum(jnp.maximum(m0, m1), m2)
        e0, e1, e2 = jnp.exp2(m0 - m), jnp.exp2(m1 - m), jnp.exp2(m2 - m)
        o = ((e0 * osc[0, r, :] + e1 * osc[1, r, :] + e2 * osc[2, r, :])
             / (e0 * dsc[0, r, :] + e1 * dsc[1, r, :] + e2 * dsc[2, r, :]))
        o_ref[0, r, :] = o.astype(o_ref.dtype)
        return carry

    lax.fori_loop(0, seq // chunk, merge, 0)


def dilated_attention(hd3, biases, *, batch, seq):
    pairs = N_HEADS // 2
    bspec = pl.BlockSpec((2, 1, 2 * BLOCK, 2 * BLOCK), lambda b, p: (0, p, 0, 0))
    return pl.pallas_call(
        _dilated_kernel,
        grid=(batch, pairs),
        in_specs=[pl.BlockSpec((1, seq, LANES), lambda b, p: (b, 0, p)),
                  pl.BlockSpec((1, seq, LANES), lambda b, p: (b, 0, pairs + p)),
                  pl.BlockSpec((1, seq, LANES), lambda b, p: (b, 0, 2 * pairs + p)),
                  bspec, bspec, bspec],
        out_specs=pl.BlockSpec((1, seq, LANES), lambda b, p: (b, 0, p)),
        out_shape=jax.ShapeDtypeStruct((batch, seq, GROUP_WIDTH), BF16),
        scratch_shapes=[pltpu.VMEM((len(D_PATTERNS), seq, LANES), F32)] * 3,
        compiler_params=_params("arbitrary", "arbitrary"),
        name="dilated_attn",
    )(hd3, hd3, hd3, *biases)


def _mla_up_kernel(cq_ref, ckv_ref, kr_ref, qn_ref, kvn_ref, wq_ref, wqs_ref, wk_ref, wv_ref, psw_ref,
                   cos_ref, sin_ref, q_ref, k_ref, v_ref, *, scale):
    def rms(x_ref, g_ref):
        x = x_ref[...].astype(F32)
        return (x * lax.rsqrt(jnp.mean(x * x, axis=-1, keepdims=True) + RMS_EPS) * g_ref[...]).astype(BF16)

    xq = rms(cq_ref, qn_ref)
    xkv = rms(ckv_ref, kvn_ref)
    cos = cos_ref[...]
    sin = sin_ref[...]
    t = _dot(xq, wq_ref[...])
    ts = _dot(xq, wqs_ref[...])
    kn = _dot(xkv, wk_ref[...])
    kr = kr_ref[...]
    rk = kr.astype(F32) * cos + _dot(kr, psw_ref[...]) * sin
    for h in range(N_HEADS):
        sl = slice(h * LANES, (h + 1) * LANES)
        q_ref[:, sl] = ((t[:, sl] * cos + ts[:, sl] * sin) * scale).astype(BF16)
        k_ref[:, sl] = (kn[:, sl] + rk).astype(BF16)
    v_ref[...] = _dot(xkv, wv_ref[...]).astype(BF16)


def mla_up(h, q_norm, kv_norm, wq, wqs, wk, wv, psw, cos_t, sin_t, *, seq, tm):
    n = h.shape[0]
    w = N_HEADS * LANES
    const = lambda shape: pl.BlockSpec(shape, lambda i: (0, 0))
    spt = seq // tm
    return pl.pallas_call(
        functools.partial(_mla_up_kernel, scale=(B_NOPE_DIM + B_ROPE_DIM) ** -0.5 * math.log2(math.e)),
        grid=(n // tm,),
        in_specs=[pl.BlockSpec((tm, B_Q_LORA), lambda i: (i, COL_CQ // B_Q_LORA)),
                  pl.BlockSpec((tm, B_KV_LORA), lambda i: (i, COL_CKV // B_KV_LORA)),
                  pl.BlockSpec((tm, LANES), lambda i: (i, COL_KR // LANES)),
                  const((1, B_Q_LORA)), const((1, B_KV_LORA)),
                  const((B_Q_LORA, w)), const((B_Q_LORA, w)), const((B_KV_LORA, w)),
                  const((B_KV_LORA, GROUP_WIDTH)), const((LANES, LANES)),
                  pl.BlockSpec((tm, LANES), lambda i: (i % spt, 0)),
                  pl.BlockSpec((tm, LANES), lambda i: (i % spt, 0))],
        out_specs=[pl.BlockSpec((tm, w), lambda i: (i, 0)),
                   pl.BlockSpec((tm, w), lambda i: (i, 0)),
                   pl.BlockSpec((tm, GROUP_WIDTH), lambda i: (i, 0))],
        out_shape=[jax.ShapeDtypeStruct((n, w), BF16), jax.ShapeDtypeStruct((n, w), BF16),
                   jax.ShapeDtypeStruct((n, GROUP_WIDTH), BF16)],
        compiler_params=_params("arbitrary"),
        name="mla_up",
    )(h, h, h, q_norm, kv_norm, wq, wqs, wk, wv, psw, cos_t, sin_t)


def _mla_attn_kernel(q_ref, k_ref, v_ref, o_ref, *, tq, heads):
    i = pl.program_id(2)
    lo = _lo_lanes(tq)
    row = lax.broadcasted_iota(jnp.int32, (tq, tq), 0)
    col = lax.broadcasted_iota(jnp.int32, (tq, tq), 1)
    causal = col <= row

    def logits(j):
        start = pl.multiple_of(j * tq, tq)
        return tuple(_dot_nt(q_ref[0, :, hh * LANES:(hh + 1) * LANES],
                             k_ref[0, pl.ds(start, tq), hh * LANES:(hh + 1) * LANES]) for hh in range(heads))

    def update(j, stats, s_all, masked):
        start = pl.multiple_of(j * tq, tq)
        out = []
        for hh in range(heads):
            m, l, acc = stats[3 * hh:3 * hh + 3]
            s = s_all[hh]
            if masked:
                s = jnp.where(causal, s, NEG_INF)
            m_new = jnp.maximum(m, jnp.max(s, axis=1, keepdims=True))
            p = jnp.exp2(s - m_new)
            a = jnp.exp2(m - m_new)
            pv = _dot(p.astype(BF16), v_ref[0, pl.ds(start, tq), (hh // 2) * LANES:(hh // 2 + 1) * LANES])
            part = functools.reduce(jnp.add, [p[:, c * LANES:(c + 1) * LANES] for c in range(tq // LANES)])
            out += [m_new, a * l + part, a * acc + pv]
        return tuple(out)

    init = (jnp.full((tq, 1), NEG_INF, F32), jnp.zeros((tq, LANES), F32), jnp.zeros((tq, LANES), F32)) * heads
    stats = lax.fori_loop(0, i, lambda j, st: update(j, st, logits(j), False), init)
    stats = update(i, stats, logits(i), True)
    for pp in range(heads // 2):
        (_, l0, a0), (_, l1, a1) = stats[6 * pp:6 * pp + 3], stats[6 * pp + 3:6 * pp + 6]
        o0 = a0 / jnp.sum(l0, axis=1, keepdims=True)
        o1 = a1 / jnp.sum(l1, axis=1, keepdims=True)
        o_ref[0, :, pp * LANES:(pp + 1) * LANES] = jnp.where(lo, o0, o1).astype(o_ref.dtype)


def mla_attention(q, k, v, *, batch, seq, tq, heads):
    pairs = heads // 2
    return pl.pallas_call(
        functools.partial(_mla_attn_kernel, tq=tq, heads=heads),
        grid=(batch, N_HEADS // heads, seq // tq),
        in_specs=[pl.BlockSpec((1, tq, heads * LANES), lambda b, g, i: (b, i, g)),
                  pl.BlockSpec((1, seq, heads * LANES), lambda b, g, i: (b, 0, g)),
                  pl.BlockSpec((1, seq, pairs * LANES), lambda b, g, i: (b, 0, g))],
        out_specs=pl.BlockSpec((1, tq, pairs * LANES), lambda b, g, i: (b, i, g)),
        out_shape=jax.ShapeDtypeStruct((batch, seq, GROUP_WIDTH), BF16),
        compiler_params=_params("arbitrary", "arbitrary", "arbitrary"),
        name="mla_attn",
    )(q, k, v)


def _stick_kernel(q_ref, k_ref, v_ref, tri_ref, o_ref, *, tq, pairs):
    i = pl.program_id(2)
    lo = _lo_lanes(tq)
    row = lax.broadcasted_iota(jnp.int32, (2 * tq, tq), 0)
    col = lax.broadcasted_iota(jnp.int32, (2 * tq, tq), 1)
    strict = col < jnp.where(row >= tq, row - tq, row)
    tri = tri_ref[...]
    qs = [_split_pair(q_ref[0, :, pp * LANES:(pp + 1) * LANES], lo) * QK_SCALE for pp in range(pairs)]

    def chunk(pp, j, run, masked):
        start = pl.multiple_of(j * tq, tq)
        z = _dot_nt(qs[pp], k_ref[0, pl.ds(start, tq), pp * LANES:(pp + 1) * LANES])
        soft = jnp.log2(1.0 + jnp.exp2(-jnp.abs(z)))
        keep = -(jnp.maximum(z, 0.0) + soft)
        hit = jnp.minimum(z, 0.0) - soft
        if masked:
            keep = jnp.where(strict, keep, 0.0)
        hi = keep.astype(BF16)
        rest = (keep - hi.astype(F32)).astype(BF16)
        after = _dot(hi, tri) + _dot(rest, tri)
        a = jnp.exp2(hit + after)
        if masked:
            a = jnp.where(strict, a, 0.0)
        contrib = jnp.exp2(run) * _dot(a.astype(BF16), v_ref[0, pl.ds(start, tq), pp * LANES:(pp + 1) * LANES])
        return contrib, run + jnp.sum(keep, axis=1, keepdims=True)

    state = []
    has_prev = i > 0
    for pp in range(pairs):
        acc, run = chunk(pp, i, jnp.zeros((2 * tq, 1), F32), True)
        contrib, run_prev = chunk(pp, jnp.maximum(i - 1, 0), run, False)
        state += [jnp.where(has_prev, run_prev, run), acc + jnp.where(has_prev, contrib, 0.0)]

    def cond(carry):
        t = carry[0]
        live = carry[1]
        for pp in range(1, pairs):
            live = jnp.maximum(live, carry[1 + 2 * pp])
        return jnp.logical_and(t < i, jnp.max(live) > -EXP2_UNDERFLOW)

    def body(carry):
        t = carry[0]
        out = [t + 1]
        for pp in range(pairs):
            contrib, run = chunk(pp, i - 1 - t, carry[1 + 2 * pp], False)
            out += [run, carry[2 + 2 * pp] + contrib]
        return tuple(out)

    final = lax.while_loop(cond, body, (jnp.int32(1), *state))
    for pp in range(pairs):
        acc = final[2 + 2 * pp]
        o_ref[0, :, pp * LANES:(pp + 1) * LANES] = jnp.where(lo, acc[:tq], acc[tq:]).astype(o_ref.dtype)


def stick_breaking_attention(h3, tri, *, batch, seq, tq, pairs):
    w = pairs * LANES
    return pl.pallas_call(
        functools.partial(_stick_kernel, tq=tq, pairs=pairs),
        grid=(batch, GROUP_WIDTH // w, seq // tq),
        in_specs=[pl.BlockSpec((1, tq, w), lambda b, g, i: (b, i, COL_QC // w + g)),
                  pl.BlockSpec((1, seq, w), lambda b, g, i: (b, 0, COL_KC // w + g)),
                  pl.BlockSpec((1, seq, w), lambda b, g, i: (b, 0, COL_VC // w + g)),
                  pl.BlockSpec((tq, tq), lambda b, g, i: (0, 0))],
        out_specs=pl.BlockSpec((1, tq, w), lambda b, g, i: (b, i, g)),
        out_shape=jax.ShapeDtypeStruct((batch, seq, GROUP_WIDTH), BF16),
        compiler_params=_params("arbitrary", "arbitrary", "arbitrary"),
        name="stick_breaking",
    )(h3, h3, h3, tri)


def _layer_norm(r, g, b):
    mu = jnp.mean(r, axis=-1, keepdims=True)
    d = r - mu
    var = jnp.mean(d * d, axis=-1, keepdims=True)
    return d * lax.rsqrt(var + LN_EPS) * g + b


def _pack_bf16_pairs(xb):
    half = xb.shape[1] // 2
    lo = lax.bitcast_convert_type(xb[:, :half].astype(F32), jnp.uint32) >> 16
    hi = lax.bitcast_convert_type(xb[:, half:].astype(F32), jnp.uint32) & jnp.uint32(0xFFFF0000)
    return lax.bitcast_convert_type(hi | lo, F32)


def _unpack_bf16_pairs(xp):
    u = lax.bitcast_convert_type(xp, jnp.uint32)
    lo = lax.bitcast_convert_type(u << 16, F32).astype(BF16)
    hi = lax.bitcast_convert_type(u & jnp.uint32(0xFFFF0000), F32).astype(BF16)
    return jnp.concatenate([lo, hi], axis=1)


def _out_proj_kernel(ga_ref, gb_ref, gc_ref, gd_ref, x_ref, mixg_ref, wo_ref, g_ref, b_ref, rt_ref,
                     xo_ref, xb_ref, xp_ref, lg_ref):
    mixed = None
    for gi, grp in enumerate((ga_ref, gb_ref, gc_ref, gd_ref)):
        xg = grp[...].astype(F32)
        y = xg * lax.rsqrt(jnp.mean(xg * xg, axis=-1, keepdims=True) + RMS_EPS) * mixg_ref[gi:gi + 1, :]
        part = _dot(y.astype(BF16), wo_ref[0, gi * GROUP_WIDTH:(gi + 1) * GROUP_WIDTH, :])
        mixed = part if mixed is None else mixed + part
    x1 = _layer_norm(ALPHA * x_ref[...].astype(F32) + mixed, g_ref[...], b_ref[...])
    xo_ref[...] = x1
    xb = x1.astype(BF16)
    xb_ref[...] = xb
    xp_ref[...] = _pack_bf16_pairs(xb)
    lg_ref[...] = _dot(xb, rt_ref[...])


def out_proj_ln(ga, gb, gc, gd, x, mix_g, wo, layer, ln_g, ln_b, router, *, tm):
    n = x.shape[0]
    row = lambda w: pl.BlockSpec((tm, w), lambda i: (i, 0))
    const = lambda shape: pl.BlockSpec(shape, lambda i: (0, 0))
    return pl.pallas_call(
        _out_proj_kernel,
        grid=(n // tm,),
        in_specs=[row(GROUP_WIDTH)] * 4 + [row(D_MODEL), const((4, GROUP_WIDTH)),
                                          pl.BlockSpec((1, D_MODEL, D_MODEL), lambda i: (layer, 0, 0)),
                                          const((1, D_MODEL)), const((1, D_MODEL)), const((D_MODEL, LANES))],
        out_specs=[row(D_MODEL), row(D_MODEL), row(D_MODEL // 2), row(LANES)],
        out_shape=[jax.ShapeDtypeStruct((n, D_MODEL), F32), jax.ShapeDtypeStruct((n, D_MODEL), BF16),
                   jax.ShapeDtypeStruct((n, D_MODEL // 2), F32), jax.ShapeDtypeStruct((n, LANES), F32)],
        compiler_params=_params("arbitrary"),
        name="out_proj_ln",
    )(ga, gb, gc, gd, x, mix_g, wo, ln_g, ln_b, router)


def _silu_mul(g, u):
    return g * (1.0 / (1.0 + jnp.exp(-g))) * u


def _gate_up_kernel(x_ref, wg_ref, wu_ref, o_ref, wgb_ref, wub_ref):
    @pl.when(pl.program_id(1) == 0)
    def _():
        wgb_ref[...] = wg_ref[0].astype(BF16)
        wub_ref[...] = wu_ref[0].astype(BF16)

    x = x_ref[...]
    o_ref[...] = _silu_mul(_dot(x, wgb_ref[...]), _dot(x, wub_ref[...])).astype(o_ref.dtype)


def ffn_gate_up(xb, wg, wu, li, *, tm, tf):
    n = xb.shape[0]
    f = wg.shape[2]
    return pl.pallas_call(
        _gate_up_kernel,
        grid=(f // tf, n // tm),
        in_specs=[pl.BlockSpec((tm, D_MODEL), lambda j, i: (i, 0)),
                  pl.BlockSpec((1, D_MODEL, tf), lambda j, i: (li, 0, j)),
                  pl.BlockSpec((1, D_MODEL, tf), lambda j, i: (li, 0, j))],
        out_specs=pl.BlockSpec((tm, tf), lambda j, i: (i, j)),
        out_shape=jax.ShapeDtypeStruct((n, f), BF16),
        scratch_shapes=[pltpu.VMEM((D_MODEL, tf), BF16), pltpu.VMEM((D_MODEL, tf), BF16)],
        compiler_params=_params("arbitrary", "arbitrary"),
        name="ffn_gate_up",
    )(xb, wg, wu)


def _down_ln_kernel(h_ref, wd_ref, x_ref, g_ref, b_ref, xo_ref, xb_ref):
    kk = pl.program_id(1)
    part = _dot(h_ref[...], wd_ref[0])

    @pl.when(kk == 0)
    def _():
        xo_ref[...] = part

    @pl.when(kk > 0)
    def _():
        xo_ref[...] += part

    @pl.when(kk == pl.num_programs(1) - 1)
    def _():
        x2 = _layer_norm(ALPHA * x_ref[...] + xo_ref[...], g_ref[...], b_ref[...])
        xo_ref[...] = x2
        xb_ref[...] = x2.astype(BF16)


def ffn_down_ln(hmid, wd, li, x, ln_g, ln_b, *, tm, tk):
    n, f = hmid.shape
    return pl.pallas_call(
        _down_ln_kernel,
        grid=(n // tm, f // tk),
        in_specs=[pl.BlockSpec((tm, tk), lambda i, k: (i, k)),
                  pl.BlockSpec((1, tk, D_MODEL), lambda i, k: (li, k, 0)),
                  pl.BlockSpec((tm, D_MODEL), lambda i, k: (i, 0)),
                  pl.BlockSpec((1, D_MODEL), lambda i, k: (0, 0)),
                  pl.BlockSpec((1, D_MODEL), lambda i, k: (0, 0))],
        out_specs=[pl.BlockSpec((tm, D_MODEL), lambda i, k: (i, 0)),
                   pl.BlockSpec((tm, D_MODEL), lambda i, k: (i, 0))],
        out_shape=[jax.ShapeDtypeStruct((n, D_MODEL), F32), jax.ShapeDtypeStruct((n, D_MODEL), BF16)],
        compiler_params=_params("arbitrary", "arbitrary"),
        name="ffn_down_ln",
    )(hmid, wd, x, ln_g, ln_b)


def _new_expert(te_ref, t):
    return jnp.logical_or(t == 0, te_ref[t] != te_ref[jnp.maximum(t - 1, 0)])


MOE_ROW_STEP = 256


def _by_valid_rows(valid, tm, rows):
    for n in range(0, tm + 1, MOE_ROW_STEP):
        @pl.when(jnp.logical_and(valid > n - MOE_ROW_STEP, valid <= n))
        def _(n=n):
            rows(n)


def _moe_gate_up_kernel(te_ref, tv_ref, x_ref, wg_ref, wu_ref, o_ref, wgb_ref, wub_ref):
    t = pl.program_id(1)

    @pl.when(_new_expert(te_ref, t))
    def _():
        wgb_ref[...] = wg_ref[0, 0].astype(BF16)
        wub_ref[...] = wu_ref[0, 0].astype(BF16)

    def rows(n):
        pad = o_ref.shape[0] - n
        if n:
            x = _unpack_bf16_pairs(x_ref[pad:, :])
            o_ref[pad:, :] = _silu_mul(_dot(x, wgb_ref[...]), _dot(x, wub_ref[...])).astype(o_ref.dtype)
        if pad:
            o_ref[0:pad, :] = jnp.zeros((pad, o_ref.shape[1]), o_ref.dtype)

    _by_valid_rows(tv_ref[t], x_ref.shape[0], rows)


def moe_gate_up(tile_expert, tile_rows, xs, wg, wu, li, *, tm, tf):
    p = xs.shape[0]
    f = wg.shape[3]
    grid_spec = pltpu.PrefetchScalarGridSpec(
        num_scalar_prefetch=2,
        grid=(f // tf, p // tm),
        in_specs=[pl.BlockSpec((tm, D_MODEL // 2), lambda j, t, te, tv: (jnp.where(tv[t] > 0, t, 0), 0)),
                  pl.BlockSpec((1, 1, D_MODEL, tf), lambda j, t, te, tv: (li, te[t], 0, j)),
                  pl.BlockSpec((1, 1, D_MODEL, tf), lambda j, t, te, tv: (li, te[t], 0, j))],
        out_specs=pl.BlockSpec((tm, tf), lambda j, t, te, tv: (t, j)),
        scratch_shapes=[pltpu.VMEM((D_MODEL, tf), BF16), pltpu.VMEM((D_MODEL, tf), BF16)],
    )
    return pl.pallas_call(
        _moe_gate_up_kernel,
        grid_spec=grid_spec,
        out_shape=jax.ShapeDtypeStruct((p, f), BF16),
        compiler_params=_params("arbitrary", "arbitrary"),
        name="moe_gate_up",
    )(tile_expert, tile_rows, xs, wg, wu)


def _moe_down_kernel(te_ref, tv_ref, h_ref, wd_ref, o_ref, wdb_ref):
    t = pl.program_id(1)

    @pl.when(_new_expert(te_ref, t))
    def _():
        wdb_ref[...] = wd_ref[0, 0].astype(BF16)

    def rows(n):
        pad = o_ref.shape[0] - n
        if n:
            o_ref[pad:, :] = _pack_bf16_pairs(_dot(h_ref[pad:, :], wdb_ref[...]).astype(BF16))
        if pad:
            o_ref[0:pad, :] = jnp.zeros((pad, o_ref.shape[1]), o_ref.dtype)

    _by_valid_rows(tv_ref[t], h_ref.shape[0], rows)


def moe_down(tile_expert, tile_rows, hs, wd, li, *, tm, tn):
    p, f = hs.shape
    grid_spec = pltpu.PrefetchScalarGridSpec(
        num_scalar_prefetch=2,
        grid=(D_MODEL // tn, p // tm),
        in_specs=[pl.BlockSpec((tm, f), lambda j, t, te, tv: (jnp.where(tv[t] > 0, t, 0), 0)),
                  pl.BlockSpec((1, 1, f, tn), lambda j, t, te, tv: (li, te[t], 0, j))],
        out_specs=pl.BlockSpec((tm, tn // 2), lambda j, t, te, tv: (t, j)),
        scratch_shapes=[pltpu.VMEM((f, tn), BF16)],
    )
    return pl.pallas_call(
        _moe_down_kernel,
        grid_spec=grid_spec,
        out_shape=jax.ShapeDtypeStruct((p, D_MODEL // 2), F32),
        compiler_params=_params("arbitrary", "arbitrary"),
        name="moe_down",
    )(tile_expert, tile_rows, hs, wd)


def _combine_ln_kernel(x_ref, y0_ref, y1_ref, gate_ref, g_ref, b_ref, xo_ref, xb_ref, *, group):
    def unpack(y_ref):
        parts = [_unpack_bf16_pairs(y_ref[:, c:c + group]) for c in range(0, y_ref.shape[1], group)]
        return jnp.concatenate(parts, axis=1).astype(F32)

    f = gate_ref[:, 0:1] * unpack(y0_ref) + gate_ref[:, 1:2] * unpack(y1_ref)
    x2 = _layer_norm(ALPHA * x_ref[...] + f, g_ref[...], b_ref[...])
    xo_ref[...] = x2
    xb_ref[...] = x2.astype(BF16)


def combine_ln(x, y0, y1, gates, ln_g, ln_b, *, tm, group):
    n = x.shape[0]
    row = pl.BlockSpec((tm, D_MODEL), lambda i: (i, 0))
    packed = pl.BlockSpec((tm, D_MODEL // 2), lambda i: (i, 0))
    const = pl.BlockSpec((1, D_MODEL), lambda i: (0, 0))
    return pl.pallas_call(
        functools.partial(_combine_ln_kernel, group=group),
        grid=(n // tm,),
        in_specs=[row, packed, packed, pl.BlockSpec((tm, TOP_K), lambda i: (i, 0)), const, const],
        out_specs=[row, row],
        out_shape=[jax.ShapeDtypeStruct((n, D_MODEL), F32), jax.ShapeDtypeStruct((n, D_MODEL), BF16)],
        compiler_params=_params("arbitrary"),
        name="combine_ln",
    )(x, y0, y1, gates, ln_g, ln_b)


def _t5_bucket(dist):
    max_exact = NUM_BUCKETS // 2
    d = jnp.maximum(dist, 1).astype(F32)
    large = max_exact + (jnp.log(d / max_exact) / math.log(T5_MAX_DISTANCE / max_exact)
                         * (NUM_BUCKETS - max_exact)).astype(jnp.int32)
    large = jnp.minimum(large, NUM_BUCKETS - 1)
    return jnp.where(dist < max_exact, dist, large)


def _band_bias_masked(table, stride, max_dist):
    dist = jnp.arange(BLOCK)[:, None] + BLOCK - jnp.arange(2 * BLOCK)[None, :]
    onehot = jax.nn.one_hot(_t5_bucket(jnp.maximum(dist, 0) * stride), NUM_BUCKETS, dtype=F32)
    bias = jnp.einsum("qkb,bh->hqk", onehot, table.astype(F32), precision=lax.Precision.HIGHEST)
    valid = (dist >= 0) & (dist <= max_dist)
    return jnp.where(valid[None], bias, NEG_INF)


def _rope_lane_tables(seq):
    half = B_ROPE_DIM // 2
    inv = ROPE_THETA ** (-jnp.arange(0, B_ROPE_DIM, 2, dtype=F32) / B_ROPE_DIM)
    ang = jnp.arange(seq, dtype=F32)[:, None] * inv[None, :]
    cos, sin = jnp.cos(ang), jnp.sin(ang)
    ones = jnp.ones((seq, ROPE_LANE0), F32)
    zeros = jnp.zeros((seq, ROPE_LANE0), F32)
    pad = jnp.zeros((seq, LANES - ROPE_LANE0 - 2 * half), F32)
    cos_t = jnp.concatenate([ones, cos, cos, pad], axis=1)
    sin_t = jnp.concatenate([zeros, -sin, sin, pad], axis=1)
    return cos_t, sin_t


def _permute_heads(t, axis, order):
    parts = [lax.slice_in_dim(t, hh * HEAD_DIM, (hh + 1) * HEAD_DIM, axis=axis) for hh in order]
    return jnp.concatenate(parts, axis=axis)


IN_SEGMENTS = (("qa", 512), ("ka", 128), ("va", 128), ("cq", 384), ("ckv", 256), ("kr", 32),
               ("qc", 512), ("kc", 512), ("vc", 512), ("qd", 512), ("kd", 512), ("vd", 512))
IN_WIDTH = sum(size for _, size in IN_SEGMENTS)


def _relayout_w_in_kernel(w_ref, main_ref, d_ref):
    w = w_ref[0]
    cols = w.shape[1]
    seg = {}
    start = 0
    for name, size in IN_SEGMENTS:
        seg[name] = w[start:start + size, :]
        start += size
    qa = jnp.concatenate([seg["qa"][hh * HEAD_DIM:(hh + 1) * HEAD_DIM, :] for hh in A_HEAD_ORDER], axis=0)
    kr = jnp.concatenate([jnp.zeros((ROPE_LANE0, cols), F32), seg["kr"],
                          jnp.zeros((LANES - ROPE_LANE0 - B_ROPE_DIM, cols), F32)], axis=0)
    main = jnp.concatenate([qa, seg["qc"], seg["kc"], seg["cq"], seg["ka"], seg["vc"], seg["ckv"], seg["va"], kr],
                           axis=0)
    main_ref[0] = main.astype(BF16)
    d_ref[0] = jnp.concatenate([seg["qd"], seg["kd"], seg["vd"]], axis=0).astype(BF16)


def _relayout_w_in(w_in, *, tc):
    d = w_in.shape[0]
    return pl.pallas_call(
        _relayout_w_in_kernel,
        grid=(d, D_MODEL // tc),
        in_specs=[pl.BlockSpec((1, IN_WIDTH, tc), lambda l, i: (l, 0, i))],
        out_specs=[pl.BlockSpec((1, IN_WIDTH_P, tc), lambda l, i: (l, 0, i)),
                   pl.BlockSpec((1, IN_WIDTH_D, tc), lambda l, i: (l, 0, i))],
        out_shape=[jax.ShapeDtypeStruct((d, IN_WIDTH_P, D_MODEL), BF16),
                   jax.ShapeDtypeStruct((d, IN_WIDTH_D, D_MODEL), BF16)],
        compiler_params=_params("arbitrary", "arbitrary"),
        name="relayout_w_in",
    )(jnp.swapaxes(w_in, 1, 2))


def _relayout_mla(w_uq, w_ukv):
    d = w_uq.shape[0]
    half = B_ROPE_DIM // 2
    wq = w_uq.astype(BF16).reshape(d, B_Q_LORA, N_HEADS, B_NOPE_DIM + B_ROPE_DIM)
    nope, r1, r2 = wq[..., :B_NOPE_DIM], wq[..., B_NOPE_DIM:B_NOPE_DIM + half], wq[..., B_NOPE_DIM + half:]
    z32 = jnp.zeros(wq.shape[:3] + (LANES - B_NOPE_DIM - B_ROPE_DIM,), BF16)
    z64 = jnp.zeros(wq.shape[:3] + (B_NOPE_DIM,), BF16)
    wq_t = jnp.concatenate([nope, r1, r2, z32], axis=-1).reshape(d, B_Q_LORA, N_HEADS * LANES)
    wq_s = jnp.concatenate([z64, r2, r1, z32], axis=-1).reshape(d, B_Q_LORA, N_HEADS * LANES)
    wkv = w_ukv.astype(BF16).reshape(d, B_KV_LORA, N_HEADS, 2 * HEAD_DIM)
    zk = jnp.zeros(wkv.shape[:3] + (LANES - B_NOPE_DIM,), BF16)
    wk_t = jnp.concatenate([wkv[..., :B_NOPE_DIM], zk], axis=-1).reshape(d, B_KV_LORA, N_HEADS * LANES)
    wv = wkv[..., B_NOPE_DIM:].reshape(d, B_KV_LORA, GROUP_WIDTH)
    return wq_t, wq_s, wk_t, wv


def _rope_swap_matrix():
    half = B_ROPE_DIM // 2
    src = jnp.arange(LANES)[:, None]
    dst = jnp.arange(LANES)[None, :]
    first = (dst >= ROPE_LANE0) & (dst < ROPE_LANE0 + half) & (src == dst + half)
    second = (dst >= ROPE_LANE0 + half) & (dst < ROPE_LANE0 + 2 * half) & (src == dst - half)
    return (first | second).astype(BF16)


def _route(logits, tm):
    n = logits.shape[0]
    top_logits, top_idx = lax.top_k(logits, TOP_K)
    gates = jax.nn.softmax(top_logits, axis=-1)
    onehot = jax.nn.one_hot(top_idx, N_EXPERTS, dtype=jnp.int32)
    member = jnp.sum(onehot, axis=1)
    rank = jnp.cumsum(member, axis=0) - member
    counts = jnp.sum(member, axis=0)
    padded = ((counts + tm - 1) // tm) * tm
    ends = jnp.cumsum(padded)
    first = ends - counts
    pos = jnp.sum(onehot * (first[None, None, :] + rank[:, None, :]), axis=-1)
    n_rows = TOP_K * n + N_EXPERTS * tm
    src = (jnp.arange(n_rows, dtype=jnp.int32) % n).at[pos.reshape(-1)].set(
        jnp.repeat(jnp.arange(n, dtype=jnp.int32), TOP_K))

    def tile_tables(tile):
        tile_start = jnp.arange(n_rows // tile, dtype=jnp.int32) * tile
        tile_expert = jnp.minimum(jnp.sum((tile_start[:, None] >= ends[None, :]).astype(jnp.int32), axis=1),
                                  N_EXPERTS - 1)
        onehot_te = jax.nn.one_hot(tile_expert, N_EXPERTS, dtype=jnp.int32)
        group_first = jnp.sum(onehot_te * first[None, :], axis=1)
        rows = jnp.clip(tile_start + tile - group_first, 0, tile)
        rows = jnp.where(tile_start < ends[-1], rows, 0)
        return tile_expert.astype(jnp.int32), rows.astype(jnp.int32)

    return pos, gates, src, tile_tables


def kernel(x, w_in, w_o, mla_q_norm, mla_kv_norm, mla_w_uq, mla_w_ukv, attn_sinks, rel_bias_table, mix_norm_g,
           ln1_g, ln1_b, ln2_g, ln2_b, ffn_w_gate, ffn_w_up, ffn_w_down, moe_router, moe_w_gate, moe_w_up,
           moe_w_down):
    batch, seq, _ = x.shape
    n = batch * seq

    w_in_p, w_in_d = _relayout_w_in(w_in, tc=256)
    w_o_p = jnp.concatenate([_permute_heads(w_o[:, :GROUP_WIDTH], 1, A_HEAD_ORDER), w_o[:, GROUP_WIDTH:]],
                            axis=1).astype(BF16)
    mix_g_p = jnp.concatenate([_permute_heads(mix_norm_g[:, :1], 2, A_HEAD_ORDER), mix_norm_g[:, 1:]], axis=1)
    wq_t, wq_s, wk_t, wv = _relayout_mla(mla_w_uq, mla_w_ukv)
    psw = _rope_swap_matrix()
    cos_t, sin_t = _rope_lane_tables(seq)
    tq = 256
    tri = (jnp.arange(tq)[:, None] > jnp.arange(tq)[None, :]).astype(BF16)
    order = list(A_HEAD_ORDER)
    def with_first_block_variant(b):
        prev = jnp.arange(2 * BLOCK) < BLOCK
        return jnp.stack([b, jnp.where(prev, NEG_INF, b)], axis=0) * LOG2E

    bias_a = _band_bias_masked(rel_bias_table[:, :N_HEADS], 1, A_WINDOW - 1)
    bias_a_rows = with_first_block_variant(jnp.concatenate([bias_a[hh] for hh in order], axis=0))
    sink_rows = jnp.concatenate([jnp.broadcast_to(attn_sinks[:, hh, None, None] * LOG2E, (DEPTH, BLOCK, LANES))
                                 for hh in order], axis=1)
    biases_d = [with_first_block_variant(_band_bias_masked(rel_bias_table[:, N_HEADS:], rate, window // rate).reshape(
        N_HEADS // 2, 2 * BLOCK, 2 * BLOCK)) for window, rate in D_PATTERNS]
    router_p = jnp.pad(moe_router, ((0, 0), (0, 0), (0, LANES - N_EXPERTS))).astype(BF16)
    wd_d = ffn_w_down.astype(BF16)
    moe_tm = 1024

    xf = x.reshape(n, D_MODEL)
    xb = xf
    for layer in range(DEPTH):
        h = matmul_ws(xb, w_in_p, layer, 1024, 1024, BF16)
        hd = matmul_ws(xb, w_in_d, layer, 1024, 512, F32)
        h3 = h.reshape(batch, seq, IN_WIDTH_P)
        ga = swa_attention(h3, bias_a_rows, sink_rows[layer], batch=batch, seq=seq)
        q_b, k_b, v_b = mla_up(h, mla_q_norm[layer][None], mla_kv_norm[layer][None], wq_t[layer], wq_s[layer],
                               wk_t[layer], wv[layer], psw, cos_t, sin_t, seq=seq, tm=512)
        gb = mla_attention(q_b.reshape(batch, seq, -1), k_b.reshape(batch, seq, -1),
                           v_b.reshape(batch, seq, -1), batch=batch, seq=seq, tq=tq, heads=8)
        gc = stick_breaking_attention(h3, tri, batch=batch, seq=seq, tq=tq, pairs=2)
        gd = dilated_attention(hd.reshape(batch, seq, IN_WIDTH_D), biases_d, batch=batch, seq=seq)
        i = layer // 2
        router = router_p[i] if layer % 2 == 1 else jnp.zeros((D_MODEL, LANES), BF16)
        xf, xb, xp, logits = out_proj_ln(ga.reshape(n, -1), gb.reshape(n, -1), gc.reshape(n, -1), gd.reshape(n, -1),
                                     xf, mix_g_p[layer], w_o_p, layer, ln1_g[layer][None], ln1_b[layer][None],
                                     router, tm=256)
        if layer % 2 == 0:
            hmid = ffn_gate_up(xb, ffn_w_gate, ffn_w_up, i, tm=1024, tf=512)
            xf, xb = ffn_down_ln(hmid, wd_d, i, xf, ln2_g[layer][None], ln2_b[layer][None], tm=512, tk=1408)
        else:
            pos, gates, src, tile_tables = _route(logits[:, :N_EXPERTS], moe_tm)
            te_up, tr_up = tile_tables(moe_tm)
            te_dn, tr_dn = tile_tables(moe_tm // 2)
            xs = jnp.take(xp, src, axis=0, mode="clip")
            hs = moe_gate_up(te_up, tr_up, xs, moe_w_gate, moe_w_up, i, tm=moe_tm, tf=512)
            ys = moe_down(te_dn, tr_dn, hs, moe_w_down, i, tm=moe_tm // 2, tn=512)
            y0 = jnp.take(ys, pos[:, 0], axis=0, mode="clip")
            y1 = jnp.take(ys, pos[:, 1], axis=0, mode="clip")
            xf, xb = combine_ln(xf, y0, y1, gates, ln2_g[layer][None], ln2_b[layer][None], tm=512, group=256)
    return xf.reshape(batch, seq, D_MODEL)
```

```python
import functools
import math

import jax
import jax.numpy as jnp
from jax import lax
from jax.experimental import pallas as pl
from jax.experimental.pallas import tpu as pltpu

D_MODEL = 2048
DEPTH = 4
HEAD_DIM = 64
N_HEADS = 8
GROUP_WIDTH = N_HEADS * HEAD_DIM
BLOCK = 128
A_KV_HEADS = 2
A_WINDOW = 128
B_NOPE_DIM = 64
B_ROPE_DIM = 32
B_Q_LORA = 384
B_KV_LORA = 256
ROPE_THETA = 10000.0
D_PATTERNS = ((128, 1), (512, 4), (2048, 16))
NUM_BUCKETS = 32
T5_MAX_DISTANCE = 2048
D_FF = 5632
N_EXPERTS = 8
TOP_K = 2
NEG_INF = -1e30
LN_EPS = 1e-5
RMS_EPS = 1e-6
ALPHA = (2 * DEPTH) ** 0.25

LANES = 128
VMEM_LIMIT = 56 * 1024 * 1024

COL_QA, COL_QC, COL_KC, COL_CQ, COL_KA = 0, 512, 1024, 1536, 1920
COL_VC, COL_CKV, COL_VA, COL_KR = 2048, 2560, 2816, 2944
IN_WIDTH_P = 3072
IN_WIDTH_D = 3 * GROUP_WIDTH
A_HEAD_ORDER = (0, 4, 1, 5, 2, 6, 3, 7)
ROPE_LANE0 = 64
LOG2E = math.log2(math.e)
EXP2_UNDERFLOW = 151.0
QK_SCALE = HEAD_DIM ** -0.5 * LOG2E

BF16 = jnp.bfloat16
F32 = jnp.float32


def _params(*sem):
    return pltpu.CompilerParams(dimension_semantics=sem, vmem_limit_bytes=VMEM_LIMIT)


def _dot(a, b):
    return jnp.dot(a, b, preferred_element_type=F32)


def _dot_nt(a, b):
    return lax.dot_general(a, b, (((1,), (1,)), ((), ())), preferred_element_type=F32)


def _lo_lanes(rows):
    return lax.broadcasted_iota(jnp.int32, (rows, LANES), 1) < HEAD_DIM


def _split_pair(t, lo):
    zero = jnp.zeros_like(t)
    return jnp.concatenate([jnp.where(lo, t, zero), jnp.where(lo, zero, t)], axis=0)


def _matmul_kernel(a_ref, w_ref, o_ref):
    o_ref[...] = _dot_nt(a_ref[...].astype(BF16), w_ref[0]).astype(o_ref.dtype)


def matmul_ws(a, w, layer, tm, tn, out_dtype):
    m, k = a.shape
    n = w.shape[1]
    return pl.pallas_call(
        _matmul_kernel,
        grid=(n // tn, m // tm),
        in_specs=[pl.BlockSpec((tm, k), lambda j, i: (i, 0)),
                  pl.BlockSpec((1, tn, k), lambda j, i: (layer, j, 0))],
        out_specs=pl.BlockSpec((tm, tn), lambda j, i: (i, j)),
        out_shape=jax.ShapeDtypeStruct((m, n), out_dtype),
        compiler_params=_params("arbitrary", "arbitrary"),
        name="matmul_ws",
    )(a, w)


def _swa_kernel(q_ref, kp_ref, kc_ref, vp_ref, vc_ref, bias_ref, sink_ref, o_ref, *, blocks):
    lo = _lo_lanes(BLOCK)
    k_all = jnp.concatenate([kp_ref[0], kc_ref[0]], axis=0)
    v_all = jnp.concatenate([vp_ref[0], vc_ref[0]], axis=0)
    first = jnp.where(pl.program_id(1) == 0, 1, 0)
    for bi in range(blocks):
        q_rows = slice(bi * BLOCK, (bi + 1) * BLOCK)
        kb = k_all[bi * BLOCK:(bi + 2) * BLOCK]
        vb = v_all[bi * BLOCK:(bi + 2) * BLOCK]
        which = first if bi == 0 else 0
        for c in range(GROUP_WIDTH // LANES):
            rows = slice(2 * c * BLOCK, 2 * (c + 1) * BLOCK)
            qs = _split_pair(q_ref[0, q_rows, c * LANES:(c + 1) * LANES], lo) * QK_SCALE
            s = _dot_nt(qs, kb) + bias_ref[which, rows, :]
            sink = sink_ref[rows, :]
            m = jnp.maximum(jnp.max(s, axis=1, keepdims=True), sink)
            p = jnp.exp2(s - jnp.concatenate([m, m], axis=1))
            den = jnp.sum(p, axis=1, keepdims=True) + jnp.exp2(sink - m)
            o = _dot(p.astype(BF16), vb) / den
            o_ref[0, q_rows, c * LANES:(c + 1) * LANES] = jnp.where(lo, o[:BLOCK], o[BLOCK:]).astype(o_ref.dtype)


def swa_attention(h3, bias_rows, sink_rows, *, batch, seq, blocks):
    span = blocks * BLOCK
    kb, vb = COL_KA // LANES, COL_VA // LANES
    prev = lambda col: pl.BlockSpec((1, BLOCK, LANES), lambda b, n: (b, jnp.maximum(n * blocks - 1, 0), col))
    cur = lambda col: pl.BlockSpec((1, span, LANES), lambda b, n: (b, n, col))
    return pl.pallas_call(
        functools.partial(_swa_kernel, blocks=blocks),
        grid=(batch, seq // span),
        in_specs=[pl.BlockSpec((1, span, GROUP_WIDTH), lambda b, n: (b, n, COL_QA // GROUP_WIDTH)),
                  prev(kb), cur(kb), prev(vb), cur(vb),
                  pl.BlockSpec((2, N_HEADS * BLOCK, 2 * BLOCK), lambda b, n: (0, 0, 0)),
                  pl.BlockSpec((N_HEADS * BLOCK, LANES), lambda b, n: (0, 0))],
        out_specs=pl.BlockSpec((1, span, GROUP_WIDTH), lambda b, n: (b, n, 0)),
        out_shape=jax.ShapeDtypeStruct((batch, seq, GROUP_WIDTH), BF16),
        compiler_params=_params("arbitrary", "arbitrary"),
        name="swa_attn",
    )(h3, h3, h3, h3, h3, bias_rows, sink_rows)


def _dilated_kernel(q_ref, k_ref, v_ref, b0_ref, b1_ref, b2_ref, o_ref, osc, msc, dsc):
    seq = q_ref.shape[1]
    lo = _lo_lanes(BLOCK)
    for p,((_, rate), b_ref) in enumerate(zip(D_PATTERNS, (b0_ref, b1_ref, b2_ref))):
        nb = seq // (rate * BLOCK)

        def unit(u, carry, p=p, rate=rate, b_ref=b_ref, nb=nb):
            n = u // rate
            res = u - n * rate
            rows = pl.ds(n * (BLOCK * rate) + res, BLOCK, stride=rate)
            qs = _split_pair((q_ref[0, rows, :] * QK_SCALE).astype(BF16), lo)
            kc = k_ref[0, rows, :].astype(BF16)
            vc = v_ref[0, rows, :].astype(BF16)
            if nb > 1:
                prow = pl.ds(jnp.maximum(n - 1, 0) * (BLOCK * rate) + res, BLOCK, stride=rate)
                kb = jnp.concatenate([k_ref[0, prow, :].astype(BF16), kc], axis=0)
                vb = jnp.concatenate([v_ref[0, prow, :].astype(BF16), vc], axis=0)
                s = _dot_nt(qs, kb) + b_ref[jnp.where(n == 0, 1, 0), 0]
            else:
                vb = vc
                s = _dot_nt(qs, kc) + b_ref[0, 0, :, BLOCK:2 * BLOCK]
            m = jnp.max(s, axis=1, keepdims=True)
            e = jnp.exp2(s - m)
            den = jnp.broadcast_to(jnp.sum(e, axis=1, keepdims=True), (2 * BLOCK, LANES))
            mx = jnp.broadcast_to(m, (2 * BLOCK, LANES))
            o = _dot(e.astype(BF16), vb)
            osc[p, rows, :] = jnp.where(lo, o[:BLOCK], o[BLOCK:])
            msc[p, rows, :] = jnp.where(lo, mx[:BLOCK], mx[BLOCK:])
            dsc[p, rows, :] = jnp.where(lo, den[:BLOCK], den[BLOCK:])
            return carry

        lax.fori_loop(0, nb * rate, unit, 0, unroll=16)

    chunk = 2 * BLOCK

    def merge(t, carry):
        r = pl.ds(pl.multiple_of(t * chunk, chunk), chunk)
        m0, m1, m2 = msc[0, r, :], msc[1, r, :], msc[2, r, :]
        m = jnp.maximum(jnp.maximum(m0, m1), m2)
        e0, e1, e2 = jnp.exp2(m0 - m), jnp.exp2(m1 - m), jnp.exp2(m2 - m)
        o = ((e0 * osc[0, r, :] + e1 * osc[1, r, :] + e2 * osc[2, r, :])
             / (e0 * dsc[0, r, :] + e1 * dsc[1, r, :] + e2 * dsc[2, r, :]))
        o_ref[0, r, :] = o.astype(o_ref.dtype)
        return carry

    lax.fori_loop(0, seq // chunk, merge, 0)


def dilated_attention(hd3, biases, *, batch, seq):
    pairs = N_HEADS // 2
    bspec = pl.BlockSpec((2, 1, 2 * BLOCK, 2 * BLOCK), lambda b, p: (0, p, 0, 0))
    return pl.pallas_call(
        _dilated_kernel,
        grid=(batch, pairs),
        in_specs=[pl.BlockSpec((1, seq, LANES), lambda b, p: (b, 0, p)),
                  pl.BlockSpec((1, seq, LANES), lambda b, p: (b, 0, pairs + p)),
                  pl.BlockSpec((1, seq, LANES), lambda b, p: (b, 0, 2 * pairs + p)),
                  bspec, bspec, bspec],
        out_specs=pl.BlockSpec((1, seq, LANES), lambda b, p: (b, 0, p)),
        out_shape=jax.ShapeDtypeStruct((batch, seq, GROUP_WIDTH), BF16),
        scratch_shapes=[pltpu.VMEM((len(D_PATTERNS), seq, LANES), F32)] * 3,
        compiler_params=_params("arbitrary", "arbitrary"),
        name="dilated_attn",
    )(hd3, hd3, hd3, *biases)


def _mla_up_kernel(cq_ref, ckv_ref, kr_ref, qn_ref, kvn_ref, wq_ref, wqs_ref, wk_ref, wv_ref, psw_ref,
                   cos_ref, sin_ref, q_ref, k_ref, v_ref, *, scale):
    def rms(x_ref, g_ref):
        x = x_ref[...].astype(F32)
        return (x * lax.rsqrt(jnp.mean(x * x, axis=-1, keepdims=True) + RMS_EPS) * g_ref[...]).astype(BF16)

    xq = rms(cq_ref, qn_ref)
    xkv = rms(ckv_ref, kvn_ref)
    cos = cos_ref[...]
    sin = sin_ref[...]
    t = _dot(xq, wq_ref[...])
    ts = _dot(xq, wqs_ref[...])
    kn = _dot(xkv, wk_ref[...])
    kr = kr_ref[...]
    rk = kr.astype(F32) * cos + _dot(kr, psw_ref[...]) * sin
    for h in range(N_HEADS):
        sl = slice(h * LANES, (h + 1) * LANES)
        q_ref[:, sl] = ((t[:, sl] * cos + ts[:, sl] * sin) * scale).astype(BF16)
        k_ref[:, sl] = (kn[:, sl] + rk).astype(BF16)
    v_ref[...] = _dot(xkv, wv_ref[...]).astype(BF16)


def mla_up(h, q_norm, kv_norm, wq, wqs, wk, wv, psw, cos_t, sin_t, *, seq, tm):
    n = h.shape[0]
    w = N_HEADS * LANES
    const = lambda shape: pl.BlockSpec(shape, lambda i: (0, 0))
    spt = seq // tm
    return pl.pallas_call(
        functools.partial(_mla_up_kernel, scale=(B_NOPE_DIM + B_ROPE_DIM) ** -0.5 * math.log2(math.e)),
        grid=(n // tm,),
        in_specs=[pl.BlockSpec((tm, B_Q_LORA), lambda i: (i, COL_CQ // B_Q_LORA)),
                  pl.BlockSpec((tm, B_KV_LORA), lambda i: (i, COL_CKV // B_KV_LORA)),
                  pl.BlockSpec((tm, LANES), lambda i: (i, COL_KR // LANES)),
                  const((1, B_Q_LORA)), const((1, B_KV_LORA)),
                  const((B_Q_LORA, w)), const((B_Q_LORA, w)), const((B_KV_LORA, w)),
                  const((B_KV_LORA, GROUP_WIDTH)), const((LANES, LANES)),
                  pl.BlockSpec((tm, LANES), lambda i: (i % spt, 0)),
                  pl.BlockSpec((tm, LANES), lambda i: (i % spt, 0))],
        out_specs=[pl.BlockSpec((tm, w), lambda i: (i, 0)),
                   pl.BlockSpec((tm, w), lambda i: (i, 0)),
                   pl.BlockSpec((tm, GROUP_WIDTH), lambda i: (i, 0))],
        out_shape=[jax.ShapeDtypeStruct((n, w), BF16), jax.ShapeDtypeStruct((n, w), BF16),
                   jax.ShapeDtypeStruct((n, GROUP_WIDTH), BF16)],
        compiler_params=_params("arbitrary"),
        name="mla_up",
    )(h, h, h, q_norm, kv_norm, wq, wqs, wk, wv, psw, cos_t, sin_t)


def _mla_attn_kernel(q_ref, k_ref, v_ref, o_ref, *, tq, heads):
    i = pl.program_id(2)
    lo = _lo_lanes(tq)
    row = lax.broadcasted_iota(jnp.int32, (tq, tq), 0)
    col = lax.broadcasted_iota(jnp.int32, (tq, tq), 1)
    causal = col <= row

    def logits(j):
        start = pl.multiple_of(j * tq, tq)
        return tuple(_dot_nt(q_ref[0, :, hh * LANES:(hh + 1) * LANES],
                             k_ref[0, pl.ds(start, tq), hh * LANES:(hh + 1) * LANES]) for hh in range(heads))

    def update(j, stats, s_all, masked):
        start = pl.multiple_of(j * tq, tq)
        out = []
        for hh in range(heads):
            m, l, acc = stats[3 * hh:3 * hh + 3]
            s = s_all[hh]
            if masked:
                s = jnp.where(causal, s, NEG_INF)
            m_new = jnp.maximum(m, jnp.max(s, axis=1, keepdims=True))
            p = jnp.exp2(s - m_new)
            a = jnp.exp2(m - m_new)
            pv = _dot(p.astype(BF16), v_ref[0, pl.ds(start, tq), (hh // 2) * LANES:(hh // 2 + 1) * LANES])
            part = functools.reduce(jnp.add, [p[:, c * LANES:(c + 1) * LANES] for c in range(tq // LANES)])
            out += [m_new, a * l + part, a * acc + pv]
        return tuple(out)

    init = (jnp.full((tq, 1), NEG_INF, F32), jnp.zeros((tq, LANES), F32), jnp.zeros((tq, LANES), F32)) * heads
    stats = lax.fori_loop(0, i, lambda j, st: update(j, st, logits(j), False), init)
    stats = update(i, stats, logits(i), True)
    for pp in range(heads // 2):
        (_, l0, a0), (_, l1, a1) = stats[6 * pp:6 * pp + 3], stats[6 * pp + 3:6 * pp + 6]
        o0 = a0 / jnp.sum(l0, axis=1, keepdims=True)
        o1 = a1 / jnp.sum(l1, axis=1, keepdims=True)
        o_ref[0, :, pp * LANES:(pp + 1) * LANES] = jnp.where(lo, o0, o1).astype(o_ref.dtype)


def mla_attention(q, k, v, *, batch, seq, tq, heads):
    pairs = heads // 2
    return pl.pallas_call(
        functools.partial(_mla_attn_kernel, tq=tq, heads=heads),
        grid=(batch, N_HEADS // heads, seq // tq),
        in_specs=[pl.BlockSpec((1, tq, heads * LANES), lambda b, g, i: (b, i, g)),
                  pl.BlockSpec((1, seq, heads * LANES), lambda b, g, i: (b, 0, g)),
                  pl.BlockSpec((1, seq, pairs * LANES), lambda b, g, i: (b, 0, g))],
        out_specs=pl.BlockSpec((1, tq, pairs * LANES), lambda b, g, i: (b, i, g)),
        out_shape=jax.ShapeDtypeStruct((batch, seq, GROUP_WIDTH), BF16),
        compiler_params=_params("arbitrary", "arbitrary", "arbitrary"),
        name="mla_attn",
    )(q, k, v)


def _stick_kernel(q_ref, k_ref, v_ref, tri_ref, o_ref, *, tq, pairs):
    i = pl.program_id(2)
    lo = _lo_lanes(tq)
    row = lax.broadcasted_iota(jnp.int32, (2 * tq, tq), 0)
    col = lax.broadcasted_iota(jnp.int32, (2 * tq, tq), 1)
    strict = col < jnp.where(row >= tq, row - tq, row)
    tri = tri_ref[...]
    qs = [_split_pair(q_ref[0, :, pp * LANES:(pp + 1) * LANES], lo) * QK_SCALE for pp in range(pairs)]

    def chunk(pp, j, run, masked):
        start = pl.multiple_of(j * tq, tq)
        z = _dot_nt(qs[pp], k_ref[0, pl.ds(start, tq), pp * LANES:(pp + 1) * LANES])
        soft = jnp.log2(1.0 + jnp.exp2(-jnp.abs(z)))
        keep = -(jnp.maximum(z, 0.0) + soft)
        hit = jnp.minimum(z, 0.0) - soft
        if masked:
            keep = jnp.where(strict, keep, 0.0)
        hi = keep.astype(BF16)
        rest = (keep - hi.astype(F32)).astype(BF16)
        after = _dot(hi, tri) + _dot(rest, tri)
        a = jnp.exp2(hit + after)
        if masked:
            a = jnp.where(strict, a, 0.0)
        contrib = jnp.exp2(run) * _dot(a.astype(BF16), v_ref[0, pl.ds(start, tq), pp * LANES:(pp + 1) * LANES])
        return contrib, run + jnp.sum(keep, axis=1, keepdims=True)

    state = []
    has_prev = i > 0
    for pp in range(pairs):
        acc, run = chunk(pp, i, jnp.zeros((2 * tq, 1), F32), True)
        contrib, run_prev = chunk(pp, jnp.maximum(i - 1, 0), run, False)
        state += [jnp.where(has_prev, run_prev, run), acc + jnp.where(has_prev, contrib, 0.0)]

    def cond(carry):
        t = carry[0]
        live = carry[1]
        for pp in range(1, pairs):
            live = jnp.maximum(live, carry[1 + 2 * pp])
        return jnp.logical_and(t < i, jnp.max(live) > -EXP2_UNDERFLOW)

    def body(carry):
        t = carry[0]
        out = [t + 1]
        for pp in range(pairs):
            contrib, run = chunk(pp, i - 1 - t, carry[1 + 2 * pp], False)
            out += [run, carry[2 + 2 * pp] + contrib]
        return tuple(out)

    final = lax.while_loop(cond, body, (jnp.int32(1), *state))
    for pp in range(pairs):
        acc = final[2 + 2 * pp]
        o_ref[0, :, pp * LANES:(pp + 1) * LANES] = jnp.where(lo, acc[:tq], acc[tq:]).astype(o_ref.dtype)


def stick_breaking_attention(h3, tri, *, batch, seq, tq, pairs):
    w = pairs * LANES
    return pl.pallas_call(
        functools.partial(_stick_kernel, tq=tq, pairs=pairs),
        grid=(batch, GROUP_WIDTH // w, seq // tq),
        in_specs=[pl.BlockSpec((1, tq, w), lambda b, g, i: (b, i, COL_QC // w + g)),
                  pl.BlockSpec((1, seq, w), lambda b, g, i: (b, 0, COL_KC // w + g)),
                  pl.BlockSpec((1, seq, w), lambda b, g, i: (b, 0, COL_VC // w + g)),
                  pl.BlockSpec((tq, tq), lambda b, g, i: (0, 0))],
        out_specs=pl.BlockSpec((1, tq, w), lambda b, g, i: (b, i, g)),
        out_shape=jax.ShapeDtypeStruct((batch, seq, GROUP_WIDTH), BF16),
        compiler_params=_params("arbitrary", "arbitrary", "arbitrary"),
        name="stick_breaking",
    )(h3, h3, h3, tri)


def _layer_norm(r, g, b):
    mu = jnp.mean(r, axis=-1, keepdims=True)
    d = r - mu
    var = jnp.mean(d * d, axis=-1, keepdims=True)
    return d * lax.rsqrt(var + LN_EPS) * g + b


def _pack_bf16_pairs(xb):
    half = xb.shape[1] // 2
    lo = lax.bitcast_convert_type(xb[:, :half].astype(F32), jnp.uint32) >> 16
    hi = lax.bitcast_convert_type(xb[:, half:].astype(F32), jnp.uint32) & jnp.uint32(0xFFFF0000)
    return lax.bitcast_convert_type(hi | lo, F32)


def _unpack_bf16_pairs(xp):
    u = lax.bitcast_convert_type(xp, jnp.uint32)
    lo = lax.bitcast_convert_type(u << 16, F32).astype(BF16)
    hi = lax.bitcast_convert_type(u & jnp.uint32(0xFFFF0000), F32).astype(BF16)
    return jnp.concatenate([lo, hi], axis=1)


def _out_proj_kernel(ga_ref, gb_ref, gc_ref, gd_ref, x_ref, mixg_ref, wo_ref, g_ref, b_ref, rt_ref,
                     xo_ref, xb_ref, xp_ref, lg_ref):
    mixed = None
    for gi, grp in enumerate((ga_ref, gb_ref, gc_ref, gd_ref)):
        xg = grp[...].astype(F32)
        y = xg * lax.rsqrt(jnp.mean(xg * xg, axis=-1, keepdims=True) + RMS_EPS) * mixg_ref[gi:gi + 1, :]
        part = _dot(y.astype(BF16), wo_ref[0, gi * GROUP_WIDTH:(gi + 1) * GROUP_WIDTH, :])
        mixed = part if mixed is None else mixed + part
    x1 = _layer_norm(ALPHA * x_ref[...].astype(F32) + mixed, g_ref[...], b_ref[...])
    xo_ref[...] = x1
    xb = x1.astype(BF16)
    xb_ref[...] = xb
    xp_ref[...] = _pack_bf16_pairs(xb)
    lg_ref[...] = _dot(xb, rt_ref[...])


def out_proj_ln(ga, gb, gc, gd, x, mix_g, wo, layer, ln_g, ln_b, router, *, tm):
    n = x.shape[0]
    row = lambda w: pl.BlockSpec((tm, w), lambda i: (i, 0))
    const = lambda shape: pl.BlockSpec(shape, lambda i: (0, 0))
    return pl.pallas_call(
        _out_proj_kernel,
        grid=(n // tm,),
        in_specs=[row(GROUP_WIDTH)] * 4 + [row(D_MODEL), const((4, GROUP_WIDTH)),
                                          pl.BlockSpec((1, D_MODEL, D_MODEL), lambda i: (layer, 0, 0)),
                                          const((1, D_MODEL)), const((1, D_MODEL)), const((D_MODEL, LANES))],
        out_specs=[row(D_MODEL), row(D_MODEL), row(D_MODEL // 2), row(LANES)],
        out_shape=[jax.ShapeDtypeStruct((n, D_MODEL), F32), jax.ShapeDtypeStruct((n, D_MODEL), BF16),
                   jax.ShapeDtypeStruct((n, D_MODEL // 2), F32), jax.ShapeDtypeStruct((n, LANES), F32)],
        compiler_params=_params("arbitrary"),
        name="out_proj_ln",
    )(ga, gb, gc, gd, x, mix_g, wo, ln_g, ln_b, router)


def _silu_mul(g, u):
    return g * (1.0 / (1.0 + jnp.exp(-g))) * u


def _gate_up_kernel(x_ref, wg_ref, wu_ref, o_ref, wgb_ref, wub_ref):
    @pl.when(pl.program_id(1) == 0)
    def _():
        wgb_ref[...] = wg_ref[0].astype(BF16)
        wub_ref[...] = wu_ref[0].astype(BF16)

    x = x_ref[...]
    o_ref[...] = _silu_mul(_dot(x, wgb_ref[...]), _dot(x, wub_ref[...])).astype(o_ref.dtype)


def ffn_gate_up(xb, wg, wu, li, *, tm, tf):
    n = xb.shape[0]
    f = wg.shape[2]
    return pl.pallas_call(
        _gate_up_kernel,
        grid=(f // tf, n // tm),
        in_specs=[pl.BlockSpec((tm, D_MODEL), lambda j, i: (i, 0)),
                  pl.BlockSpec((1, D_MODEL, tf), lambda j, i: (li, 0, j)),
                  pl.BlockSpec((1, D_MODEL, tf), lambda j, i: (li, 0, j))],
        out_specs=pl.BlockSpec((tm, tf), lambda j, i: (i, j)),
        out_shape=jax.ShapeDtypeStruct((n, f), BF16),
        scratch_shapes=[pltpu.VMEM((D_MODEL, tf), BF16), pltpu.VMEM((D_MODEL, tf), BF16)],
        compiler_params=_params("arbitrary", "arbitrary"),
        name="ffn_gate_up",
    )(xb, wg, wu)


def _down_ln_kernel(h_ref, wd_ref, x_ref, g_ref, b_ref, xo_ref, xb_ref):
    kk = pl.program_id(1)
    part = _dot(h_ref[...], wd_ref[0])

    @pl.when(kk == 0)
    def _():
        xo_ref[...] = part

    @pl.when(kk > 0)
    def _():
        xo_ref[...] += part

    @pl.when(kk == pl.num_programs(1) - 1)
    def _():
        x2 = _layer_norm(ALPHA * x_ref[...] + xo_ref[...], g_ref[...], b_ref[...])
        xo_ref[...] = x2
        xb_ref[...] = x2.astype(BF16)


def ffn_down_ln(hmid, wd, li, x, ln_g, ln_b, *, tm, tk):
    n, f = hmid.shape
    return pl.pallas_call(
        _down_ln_kernel,
        grid=(n // tm, f // tk),
        in_specs=[pl.BlockSpec((tm, tk), lambda i, k: (i, k)),
                  pl.BlockSpec((1, tk, D_MODEL), lambda i, k: (li, k, 0)),
                  pl.BlockSpec((tm, D_MODEL), lambda i, k: (i, 0)),
                  pl.BlockSpec((1, D_MODEL), lambda i, k: (0, 0)),
                  pl.BlockSpec((1, D_MODEL), lambda i, k: (0, 0))],
        out_specs=[pl.BlockSpec((tm, D_MODEL), lambda i, k: (i, 0)),
                   pl.BlockSpec((tm, D_MODEL), lambda i, k: (i, 0))],
        out_shape=[jax.ShapeDtypeStruct((n, D_MODEL), F32), jax.ShapeDtypeStruct((n, D_MODEL), BF16)],
        compiler_params=_params("arbitrary", "arbitrary"),
        name="ffn_down_ln",
    )(hmid, wd, x, ln_g, ln_b)


def _new_expert(te_ref, t):
    return jnp.logical_or(t == 0, te_ref[t] != te_ref[jnp.maximum(t - 1, 0)])


MOE_ROW_STEP = 256


def _by_valid_rows(valid, tm, rows):
    for n in range(0, tm + 1, MOE_ROW_STEP):
        @pl.when(jnp.logical_and(valid > n - MOE_ROW_STEP, valid <= n))
        def _(n=n):
            rows(n)


def _moe_gate_up_kernel(te_ref, tv_ref, x_ref, wg_ref, wu_ref, o_ref, wgb_ref, wub_ref):
    t = pl.program_id(1)

    @pl.when(_new_expert(te_ref, t))
    def _():
        wgb_ref[...] = wg_ref[0, 0].astype(BF16)
        wub_ref[...] = wu_ref[0, 0].astype(BF16)

    def rows(n):
        pad = o_ref.shape[0] - n
        if n:
            x = _unpack_bf16_pairs(x_ref[pad:, :])
            o_ref[pad:, :] = _silu_mul(_dot(x, wgb_ref[...]), _dot(x, wub_ref[...])).astype(o_ref.dtype)
        if pad:
            o_ref[0:pad, :] = jnp.zeros((pad, o_ref.shape[1]), o_ref.dtype)

    _by_valid_rows(tv_ref[t], x_ref.shape[0], rows)


def moe_gate_up(tile_expert, tile_rows, xs, wg, wu, li, *, tm, tf):
    p = xs.shape[0]
    f = wg.shape[3]
    grid_spec = pltpu.PrefetchScalarGridSpec(
        num_scalar_prefetch=2,
        grid=(f // tf, p // tm),
        in_specs=[pl.BlockSpec((tm, D_MODEL // 2), lambda j, t, te, tv: (jnp.where(tv[t] > 0, t, 0), 0)),
                  pl.BlockSpec((1, 1, D_MODEL, tf), lambda j, t, te, tv: (li, te[t], 0, j)),
                  pl.BlockSpec((1, 1, D_MODEL, tf), lambda j, t, te, tv: (li, te[t], 0, j))],
        out_specs=pl.BlockSpec((tm, tf), lambda j, t, te, tv: (t, j)),
        scratch_shapes=[pltpu.VMEM((D_MODEL, tf), BF16), pltpu.VMEM((D_MODEL, tf), BF16)],
    )
    return pl.pallas_call(
        _moe_gate_up_kernel,
        grid_spec=grid_spec,
        out_shape=jax.ShapeDtypeStruct((p, f), BF16),
        compiler_params=_params("arbitrary", "arbitrary"),
        name="moe_gate_up",
    )(tile_expert, tile_rows, xs, wg, wu)


def _moe_down_kernel(te_ref, tv_ref, h_ref, wd_ref, o_ref, wdb_ref):
    t = pl.program_id(1)

    @pl.when(_new_expert(te_ref, t))
    def _():
        wdb_ref[...] = wd_ref[0, 0].astype(BF16)

    def rows(n):
        pad = o_ref.shape[0] - n
        if n:
            o_ref[pad:, :] = _pack_bf16_pairs(_dot(h_ref[pad:, :], wdb_ref[...]).astype(BF16))
        if pad:
            o_ref[0:pad, :] = jnp.zeros((pad, o_ref.shape[1]), o_ref.dtype)

    _by_valid_rows(tv_ref[t], h_ref.shape[0], rows)


def moe_down(tile_expert, tile_rows, hs, wd, li, *, tm, tn):
    p, f = hs.shape
    grid_spec = pltpu.PrefetchScalarGridSpec(
        num_scalar_prefetch=2,
        grid=(D_MODEL // tn, p // tm),
        in_specs=[pl.BlockSpec((tm, f), lambda j, t, te, tv: (jnp.where(tv[t] > 0, t, 0), 0)),
                  pl.BlockSpec((1, 1, f, tn), lambda j, t, te, tv: (li, te[t], 0, j))],
        out_specs=pl.BlockSpec((tm, tn // 2), lambda j, t, te, tv: (t, j)),
        scratch_shapes=[pltpu.VMEM((f, tn), BF16)],
    )
    return pl.pallas_call(
        _moe_down_kernel,
        grid_spec=grid_spec,
        out_shape=jax.ShapeDtypeStruct((p, D_MODEL // 2), F32),
        compiler_params=_params("arbitrary", "arbitrary"),
        name="moe_down",
    )(tile_expert, tile_rows, hs, wd)


def _combine_ln_kernel(x_ref, y0_ref, y1_ref, gate_ref, g_ref, b_ref, xo_ref, xb_ref, *, group):
    def unpack(y_ref):
        parts = [_unpack_bf16_pairs(y_ref[:, c:c + group]) for c in range(0, y_ref.shape[1], group)]
        return jnp.concatenate(parts, axis=1).astype(F32)

    f = gate_ref[:, 0:1] * unpack(y0_ref) + gate_ref[:, 1:2] * unpack(y1_ref)
    x2 = _layer_norm(ALPHA * x_ref[...] + f, g_ref[...], b_ref[...])
    xo_ref[...] = x2
    xb_ref[...] = x2.astype(BF16)


def combine_ln(x, y0, y1, gates, ln_g, ln_b, *, tm, group):
    n = x.shape[0]
    row = pl.BlockSpec((tm, D_MODEL), lambda i: (i, 0))
    packed = pl.BlockSpec((tm, D_MODEL // 2), lambda i: (i, 0))
    const = pl.BlockSpec((1, D_MODEL), lambda i: (0, 0))
    return pl.pallas_call(
        functools.partial(_combine_ln_kernel, group=group),
        grid=(n // tm,),
        in_specs=[row, packed, packed, pl.BlockSpec((tm, TOP_K), lambda i: (i, 0)), const, const],
        out_specs=[row, row],
        out_shape=[jax.ShapeDtypeStruct((n, D_MODEL), F32), jax.ShapeDtypeStruct((n, D_MODEL), BF16)],
        compiler_params=_params("arbitrary"),
        name="combine_ln",
    )(x, y0, y1, gates, ln_g, ln_b)


def _t5_bucket(dist):
    max_exact = NUM_BUCKETS // 2
    d = jnp.maximum(dist, 1).astype(F32)
    large = max_exact + (jnp.log(d / max_exact) / math.log(T5_MAX_DISTANCE / max_exact)
                         * (NUM_BUCKETS - max_exact)).astype(jnp.int32)
    large = jnp.minimum(large, NUM_BUCKETS - 1)
    return jnp.where(dist < max_exact, dist, large)


def _band_bias_masked(table, stride, max_dist):
    dist = jnp.arange(BLOCK)[:, None] + BLOCK - jnp.arange(2 * BLOCK)[None, :]
    onehot = jax.nn.one_hot(_t5_bucket(jnp.maximum(dist, 0) * stride), NUM_BUCKETS, dtype=F32)
    bias = jnp.einsum("qkb,bh->hqk", onehot, table.astype(F32), precision=lax.Precision.HIGHEST)
    valid = (dist >= 0) & (dist <= max_dist)
    return jnp.where(valid[None], bias, NEG_INF)


def _rope_lane_tables(seq):
    half = B_ROPE_DIM // 2
    inv = ROPE_THETA ** (-jnp.arange(0, B_ROPE_DIM, 2, dtype=F32) / B_ROPE_DIM)
    ang = jnp.arange(seq, dtype=F32)[:, None] * inv[None, :]
    cos, sin = jnp.cos(ang), jnp.sin(ang)
    ones = jnp.ones((seq, ROPE_LANE0), F32)
    zeros = jnp.zeros((seq, ROPE_LANE0), F32)
    pad = jnp.zeros((seq, LANES - ROPE_LANE0 - 2 * half), F32)
    cos_t = jnp.concatenate([ones, cos, cos, pad], axis=1)
    sin_t = jnp.concatenate([zeros, -sin, sin, pad], axis=1)
    return cos_t, sin_t


def _permute_heads(t, axis, order):
    parts = [lax.slice_in_dim(t, hh * HEAD_DIM, (hh + 1) * HEAD_DIM, axis=axis) for hh in order]
    return jnp.concatenate(parts, axis=axis)


IN_SEGMENTS = (("qa", 512), ("ka", 128), ("va", 128), ("cq", 384), ("ckv", 256), ("kr", 32),
               ("qc", 512), ("kc", 512), ("vc", 512), ("qd", 512), ("kd", 512), ("vd", 512))
IN_WIDTH = sum(size for _, size in IN_SEGMENTS)


def _relayout_w_in_kernel(w_ref, main_ref, d_ref):
    w = w_ref[0]
    cols = w.shape[1]
    seg = {}
    start = 0
    for name, size in IN_SEGMENTS:
        seg[name] = w[start:start + size, :]
        start += size
    qa = jnp.concatenate([seg["qa"][hh * HEAD_DIM:(hh + 1) * HEAD_DIM, :] for hh in A_HEAD_ORDER], axis=0)
    kr = jnp.concatenate([jnp.zeros((ROPE_LANE0, cols), F32), seg["kr"],
                          jnp.zeros((LANES - ROPE_LANE0 - B_ROPE_DIM, cols), F32)], axis=0)
    main = jnp.concatenate([qa, seg["qc"], seg["kc"], seg["cq"], seg["ka"], seg["vc"], seg["ckv"], seg["va"], kr],
                           axis=0)
    main_ref[0] = main.astype(BF16)
    d_ref[0] = jnp.concatenate([seg["qd"], seg["kd"], seg["vd"]], axis=0).astype(BF16)


def _relayout_w_in(w_in, *, tc):
    d = w_in.shape[0]
    return pl.pallas_call(
        _relayout_w_in_kernel,
        grid=(d, D_MODEL // tc),
        in_specs=[pl.BlockSpec((1, IN_WIDTH, tc), lambda l, i: (l, 0, i))],
        out_specs=[pl.BlockSpec((1, IN_WIDTH_P, tc), lambda l, i: (l, 0, i)),
                   pl.BlockSpec((1, IN_WIDTH_D, tc), lambda l, i: (l, 0, i))],
        out_shape=[jax.ShapeDtypeStruct((d, IN_WIDTH_P, D_MODEL), BF16),
                   jax.ShapeDtypeStruct((d, IN_WIDTH_D, D_MODEL), BF16)],
        compiler_params=_params("arbitrary", "arbitrary"),
        name="relayout_w_in",
    )(jnp.swapaxes(w_in, 1, 2))


def _relayout_mla(w_uq, w_ukv):
    d = w_uq.shape[0]
    half = B_ROPE_DIM // 2
    wq = w_uq.astype(BF16).reshape(d, B_Q_LORA, N_HEADS, B_NOPE_DIM + B_ROPE_DIM)
    nope, r1, r2 = wq[..., :B_NOPE_DIM], wq[..., B_NOPE_DIM:B_NOPE_DIM + half], wq[..., B_NOPE_DIM + half:]
    z32 = jnp.zeros(wq.shape[:3] + (LANES - B_NOPE_DIM - B_ROPE_DIM,), BF16)
    z64 = jnp.zeros(wq.shape[:3] + (B_NOPE_DIM,), BF16)
    wq_t = jnp.concatenate([nope, r1, r2, z32], axis=-1).reshape(d, B_Q_LORA, N_HEADS * LANES)
    wq_s = jnp.concatenate([z64, r2, r1, z32], axis=-1).reshape(d, B_Q_LORA, N_HEADS * LANES)
    wkv = w_ukv.astype(BF16).reshape(d, B_KV_LORA, N_HEADS, 2 * HEAD_DIM)
    zk = jnp.zeros(wkv.shape[:3] + (LANES - B_NOPE_DIM,), BF16)
    wk_t = jnp.concatenate([wkv[..., :B_NOPE_DIM], zk], axis=-1).reshape(d, B_KV_LORA, N_HEADS * LANES)
    wv = wkv[..., B_NOPE_DIM:].reshape(d, B_KV_LORA, GROUP_WIDTH)
    return wq_t, wq_s, wk_t, wv


def _rope_swap_matrix():
    half = B_ROPE_DIM // 2
    src = jnp.arange(LANES)[:, None]
    dst = jnp.arange(LANES)[None, :]
    first = (dst >= ROPE_LANE0) & (dst < ROPE_LANE0 + half) & (src == dst + half)
    second = (dst >= ROPE_LANE0 + half) & (dst < ROPE_LANE0 + 2 * half) & (src == dst - half)
    return (first | second).astype(BF16)


def _route(logits, tm):
    n = logits.shape[0]
    top_logits, top_idx = lax.top_k(logits, TOP_K)
    gates = jax.nn.softmax(top_logits, axis=-1)
    onehot = jax.nn.one_hot(top_idx, N_EXPERTS, dtype=jnp.int32)
    member = jnp.sum(onehot, axis=1)
    rank = jnp.cumsum(member, axis=0) - member
    counts = jnp.sum(member, axis=0)
    padded = ((counts + tm - 1) // tm) * tm
    ends = jnp.cumsum(padded)
    first = ends - counts
    pos = jnp.sum(onehot * (first[None, None, :] + rank[:, None, :]), axis=-1)
    n_rows = TOP_K * n + N_EXPERTS * tm
    src = (jnp.arange(n_rows, dtype=jnp.int32) % n).at[pos.reshape(-1)].set(
        jnp.repeat(jnp.arange(n, dtype=jnp.int32), TOP_K))

    def tile_tables(tile):
        tile_start = jnp.arange(n_rows // tile, dtype=jnp.int32) * tile
        tile_expert = jnp.minimum(jnp.sum((tile_start[:, None] >= ends[None, :]).astype(jnp.int32), axis=1),
                                  N_EXPERTS - 1)
        onehot_te = jax.nn.one_hot(tile_expert, N_EXPERTS, dtype=jnp.int32)
        group_first = jnp.sum(onehot_te * first[None, :], axis=1)
        rows = jnp.clip(tile_start + tile - group_first, 0, tile)
        rows = jnp.where(tile_start < ends[-1], rows, 0)
        return tile_expert.astype(jnp.int32), rows.astype(jnp.int32)

    return pos, gates, src, tile_tables


def kernel(x, w_in, w_o, mla_q_norm, mla_kv_norm, mla_w_uq, mla_w_ukv, attn_sinks, rel_bias_table, mix_norm_g,
           ln1_g, ln1_b, ln2_g, ln2_b, ffn_w_gate, ffn_w_up, ffn_w_down, moe_router, moe_w_gate, moe_w_up,
           moe_w_down):
    batch, seq, _ = x.shape
    n = batch * seq

    w_in_p, w_in_d = _relayout_w_in(w_in, tc=256)
    w_o_p = jnp.concatenate([_permute_heads(w_o[:, :GROUP_WIDTH], 1, A_HEAD_ORDER), w_o[:, GROUP_WIDTH:]],
                            axis=1).astype(BF16)
    mix_g_p = jnp.concatenate([_permute_heads(mix_norm_g[:, :1], 2, A_HEAD_ORDER), mix_norm_g[:, 1:]], axis=1)
    wq_t, wq_s, wk_t, wv = _relayout_mla(mla_w_uq, mla_w_ukv)
    psw = _rope_swap_matrix()
    cos_t, sin_t = _rope_lane_tables(seq)
    tq = 256
    tri = (jnp.arange(tq)[:, None] > jnp.arange(tq)[None, :]).astype(BF16)
    order = list(A_HEAD_ORDER)
    def with_first_block_variant(b):
        prev = jnp.arange(2 * BLOCK) < BLOCK
        return jnp.stack([b, jnp.where(prev, NEG_INF, b)], axis=0) * LOG2E

    bias_a = _band_bias_masked(rel_bias_table[:, :N_HEADS], 1, A_WINDOW - 1)
    bias_a_rows = with_first_block_variant(jnp.concatenate([bias_a[hh] for hh in order], axis=0))
    sink_rows = jnp.concatenate([jnp.broadcast_to(attn_sinks[:, hh, None, None] * LOG2E, (DEPTH, BLOCK, LANES))
                                 for hh in order], axis=1)
    biases_d = [with_first_block_variant(_band_bias_masked(rel_bias_table[:, N_HEADS:], rate, window // rate).reshape(
        N_HEADS // 2, 2 * BLOCK, 2 * BLOCK)) for window, rate in D_PATTERNS]
    router_p = jnp.pad(moe_router, ((0, 0), (0, 0), (0, LANES - N_EXPERTS))).astype(BF16)
    wd_d = ffn_w_down.astype(BF16)
    moe_tm = 1024

    xf = x.reshape(n, D_MODEL)
    xb = xf
    for layer in range(DEPTH):
        h = matmul_ws(xb, w_in_p, layer, 1024, 1024, BF16)
        hd = matmul_ws(xb, w_in_d, layer, 1024, 512, F32)
        h3 = h.reshape(batch, seq, IN_WIDTH_P)
        ga = swa_attention(h3, bias_a_rows, sink_rows[layer], batch=batch, seq=seq, blocks=4)
        q_b, k_b, v_b = mla_up(h, mla_q_norm[layer][None], mla_kv_norm[layer][None], wq_t[layer], wq_s[layer],
                               wk_t[layer], wv[layer], psw, cos_t, sin_t, seq=seq, tm=512)
        gb = mla_attention(q_b.reshape(batch, seq, -1), k_b.reshape(batch, seq, -1),
                           v_b.reshape(batch, seq, -1), batch=batch, seq=seq, tq=tq, heads=8)
        gc = stick_breaking_attention(h3, tri, batch=batch, seq=seq, tq=tq, pairs=4)
        gd = dilated_attention(hd.reshape(batch, seq, IN_WIDTH_D), biases_d, batch=batch, seq=seq)
        i = layer // 2
        router = router_p[i] if layer % 2 == 1 else jnp.zeros((D_MODEL, LANES), BF16)
        xf, xb, xp, logits = out_proj_ln(ga.reshape(n, -1), gb.reshape(n, -1), gc.reshape(n, -1), gd.reshape(n, -1),
                                     xf, mix_g_p[layer], w_o_p, layer, ln1_g[layer][None], ln1_b[layer][None],
                                     router, tm=256)
        if layer % 2 == 0:
            hmid = ffn_gate_up(xb, ffn_w_gate, ffn_w_up, i, tm=1024, tf=512)
            xf, xb = ffn_down_ln(hmid, wd_d, i, xf, ln2_g[layer][None], ln2_b[layer][None], tm=512, tk=1408)
        else:
            pos, gates, src, tile_tables = _route(logits[:, :N_EXPERTS], moe_tm)
            te_up, tr_up = tile_tables(moe_tm)
            te_dn, tr_dn = tile_tables(moe_tm // 2)
            xs = jnp.take(xp, src, axis=0, mode="clip")
            hs = moe_gate_up(te_up, tr_up, xs, moe_w_gate, moe_w_up, i, tm=moe_tm, tf=512)
            ys = moe_down(te_dn, tr_dn, hs, moe_w_down, i, tm=moe_tm // 2, tn=512)
            y0 = jnp.take(ys, pos[:, 0], axis=0, mode="clip")
            y1 = jnp.take(ys, pos[:, 1], axis=0, mode="clip")
            xf, xb = combine_ln(xf, y0, y1, gates, ln2_g[layer][None], ln2_b[layer][None], tm=512, group=256)
    return xf.reshape(batch, seq, D_MODEL)
```

```python
import functools
import math

import jax
import jax.numpy as jnp
from jax import lax
from jax.experimental import pallas as pl
from jax.experimental.pallas import tpu as pltpu

D_MODEL = 2048
DEPTH = 4
HEAD_DIM = 64
N_HEADS = 8
GROUP_WIDTH = N_HEADS * HEAD_DIM
BLOCK = 128
A_KV_HEADS = 2
A_WINDOW = 128
B_NOPE_DIM = 64
B_ROPE_DIM = 32
B_Q_LORA = 384
B_KV_LORA = 256
ROPE_THETA = 10000.0
D_PATTERNS = ((128, 1), (512, 4), (2048, 16))
NUM_BUCKETS = 32
T5_MAX_DISTANCE = 2048
D_FF = 5632
N_EXPERTS = 8
TOP_K = 2
NEG_INF = -1e30
LN_EPS = 1e-5
RMS_EPS = 1e-6
ALPHA = (2 * DEPTH) ** 0.25

LANES = 128
VMEM_LIMIT = 56 * 1024 * 1024

COL_QA, COL_QC, COL_KC, COL_CQ, COL_KA = 0, 512, 1024, 1536, 1920
COL_VC, COL_CKV, COL_VA, COL_KR = 2048, 2560, 2816, 2944
IN_WIDTH_P = 3072
IN_WIDTH_D = 3 * GROUP_WIDTH
A_HEAD_ORDER = (0, 4, 1, 5, 2, 6, 3, 7)
ROPE_LANE0 = 64
LOG2E = math.log2(math.e)
EXP2_UNDERFLOW = 151.0
QK_SCALE = HEAD_DIM ** -0.5 * LOG2E

BF16 = jnp.bfloat16
F32 = jnp.float32


def _params(*sem):
    return pltpu.CompilerParams(dimension_semantics=sem, vmem_limit_bytes=VMEM_LIMIT)


def _dot(a, b):
    return jnp.dot(a, b, preferred_element_type=F32)


def _dot_nt(a, b):
    return lax.dot_general(a, b, (((1,), (1,)), ((), ())), preferred_element_type=F32)


def _lo_lanes(rows):
    return lax.broadcasted_iota(jnp.int32, (rows, LANES), 1) < HEAD_DIM


def _split_pair(t, lo):
    zero = jnp.zeros_like(t)
    return jnp.concatenate([jnp.where(lo, t, zero), jnp.where(lo, zero, t)], axis=0)


def _matmul_kernel(a_ref, w_ref, o_ref):
    o_ref[...] = _dot_nt(a_ref[...].astype(BF16), w_ref[0]).astype(o_ref.dtype)


def matmul_ws(a, w, layer, tm, tn, out_dtype):
    m, k = a.shape
    n = w.shape[1]
    return pl.pallas_call(
        _matmul_kernel,
        grid=(n // tn, m // tm),
        in_specs=[pl.BlockSpec((tm, k), lambda j, i: (i, 0)),
                  pl.BlockSpec((1, tn, k), lambda j, i: (layer, j, 0))],
        out_specs=pl.BlockSpec((tm, tn), lambda j, i: (i, j)),
        out_shape=jax.ShapeDtypeStruct((m, n), out_dtype),
        compiler_params=_params("arbitrary", "arbitrary"),
        name="matmul_ws",
    )(a, w)


def _swa_kernel(q_ref, kp_ref, kc_ref, vp_ref, vc_ref, bias_ref, sink_ref, o_ref, *, blocks):
    lo = _lo_lanes(BLOCK)
    k_all = jnp.concatenate([kp_ref[0], kc_ref[0]], axis=0)
    v_all = jnp.concatenate([vp_ref[0], vc_ref[0]], axis=0)
    first = jnp.where(pl.program_id(1) == 0, 1, 0)
    for bi in range(blocks):
        q_rows = slice(bi * BLOCK, (bi + 1) * BLOCK)
        kb = k_all[bi * BLOCK:(bi + 2) * BLOCK]
        vb = v_all[bi * BLOCK:(bi + 2) * BLOCK]
        which = first if bi == 0 else 0
        for c in range(GROUP_WIDTH // LANES):
            rows = slice(2 * c * BLOCK, 2 * (c + 1) * BLOCK)
            qs = _split_pair(q_ref[0, q_rows, c * LANES:(c + 1) * LANES], lo) * QK_SCALE
            s = _dot_nt(qs, kb) + bias_ref[which, rows, :]
            sink = sink_ref[rows, :]
            m = jnp.maximum(jnp.max(s, axis=1, keepdims=True), sink)
            p = jnp.exp2(s - jnp.concatenate([m, m], axis=1))
            den = jnp.sum(p, axis=1, keepdims=True) + jnp.exp2(sink - m)
            o = _dot(p.astype(BF16), vb) / den
            o_ref[0, q_rows, c * LANES:(c + 1) * LANES] = jnp.where(lo, o[:BLOCK], o[BLOCK:]).astype(o_ref.dtype)


def swa_attention(h3, bias_rows, sink_rows, *, batch, seq, blocks):
    span = blocks * BLOCK
    kb, vb = COL_KA // LANES, COL_VA // LANES
    prev = lambda col: pl.BlockSpec((1, BLOCK, LANES), lambda b, n: (b, jnp.maximum(n * blocks - 1, 0), col))
    cur = lambda col: pl.BlockSpec((1, span, LANES), lambda b, n: (b, n, col))
    return pl.pallas_call(
        functools.partial(_swa_kernel, blocks=blocks),
        grid=(batch, seq // span),
        in_specs=[pl.BlockSpec((1, span, GROUP_WIDTH), lambda b, n: (b, n, COL_QA // GROUP_WIDTH)),
                  prev(kb), cur(kb), prev(vb), cur(vb),
                  pl.BlockSpec((2, N_HEADS * BLOCK, 2 * BLOCK), lambda b, n: (0, 0, 0)),
                  pl.BlockSpec((N_HEADS * BLOCK, LANES), lambda b, n: (0, 0))],
        out_specs=pl.BlockSpec((1, span, GROUP_WIDTH), lambda b, n: (b, n, 0)),
        out_shape=jax.ShapeDtypeStruct((batch, seq, GROUP_WIDTH), BF16),
        compiler_params=_params("arbitrary", "arbitrary"),
        name="swa_attn",
    )(h3, h3, h3, h3, h3, bias_rows, sink_rows)


def _dilated_kernel(q_ref, k_ref, v_ref, b0_ref, b1_ref, b2_ref, o_ref, osc, msc, dsc):
    seq = q_ref.shape[1]
    lo = _lo_lanes(BLOCK)
    for p,((_, rate), b_ref) in enumerate(zip(D_PATTERNS, (b0_ref, b1_ref, b2_ref))):
        nb = seq // (rate * BLOCK)

        def unit(u, carry, p=p, rate=rate, b_ref=b_ref, nb=nb):
            n = u // rate
            res = u - n * rate
            rows = pl.ds(n * (BLOCK * rate) + res, BLOCK, stride=rate)
            qs = _split_pair((q_ref[0, rows, :] * QK_SCALE).astype(BF16), lo)
            kc = k_ref[0, rows, :].astype(BF16)
            vc = v_ref[0, rows, :].astype(BF16)
            if nb > 1:
                prow = pl.ds(jnp.maximum(n - 1, 0) * (BLOCK * rate) + res, BLOCK, stride=rate)
                kb = jnp.concatenate([k_ref[0, prow, :].astype(BF16), kc], axis=0)
                vb = jnp.concatenate([v_ref[0, prow, :].astype(BF16), vc], axis=0)
                s = _dot_nt(qs, kb) + b_ref[jnp.where(n == 0, 1, 0), 0]
            else:
                vb = vc
                s = _dot_nt(qs, kc) + b_ref[0, 0, :, BLOCK:2 * BLOCK]
            m = jnp.max(s, axis=1, keepdims=True)
            e = jnp.exp2(s - m)
            den = jnp.broadcast_to(jnp.sum(e, axis=1, keepdims=True), (2 * BLOCK, LANES))
            mx = jnp.broadcast_to(m, (2 * BLOCK, LANES))
            o = _dot(e.astype(BF16), vb)
            osc[p, rows, :] = jnp.where(lo, o[:BLOCK], o[BLOCK:])
            msc[p, rows, :] = jnp.where(lo, mx[:BLOCK], mx[BLOCK:])
            dsc[p, rows, :] = jnp.where(lo, den[:BLOCK], den[BLOCK:])
            return carry

        lax.fori_loop(0, nb * rate, unit, 0, unroll=16)

    chunk = 2 * BLOCK

    def merge(t, carry):
        r = pl.ds(pl.multiple_of(t * chunk, chunk), chunk)
        m0, m1, m2 = msc[0, r, :], msc[1, r, :], msc[2, r, :]
        m = jnp.maximum(jnp.maximum(m0, m1), m2)
        e0, e1, e2 = jnp.exp2(m0 - m), jnp.exp2(m1 - m), jnp.exp2(m2 - m)
        o = ((e0 * osc[0, r, :] + e1 * osc[1, r, :] + e2 * osc[2, r, :])
             / (e0 * dsc[0, r, :] + e1 * dsc[1, r, :] + e2 * dsc[2, r, :]))
        o_ref[0, r, :] = o.astype(o_ref.dtype)
        return carry

    lax.fori_loop(0, seq // chunk, merge, 0)


def dilated_attention(hd3, biases, *, batch, seq):
    pairs = N_HEADS // 2
    bspec = pl.BlockSpec((2, 1, 2 * BLOCK, 2 * BLOCK), lambda b, p: (0, p, 0, 0))
    return pl.pallas_call(
        _dilated_kernel,
        grid=(batch, pairs),
        in_specs=[pl.BlockSpec((1, seq, LANES), lambda b, p: (b, 0, p)),
                  pl.BlockSpec((1, seq, LANES), lambda b, p: (b, 0, pairs + p)),
                  pl.BlockSpec((1, seq, LANES), lambda b, p: (b, 0, 2 * pairs + p)),
                  bspec, bspec, bspec],
        out_specs=pl.BlockSpec((1, seq, LANES), lambda b, p: (b, 0, p)),
        out_shape=jax.ShapeDtypeStruct((batch, seq, GROUP_WIDTH), BF16),
        scratch_shapes=[pltpu.VMEM((len(D_PATTERNS), seq, LANES), F32)] * 3,
        compiler_params=_params("arbitrary", "arbitrary"),
        name="dilated_attn",
    )(hd3, hd3, hd3, *biases)


def _mla_up_kernel(cq_ref, ckv_ref, kr_ref, qn_ref, kvn_ref, wq_ref, wqs_ref, wk_ref, wv_ref, psw_ref,
                   cos_ref, sin_ref, q_ref, k_ref, v_ref, *, scale):
    def rms(x_ref, g_ref):
        x = x_ref[...].astype(F32)
        return (x * lax.rsqrt(jnp.mean(x * x, axis=-1, keepdims=True) + RMS_EPS) * g_ref[...]).astype(BF16)

    xq = rms(cq_ref, qn_ref)
    xkv = rms(ckv_ref, kvn_ref)
    cos = cos_ref[...]
    sin = sin_ref[...]
    t = _dot(xq, wq_ref[...])
    ts = _dot(xq, wqs_ref[...])
    kn = _dot(xkv, wk_ref[...])
    kr = kr_ref[...]
    rk = kr.astype(F32) * cos + _dot(kr, psw_ref[...]) * sin
    for h in range(N_HEADS):
        sl = slice(h * LANES, (h + 1) * LANES)
        q_ref[:, sl] = ((t[:, sl] * cos + ts[:, sl] * sin) * scale).astype(BF16)
        k_ref[:, sl] = (kn[:, sl] + rk).astype(BF16)
    v_ref[...] = _dot(xkv, wv_ref[...]).astype(BF16)


def mla_up(h, q_norm, kv_norm, wq, wqs, wk, wv, psw, cos_t, sin_t, *, seq, tm):
    n = h.shape[0]
    w = N_HEADS * LANES
    const = lambda shape: pl.BlockSpec(shape, lambda i: (0, 0))
    spt = seq // tm
    return pl.pallas_call(
        functools.partial(_mla_up_kernel, scale=(B_NOPE_DIM + B_ROPE_DIM) ** -0.5 * math.log2(math.e)),
        grid=(n // tm,),
        in_specs=[pl.BlockSpec((tm, B_Q_LORA), lambda i: (i, COL_CQ // B_Q_LORA)),
                  pl.BlockSpec((tm, B_KV_LORA), lambda i: (i, COL_CKV // B_KV_LORA)),
                  pl.BlockSpec((tm, LANES), lambda i: (i, COL_KR // LANES)),
                  const((1, B_Q_LORA)), const((1, B_KV_LORA)),
                  const((B_Q_LORA, w)), const((B_Q_LORA, w)), const((B_KV_LORA, w)),
                  const((B_KV_LORA, GROUP_WIDTH)), const((LANES, LANES)),
                  pl.BlockSpec((tm, LANES), lambda i: (i % spt, 0)),
                  pl.BlockSpec((tm, LANES), lambda i: (i % spt, 0))],
        out_specs=[pl.BlockSpec((tm, w), lambda i: (i, 0)),
                   pl.BlockSpec((tm, w), lambda i: (i, 0)),
                   pl.BlockSpec((tm, GROUP_WIDTH), lambda i: (i, 0))],
        out_shape=[jax.ShapeDtypeStruct((n, w), BF16), jax.ShapeDtypeStruct((n, w), BF16),
                   jax.ShapeDtypeStruct((n, GROUP_WIDTH), BF16)],
        compiler_params=_params("arbitrary"),
        name="mla_up",
    )(h, h, h, q_norm, kv_norm, wq, wqs, wk, wv, psw, cos_t, sin_t)


def _mla_attn_kernel(q_ref, k_ref, v_ref, o_ref, *, tq, heads):
    i = pl.program_id(2)
    lo = _lo_lanes(tq)
    row = lax.broadcasted_iota(jnp.int32, (tq, tq), 0)
    col = lax.broadcasted_iota(jnp.int32, (tq, tq), 1)
    causal = col <= row

    def logits(j):
        start = pl.multiple_of(j * tq, tq)
        return tuple(_dot_nt(q_ref[0, :, hh * LANES:(hh + 1) * LANES],
                             k_ref[0, pl.ds(start, tq), hh * LANES:(hh + 1) * LANES]) for hh in range(heads))

    def update(j, stats, s_all, masked):
        start = pl.multiple_of(j * tq, tq)
        out = []
        for hh in range(heads):
            m, l, acc = stats[3 * hh:3 * hh + 3]
            s = s_all[hh]
            if masked:
                s = jnp.where(causal, s, NEG_INF)
            m_new = jnp.maximum(m, jnp.max(s, axis=1, keepdims=True))
            p = jnp.exp2(s - m_new)
            a = jnp.exp2(m - m_new)
            pv = _dot(p.astype(BF16), v_ref[0, pl.ds(start, tq), (hh // 2) * LANES:(hh // 2 + 1) * LANES])
            part = functools.reduce(jnp.add, [p[:, c * LANES:(c + 1) * LANES] for c in range(tq // LANES)])
            out += [m_new, a * l + part, a * acc + pv]
        return tuple(out)

    init = (jnp.full((tq, 1), NEG_INF, F32), jnp.zeros((tq, LANES), F32), jnp.zeros((tq, LANES), F32)) * heads
    stats = lax.fori_loop(0, i, lambda j, st: update(j, st, logits(j), False), init)
    stats = update(i, stats, logits(i), True)
    for pp in range(heads // 2):
        (_, l0, a0), (_, l1, a1) = stats[6 * pp:6 * pp + 3], stats[6 * pp + 3:6 * pp + 6]
        o0 = a0 / jnp.sum(l0, axis=1, keepdims=True)
        o1 = a1 / jnp.sum(l1, axis=1, keepdims=True)
        o_ref[0, :, pp * LANES:(pp + 1) * LANES] = jnp.where(lo, o0, o1).astype(o_ref.dtype)


def mla_attention(q, k, v, *, batch, seq, tq, heads):
    pairs = heads // 2
    return pl.pallas_call(
        functools.partial(_mla_attn_kernel, tq=tq, heads=heads),
        grid=(batch, N_HEADS // heads, seq // tq),
        in_specs=[pl.BlockSpec((1, tq, heads * LANES), lambda b, g, i: (b, i, g)),
                  pl.BlockSpec((1, seq, heads * LANES), lambda b, g, i: (b, 0, g)),
                  pl.BlockSpec((1, seq, pairs * LANES), lambda b, g, i: (b, 0, g))],
        out_specs=pl.BlockSpec((1, tq, pairs * LANES), lambda b, g, i: (b, i, g)),
        out_shape=jax.ShapeDtypeStruct((batch, seq, GROUP_WIDTH), BF16),
        compiler_params=_params("arbitrary", "arbitrary", "arbitrary"),
        name="mla_attn",
    )(q, k, v)


def _stick_kernel(q_ref, k_ref, v_ref, tri_ref, o_ref, *, tq, pairs):
    i = pl.program_id(2)
    lo = _lo_lanes(tq)
    row = lax.broadcasted_iota(jnp.int32, (2 * tq, tq), 0)
    col = lax.broadcasted_iota(jnp.int32, (2 * tq, tq), 1)
    strict = col < jnp.where(row >= tq, row - tq, row)
    tri = tri_ref[...]
    qs = [_split_pair(q_ref[0, :, pp * LANES:(pp + 1) * LANES], lo) * QK_SCALE for pp in range(pairs)]

    def chunk(pp, j, run, masked):
        start = pl.multiple_of(j * tq, tq)
        z = _dot_nt(qs[pp], k_ref[0, pl.ds(start, tq), pp * LANES:(pp + 1) * LANES])
        soft = jnp.log2(1.0 + jnp.exp2(-jnp.abs(z)))
        keep = -(jnp.maximum(z, 0.0) + soft)
        hit = jnp.minimum(z, 0.0) - soft
        if masked:
            keep = jnp.where(strict, keep, 0.0)
        hi = keep.astype(BF16)
        rest = (keep - hi.astype(F32)).astype(BF16)
        after = _dot(hi, tri) + _dot(rest, tri)
        a = jnp.exp2(hit + after)
        if masked:
            a = jnp.where(strict, a, 0.0)
        contrib = jnp.exp2(run) * _dot(a.astype(BF16), v_ref[0, pl.ds(start, tq), pp * LANES:(pp + 1) * LANES])
        return contrib, run + jnp.sum(keep, axis=1, keepdims=True)

    state = []
    has_prev = i > 0
    for pp in range(pairs):
        acc, run = chunk(pp, i, jnp.zeros((2 * tq, 1), F32), True)
        contrib, run_prev = chunk(pp, jnp.maximum(i - 1, 0), run, False)
        state += [jnp.where(has_prev, run_prev, run), acc + jnp.where(has_prev, contrib, 0.0)]

    def cond(carry):
        t = carry[0]
        live = carry[1]
        for pp in range(1, pairs):
            live = jnp.maximum(live, carry[1 + 2 * pp])
        return jnp.logical_and(t < i, jnp.max(live) > -EXP2_UNDERFLOW)

    def body(carry):
        t = carry[0]
        out = [t + 1]
        for pp in range(pairs):
            contrib, run = chunk(pp, i - 1 - t, carry[1 + 2 * pp], False)
            out += [run, carry[2 + 2 * pp] + contrib]
        return tuple(out)

    final = lax.while_loop(cond, body, (jnp.int32(1), *state))
    for pp in range(pairs):
        acc = final[2 + 2 * pp]
        o_ref[0, :, pp * LANES:(pp + 1) * LANES] = jnp.where(lo, acc[:tq], acc[tq:]).astype(o_ref.dtype)


def stick_breaking_attention(h3, tri, *, batch, seq, tq, pairs):
    w = pairs * LANES
    return pl.pallas_call(
        functools.partial(_stick_kernel, tq=tq, pairs=pairs),
        grid=(batch, GROUP_WIDTH // w, seq // tq),
        in_specs=[pl.BlockSpec((1, tq, w), lambda b, g, i: (b, i, COL_QC // w + g)),
                  pl.BlockSpec((1, seq, w), lambda b, g, i: (b, 0, COL_KC // w + g)),
                  pl.BlockSpec((1, seq, w), lambda b, g, i: (b, 0, COL_VC // w + g)),
                  pl.BlockSpec((tq, tq), lambda b, g, i: (0, 0))],
        out_specs=pl.BlockSpec((1, tq, w), lambda b, g, i: (b, i, g)),
        out_shape=jax.ShapeDtypeStruct((batch, seq, GROUP_WIDTH), BF16),
        compiler_params=_params("arbitrary", "arbitrary", "arbitrary"),
        name="stick_breaking",
    )(h3, h3, h3, tri)


def _layer_norm(r, g, b):
    mu = jnp.mean(r, axis=-1, keepdims=True)
    d = r - mu
    var = jnp.mean(d * d, axis=-1, keepdims=True)
    return d * lax.rsqrt(var + LN_EPS) * g + b


def _pack_bf16_pairs(xb):
    half = xb.shape[1] // 2
    lo = lax.bitcast_convert_type(xb[:, :half].astype(F32), jnp.uint32) >> 16
    hi = lax.bitcast_convert_type(xb[:, half:].astype(F32), jnp.uint32) & jnp.uint32(0xFFFF0000)
    return lax.bitcast_convert_type(hi | lo, F32)


def _unpack_bf16_pairs(xp):
    u = lax.bitcast_convert_type(xp, jnp.uint32)
    lo = lax.bitcast_convert_type(u << 16, F32).astype(BF16)
    hi = lax.bitcast_convert_type(u & jnp.uint32(0xFFFF0000), F32).astype(BF16)
    return jnp.concatenate([lo, hi], axis=1)


def _out_proj_kernel(ga_ref, gb_ref, gc_ref, gd_ref, x_ref, mixg_ref, wo_ref, g_ref, b_ref, rt_ref,
                     xo_ref, xb_ref, xp_ref, lg_ref):
    mixed = None
    for gi, grp in enumerate((ga_ref, gb_ref, gc_ref, gd_ref)):
        xg = grp[...].astype(F32)
        y = xg * lax.rsqrt(jnp.mean(xg * xg, axis=-1, keepdims=True) + RMS_EPS) * mixg_ref[gi:gi + 1, :]
        part = _dot(y.astype(BF16), wo_ref[0, gi * GROUP_WIDTH:(gi + 1) * GROUP_WIDTH, :])
        mixed = part if mixed is None else mixed + part
    x1 = _layer_norm(ALPHA * x_ref[...].astype(F32) + mixed, g_ref[...], b_ref[...])
    xo_ref[...] = x1
    xb = x1.astype(BF16)
    xb_ref[...] = xb
    xp_ref[...] = _pack_bf16_pairs(xb)
    lg_ref[...] = _dot(xb, rt_ref[...])


def out_proj_ln(ga, gb, gc, gd, x, mix_g, wo, layer, ln_g, ln_b, router, *, tm):
    n = x.shape[0]
    row = lambda w: pl.BlockSpec((tm, w), lambda i: (i, 0))
    const = lambda shape: pl.BlockSpec(shape, lambda i: (0, 0))
    return pl.pallas_call(
        _out_proj_kernel,
        grid=(n // tm,),
        in_specs=[row(GROUP_WIDTH)] * 4 + [row(D_MODEL), const((4, GROUP_WIDTH)),
                                          pl.BlockSpec((1, D_MODEL, D_MODEL), lambda i: (layer, 0, 0)),
                                          const((1, D_MODEL)), const((1, D_MODEL)), const((D_MODEL, LANES))],
        out_specs=[row(D_MODEL), row(D_MODEL), row(D_MODEL // 2), row(LANES)],
        out_shape=[jax.ShapeDtypeStruct((n, D_MODEL), F32), jax.ShapeDtypeStruct((n, D_MODEL), BF16),
                   jax.ShapeDtypeStruct((n, D_MODEL // 2), F32), jax.ShapeDtypeStruct((n, LANES), F32)],
        compiler_params=_params("arbitrary"),
        name="out_proj_ln",
    )(ga, gb, gc, gd, x, mix_g, wo, ln_g, ln_b, router)


def _silu_mul(g, u):
    return g * (1.0 / (1.0 + jnp.exp(-g))) * u


def _gate_up_kernel(x_ref, wg_ref, wu_ref, o_ref, wgb_ref, wub_ref):
    @pl.when(pl.program_id(1) == 0)
    def _():
        wgb_ref[...] = wg_ref[0].astype(BF16)
        wub_ref[...] = wu_ref[0].astype(BF16)

    x = x_ref[...]
    o_ref[...] = _silu_mul(_dot(x, wgb_ref[...]), _dot(x, wub_ref[...])).astype(o_ref.dtype)


def ffn_gate_up(xb, wg, wu, li, *, tm, tf):
    n = xb.shape[0]
    f = wg.shape[2]
    return pl.pallas_call(
        _gate_up_kernel,
        grid=(f // tf, n // tm),
        in_specs=[pl.BlockSpec((tm, D_MODEL), lambda j, i: (i, 0)),
                  pl.BlockSpec((1, D_MODEL, tf), lambda j, i: (li, 0, j)),
                  pl.BlockSpec((1, D_MODEL, tf), lambda j, i: (li, 0, j))],
        out_specs=pl.BlockSpec((tm, tf), lambda j, i: (i, j)),
        out_shape=jax.ShapeDtypeStruct((n, f), BF16),
        scratch_shapes=[pltpu.VMEM((D_MODEL, tf), BF16), pltpu.VMEM((D_MODEL, tf), BF16)],
        compiler_params=_params("arbitrary", "arbitrary"),
        name="ffn_gate_up",
    )(xb, wg, wu)


def _down_ln_kernel(h_ref, wd_ref, x_ref, g_ref, b_ref, xo_ref, xb_ref):
    kk = pl.program_id(1)
    part = _dot(h_ref[...], wd_ref[0])

    @pl.when(kk == 0)
    def _():
        xo_ref[...] = part

    @pl.when(kk > 0)
    def _():
        xo_ref[...] += part

    @pl.when(kk == pl.num_programs(1) - 1)
    def _():
        x2 = _layer_norm(ALPHA * x_ref[...] + xo_ref[...], g_ref[...], b_ref[...])
        xo_ref[...] = x2
        xb_ref[...] = x2.astype(BF16)


def ffn_down_ln(hmid, wd, li, x, ln_g, ln_b, *, tm, tk):
    n, f = hmid.shape
    return pl.pallas_call(
        _down_ln_kernel,
        grid=(n // tm, f // tk),
        in_specs=[pl.BlockSpec((tm, tk), lambda i, k: (i, k)),
                  pl.BlockSpec((1, tk, D_MODEL), lambda i, k: (li, k, 0)),
                  pl.BlockSpec((tm, D_MODEL), lambda i, k: (i, 0)),
                  pl.BlockSpec((1, D_MODEL), lambda i, k: (0, 0)),
                  pl.BlockSpec((1, D_MODEL), lambda i, k: (0, 0))],
        out_specs=[pl.BlockSpec((tm, D_MODEL), lambda i, k: (i, 0)),
                   pl.BlockSpec((tm, D_MODEL), lambda i, k: (i, 0))],
        out_shape=[jax.ShapeDtypeStruct((n, D_MODEL), F32), jax.ShapeDtypeStruct((n, D_MODEL), BF16)],
        compiler_params=_params("arbitrary", "arbitrary"),
        name="ffn_down_ln",
    )(hmid, wd, x, ln_g, ln_b)


def _new_expert(te_ref, t):
    return jnp.logical_or(t == 0, te_ref[t] != te_ref[jnp.maximum(t - 1, 0)])


MOE_ROW_STEP = 128


def _by_valid_rows(valid, tm, rows):
    for n in range(0, tm + 1, MOE_ROW_STEP):
        @pl.when(jnp.logical_and(valid > n - MOE_ROW_STEP, valid <= n))
        def _(n=n):
            rows(n)


def _moe_gate_up_kernel(te_ref, tv_ref, x_ref, wg_ref, wu_ref, o_ref, wgb_ref, wub_ref):
    t = pl.program_id(1)

    @pl.when(_new_expert(te_ref, t))
    def _():
        wgb_ref[...] = wg_ref[0, 0].astype(BF16)
        wub_ref[...] = wu_ref[0, 0].astype(BF16)

    def rows(n):
        pad = o_ref.shape[0] - n
        if n:
            x = _unpack_bf16_pairs(x_ref[pad:, :])
            o_ref[pad:, :] = _silu_mul(_dot(x, wgb_ref[...]), _dot(x, wub_ref[...])).astype(o_ref.dtype)
        if pad:
            o_ref[0:pad, :] = jnp.zeros((pad, o_ref.shape[1]), o_ref.dtype)

    _by_valid_rows(tv_ref[t], x_ref.shape[0], rows)


def moe_gate_up(tile_expert, tile_rows, xs, wg, wu, li, *, tm, tf):
    p = xs.shape[0]
    f = wg.shape[3]
    grid_spec = pltpu.PrefetchScalarGridSpec(
        num_scalar_prefetch=2,
        grid=(f // tf, p // tm),
        in_specs=[pl.BlockSpec((tm, D_MODEL // 2), lambda j, t, te, tv: (jnp.where(tv[t] > 0, t, 0), 0)),
                  pl.BlockSpec((1, 1, D_MODEL, tf), lambda j, t, te, tv: (li, te[t], 0, j)),
                  pl.BlockSpec((1, 1, D_MODEL, tf), lambda j, t, te, tv: (li, te[t], 0, j))],
        out_specs=pl.BlockSpec((tm, tf), lambda j, t, te, tv: (t, j)),
        scratch_shapes=[pltpu.VMEM((D_MODEL, tf), BF16), pltpu.VMEM((D_MODEL, tf), BF16)],
    )
    return pl.pallas_call(
        _moe_gate_up_kernel,
        grid_spec=grid_spec,
        out_shape=jax.ShapeDtypeStruct((p, f), BF16),
        compiler_params=_params("arbitrary", "arbitrary"),
        name="moe_gate_up",
    )(tile_expert, tile_rows, xs, wg, wu)


def _moe_down_kernel(te_ref, tv_ref, h_ref, wd_ref, o_ref, wdb_ref):
    t = pl.program_id(1)

    @pl.when(_new_expert(te_ref, t))
    def _():
        wdb_ref[...] = wd_ref[0, 0].astype(BF16)

    def rows(n):
        pad = o_ref.shape[0] - n
        if n:
            o_ref[pad:, :] = _pack_bf16_pairs(_dot(h_ref[pad:, :], wdb_ref[...]).astype(BF16))
        if pad:
            o_ref[0:pad, :] = jnp.zeros((pad, o_ref.shape[1]), o_ref.dtype)

    _by_valid_rows(tv_ref[t], h_ref.shape[0], rows)


def moe_down(tile_expert, tile_rows, hs, wd, li, *, tm, tn):
    p, f = hs.shape
    grid_spec = pltpu.PrefetchScalarGridSpec(
        num_scalar_prefetch=2,
        grid=(D_MODEL // tn, p // tm),
        in_specs=[pl.BlockSpec((tm, f), lambda j, t, te, tv: (jnp.where(tv[t] > 0, t, 0), 0)),
                  pl.BlockSpec((1, 1, f, tn), lambda j, t, te, tv: (li, te[t], 0, j))],
        out_specs=pl.BlockSpec((tm, tn // 2), lambda j, t, te, tv: (t, j)),
        scratch_shapes=[pltpu.VMEM((f, tn), BF16)],
    )
    return pl.pallas_call(
        _moe_down_kernel,
        grid_spec=grid_spec,
        out_shape=jax.ShapeDtypeStruct((p, D_MODEL // 2), F32),
        compiler_params=_params("arbitrary", "arbitrary"),
        name="moe_down",
    )(tile_expert, tile_rows, hs, wd)


def _combine_ln_kernel(x_ref, y0_ref, y1_ref, gate_ref, g_ref, b_ref, xo_ref, xb_ref, *, group):
    def unpack(y_ref):
        parts = [_unpack_bf16_pairs(y_ref[:, c:c + group]) for c in range(0, y_ref.shape[1], group)]
        return jnp.concatenate(parts, axis=1).astype(F32)

    f = gate_ref[:, 0:1] * unpack(y0_ref) + gate_ref[:, 1:2] * unpack(y1_ref)
    x2 = _layer_norm(ALPHA * x_ref[...] + f, g_ref[...], b_ref[...])
    xo_ref[...] = x2
    xb_ref[...] = x2.astype(BF16)


def combine_ln(x, y0, y1, gates, ln_g, ln_b, *, tm, group):
    n = x.shape[0]
    row = pl.BlockSpec((tm, D_MODEL), lambda i: (i, 0))
    packed = pl.BlockSpec((tm, D_MODEL // 2), lambda i: (i, 0))
    const = pl.BlockSpec((1, D_MODEL), lambda i: (0, 0))
    return pl.pallas_call(
        functools.partial(_combine_ln_kernel, group=group),
        grid=(n // tm,),
        in_specs=[row, packed, packed, pl.BlockSpec((tm, TOP_K), lambda i: (i, 0)), const, const],
        out_specs=[row, row],
        out_shape=[jax.ShapeDtypeStruct((n, D_MODEL), F32), jax.ShapeDtypeStruct((n, D_MODEL), BF16)],
        compiler_params=_params("arbitrary"),
        name="combine_ln",
    )(x, y0, y1, gates, ln_g, ln_b)


def _t5_bucket(dist):
    max_exact = NUM_BUCKETS // 2
    d = jnp.maximum(dist, 1).astype(F32)
    large = max_exact + (jnp.log(d / max_exact) / math.log(T5_MAX_DISTANCE / max_exact)
                         * (NUM_BUCKETS - max_exact)).astype(jnp.int32)
    large = jnp.minimum(large, NUM_BUCKETS - 1)
    return jnp.where(dist < max_exact, dist, large)


def _band_bias_masked(table, stride, max_dist):
    dist = jnp.arange(BLOCK)[:, None] + BLOCK - jnp.arange(2 * BLOCK)[None, :]
    onehot = jax.nn.one_hot(_t5_bucket(jnp.maximum(dist, 0) * stride), NUM_BUCKETS, dtype=F32)
    bias = jnp.einsum("qkb,bh->hqk", onehot, table.astype(F32), precision=lax.Precision.HIGHEST)
    valid = (dist >= 0) & (dist <= max_dist)
    return jnp.where(valid[None], bias, NEG_INF)


def _rope_lane_tables(seq):
    half = B_ROPE_DIM // 2
    inv = ROPE_THETA ** (-jnp.arange(0, B_ROPE_DIM, 2, dtype=F32) / B_ROPE_DIM)
    ang = jnp.arange(seq, dtype=F32)[:, None] * inv[None, :]
    cos, sin = jnp.cos(ang), jnp.sin(ang)
    ones = jnp.ones((seq, ROPE_LANE0), F32)
    zeros = jnp.zeros((seq, ROPE_LANE0), F32)
    pad = jnp.zeros((seq, LANES - ROPE_LANE0 - 2 * half), F32)
    cos_t = jnp.concatenate([ones, cos, cos, pad], axis=1)
    sin_t = jnp.concatenate([zeros, -sin, sin, pad], axis=1)
    return cos_t, sin_t


def _permute_heads(t, axis, order):
    parts = [lax.slice_in_dim(t, hh * HEAD_DIM, (hh + 1) * HEAD_DIM, axis=axis) for hh in order]
    return jnp.concatenate(parts, axis=axis)


IN_SEGMENTS = (("qa", 512), ("ka", 128), ("va", 128), ("cq", 384), ("ckv", 256), ("kr", 32),
               ("qc", 512), ("kc", 512), ("vc", 512), ("qd", 512), ("kd", 512), ("vd", 512))
IN_WIDTH = sum(size for _, size in IN_SEGMENTS)


def _relayout_w_in_kernel(w_ref, main_ref, d_ref):
    w = w_ref[0]
    cols = w.shape[1]
    seg = {}
    start = 0
    for name, size in IN_SEGMENTS:
        seg[name] = w[start:start + size, :]
        start += size
    qa = jnp.concatenate([seg["qa"][hh * HEAD_DIM:(hh + 1) * HEAD_DIM, :] for hh in A_HEAD_ORDER], axis=0)
    kr = jnp.concatenate([jnp.zeros((ROPE_LANE0, cols), F32), seg["kr"],
                          jnp.zeros((LANES - ROPE_LANE0 - B_ROPE_DIM, cols), F32)], axis=0)
    main = jnp.concatenate([qa, seg["qc"], seg["kc"], seg["cq"], seg["ka"], seg["vc"], seg["ckv"], seg["va"], kr],
                           axis=0)
    main_ref[0] = main.astype(BF16)
    d_ref[0] = jnp.concatenate([seg["qd"], seg["kd"], seg["vd"]], axis=0).astype(BF16)


def _relayout_w_in(w_in, *, tc):
    d = w_in.shape[0]
    return pl.pallas_call(
        _relayout_w_in_kernel,
        grid=(d, D_MODEL // tc),
        in_specs=[pl.BlockSpec((1, IN_WIDTH, tc), lambda l, i: (l, 0, i))],
        out_specs=[pl.BlockSpec((1, IN_WIDTH_P, tc), lambda l, i: (l, 0, i)),
                   pl.BlockSpec((1, IN_WIDTH_D, tc), lambda l, i: (l, 0, i))],
        out_shape=[jax.ShapeDtypeStruct((d, IN_WIDTH_P, D_MODEL), BF16),
                   jax.ShapeDtypeStruct((d, IN_WIDTH_D, D_MODEL), BF16)],
        compiler_params=_params("arbitrary", "arbitrary"),
        name="relayout_w_in",
    )(jnp.swapaxes(w_in, 1, 2))


def _relayout_mla(w_uq, w_ukv):
    d = w_uq.shape[0]
    half = B_ROPE_DIM // 2
    wq = w_uq.astype(BF16).reshape(d, B_Q_LORA, N_HEADS, B_NOPE_DIM + B_ROPE_DIM)
    nope, r1, r2 = wq[..., :B_NOPE_DIM], wq[..., B_NOPE_DIM:B_NOPE_DIM + half], wq[..., B_NOPE_DIM + half:]
    z32 = jnp.zeros(wq.shape[:3] + (LANES - B_NOPE_DIM - B_ROPE_DIM,), BF16)
    z64 = jnp.zeros(wq.shape[:3] + (B_NOPE_DIM,), BF16)
    wq_t = jnp.concatenate([nope, r1, r2, z32], axis=-1).reshape(d, B_Q_LORA, N_HEADS * LANES)
    wq_s = jnp.concatenate([z64, r2, r1, z32], axis=-1).reshape(d, B_Q_LORA, N_HEADS * LANES)
    wkv = w_ukv.astype(BF16).reshape(d, B_KV_LORA, N_HEADS, 2 * HEAD_DIM)
    zk = jnp.zeros(wkv.shape[:3] + (LANES - B_NOPE_DIM,), BF16)
    wk_t = jnp.concatenate([wkv[..., :B_NOPE_DIM], zk], axis=-1).reshape(d, B_KV_LORA, N_HEADS * LANES)
    wv = wkv[..., B_NOPE_DIM:].reshape(d, B_KV_LORA, GROUP_WIDTH)
    return wq_t, wq_s, wk_t, wv


def _rope_swap_matrix():
    half = B_ROPE_DIM // 2
    src = jnp.arange(LANES)[:, None]
    dst = jnp.arange(LANES)[None, :]
    first = (dst >= ROPE_LANE0) & (dst < ROPE_LANE0 + half) & (src == dst + half)
    second = (dst >= ROPE_LANE0 + half) & (dst < ROPE_LANE0 + 2 * half) & (src == dst - half)
    return (first | second).astype(BF16)


def _route(logits, tm):
    n = logits.shape[0]
    top_logits, top_idx = lax.top_k(logits, TOP_K)
    gates = jax.nn.softmax(top_logits, axis=-1)
    onehot = jax.nn.one_hot(top_idx, N_EXPERTS, dtype=jnp.int32)
    member = jnp.sum(onehot, axis=1)
    rank = jnp.cumsum(member, axis=0) - member
    counts = jnp.sum(member, axis=0)
    padded = ((counts + tm - 1) // tm) * tm
    ends = jnp.cumsum(padded)
    first = ends - counts
    pos = jnp.sum(onehot * (first[None, None, :] + rank[:, None, :]), axis=-1)
    n_rows = TOP_K * n + N_EXPERTS * tm
    src = (jnp.arange(n_rows, dtype=jnp.int32) % n).at[pos.reshape(-1)].set(
        jnp.repeat(jnp.arange(n, dtype=jnp.int32), TOP_K))

    def tile_tables(tile):
        tile_start = jnp.arange(n_rows // tile, dtype=jnp.int32) * tile
        tile_expert = jnp.minimum(jnp.sum((tile_start[:, None] >= ends[None, :]).astype(jnp.int32), axis=1),
                                  N_EXPERTS - 1)
        onehot_te = jax.nn.one_hot(tile_expert, N_EXPERTS, dtype=jnp.int32)
        group_first = jnp.sum(onehot_te * first[None, :], axis=1)
        rows = jnp.clip(tile_start + tile - group_first, 0, tile)
        rows = jnp.where(tile_start < ends[-1], rows, 0)
        return tile_expert.astype(jnp.int32), rows.astype(jnp.int32)

    return pos, gates, src, tile_tables


def kernel(x, w_in, w_o, mla_q_norm, mla_kv_norm, mla_w_uq, mla_w_ukv, attn_sinks, rel_bias_table, mix_norm_g,
           ln1_g, ln1_b, ln2_g, ln2_b, ffn_w_gate, ffn_w_up, ffn_w_down, moe_router, moe_w_gate, moe_w_up,
           moe_w_down):
    batch, seq, _ = x.shape
    n = batch * seq

    w_in_p, w_in_d = _relayout_w_in(w_in, tc=256)
    w_o_p = jnp.concatenate([_permute_heads(w_o[:, :GROUP_WIDTH], 1, A_HEAD_ORDER), w_o[:, GROUP_WIDTH:]],
                            axis=1).astype(BF16)
    mix_g_p = jnp.concatenate([_permute_heads(mix_norm_g[:, :1], 2, A_HEAD_ORDER), mix_norm_g[:, 1:]], axis=1)
    wq_t, wq_s, wk_t, wv = _relayout_mla(mla_w_uq, mla_w_ukv)
    psw = _rope_swap_matrix()
    cos_t, sin_t = _rope_lane_tables(seq)
    tq = 256
    tri = (jnp.arange(tq)[:, None] > jnp.arange(tq)[None, :]).astype(BF16)
    order = list(A_HEAD_ORDER)
    def with_first_block_variant(b):
        prev = jnp.arange(2 * BLOCK) < BLOCK
        return jnp.stack([b, jnp.where(prev, NEG_INF, b)], axis=0) * LOG2E

    bias_a = _band_bias_masked(rel_bias_table[:, :N_HEADS], 1, A_WINDOW - 1)
    bias_a_rows = with_first_block_variant(jnp.concatenate([bias_a[hh] for hh in order], axis=0))
    sink_rows = jnp.concatenate([jnp.broadcast_to(attn_sinks[:, hh, None, None] * LOG2E, (DEPTH, BLOCK, LANES))
                                 for hh in order], axis=1)
    biases_d = [with_first_block_variant(_band_bias_masked(rel_bias_table[:, N_HEADS:], rate, window // rate).reshape(
        N_HEADS // 2, 2 * BLOCK, 2 * BLOCK)) for window, rate in D_PATTERNS]
    router_p = jnp.pad(moe_router, ((0, 0), (0, 0), (0, LANES - N_EXPERTS))).astype(BF16)
    wd_d = ffn_w_down.astype(BF16)
    moe_tm = 1024

    xf = x.reshape(n, D_MODEL)
    xb = xf
    for layer in range(DEPTH):
        h = matmul_ws(xb, w_in_p, layer, 1024, 1024, BF16)
        hd = matmul_ws(xb, w_in_d, layer, 1024, 768, F32)
        h3 = h.reshape(batch, seq, IN_WIDTH_P)
        ga = swa_attention(h3, bias_a_rows, sink_rows[layer], batch=batch, seq=seq, blocks=8)
        q_b, k_b, v_b = mla_up(h, mla_q_norm[layer][None], mla_kv_norm[layer][None], wq_t[layer], wq_s[layer],
                               wk_t[layer], wv[layer], psw, cos_t, sin_t, seq=seq, tm=512)
        gb = mla_attention(q_b.reshape(batch, seq, -1), k_b.reshape(batch, seq, -1),
                           v_b.reshape(batch, seq, -1), batch=batch, seq=seq, tq=tq, heads=8)
        gc = stick_breaking_attention(h3, tri, batch=batch, seq=seq, tq=tq, pairs=4)
        gd = dilated_attention(hd.reshape(batch, seq, IN_WIDTH_D), biases_d, batch=batch, seq=seq)
        i = layer // 2
        router = router_p[i] if layer % 2 == 1 else jnp.zeros((D_MODEL, LANES), BF16)
        xf, xb, xp, logits = out_proj_ln(ga.reshape(n, -1), gb.reshape(n, -1), gc.reshape(n, -1), gd.reshape(n, -1),
                                     xf, mix_g_p[layer], w_o_p, layer, ln1_g[layer][None], ln1_b[layer][None],
                                     router, tm=256)
        if layer % 2 == 0:
            hmid = ffn_gate_up(xb, ffn_w_gate, ffn_w_up, i, tm=1024, tf=512)
            xf, xb = ffn_down_ln(hmid, wd_d, i, xf, ln2_g[layer][None], ln2_b[layer][None], tm=512, tk=1408)
        else:
            pos, gates, src, tile_tables = _route(logits[:, :N_EXPERTS], moe_tm)
            te_up, tr_up = tile_tables(moe_tm)
            te_dn, tr_dn = tile_tables(moe_tm // 2)
            xs = jnp.take(xp, src, axis=0, mode="clip")
            hs = moe_gate_up(te_up, tr_up, xs, moe_w_gate, moe_w_up, i, tm=moe_tm, tf=512)
            ys = moe_down(te_dn, tr_dn, hs, moe_w_down, i, tm=moe_tm // 2, tn=512)
            y0 = jnp.take(ys, pos[:, 0], axis=0, mode="clip")
            y1 = jnp.take(ys, pos[:, 1], axis=0, mode="clip")
            xf, xb = combine_ln(xf, y0, y1, gates, ln2_g[layer][None], ln2_b[layer][None], tm=512, group=256)
    return xf.reshape(batch, seq, D_MODEL)
```

```python
import functools
import math

import jax
import jax.numpy as jnp
from jax import lax
from jax.experimental import pallas as pl
from jax.experimental.pallas import tpu as pltpu

D_MODEL = 2048
DEPTH = 4
HEAD_DIM = 64
N_HEADS = 8
GROUP_WIDTH = N_HEADS * HEAD_DIM
BLOCK = 128
A_WINDOW = 128
B_NOPE_DIM = 64
B_ROPE_DIM = 32
B_Q_LORA = 384
B_KV_LORA = 256
ROPE_THETA = 10000.0
D_PATTERNS = ((128, 1), (512, 4), (2048, 16))
NUM_BUCKETS = 32
T5_MAX_DISTANCE = 2048
D_FF = 5632
N_EXPERTS = 8
TOP_K = 2
NEG_INF = -1e30
LN_EPS = 1e-5
RMS_EPS = 1e-6
ALPHA = (2 * DEPTH) ** 0.25

LANES = 128
VMEM_LIMIT = 56 * 1024 * 1024

IN_PROJ_TILE = (1024, 1024)
IN_PROJ_D_TILE = (1024, 768)
RELAYOUT_COLS = 256
SWA_BLOCKS = 8
MLA_UP_ROWS = 512
ATTN_TILE = 256
MLA_HEADS = 8
STICK_PAIRS = 4
OUT_PROJ_ROWS = 256
FFN_UP_TILE = (1024, 512)
FFN_DOWN_TILE = (512, D_FF // 4)
MOE_ROWS = 1024
MOE_COLS = 512
MOE_ROW_STEP = 128
COMBINE_ROWS = 512

COL_QA, COL_QC, COL_KC, COL_CQ, COL_KA = 0, 512, 1024, 1536, 1920
COL_VC, COL_CKV, COL_VA, COL_KR = 2048, 2560, 2816, 2944
IN_WIDTH_P = 3072
IN_WIDTH_D = 3 * GROUP_WIDTH
A_HEAD_ORDER = (0, 4, 1, 5, 2, 6, 3, 7)
ROPE_LANE0 = 64
LOG2E = math.log2(math.e)
EXP2_UNDERFLOW = 151.0
QK_SCALE = HEAD_DIM ** -0.5 * LOG2E

BF16 = jnp.bfloat16
F32 = jnp.float32


def _params(*sem):
    return pltpu.CompilerParams(dimension_semantics=sem, vmem_limit_bytes=VMEM_LIMIT)


def _dot(a, b):
    return jnp.dot(a, b, preferred_element_type=F32)


def _dot_nt(a, b):
    return lax.dot_general(a, b, (((1,), (1,)), ((), ())), preferred_element_type=F32)


def _lo_lanes(rows):
    return lax.broadcasted_iota(jnp.int32, (rows, LANES), 1) < HEAD_DIM


def _split_pair(t, lo):
    zero = jnp.zeros_like(t)
    return jnp.concatenate([jnp.where(lo, t, zero), jnp.where(lo, zero, t)], axis=0)


def _matmul_kernel(a_ref, w_ref, o_ref):
    o_ref[...] = _dot_nt(a_ref[...].astype(BF16), w_ref[0]).astype(o_ref.dtype)


def matmul_ws(a, w, layer, tm, tn, out_dtype):
    m, k = a.shape
    n = w.shape[1]
    return pl.pallas_call(
        _matmul_kernel,
        grid=(n // tn, m // tm),
        in_specs=[pl.BlockSpec((tm, k), lambda j, i: (i, 0)),
                  pl.BlockSpec((1, tn, k), lambda j, i: (layer, j, 0))],
        out_specs=pl.BlockSpec((tm, tn), lambda j, i: (i, j)),
        out_shape=jax.ShapeDtypeStruct((m, n), out_dtype),
        compiler_params=_params("arbitrary", "arbitrary"),
        name="matmul_ws",
    )(a, w)


def _swa_kernel(q_ref, kp_ref, kc_ref, vp_ref, vc_ref, bias_ref, sink_ref, o_ref, *, blocks):
    lo = _lo_lanes(BLOCK)
    k_all = jnp.concatenate([kp_ref[0], kc_ref[0]], axis=0)
    v_all = jnp.concatenate([vp_ref[0], vc_ref[0]], axis=0)
    first = jnp.where(pl.program_id(1) == 0, 1, 0)
    for bi in range(blocks):
        q_rows = slice(bi * BLOCK, (bi + 1) * BLOCK)
        kb = k_all[bi * BLOCK:(bi + 2) * BLOCK]
        vb = v_all[bi * BLOCK:(bi + 2) * BLOCK]
        which = first if bi == 0 else 0
        for c in range(GROUP_WIDTH // LANES):
            rows = slice(2 * c * BLOCK, 2 * (c + 1) * BLOCK)
            qs = _split_pair(q_ref[0, q_rows, c * LANES:(c + 1) * LANES], lo) * QK_SCALE
            s = _dot_nt(qs, kb) + bias_ref[which, rows, :]
            sink = sink_ref[rows, :]
            m = jnp.maximum(jnp.max(s, axis=1, keepdims=True), sink)
            p = jnp.exp2(s - jnp.concatenate([m, m], axis=1))
            den = jnp.sum(p, axis=1, keepdims=True) + jnp.exp2(sink - m)
            o = _dot(p.astype(BF16), vb) / den
            o_ref[0, q_rows, c * LANES:(c + 1) * LANES] = jnp.where(lo, o[:BLOCK], o[BLOCK:]).astype(o_ref.dtype)


def swa_attention(h3, bias_rows, sink_rows, *, batch, seq, blocks):
    span = blocks * BLOCK
    kb, vb = COL_KA // LANES, COL_VA // LANES
    prev = lambda col: pl.BlockSpec((1, BLOCK, LANES), lambda b, n: (b, jnp.maximum(n * blocks - 1, 0), col))
    cur = lambda col: pl.BlockSpec((1, span, LANES), lambda b, n: (b, n, col))
    return pl.pallas_call(
        functools.partial(_swa_kernel, blocks=blocks),
        grid=(batch, seq // span),
        in_specs=[pl.BlockSpec((1, span, GROUP_WIDTH), lambda b, n: (b, n, COL_QA // GROUP_WIDTH)),
                  prev(kb), cur(kb), prev(vb), cur(vb),
                  pl.BlockSpec((2, N_HEADS * BLOCK, 2 * BLOCK), lambda b, n: (0, 0, 0)),
                  pl.BlockSpec((N_HEADS * BLOCK, LANES), lambda b, n: (0, 0))],
        out_specs=pl.BlockSpec((1, span, GROUP_WIDTH), lambda b, n: (b, n, 0)),
        out_shape=jax.ShapeDtypeStruct((batch, seq, GROUP_WIDTH), BF16),
        compiler_params=_params("arbitrary", "arbitrary"),
        name="swa_attn",
    )(h3, h3, h3, h3, h3, bias_rows, sink_rows)


def _dilated_kernel(q_ref, k_ref, v_ref, b0_ref, b1_ref, b2_ref, o_ref, osc, msc, dsc):
    seq = q_ref.shape[1]
    lo = _lo_lanes(BLOCK)
    for p,((_, rate), b_ref) in enumerate(zip(D_PATTERNS, (b0_ref, b1_ref, b2_ref))):
        nb = seq // (rate * BLOCK)

        def unit(u, carry, p=p, rate=rate, b_ref=b_ref, nb=nb):
            n = u // rate
            res = u - n * rate
            rows = pl.ds(n * (BLOCK * rate) + res, BLOCK, stride=rate)
            qs = _split_pair((q_ref[0, rows, :] * QK_SCALE).astype(BF16), lo)
            kc = k_ref[0, rows, :].astype(BF16)
            vc = v_ref[0, rows, :].astype(BF16)
            if nb > 1:
                prow = pl.ds(jnp.maximum(n - 1, 0) * (BLOCK * rate) + res, BLOCK, stride=rate)
                kb = jnp.concatenate([k_ref[0, prow, :].astype(BF16), kc], axis=0)
                vb = jnp.concatenate([v_ref[0, prow, :].astype(BF16), vc], axis=0)
                s = _dot_nt(qs, kb) + b_ref[jnp.where(n == 0, 1, 0), 0]
            else:
                vb = vc
                s = _dot_nt(qs, kc) + b_ref[0, 0, :, BLOCK:2 * BLOCK]
            m = jnp.max(s, axis=1, keepdims=True)
            e = jnp.exp2(s - m)
            den = jnp.broadcast_to(jnp.sum(e, axis=1, keepdims=True), (2 * BLOCK, LANES))
            mx = jnp.broadcast_to(m, (2 * BLOCK, LANES))
            o = _dot(e.astype(BF16), vb)
            osc[p, rows, :] = jnp.where(lo, o[:BLOCK], o[BLOCK:])
            msc[p, rows, :] = jnp.where(lo, mx[:BLOCK], mx[BLOCK:])
            dsc[p, rows, :] = jnp.where(lo, den[:BLOCK], den[BLOCK:])
            return carry

        lax.fori_loop(0, nb * rate, unit, 0, unroll=16)

    chunk = 2 * BLOCK

    def merge(t, carry):
        r = pl.ds(pl.multiple_of(t * chunk, chunk), chunk)
        m0, m1, m2 = msc[0, r, :], msc[1, r, :], msc[2, r, :]
        m = jnp.maximum(jnp.maximum(m0, m1), m2)
        e0, e1, e2 = jnp.exp2(m0 - m), jnp.exp2(m1 - m), jnp.exp2(m2 - m)
        o = ((e0 * osc[0, r, :] + e1 * osc[1, r, :] + e2 * osc[2, r, :])
             / (e0 * dsc[0, r, :] + e1 * dsc[1, r, :] + e2 * dsc[2, r, :]))
        o_ref[0, r, :] = o.astype(o_ref.dtype)
        return carry

    lax.fori_loop(0, seq // chunk, merge, 0)


def dilated_attention(hd3, biases, *, batch, seq):
    pairs = N_HEADS // 2
    bspec = pl.BlockSpec((2, 1, 2 * BLOCK, 2 * BLOCK), lambda b, p: (0, p, 0, 0))
    return pl.pallas_call(
        _dilated_kernel,
        grid=(batch, pairs),
        in_specs=[pl.BlockSpec((1, seq, LANES), lambda b, p: (b, 0, p)),
                  pl.BlockSpec((1, seq, LANES), lambda b, p: (b, 0, pairs + p)),
                  pl.BlockSpec((1, seq, LANES), lambda b, p: (b, 0, 2 * pairs + p)),
                  bspec, bspec, bspec],
        out_specs=pl.BlockSpec((1, seq, LANES), lambda b, p: (b, 0, p)),
        out_shape=jax.ShapeDtypeStruct((batch, seq, GROUP_WIDTH), BF16),
        scratch_shapes=[pltpu.VMEM((len(D_PATTERNS), seq, LANES), F32)] * 3,
        compiler_params=_params("arbitrary", "arbitrary"),
        name="dilated_attn",
    )(hd3, hd3, hd3, *biases)


def _mla_up_kernel(cq_ref, ckv_ref, kr_ref, qn_ref, kvn_ref, wq_ref, wqs_ref, wk_ref, wv_ref, psw_ref,
                   cos_ref, sin_ref, q_ref, k_ref, v_ref, *, scale):
    def rms(x_ref, g_ref):
        x = x_ref[...].astype(F32)
        return (x * lax.rsqrt(jnp.mean(x * x, axis=-1, keepdims=True) + RMS_EPS) * g_ref[...]).astype(BF16)

    xq = rms(cq_ref, qn_ref)
    xkv = rms(ckv_ref, kvn_ref)
    cos = cos_ref[...]
    sin = sin_ref[...]
    t = _dot(xq, wq_ref[...])
    ts = _dot(xq, wqs_ref[...])
    kn = _dot(xkv, wk_ref[...])
    kr = kr_ref[...]
    rk = kr.astype(F32) * cos + _dot(kr, psw_ref[...]) * sin
    for h in range(N_HEADS):
        sl = slice(h * LANES, (h + 1) * LANES)
        q_ref[:, sl] = ((t[:, sl] * cos + ts[:, sl] * sin) * scale).astype(BF16)
        k_ref[:, sl] = (kn[:, sl] + rk).astype(BF16)
    v_ref[...] = _dot(xkv, wv_ref[...]).astype(BF16)


def mla_up(h, q_norm, kv_norm, wq, wqs, wk, wv, psw, cos_t, sin_t, *, seq, tm):
    n = h.shape[0]
    w = N_HEADS * LANES
    const = lambda shape: pl.BlockSpec(shape, lambda i: (0, 0))
    spt = seq // tm
    return pl.pallas_call(
        functools.partial(_mla_up_kernel, scale=(B_NOPE_DIM + B_ROPE_DIM) ** -0.5 * LOG2E),
        grid=(n // tm,),
        in_specs=[pl.BlockSpec((tm, B_Q_LORA), lambda i: (i, COL_CQ // B_Q_LORA)),
                  pl.BlockSpec((tm, B_KV_LORA), lambda i: (i, COL_CKV // B_KV_LORA)),
                  pl.BlockSpec((tm, LANES), lambda i: (i, COL_KR // LANES)),
                  const((1, B_Q_LORA)), const((1, B_KV_LORA)),
                  const((B_Q_LORA, w)), const((B_Q_LORA, w)), const((B_KV_LORA, w)),
                  const((B_KV_LORA, GROUP_WIDTH)), const((LANES, LANES)),
                  pl.BlockSpec((tm, LANES), lambda i: (i % spt, 0)),
                  pl.BlockSpec((tm, LANES), lambda i: (i % spt, 0))],
        out_specs=[pl.BlockSpec((tm, w), lambda i: (i, 0)),
                   pl.BlockSpec((tm, w), lambda i: (i, 0)),
                   pl.BlockSpec((tm, GROUP_WIDTH), lambda i: (i, 0))],
        out_shape=[jax.ShapeDtypeStruct((n, w), BF16), jax.ShapeDtypeStruct((n, w), BF16),
                   jax.ShapeDtypeStruct((n, GROUP_WIDTH), BF16)],
        compiler_params=_params("arbitrary"),
        name="mla_up",
    )(h, h, h, q_norm, kv_norm, wq, wqs, wk, wv, psw, cos_t, sin_t)


def _mla_attn_kernel(q_ref, k_ref, v_ref, o_ref, *, tq, heads):
    i = pl.program_id(2)
    lo = _lo_lanes(tq)
    row = lax.broadcasted_iota(jnp.int32, (tq, tq), 0)
    col = lax.broadcasted_iota(jnp.int32, (tq, tq), 1)
    causal = col <= row

    def logits(j):
        start = pl.multiple_of(j * tq, tq)
        return tuple(_dot_nt(q_ref[0, :, hh * LANES:(hh + 1) * LANES],
                             k_ref[0, pl.ds(start, tq), hh * LANES:(hh + 1) * LANES]) for hh in range(heads))

    def update(j, stats, s_all, masked):
        start = pl.multiple_of(j * tq, tq)
        out = []
        for pp in range(heads // 2):
            v = v_ref[0, pl.ds(start, tq), pp * LANES:(pp + 1) * LANES]
            v_both = jnp.concatenate([jnp.where(lo, v, jnp.zeros_like(v)), jnp.where(lo, jnp.zeros_like(v), v)], axis=0)
            ps, scales = [], []
            for hh in (2 * pp, 2 * pp + 1):
                m, l = stats[5 * pp + 2 * (hh % 2)], stats[5 * pp + 2 * (hh % 2) + 1]
                s = s_all[hh]
                if masked:
                    s = jnp.where(causal, s, NEG_INF)
                m_new = jnp.maximum(m, jnp.max(s, axis=1, keepdims=True))
                p = jnp.exp2(s - m_new)
                a = jnp.exp2(m - m_new)
                part = functools.reduce(jnp.add, [p[:, c * LANES:(c + 1) * LANES] for c in range(tq // LANES)])
                ps.append(p.astype(BF16))
                scales.append(a)
                out += [m_new, a * l + part]
            acc = stats[5 * pp + 4]
            out.append(jnp.where(lo, scales[0], scales[1]) * acc + _dot(jnp.concatenate(ps, axis=1), v_both))
        return tuple(out)

    pair_init = (jnp.full((tq, 1), NEG_INF, F32), jnp.zeros((tq, LANES), F32)) * 2 + (jnp.zeros((tq, LANES), F32),)
    stats = lax.fori_loop(0, i, lambda j, st: update(j, st, logits(j), False), pair_init * (heads // 2))
    stats = update(i, stats, logits(i), True)
    for pp in range(heads // 2):
        _, l0, _, l1, acc = stats[5 * pp:5 * pp + 5]
        den = jnp.where(lo, jnp.sum(l0, axis=1, keepdims=True), jnp.sum(l1, axis=1, keepdims=True))
        o_ref[0, :, pp * LANES:(pp + 1) * LANES] = (acc / den).astype(o_ref.dtype)


def mla_attention(q, k, v, *, batch, seq, tq, heads):
    pairs = heads // 2
    return pl.pallas_call(
        functools.partial(_mla_attn_kernel, tq=tq, heads=heads),
        grid=(batch, N_HEADS // heads, seq // tq),
        in_specs=[pl.BlockSpec((1, tq, heads * LANES), lambda b, g, i: (b, i, g)),
                  pl.BlockSpec((1, seq, heads * LANES), lambda b, g, i: (b, 0, g)),
                  pl.BlockSpec((1, seq, pairs * LANES), lambda b, g, i: (b, 0, g))],
        out_specs=pl.BlockSpec((1, tq, pairs * LANES), lambda b, g, i: (b, i, g)),
        out_shape=jax.ShapeDtypeStruct((batch, seq, GROUP_WIDTH), BF16),
        compiler_params=_params("arbitrary", "arbitrary", "arbitrary"),
        name="mla_attn",
    )(q, k, v)


def _stick_kernel(q_ref, k_ref, v_ref, tri_ref, o_ref, *, tq, pairs):
    i = pl.program_id(2)
    lo = _lo_lanes(tq)
    row = lax.broadcasted_iota(jnp.int32, (2 * tq, tq), 0)
    col = lax.broadcasted_iota(jnp.int32, (2 * tq, tq), 1)
    strict = col < jnp.where(row >= tq, row - tq, row)
    tri = tri_ref[...]
    qs = [_split_pair(q_ref[0, :, pp * LANES:(pp + 1) * LANES], lo) * QK_SCALE for pp in range(pairs)]

    def chunk(pp, j, run, masked):
        start = pl.multiple_of(j * tq, tq)
        z = _dot_nt(qs[pp], k_ref[0, pl.ds(start, tq), pp * LANES:(pp + 1) * LANES])
        soft = jnp.log2(1.0 + jnp.exp2(-jnp.abs(z)))
        keep = -(jnp.maximum(z, 0.0) + soft)
        hit = jnp.minimum(z, 0.0) - soft
        if masked:
            keep = jnp.where(strict, keep, 0.0)
        hi = keep.astype(BF16)
        rest = (keep - hi.astype(F32)).astype(BF16)
        after = _dot(hi, tri) + _dot(rest, tri)
        a = jnp.exp2(hit + after)
        if masked:
            a = jnp.where(strict, a, 0.0)
        contrib = jnp.exp2(run) * _dot(a.astype(BF16), v_ref[0, pl.ds(start, tq), pp * LANES:(pp + 1) * LANES])
        return contrib, run + jnp.sum(keep, axis=1, keepdims=True)

    state = []
    has_prev = i > 0
    for pp in range(pairs):
        acc, run = chunk(pp, i, jnp.zeros((2 * tq, 1), F32), True)
        contrib, run_prev = chunk(pp, jnp.maximum(i - 1, 0), run, False)
        state += [jnp.where(has_prev, run_prev, run), acc + jnp.where(has_prev, contrib, 0.0)]

    def cond(carry):
        t = carry[0]
        live = carry[1]
        for pp in range(1, pairs):
            live = jnp.maximum(live, carry[1 + 2 * pp])
        return jnp.logical_and(t < i, jnp.max(live) > -EXP2_UNDERFLOW)

    def body(carry):
        t = carry[0]
        out = [t + 1]
        for pp in range(pairs):
            contrib, run = chunk(pp, i - 1 - t, carry[1 + 2 * pp], False)
            out += [run, carry[2 + 2 * pp] + contrib]
        return tuple(out)

    final = lax.while_loop(cond, body, (jnp.int32(1), *state))
    for pp in range(pairs):
        acc = final[2 + 2 * pp]
        o_ref[0, :, pp * LANES:(pp + 1) * LANES] = jnp.where(lo, acc[:tq], acc[tq:]).astype(o_ref.dtype)


def stick_breaking_attention(h3, tri, *, batch, seq, tq, pairs):
    w = pairs * LANES
    return pl.pallas_call(
        functools.partial(_stick_kernel, tq=tq, pairs=pairs),
        grid=(batch, GROUP_WIDTH // w, seq // tq),
        in_specs=[pl.BlockSpec((1, tq, w), lambda b, g, i: (b, i, COL_QC // w + g)),
                  pl.BlockSpec((1, seq, w), lambda b, g, i: (b, 0, COL_KC // w + g)),
                  pl.BlockSpec((1, seq, w), lambda b, g, i: (b, 0, COL_VC // w + g)),
                  pl.BlockSpec((tq, tq), lambda b, g, i: (0, 0))],
        out_specs=pl.BlockSpec((1, tq, w), lambda b, g, i: (b, i, g)),
        out_shape=jax.ShapeDtypeStruct((batch, seq, GROUP_WIDTH), BF16),
        compiler_params=_params("arbitrary", "arbitrary", "arbitrary"),
        name="stick_breaking",
    )(h3, h3, h3, tri)


def _layer_norm(r, g, b):
    mu = jnp.mean(r, axis=-1, keepdims=True)
    d = r - mu
    var = jnp.mean(d * d, axis=-1, keepdims=True)
    return d * lax.rsqrt(var + LN_EPS) * g + b


def _pack_bf16_pairs(xb):
    half = xb.shape[1] // 2
    lo = lax.bitcast_convert_type(xb[:, :half].astype(F32), jnp.uint32) >> 16
    hi = lax.bitcast_convert_type(xb[:, half:].astype(F32), jnp.uint32) & jnp.uint32(0xFFFF0000)
    return lax.bitcast_convert_type(hi | lo, F32)


def _unpack_bf16_pairs(xp):
    u = lax.bitcast_convert_type(xp, jnp.uint32)
    lo = lax.bitcast_convert_type(u << 16, F32).astype(BF16)
    hi = lax.bitcast_convert_type(u & jnp.uint32(0xFFFF0000), F32).astype(BF16)
    return jnp.concatenate([lo, hi], axis=1)


def _out_proj_kernel(ga_ref, gb_ref, gc_ref, gd_ref, x_ref, mixg_ref, wo_ref, g_ref, b_ref, rt_ref,
                     xo_ref, xb_ref, xp_ref, lg_ref):
    mixed = None
    for gi, grp in enumerate((ga_ref, gb_ref, gc_ref, gd_ref)):
        xg = grp[...].astype(F32)
        y = xg * lax.rsqrt(jnp.mean(xg * xg, axis=-1, keepdims=True) + RMS_EPS) * mixg_ref[gi:gi + 1, :]
        part = _dot(y.astype(BF16), wo_ref[0, gi * GROUP_WIDTH:(gi + 1) * GROUP_WIDTH, :])
        mixed = part if mixed is None else mixed + part
    x1 = _layer_norm(ALPHA * x_ref[...].astype(F32) + mixed, g_ref[...], b_ref[...])
    xo_ref[...] = x1
    xb = x1.astype(BF16)
    xb_ref[...] = xb
    xp_ref[...] = _pack_bf16_pairs(xb)
    lg_ref[...] = _dot(xb, rt_ref[...])


def out_proj_ln(ga, gb, gc, gd, x, mix_g, wo, layer, ln_g, ln_b, router, *, tm):
    n = x.shape[0]
    row = lambda w: pl.BlockSpec((tm, w), lambda i: (i, 0))
    const = lambda shape: pl.BlockSpec(shape, lambda i: (0, 0))
    return pl.pallas_call(
        _out_proj_kernel,
        grid=(n // tm,),
        in_specs=[row(GROUP_WIDTH)] * 4 + [row(D_MODEL), const((4, GROUP_WIDTH)),
                                          pl.BlockSpec((1, D_MODEL, D_MODEL), lambda i: (layer, 0, 0)),
                                          const((1, D_MODEL)), const((1, D_MODEL)), const((D_MODEL, LANES))],
        out_specs=[row(D_MODEL), row(D_MODEL), row(D_MODEL // 2), row(LANES)],
        out_shape=[jax.ShapeDtypeStruct((n, D_MODEL), F32), jax.ShapeDtypeStruct((n, D_MODEL), BF16),
                   jax.ShapeDtypeStruct((n, D_MODEL // 2), F32), jax.ShapeDtypeStruct((n, LANES), F32)],
        compiler_params=_params("arbitrary"),
        name="out_proj_ln",
    )(ga, gb, gc, gd, x, mix_g, wo, ln_g, ln_b, router)


def _silu_mul(g, u):
    return g * (1.0 / (1.0 + jnp.exp(-g))) * u


def _gate_up_kernel(x_ref, wg_ref, wu_ref, o_ref, wgb_ref, wub_ref):
    @pl.when(pl.program_id(1) == 0)
    def _():
        wgb_ref[...] = wg_ref[0].astype(BF16)
        wub_ref[...] = wu_ref[0].astype(BF16)

    x = x_ref[...]
    o_ref[...] = _silu_mul(_dot(x, wgb_ref[...]), _dot(x, wub_ref[...])).astype(o_ref.dtype)


def ffn_gate_up(xb, wg, wu, li, *, tm, tf):
    n = xb.shape[0]
    f = wg.shape[2]
    return pl.pallas_call(
        _gate_up_kernel,
        grid=(f // tf, n // tm),
        in_specs=[pl.BlockSpec((tm, D_MODEL), lambda j, i: (i, 0)),
                  pl.BlockSpec((1, D_MODEL, tf), lambda j, i: (li, 0, j)),
                  pl.BlockSpec((1, D_MODEL, tf), lambda j, i: (li, 0, j))],
        out_specs=pl.BlockSpec((tm, tf), lambda j, i: (i, j)),
        out_shape=jax.ShapeDtypeStruct((n, f), BF16),
        scratch_shapes=[pltpu.VMEM((D_MODEL, tf), BF16), pltpu.VMEM((D_MODEL, tf), BF16)],
        compiler_params=_params("arbitrary", "arbitrary"),
        name="ffn_gate_up",
    )(xb, wg, wu)


def _down_ln_kernel(h_ref, wd_ref, x_ref, g_ref, b_ref, xo_ref, xb_ref):
    kk = pl.program_id(1)
    part = _dot(h_ref[...], wd_ref[0])

    @pl.when(kk == 0)
    def _():
        xo_ref[...] = part

    @pl.when(kk > 0)
    def _():
        xo_ref[...] += part

    @pl.when(kk == pl.num_programs(1) - 1)
    def _():
        x2 = _layer_norm(ALPHA * x_ref[...] + xo_ref[...], g_ref[...], b_ref[...])
        xo_ref[...] = x2
        xb_ref[...] = x2.astype(BF16)


def ffn_down_ln(hmid, wd, li, x, ln_g, ln_b, *, tm, tk):
    n, f = hmid.shape
    return pl.pallas_call(
        _down_ln_kernel,
        grid=(n // tm, f // tk),
        in_specs=[pl.BlockSpec((tm, tk), lambda i, k: (i, k)),
                  pl.BlockSpec((1, tk, D_MODEL), lambda i, k: (li, k, 0)),
                  pl.BlockSpec((tm, D_MODEL), lambda i, k: (i, 0)),
                  pl.BlockSpec((1, D_MODEL), lambda i, k: (0, 0)),
                  pl.BlockSpec((1, D_MODEL), lambda i, k: (0, 0))],
        out_specs=[pl.BlockSpec((tm, D_MODEL), lambda i, k: (i, 0)),
                   pl.BlockSpec((tm, D_MODEL), lambda i, k: (i, 0))],
        out_shape=[jax.ShapeDtypeStruct((n, D_MODEL), F32), jax.ShapeDtypeStruct((n, D_MODEL), BF16)],
        compiler_params=_params("arbitrary", "arbitrary"),
        name="ffn_down_ln",
    )(hmid, wd, x, ln_g, ln_b)


def _new_expert(te_ref, t):
    return jnp.logical_or(t == 0, te_ref[t] != te_ref[jnp.maximum(t - 1, 0)])


def _by_valid_rows(valid, tm, rows):
    for n in range(0, tm + 1, MOE_ROW_STEP):
        @pl.when(jnp.logical_and(valid > n - MOE_ROW_STEP, valid <= n))
        def _(n=n):
            rows(n)


def _moe_gate_up_kernel(te_ref, tv_ref, x_ref, wg_ref, wu_ref, o_ref, wgb_ref, wub_ref):
    t = pl.program_id(1)

    @pl.when(_new_expert(te_ref, t))
    def _():
        wgb_ref[...] = wg_ref[0, 0].astype(BF16)
        wub_ref[...] = wu_ref[0, 0].astype(BF16)

    def rows(n):
        pad = o_ref.shape[0] - n
        if n:
            x = _unpack_bf16_pairs(x_ref[pad:, :])
            o_ref[pad:, :] = _silu_mul(_dot(x, wgb_ref[...]), _dot(x, wub_ref[...])).astype(o_ref.dtype)
        if pad:
            o_ref[0:pad, :] = jnp.zeros((pad, o_ref.shape[1]), o_ref.dtype)

    _by_valid_rows(tv_ref[t], x_ref.shape[0], rows)


def moe_gate_up(tile_expert, tile_rows, xs, wg, wu, li, *, tm, tf):
    p = xs.shape[0]
    f = wg.shape[3]
    grid_spec = pltpu.PrefetchScalarGridSpec(
        num_scalar_prefetch=2,
        grid=(f // tf, p // tm),
        in_specs=[pl.BlockSpec((tm, D_MODEL // 2), lambda j, t, te, tv: (jnp.where(tv[t] > 0, t, 0), 0)),
                  pl.BlockSpec((1, 1, D_MODEL, tf), lambda j, t, te, tv: (li, te[t], 0, j)),
                  pl.BlockSpec((1, 1, D_MODEL, tf), lambda j, t, te, tv: (li, te[t], 0, j))],
        out_specs=pl.BlockSpec((tm, tf), lambda j, t, te, tv: (t, j)),
        scratch_shapes=[pltpu.VMEM((D_MODEL, tf), BF16), pltpu.VMEM((D_MODEL, tf), BF16)],
    )
    return pl.pallas_call(
        _moe_gate_up_kernel,
        grid_spec=grid_spec,
        out_shape=jax.ShapeDtypeStruct((p, f), BF16),
        compiler_params=_params("arbitrary", "arbitrary"),
        name="moe_gate_up",
    )(tile_expert, tile_rows, xs, wg, wu)


def _moe_down_kernel(te_ref, tv_ref, h_ref, wd_ref, o_ref, wdb_ref):
    t = pl.program_id(1)

    @pl.when(_new_expert(te_ref, t))
    def _():
        wdb_ref[...] = wd_ref[0, 0].astype(BF16)

    def rows(n):
        pad = o_ref.shape[0] - n
        if n:
            o_ref[pad:, :] = _pack_bf16_pairs(_dot(h_ref[pad:, :], wdb_ref[...]).astype(BF16))
        if pad:
            o_ref[0:pad, :] = jnp.zeros((pad, o_ref.shape[1]), o_ref.dtype)

    _by_valid_rows(tv_ref[t], h_ref.shape[0], rows)


def moe_down(tile_expert, tile_rows, hs, wd, li, *, tm, tn):
    p, f = hs.shape
    grid_spec = pltpu.PrefetchScalarGridSpec(
        num_scalar_prefetch=2,
        grid=(D_MODEL // tn, p // tm),
        in_specs=[pl.BlockSpec((tm, f), lambda j, t, te, tv: (jnp.where(tv[t] > 0, t, 0), 0)),
                  pl.BlockSpec((1, 1, f, tn), lambda j, t, te, tv: (li, te[t], 0, j))],
        out_specs=pl.BlockSpec((tm, tn // 2), lambda j, t, te, tv: (t, j)),
        scratch_shapes=[pltpu.VMEM((f, tn), BF16)],
    )
    return pl.pallas_call(
        _moe_down_kernel,
        grid_spec=grid_spec,
        out_shape=jax.ShapeDtypeStruct((p, D_MODEL // 2), F32),
        compiler_params=_params("arbitrary", "arbitrary"),
        name="moe_down",
    )(tile_expert, tile_rows, hs, wd)


def _combine_ln_kernel(x_ref, y0_ref, y1_ref, gate_ref, g_ref, b_ref, xo_ref, xb_ref, *, group):
    def unpack(y_ref):
        parts = [_unpack_bf16_pairs(y_ref[:, c:c + group]) for c in range(0, y_ref.shape[1], group)]
        return jnp.concatenate(parts, axis=1).astype(F32)

    f = gate_ref[:, 0:1] * unpack(y0_ref) + gate_ref[:, 1:2] * unpack(y1_ref)
    x2 = _layer_norm(ALPHA * x_ref[...] + f, g_ref[...], b_ref[...])
    xo_ref[...] = x2
    xb_ref[...] = x2.astype(BF16)


def combine_ln(x, y0, y1, gates, ln_g, ln_b, *, tm, group):
    n = x.shape[0]
    row = pl.BlockSpec((tm, D_MODEL), lambda i: (i, 0))
    packed = pl.BlockSpec((tm, D_MODEL // 2), lambda i: (i, 0))
    const = pl.BlockSpec((1, D_MODEL), lambda i: (0, 0))
    return pl.pallas_call(
        functools.partial(_combine_ln_kernel, group=group),
        grid=(n // tm,),
        in_specs=[row, packed, packed, pl.BlockSpec((tm, TOP_K), lambda i: (i, 0)), const, const],
        out_specs=[row, row],
        out_shape=[jax.ShapeDtypeStruct((n, D_MODEL), F32), jax.ShapeDtypeStruct((n, D_MODEL), BF16)],
        compiler_params=_params("arbitrary"),
        name="combine_ln",
    )(x, y0, y1, gates, ln_g, ln_b)


def _t5_bucket(dist):
    max_exact = NUM_BUCKETS // 2
    d = jnp.maximum(dist, 1).astype(F32)
    large = max_exact + (jnp.log(d / max_exact) / math.log(T5_MAX_DISTANCE / max_exact)
                         * (NUM_BUCKETS - max_exact)).astype(jnp.int32)
    large = jnp.minimum(large, NUM_BUCKETS - 1)
    return jnp.where(dist < max_exact, dist, large)


def _band_bias_masked(table, stride, max_dist):
    dist = jnp.arange(BLOCK)[:, None] + BLOCK - jnp.arange(2 * BLOCK)[None, :]
    onehot = jax.nn.one_hot(_t5_bucket(jnp.maximum(dist, 0) * stride), NUM_BUCKETS, dtype=F32)
    bias = jnp.einsum("qkb,bh->hqk", onehot, table.astype(F32), precision=lax.Precision.HIGHEST)
    valid = (dist >= 0) & (dist <= max_dist)
    return jnp.where(valid[None], bias, NEG_INF)


def _rope_lane_tables(seq):
    half = B_ROPE_DIM // 2
    inv = ROPE_THETA ** (-jnp.arange(0, B_ROPE_DIM, 2, dtype=F32) / B_ROPE_DIM)
    ang = jnp.arange(seq, dtype=F32)[:, None] * inv[None, :]
    cos, sin = jnp.cos(ang), jnp.sin(ang)
    ones = jnp.ones((seq, ROPE_LANE0), F32)
    zeros = jnp.zeros((seq, ROPE_LANE0), F32)
    pad = jnp.zeros((seq, LANES - ROPE_LANE0 - 2 * half), F32)
    cos_t = jnp.concatenate([ones, cos, cos, pad], axis=1)
    sin_t = jnp.concatenate([zeros, -sin, sin, pad], axis=1)
    return cos_t, sin_t


def _permute_heads(t, axis, order):
    parts = [lax.slice_in_dim(t, hh * HEAD_DIM, (hh + 1) * HEAD_DIM, axis=axis) for hh in order]
    return jnp.concatenate(parts, axis=axis)


IN_SEGMENTS = (("qa", 512), ("ka", 128), ("va", 128), ("cq", 384), ("ckv", 256), ("kr", 32),
               ("qc", 512), ("kc", 512), ("vc", 512), ("qd", 512), ("kd", 512), ("vd", 512))
IN_WIDTH = sum(size for _, size in IN_SEGMENTS)


def _relayout_w_in_kernel(w_ref, main_ref, d_ref):
    w = w_ref[0]
    cols = w.shape[1]
    seg = {}
    start = 0
    for name, size in IN_SEGMENTS:
        seg[name] = w[start:start + size, :]
        start += size
    qa = jnp.concatenate([seg["qa"][hh * HEAD_DIM:(hh + 1) * HEAD_DIM, :] for hh in A_HEAD_ORDER], axis=0)
    kr = jnp.concatenate([jnp.zeros((ROPE_LANE0, cols), F32), seg["kr"],
                          jnp.zeros((LANES - ROPE_LANE0 - B_ROPE_DIM, cols), F32)], axis=0)
    main = jnp.concatenate([qa, seg["qc"], seg["kc"], seg["cq"], seg["ka"], seg["vc"], seg["ckv"], seg["va"], kr],
                           axis=0)
    main_ref[0] = main.astype(BF16)
    d_ref[0] = jnp.concatenate([seg["qd"], seg["kd"], seg["vd"]], axis=0).astype(BF16)


def _relayout_w_in(w_in, *, tc):
    d = w_in.shape[0]
    return pl.pallas_call(
        _relayout_w_in_kernel,
        grid=(d, D_MODEL // tc),
        in_specs=[pl.BlockSpec((1, IN_WIDTH, tc), lambda l, i: (l, 0, i))],
        out_specs=[pl.BlockSpec((1, IN_WIDTH_P, tc), lambda l, i: (l, 0, i)),
                   pl.BlockSpec((1, IN_WIDTH_D, tc), lambda l, i: (l, 0, i))],
        out_shape=[jax.ShapeDtypeStruct((d, IN_WIDTH_P, D_MODEL), BF16),
                   jax.ShapeDtypeStruct((d, IN_WIDTH_D, D_MODEL), BF16)],
        compiler_params=_params("arbitrary", "arbitrary"),
        name="relayout_w_in",
    )(jnp.swapaxes(w_in, 1, 2))


def _relayout_mla(w_uq, w_ukv):
    d = w_uq.shape[0]
    half = B_ROPE_DIM // 2
    wq = w_uq.astype(BF16).reshape(d, B_Q_LORA, N_HEADS, B_NOPE_DIM + B_ROPE_DIM)
    nope, r1, r2 = wq[..., :B_NOPE_DIM], wq[..., B_NOPE_DIM:B_NOPE_DIM + half], wq[..., B_NOPE_DIM + half:]
    z32 = jnp.zeros(wq.shape[:3] + (LANES - B_NOPE_DIM - B_ROPE_DIM,), BF16)
    z64 = jnp.zeros(wq.shape[:3] + (B_NOPE_DIM,), BF16)
    wq_t = jnp.concatenate([nope, r1, r2, z32], axis=-1).reshape(d, B_Q_LORA, N_HEADS * LANES)
    wq_s = jnp.concatenate([z64, r2, r1, z32], axis=-1).reshape(d, B_Q_LORA, N_HEADS * LANES)
    wkv = w_ukv.astype(BF16).reshape(d, B_KV_LORA, N_HEADS, 2 * HEAD_DIM)
    zk = jnp.zeros(wkv.shape[:3] + (LANES - B_NOPE_DIM,), BF16)
    wk_t = jnp.concatenate([wkv[..., :B_NOPE_DIM], zk], axis=-1).reshape(d, B_KV_LORA, N_HEADS * LANES)
    wv = wkv[..., B_NOPE_DIM:].reshape(d, B_KV_LORA, GROUP_WIDTH)
    return wq_t, wq_s, wk_t, wv


def _rope_swap_matrix():
    half = B_ROPE_DIM // 2
    src = jnp.arange(LANES)[:, None]
    dst = jnp.arange(LANES)[None, :]
    first = (dst >= ROPE_LANE0) & (dst < ROPE_LANE0 + half) & (src == dst + half)
    second = (dst >= ROPE_LANE0 + half) & (dst < ROPE_LANE0 + 2 * half) & (src == dst - half)
    return (first | second).astype(BF16)


def _route(logits, tm):
    n = logits.shape[0]
    top_logits, top_idx = lax.top_k(logits, TOP_K)
    gates = jax.nn.softmax(top_logits, axis=-1)
    onehot = jax.nn.one_hot(top_idx, N_EXPERTS, dtype=jnp.int32)
    member = jnp.sum(onehot, axis=1)
    rank = jnp.cumsum(member, axis=0) - member
    counts = jnp.sum(member, axis=0)
    padded = ((counts + tm - 1) // tm) * tm
    ends = jnp.cumsum(padded)
    first = ends - counts
    pos = jnp.sum(onehot * (first[None, None, :] + rank[:, None, :]), axis=-1)
    n_rows = TOP_K * n + N_EXPERTS * tm
    src = (jnp.arange(n_rows, dtype=jnp.int32) % n).at[pos.reshape(-1)].set(
        jnp.repeat(jnp.arange(n, dtype=jnp.int32), TOP_K))

    tile_start = jnp.arange(n_rows // tm, dtype=jnp.int32) * tm
    tile_expert = jnp.minimum(jnp.sum((tile_start[:, None] >= ends[None, :]).astype(jnp.int32), axis=1),
                              N_EXPERTS - 1)
    onehot_te = jax.nn.one_hot(tile_expert, N_EXPERTS, dtype=jnp.int32)
    group_first = jnp.sum(onehot_te * first[None, :], axis=1)
    tile_rows = jnp.clip(tile_start + tm - group_first, 0, tm)
    tile_rows = jnp.where(tile_start < ends[-1], tile_rows, 0)
    return pos, gates, src, tile_expert.astype(jnp.int32), tile_rows.astype(jnp.int32)


def kernel(x, w_in, w_o, mla_q_norm, mla_kv_norm, mla_w_uq, mla_w_ukv, attn_sinks, rel_bias_table, mix_norm_g,
           ln1_g, ln1_b, ln2_g, ln2_b, ffn_w_gate, ffn_w_up, ffn_w_down, moe_router, moe_w_gate, moe_w_up,
           moe_w_down):
    batch, seq, _ = x.shape
    n = batch * seq

    w_in_p, w_in_d = _relayout_w_in(w_in, tc=RELAYOUT_COLS)
    w_o_p = jnp.concatenate([_permute_heads(w_o[:, :GROUP_WIDTH], 1, A_HEAD_ORDER), w_o[:, GROUP_WIDTH:]],
                            axis=1).astype(BF16)
    mix_g_p = jnp.concatenate([_permute_heads(mix_norm_g[:, :1], 2, A_HEAD_ORDER), mix_norm_g[:, 1:]], axis=1)
    wq_t, wq_s, wk_t, wv = _relayout_mla(mla_w_uq, mla_w_ukv)
    psw = _rope_swap_matrix()
    cos_t, sin_t = _rope_lane_tables(seq)
    tri = (jnp.arange(ATTN_TILE)[:, None] > jnp.arange(ATTN_TILE)[None, :]).astype(BF16)
    order = list(A_HEAD_ORDER)
    def with_first_block_variant(b):
        prev = jnp.arange(2 * BLOCK) < BLOCK
        return jnp.stack([b, jnp.where(prev, NEG_INF, b)], axis=0) * LOG2E

    bias_a = _band_bias_masked(rel_bias_table[:, :N_HEADS], 1, A_WINDOW - 1)
    bias_a_rows = with_first_block_variant(jnp.concatenate([bias_a[hh] for hh in order], axis=0))
    sink_rows = jnp.concatenate([jnp.broadcast_to(attn_sinks[:, hh, None, None] * LOG2E, (DEPTH, BLOCK, LANES))
                                 for hh in order], axis=1)
    biases_d = [with_first_block_variant(_band_bias_masked(rel_bias_table[:, N_HEADS:], rate, window // rate).reshape(
        N_HEADS // 2, 2 * BLOCK, 2 * BLOCK)) for window, rate in D_PATTERNS]
    router_p = jnp.pad(moe_router, ((0, 0), (0, 0), (0, LANES - N_EXPERTS))).astype(BF16)
    wd_d = ffn_w_down.astype(BF16)

    xf = x.reshape(n, D_MODEL)
    xb = xf
    for layer in range(DEPTH):
        h = matmul_ws(xb, w_in_p, layer, *IN_PROJ_TILE, BF16)
        hd = matmul_ws(xb, w_in_d, layer, *IN_PROJ_D_TILE, F32)
        h3 = h.reshape(batch, seq, IN_WIDTH_P)
        ga = swa_attention(h3, bias_a_rows, sink_rows[layer], batch=batch, seq=seq, blocks=SWA_BLOCKS)
        q_b, k_b, v_b = mla_up(h, mla_q_norm[layer][None], mla_kv_norm[layer][None], wq_t[layer], wq_s[layer],
                               wk_t[layer], wv[layer], psw, cos_t, sin_t, seq=seq, tm=MLA_UP_ROWS)
        gb = mla_attention(q_b.reshape(batch, seq, -1), k_b.reshape(batch, seq, -1),
                           v_b.reshape(batch, seq, -1), batch=batch, seq=seq, tq=ATTN_TILE, heads=MLA_HEADS)
        gc = stick_breaking_attention(h3, tri, batch=batch, seq=seq, tq=ATTN_TILE, pairs=STICK_PAIRS)
        gd = dilated_attention(hd.reshape(batch, seq, IN_WIDTH_D), biases_d, batch=batch, seq=seq)
        i = layer // 2
        router = router_p[i] if layer % 2 == 1 else jnp.zeros((D_MODEL, LANES), BF16)
        xf, xb, xp, logits = out_proj_ln(ga.reshape(n, -1), gb.reshape(n, -1), gc.reshape(n, -1), gd.reshape(n, -1),
                                     xf, mix_g_p[layer], w_o_p, layer, ln1_g[layer][None], ln1_b[layer][None],
                                     router, tm=OUT_PROJ_ROWS)
        if layer % 2 == 0:
            hmid = ffn_gate_up(xb, ffn_w_gate, ffn_w_up, i, tm=FFN_UP_TILE[0], tf=FFN_UP_TILE[1])
            xf, xb = ffn_down_ln(hmid, wd_d, i, xf, ln2_g[layer][None], ln2_b[layer][None],
                                 tm=FFN_DOWN_TILE[0], tk=FFN_DOWN_TILE[1])
        else:
            pos, gates, src, tile_expert, tile_rows = _route(logits[:, :N_EXPERTS], MOE_ROWS)
            xs = jnp.take(xp, src, axis=0, mode="clip")
            hs = moe_gate_up(tile_expert, tile_rows, xs, moe_w_gate, moe_w_up, i, tm=MOE_ROWS, tf=MOE_COLS)
            ys = moe_down(tile_expert, tile_rows, hs, moe_w_down, i, tm=MOE_ROWS, tn=MOE_COLS)
            y0 = jnp.take(ys, pos[:, 0], axis=0, mode="clip")
            y1 = jnp.take(ys, pos[:, 1], axis=0, mode="clip")
            xf, xb = combine_ln(xf, y0, y1, gates, ln2_g[layer][None], ln2_b[layer][None], tm=COMBINE_ROWS,
                                group=MOE_COLS // 2)
    return xf.reshape(batch, seq, D_MODEL)
```

```python
import functools
import math

import jax
import jax.numpy as jnp
from jax import lax
from jax.experimental import pallas as pl
from jax.experimental.pallas import tpu as pltpu

D_MODEL = 2048
DEPTH = 4
HEAD_DIM = 64
N_HEADS = 8
GROUP_WIDTH = N_HEADS * HEAD_DIM
BLOCK = 128
A_WINDOW = 128
B_NOPE_DIM = 64
B_ROPE_DIM = 32
B_Q_LORA = 384
B_KV_LORA = 256
ROPE_THETA = 10000.0
D_PATTERNS = ((128, 1), (512, 4), (2048, 16))
NUM_BUCKETS = 32
T5_MAX_DISTANCE = 2048
D_FF = 5632
N_EXPERTS = 8
TOP_K = 2
NEG_INF = -1e30
LN_EPS = 1e-5
RMS_EPS = 1e-6
ALPHA = (2 * DEPTH) ** 0.25

LANES = 128
VMEM_LIMIT = 56 * 1024 * 1024

IN_PROJ_TILE = (1024, 1024)
IN_PROJ_D_TILE = (1024, 768)
RELAYOUT_COLS = 256
SWA_BLOCKS = 8
MLA_UP_ROWS = 512
ATTN_TILE = 256
MLA_HEADS = 8
STICK_PAIRS = 4
OUT_PROJ_ROWS = 256
FFN_UP_TILE = (1024, 512)
FFN_DOWN_TILE = (512, D_FF // 2)
MOE_ROWS = 1024
MOE_COLS = 512
MOE_ROW_STEP = 128
COMBINE_ROWS = 512

COL_QA, COL_QC, COL_KC, COL_CQ, COL_KA = 0, 512, 1024, 1536, 1920
COL_VC, COL_CKV, COL_VA, COL_KR = 2048, 2560, 2816, 2944
IN_WIDTH_P = 3072
IN_WIDTH_D = 3 * GROUP_WIDTH
A_HEAD_ORDER = (0, 4, 1, 5, 2, 6, 3, 7)
ROPE_LANE0 = 64
LOG2E = math.log2(math.e)
EXP2_UNDERFLOW = 151.0
QK_SCALE = HEAD_DIM ** -0.5 * LOG2E

BF16 = jnp.bfloat16
F32 = jnp.float32


def _params(*sem):
    return pltpu.CompilerParams(dimension_semantics=sem, vmem_limit_bytes=VMEM_LIMIT)


def _dot(a, b):
    return jnp.dot(a, b, preferred_element_type=F32)


def _dot_nt(a, b):
    return lax.dot_general(a, b, (((1,), (1,)), ((), ())), preferred_element_type=F32)


def _lo_lanes(rows):
    return lax.broadcasted_iota(jnp.int32, (rows, LANES), 1) < HEAD_DIM


def _split_pair(t, lo):
    zero = jnp.zeros_like(t)
    return jnp.concatenate([jnp.where(lo, t, zero), jnp.where(lo, zero, t)], axis=0)


def _matmul_kernel(a_ref, w_ref, o_ref):
    o_ref[...] = _dot_nt(a_ref[...].astype(BF16), w_ref[0]).astype(o_ref.dtype)


def matmul_ws(a, w, layer, tm, tn, out_dtype):
    m, k = a.shape
    n = w.shape[1]
    return pl.pallas_call(
        _matmul_kernel,
        grid=(n // tn, m // tm),
        in_specs=[pl.BlockSpec((tm, k), lambda j, i: (i, 0)),
                  pl.BlockSpec((1, tn, k), lambda j, i: (layer, j, 0))],
        out_specs=pl.BlockSpec((tm, tn), lambda j, i: (i, j)),
        out_shape=jax.ShapeDtypeStruct((m, n), out_dtype),
        compiler_params=_params("arbitrary", "arbitrary"),
        name="matmul_ws",
    )(a, w)


def _swa_kernel(q_ref, kp_ref, kc_ref, vp_ref, vc_ref, bias_ref, sink_ref, o_ref, *, blocks):
    lo = _lo_lanes(BLOCK)
    k_all = jnp.concatenate([kp_ref[0], kc_ref[0]], axis=0)
    v_all = jnp.concatenate([vp_ref[0], vc_ref[0]], axis=0)
    first = jnp.where(pl.program_id(1) == 0, 1, 0)
    for bi in range(blocks):
        q_rows = slice(bi * BLOCK, (bi + 1) * BLOCK)
        kb = k_all[bi * BLOCK:(bi + 2) * BLOCK]
        vb = v_all[bi * BLOCK:(bi + 2) * BLOCK]
        which = first if bi == 0 else 0
        for c in range(GROUP_WIDTH // LANES):
            rows = slice(2 * c * BLOCK, 2 * (c + 1) * BLOCK)
            qs = _split_pair(q_ref[0, q_rows, c * LANES:(c + 1) * LANES], lo) * QK_SCALE
            s = _dot_nt(qs, kb) + bias_ref[which, rows, :]
            sink = sink_ref[rows, :]
            m = jnp.maximum(jnp.max(s, axis=1, keepdims=True), sink)
            p = jnp.exp2(s - jnp.concatenate([m, m], axis=1))
            den = jnp.sum(p, axis=1, keepdims=True) + jnp.exp2(sink - m)
            o = _dot(p.astype(BF16), vb) / den
            o_ref[0, q_rows, c * LANES:(c + 1) * LANES] = jnp.where(lo, o[:BLOCK], o[BLOCK:]).astype(o_ref.dtype)


def swa_attention(h3, bias_rows, sink_rows, *, batch, seq, blocks):
    span = blocks * BLOCK
    kb, vb = COL_KA // LANES, COL_VA // LANES
    prev = lambda col: pl.BlockSpec((1, BLOCK, LANES), lambda b, n: (b, jnp.maximum(n * blocks - 1, 0), col))
    cur = lambda col: pl.BlockSpec((1, span, LANES), lambda b, n: (b, n, col))
    return pl.pallas_call(
        functools.partial(_swa_kernel, blocks=blocks),
        grid=(batch, seq // span),
        in_specs=[pl.BlockSpec((1, span, GROUP_WIDTH), lambda b, n: (b, n, COL_QA // GROUP_WIDTH)),
                  prev(kb), cur(kb), prev(vb), cur(vb),
                  pl.BlockSpec((2, N_HEADS * BLOCK, 2 * BLOCK), lambda b, n: (0, 0, 0)),
                  pl.BlockSpec((N_HEADS * BLOCK, LANES), lambda b, n: (0, 0))],
        out_specs=pl.BlockSpec((1, span, GROUP_WIDTH), lambda b, n: (b, n, 0)),
        out_shape=jax.ShapeDtypeStruct((batch, seq, GROUP_WIDTH), BF16),
        compiler_params=_params("arbitrary", "arbitrary"),
        name="swa_attn",
    )(h3, h3, h3, h3, h3, bias_rows, sink_rows)


def _dilated_kernel(q_ref, k_ref, v_ref, b0_ref, b1_ref, b2_ref, o_ref, osc, msc, dsc):
    seq = q_ref.shape[1]
    lo = _lo_lanes(BLOCK)
    for p,((_, rate), b_ref) in enumerate(zip(D_PATTERNS, (b0_ref, b1_ref, b2_ref))):
        nb = seq // (rate * BLOCK)

        def unit(u, carry, p=p, rate=rate, b_ref=b_ref, nb=nb):
            n = u // rate
            res = u - n * rate
            rows = pl.ds(n * (BLOCK * rate) + res, BLOCK, stride=rate)
            qs = _split_pair((q_ref[0, rows, :] * QK_SCALE).astype(BF16), lo)
            kc = k_ref[0, rows, :].astype(BF16)
            vc = v_ref[0, rows, :].astype(BF16)
            if nb > 1:
                prow = pl.ds(jnp.maximum(n - 1, 0) * (BLOCK * rate) + res, BLOCK, stride=rate)
                kb = jnp.concatenate([k_ref[0, prow, :].astype(BF16), kc], axis=0)
                vb = jnp.concatenate([v_ref[0, prow, :].astype(BF16), vc], axis=0)
                s = _dot_nt(qs, kb) + b_ref[jnp.where(n == 0, 1, 0), 0]
            else:
                vb = vc
                s = _dot_nt(qs, kc) + b_ref[0, 0, :, BLOCK:2 * BLOCK]
            m = jnp.max(s, axis=1, keepdims=True)
            e = jnp.exp2(s - m)
            den = jnp.broadcast_to(jnp.sum(e, axis=1, keepdims=True), (2 * BLOCK, LANES))
            mx = jnp.broadcast_to(m, (2 * BLOCK, LANES))
            o = _dot(e.astype(BF16), vb)
            osc[p, rows, :] = jnp.where(lo, o[:BLOCK], o[BLOCK:])
            msc[p, rows, :] = jnp.where(lo, mx[:BLOCK], mx[BLOCK:])
            dsc[p, rows, :] = jnp.where(lo, den[:BLOCK], den[BLOCK:])
            return carry

        lax.fori_loop(0, nb * rate, unit, 0, unroll=16)

    chunk = 2 * BLOCK

    def merge(t, carry):
        r = pl.ds(pl.multiple_of(t * chunk, chunk), chunk)
        m0, m1, m2 = msc[0, r, :], msc[1, r, :], msc[2, r, :]
        m = jnp.maximum(jnp.maximum(m0, m1), m2)
        e0, e1, e2 = jnp.exp2(m0 - m), jnp.exp2(m1 - m), jnp.exp2(m2 - m)
        o = ((e0 * osc[0, r, :] + e1 * osc[1, r, :] + e2 * osc[2, r, :])
             / (e0 * dsc[0, r, :] + e1 * dsc[1, r, :] + e2 * dsc[2, r, :]))
        o_ref[0, r, :] = o.astype(o_ref.dtype)
        return carry

    lax.fori_loop(0, seq // chunk, merge, 0)


def dilated_attention(hd3, biases, *, batch, seq):
    pairs = N_HEADS // 2
    bspec = pl.BlockSpec((2, 1, 2 * BLOCK, 2 * BLOCK), lambda b, p: (0, p, 0, 0))
    return pl.pallas_call(
        _dilated_kernel,
        grid=(batch, pairs),
        in_specs=[pl.BlockSpec((1, seq, LANES), lambda b, p: (b, 0, p)),
                  pl.BlockSpec((1, seq, LANES), lambda b, p: (b, 0, pairs + p)),
                  pl.BlockSpec((1, seq, LANES), lambda b, p: (b, 0, 2 * pairs + p)),
                  bspec, bspec, bspec],
        out_specs=pl.BlockSpec((1, seq, LANES), lambda b, p: (b, 0, p)),
        out_shape=jax.ShapeDtypeStruct((batch, seq, GROUP_WIDTH), BF16),
        scratch_shapes=[pltpu.VMEM((len(D_PATTERNS), seq, LANES), F32)] * 3,
        compiler_params=_params("arbitrary", "arbitrary"),
        name="dilated_attn",
    )(hd3, hd3, hd3, *biases)


def _mla_up_kernel(cq_ref, ckv_ref, kr_ref, qn_ref, kvn_ref, wq_ref, wqs_ref, wk_ref, wv_ref, psw_ref,
                   cos_ref, sin_ref, q_ref, k_ref, v_ref, *, scale):
    def rms(x_ref, g_ref):
        x = x_ref[...].astype(F32)
        return (x * lax.rsqrt(jnp.mean(x * x, axis=-1, keepdims=True) + RMS_EPS) * g_ref[...]).astype(BF16)

    xq = rms(cq_ref, qn_ref)
    xkv = rms(ckv_ref, kvn_ref)
    cos = cos_ref[...]
    sin = sin_ref[...]
    t = _dot(xq, wq_ref[...])
    ts = _dot(xq, wqs_ref[...])
    kn = _dot(xkv, wk_ref[...])
    kr = kr_ref[...]
    rk = kr.astype(F32) * cos + _dot(kr, psw_ref[...]) * sin
    for h in range(N_HEADS):
        sl = slice(h * LANES, (h + 1) * LANES)
        q_ref[:, sl] = ((t[:, sl] * cos + ts[:, sl] * sin) * scale).astype(BF16)
        k_ref[:, sl] = (kn[:, sl] + rk).astype(BF16)
    v_ref[...] = _dot(xkv, wv_ref[...]).astype(BF16)


def mla_up(h, q_norm, kv_norm, wq, wqs, wk, wv, psw, cos_t, sin_t, *, seq, tm):
    n = h.shape[0]
    w = N_HEADS * LANES
    const = lambda shape: pl.BlockSpec(shape, lambda i: (0, 0))
    spt = seq // tm
    return pl.pallas_call(
        functools.partial(_mla_up_kernel, scale=(B_NOPE_DIM + B_ROPE_DIM) ** -0.5 * LOG2E),
        grid=(n // tm,),
        in_specs=[pl.BlockSpec((tm, B_Q_LORA), lambda i: (i, COL_CQ // B_Q_LORA)),
                  pl.BlockSpec((tm, B_KV_LORA), lambda i: (i, COL_CKV // B_KV_LORA)),
                  pl.BlockSpec((tm, LANES), lambda i: (i, COL_KR // LANES)),
                  const((1, B_Q_LORA)), const((1, B_KV_LORA)),
                  const((B_Q_LORA, w)), const((B_Q_LORA, w)), const((B_KV_LORA, w)),
                  const((B_KV_LORA, GROUP_WIDTH)), const((LANES, LANES)),
                  pl.BlockSpec((tm, LANES), lambda i: (i % spt, 0)),
                  pl.BlockSpec((tm, LANES), lambda i: (i % spt, 0))],
        out_specs=[pl.BlockSpec((tm, w), lambda i: (i, 0)),
                   pl.BlockSpec((tm, w), lambda i: (i, 0)),
                   pl.BlockSpec((tm, GROUP_WIDTH), lambda i: (i, 0))],
        out_shape=[jax.ShapeDtypeStruct((n, w), BF16), jax.ShapeDtypeStruct((n, w), BF16),
                   jax.ShapeDtypeStruct((n, GROUP_WIDTH), BF16)],
        compiler_params=_params("arbitrary"),
        name="mla_up",
    )(h, h, h, q_norm, kv_norm, wq, wqs, wk, wv, psw, cos_t, sin_t)


def _mla_attn_kernel(q_ref, k_ref, v_ref, o_ref, *, tq, heads):
    i = pl.program_id(2)
    lo = _lo_lanes(tq)
    row = lax.broadcasted_iota(jnp.int32, (tq, tq), 0)
    col = lax.broadcasted_iota(jnp.int32, (tq, tq), 1)
    causal = col <= row

    def logits(j):
        start = pl.multiple_of(j * tq, tq)
        return tuple(_dot_nt(q_ref[0, :, hh * LANES:(hh + 1) * LANES],
                             k_ref[0, pl.ds(start, tq), hh * LANES:(hh + 1) * LANES]) for hh in range(heads))

    def update(j, stats, s_all, masked):
        start = pl.multiple_of(j * tq, tq)
        out = []
        for pp in range(heads // 2):
            v = v_ref[0, pl.ds(start, tq), pp * LANES:(pp + 1) * LANES]
            v_both = jnp.concatenate([jnp.where(lo, v, jnp.zeros_like(v)), jnp.where(lo, jnp.zeros_like(v), v)], axis=0)
            ps, scales = [], []
            for hh in (2 * pp, 2 * pp + 1):
                m, l = stats[5 * pp + 2 * (hh % 2)], stats[5 * pp + 2 * (hh % 2) + 1]
                s = s_all[hh]
                if masked:
                    s = jnp.where(causal, s, NEG_INF)
                m_new = jnp.maximum(m, jnp.max(s, axis=1, keepdims=True))
                p = jnp.exp2(s - m_new)
                a = jnp.exp2(m - m_new)
                part = functools.reduce(jnp.add, [p[:, c * LANES:(c + 1) * LANES] for c in range(tq // LANES)])
                ps.append(p.astype(BF16))
                scales.append(a)
                out += [m_new, a * l + part]
            acc = stats[5 * pp + 4]
            out.append(jnp.where(lo, scales[0], scales[1]) * acc + _dot(jnp.concatenate(ps, axis=1), v_both))
        return tuple(out)

    pair_init = (jnp.full((tq, 1), NEG_INF, F32), jnp.zeros((tq, LANES), F32)) * 2 + (jnp.zeros((tq, LANES), F32),)
    stats = lax.fori_loop(0, i, lambda j, st: update(j, st, logits(j), False), pair_init * (heads // 2))
    stats = update(i, stats, logits(i), True)
    for pp in range(heads // 2):
        _, l0, _, l1, acc = stats[5 * pp:5 * pp + 5]
        den = jnp.where(lo, jnp.sum(l0, axis=1, keepdims=True), jnp.sum(l1, axis=1, keepdims=True))
        o_ref[0, :, pp * LANES:(pp + 1) * LANES] = (acc / den).astype(o_ref.dtype)


def mla_attention(q, k, v, *, batch, seq, tq, heads):
    pairs = heads // 2
    return pl.pallas_call(
        functools.partial(_mla_attn_kernel, tq=tq, heads=heads),
        grid=(batch, N_HEADS // heads, seq // tq),
        in_specs=[pl.BlockSpec((1, tq, heads * LANES), lambda b, g, i: (b, i, g)),
                  pl.BlockSpec((1, seq, heads * LANES), lambda b, g, i: (b, 0, g)),
                  pl.BlockSpec((1, seq, pairs * LANES), lambda b, g, i: (b, 0, g))],
        out_specs=pl.BlockSpec((1, tq, pairs * LANES), lambda b, g, i: (b, i, g)),
        out_shape=jax.ShapeDtypeStruct((batch, seq, GROUP_WIDTH), BF16),
        compiler_params=_params("arbitrary", "arbitrary", "arbitrary"),
        name="mla_attn",
    )(q, k, v)


def _stick_kernel(q_ref, k_ref, v_ref, tri_ref, o_ref, *, tq, pairs):
    i = pl.program_id(2)
    lo = _lo_lanes(tq)
    row = lax.broadcasted_iota(jnp.int32, (2 * tq, tq), 0)
    col = lax.broadcasted_iota(jnp.int32, (2 * tq, tq), 1)
    strict = col < jnp.where(row >= tq, row - tq, row)
    tri = tri_ref[...]
    qs = [_split_pair(q_ref[0, :, pp * LANES:(pp + 1) * LANES], lo) * QK_SCALE for pp in range(pairs)]

    def chunk(pp, j, run, masked):
        start = pl.multiple_of(j * tq, tq)
        z = _dot_nt(qs[pp], k_ref[0, pl.ds(start, tq), pp * LANES:(pp + 1) * LANES])
        soft = jnp.log2(1.0 + jnp.exp2(-jnp.abs(z)))
        keep = -(jnp.maximum(z, 0.0) + soft)
        hit = jnp.minimum(z, 0.0) - soft
        if masked:
            keep = jnp.where(strict, keep, 0.0)
        hi = keep.astype(BF16)
        rest = (keep - hi.astype(F32)).astype(BF16)
        after = _dot(hi, tri) + _dot(rest, tri)
        a = jnp.exp2(hit + after)
        if masked:
            a = jnp.where(strict, a, 0.0)
        contrib = jnp.exp2(run) * _dot(a.astype(BF16), v_ref[0, pl.ds(start, tq), pp * LANES:(pp + 1) * LANES])
        return contrib, run + jnp.sum(keep, axis=1, keepdims=True)

    state = []
    has_prev = i > 0
    for pp in range(pairs):
        acc, run = chunk(pp, i, jnp.zeros((2 * tq, 1), F32), True)
        contrib, run_prev = chunk(pp, jnp.maximum(i - 1, 0), run, False)
        state += [jnp.where(has_prev, run_prev, run), acc + jnp.where(has_prev, contrib, 0.0)]

    def cond(carry):
        t = carry[0]
        live = carry[1]
        for pp in range(1, pairs):
            live = jnp.maximum(live, carry[1 + 2 * pp])
        return jnp.logical_and(t < i, jnp.max(live) > -EXP2_UNDERFLOW)

    def body(carry):
        t = carry[0]
        out = [t + 1]
        for pp in range(pairs):
            contrib, run = chunk(pp, i - 1 - t, carry[1 + 2 * pp], False)
            out += [run, carry[2 + 2 * pp] + contrib]
        return tuple(out)

    final = lax.while_loop(cond, body, (jnp.int32(1), *state))
    for pp in range(pairs):
        acc = final[2 + 2 * pp]
        o_ref[0, :, pp * LANES:(pp + 1) * LANES] = jnp.where(lo, acc[:tq], acc[tq:]).astype(o_ref.dtype)


def stick_breaking_attention(h3, tri, *, batch, seq, tq, pairs):
    w = pairs * LANES
    return pl.pallas_call(
        functools.partial(_stick_kernel, tq=tq, pairs=pairs),
        grid=(batch, GROUP_WIDTH // w, seq // tq),
        in_specs=[pl.BlockSpec((1, tq, w), lambda b, g, i: (b, i, COL_QC // w + g)),
                  pl.BlockSpec((1, seq, w), lambda b, g, i: (b, 0, COL_KC // w + g)),
                  pl.BlockSpec((1, seq, w), lambda b, g, i: (b, 0, COL_VC // w + g)),
                  pl.BlockSpec((tq, tq), lambda b, g, i: (0, 0))],
        out_specs=pl.BlockSpec((1, tq, w), lambda b, g, i: (b, i, g)),
        out_shape=jax.ShapeDtypeStruct((batch, seq, GROUP_WIDTH), BF16),
        compiler_params=_params("arbitrary", "arbitrary", "arbitrary"),
        name="stick_breaking",
    )(h3, h3, h3, tri)


def _layer_norm(r, g, b):
    mu = jnp.mean(r, axis=-1, keepdims=True)
    d = r - mu
    var = jnp.mean(d * d, axis=-1, keepdims=True)
    return d * lax.rsqrt(var + LN_EPS) * g + b


def _pack_bf16_pairs(xb):
    half = xb.shape[1] // 2
    lo = lax.bitcast_convert_type(xb[:, :half].astype(F32), jnp.uint32) >> 16
    hi = lax.bitcast_convert_type(xb[:, half:].astype(F32), jnp.uint32) & jnp.uint32(0xFFFF0000)
    return lax.bitcast_convert_type(hi | lo, F32)


def _unpack_bf16_pairs(xp):
    u = lax.bitcast_convert_type(xp, jnp.uint32)
    lo = lax.bitcast_convert_type(u << 16, F32).astype(BF16)
    hi = lax.bitcast_convert_type(u & jnp.uint32(0xFFFF0000), F32).astype(BF16)
    return jnp.concatenate([lo, hi], axis=1)


def _out_proj_kernel(ga_ref, gb_ref, gc_ref, gd_ref, x_ref, mixg_ref, wo_ref, g_ref, b_ref, rt_ref,
                     xo_ref, xb_ref, xp_ref, lg_ref):
    mixed = None
    for gi, grp in enumerate((ga_ref, gb_ref, gc_ref, gd_ref)):
        xg = grp[...].astype(F32)
        y = xg * lax.rsqrt(jnp.mean(xg * xg, axis=-1, keepdims=True) + RMS_EPS) * mixg_ref[gi:gi + 1, :]
        part = _dot(y.astype(BF16), wo_ref[0, gi * GROUP_WIDTH:(gi + 1) * GROUP_WIDTH, :])
        mixed = part if mixed is None else mixed + part
    x1 = _layer_norm(ALPHA * x_ref[...].astype(F32) + mixed, g_ref[...], b_ref[...])
    xo_ref[...] = x1
    xb = x1.astype(BF16)
    xb_ref[...] = xb
    xp_ref[...] = _pack_bf16_pairs(xb)
    lg_ref[...] = _dot(xb, rt_ref[...])


def out_proj_ln(ga, gb, gc, gd, x, mix_g, wo, layer, ln_g, ln_b, router, *, tm):
    n = x.shape[0]
    row = lambda w: pl.BlockSpec((tm, w), lambda i: (i, 0))
    const = lambda shape: pl.BlockSpec(shape, lambda i: (0, 0))
    return pl.pallas_call(
        _out_proj_kernel,
        grid=(n // tm,),
        in_specs=[row(GROUP_WIDTH)] * 4 + [row(D_MODEL), const((4, GROUP_WIDTH)),
                                          pl.BlockSpec((1, D_MODEL, D_MODEL), lambda i: (layer, 0, 0)),
                                          const((1, D_MODEL)), const((1, D_MODEL)), const((D_MODEL, LANES))],
        out_specs=[row(D_MODEL), row(D_MODEL), row(D_MODEL // 2), row(LANES)],
        out_shape=[jax.ShapeDtypeStruct((n, D_MODEL), F32), jax.ShapeDtypeStruct((n, D_MODEL), BF16),
                   jax.ShapeDtypeStruct((n, D_MODEL // 2), F32), jax.ShapeDtypeStruct((n, LANES), F32)],
        compiler_params=_params("arbitrary"),
        name="out_proj_ln",
    )(ga, gb, gc, gd, x, mix_g, wo, ln_g, ln_b, router)


def _silu_mul(g, u):
    return g * (1.0 / (1.0 + jnp.exp(-g))) * u


def _gate_up_kernel(x_ref, wg_ref, wu_ref, o_ref, wgb_ref, wub_ref):
    @pl.when(pl.program_id(1) == 0)
    def _():
        wgb_ref[...] = wg_ref[0].astype(BF16)
        wub_ref[...] = wu_ref[0].astype(BF16)

    x = x_ref[...]
    o_ref[...] = _silu_mul(_dot(x, wgb_ref[...]), _dot(x, wub_ref[...])).astype(o_ref.dtype)


def ffn_gate_up(xb, wg, wu, li, *, tm, tf):
    n = xb.shape[0]
    f = wg.shape[2]
    return pl.pallas_call(
        _gate_up_kernel,
        grid=(f // tf, n // tm),
        in_specs=[pl.BlockSpec((tm, D_MODEL), lambda j, i: (i, 0)),
                  pl.BlockSpec((1, D_MODEL, tf), lambda j, i: (li, 0, j)),
                  pl.BlockSpec((1, D_MODEL, tf), lambda j, i: (li, 0, j))],
        out_specs=pl.BlockSpec((tm, tf), lambda j, i: (i, j)),
        out_shape=jax.ShapeDtypeStruct((n, f), BF16),
        scratch_shapes=[pltpu.VMEM((D_MODEL, tf), BF16), pltpu.VMEM((D_MODEL, tf), BF16)],
        compiler_params=_params("arbitrary", "arbitrary"),
        name="ffn_gate_up",
    )(xb, wg, wu)


def _down_ln_kernel(h_ref, wd_ref, x_ref, g_ref, b_ref, xo_ref, xb_ref):
    kk = pl.program_id(1)
    part = _dot(h_ref[...], wd_ref[0])

    @pl.when(kk == 0)
    def _():
        xo_ref[...] = part

    @pl.when(kk > 0)
    def _():
        xo_ref[...] += part

    @pl.when(kk == pl.num_programs(1) - 1)
    def _():
        x2 = _layer_norm(ALPHA * x_ref[...] + xo_ref[...], g_ref[...], b_ref[...])
        xo_ref[...] = x2
        xb_ref[...] = x2.astype(BF16)


def ffn_down_ln(hmid, wd, li, x, ln_g, ln_b, *, tm, tk):
    n, f = hmid.shape
    return pl.pallas_call(
        _down_ln_kernel,
        grid=(n // tm, f // tk),
        in_specs=[pl.BlockSpec((tm, tk), lambda i, k: (i, k)),
                  pl.BlockSpec((1, tk, D_MODEL), lambda i, k: (li, k, 0)),
                  pl.BlockSpec((tm, D_MODEL), lambda i, k: (i, 0)),
                  pl.BlockSpec((1, D_MODEL), lambda i, k: (0, 0)),
                  pl.BlockSpec((1, D_MODEL), lambda i, k: (0, 0))],
        out_specs=[pl.BlockSpec((tm, D_MODEL), lambda i, k: (i, 0)),
                   pl.BlockSpec((tm, D_MODEL), lambda i, k: (i, 0))],
        out_shape=[jax.ShapeDtypeStruct((n, D_MODEL), F32), jax.ShapeDtypeStruct((n, D_MODEL), BF16)],
        compiler_params=_params("arbitrary", "arbitrary"),
        name="ffn_down_ln",
    )(hmid, wd, x, ln_g, ln_b)


def _new_expert(te_ref, t):
    return jnp.logical_or(t == 0, te_ref[t] != te_ref[jnp.maximum(t - 1, 0)])


def _by_valid_rows(valid, tm, rows):
    for n in range(0, tm + 1, MOE_ROW_STEP):
        @pl.when(jnp.logical_and(valid > n - MOE_ROW_STEP, valid <= n))
        def _(n=n):
            rows(n)


def _moe_gate_up_kernel(te_ref, tv_ref, x_ref, wg_ref, wu_ref, o_ref, wgb_ref, wub_ref):
    t = pl.program_id(1)

    @pl.when(_new_expert(te_ref, t))
    def _():
        wgb_ref[...] = wg_ref[0, 0].astype(BF16)
        wub_ref[...] = wu_ref[0, 0].astype(BF16)

    def rows(n):
        pad = o_ref.shape[0] - n
        if n:
            x = _unpack_bf16_pairs(x_ref[pad:, :])
            o_ref[pad:, :] = _silu_mul(_dot(x, wgb_ref[...]), _dot(x, wub_ref[...])).astype(o_ref.dtype)
        if pad:
            o_ref[0:pad, :] = jnp.zeros((pad, o_ref.shape[1]), o_ref.dtype)

    _by_valid_rows(tv_ref[t], x_ref.shape[0], rows)


def moe_gate_up(tile_expert, tile_rows, xs, wg, wu, li, *, tm, tf):
    p = xs.shape[0]
    f = wg.shape[3]
    grid_spec = pltpu.PrefetchScalarGridSpec(
        num_scalar_prefetch=2,
        grid=(f // tf, p // tm),
        in_specs=[pl.BlockSpec((tm, D_MODEL // 2), lambda j, t, te, tv: (jnp.where(tv[t] > 0, t, 0), 0)),
                  pl.BlockSpec((1, 1, D_MODEL, tf), lambda j, t, te, tv: (li, te[t], 0, j)),
                  pl.BlockSpec((1, 1, D_MODEL, tf), lambda j, t, te, tv: (li, te[t], 0, j))],
        out_specs=pl.BlockSpec((tm, tf), lambda j, t, te, tv: (t, j)),
        scratch_shapes=[pltpu.VMEM((D_MODEL, tf), BF16), pltpu.VMEM((D_MODEL, tf), BF16)],
    )
    return pl.pallas_call(
        _moe_gate_up_kernel,
        grid_spec=grid_spec,
        out_shape=jax.ShapeDtypeStruct((p, f), BF16),
        compiler_params=_params("arbitrary", "arbitrary"),
        name="moe_gate_up",
    )(tile_expert, tile_rows, xs, wg, wu)


def _moe_down_kernel(te_ref, tv_ref, h_ref, wd_ref, o_ref, wdb_ref):
    t = pl.program_id(1)

    @pl.when(_new_expert(te_ref, t))
    def _():
        wdb_ref[...] = wd_ref[0, 0].astype(BF16)

    def rows(n):
        pad = o_ref.shape[0] - n
        if n:
            o_ref[pad:, :] = _pack_bf16_pairs(_dot(h_ref[pad:, :], wdb_ref[...]).astype(BF16))
        if pad:
            o_ref[0:pad, :] = jnp.zeros((pad, o_ref.shape[1]), o_ref.dtype)

    _by_valid_rows(tv_ref[t], h_ref.shape[0], rows)


def moe_down(tile_expert, tile_rows, hs, wd, li, *, tm, tn):
    p, f = hs.shape
    grid_spec = pltpu.PrefetchScalarGridSpec(
        num_scalar_prefetch=2,
        grid=(D_MODEL // tn, p // tm),
        in_specs=[pl.BlockSpec((tm, f), lambda j, t, te, tv: (jnp.where(tv[t] > 0, t, 0), 0)),
                  pl.BlockSpec((1, 1, f, tn), lambda j, t, te, tv: (li, te[t], 0, j))],
        out_specs=pl.BlockSpec((tm, tn // 2), lambda j, t, te, tv: (t, j)),
        scratch_shapes=[pltpu.VMEM((f, tn), BF16)],
    )
    return pl.pallas_call(
        _moe_down_kernel,
        grid_spec=grid_spec,
        out_shape=jax.ShapeDtypeStruct((p, D_MODEL // 2), F32),
        compiler_params=_params("arbitrary", "arbitrary"),
        name="moe_down",
    )(tile_expert, tile_rows, hs, wd)


def _combine_ln_kernel(x_ref, y0_ref, y1_ref, gate_ref, g_ref, b_ref, xo_ref, xb_ref, *, group):
    def unpack(y_ref):
        parts = [_unpack_bf16_pairs(y_ref[:, c:c + group]) for c in range(0, y_ref.shape[1], group)]
        return jnp.concatenate(parts, axis=1).astype(F32)

    f = gate_ref[:, 0:1] * unpack(y0_ref) + gate_ref[:, 1:2] * unpack(y1_ref)
    x2 = _layer_norm(ALPHA * x_ref[...] + f, g_ref[...], b_ref[...])
    xo_ref[...] = x2
    xb_ref[...] = x2.astype(BF16)


def combine_ln(x, y0, y1, gates, ln_g, ln_b, *, tm, group):
    n = x.shape[0]
    row = pl.BlockSpec((tm, D_MODEL), lambda i: (i, 0))
    packed = pl.BlockSpec((tm, D_MODEL // 2), lambda i: (i, 0))
    const = pl.BlockSpec((1, D_MODEL), lambda i: (0, 0))
    return pl.pallas_call(
        functools.partial(_combine_ln_kernel, group=group),
        grid=(n // tm,),
        in_specs=[row, packed, packed, pl.BlockSpec((tm, TOP_K), lambda i: (i, 0)), const, const],
        out_specs=[row, row],
        out_shape=[jax.ShapeDtypeStruct((n, D_MODEL), F32), jax.ShapeDtypeStruct((n, D_MODEL), BF16)],
        compiler_params=_params("arbitrary"),
        name="combine_ln",
    )(x, y0, y1, gates, ln_g, ln_b)


def _t5_bucket(dist):
    max_exact = NUM_BUCKETS // 2
    d = jnp.maximum(dist, 1).astype(F32)
    large = max_exact + (jnp.log(d / max_exact) / math.log(T5_MAX_DISTANCE / max_exact)
                         * (NUM_BUCKETS - max_exact)).astype(jnp.int32)
    large = jnp.minimum(large, NUM_BUCKETS - 1)
    return jnp.where(dist < max_exact, dist, large)


def _band_bias_masked(table, stride, max_dist):
    dist = jnp.arange(BLOCK)[:, None] + BLOCK - jnp.arange(2 * BLOCK)[None, :]
    onehot = jax.nn.one_hot(_t5_bucket(jnp.maximum(dist, 0) * stride), NUM_BUCKETS, dtype=F32)
    bias = jnp.einsum("qkb,bh->hqk", onehot, table.astype(F32), precision=lax.Precision.HIGHEST)
    valid = (dist >= 0) & (dist <= max_dist)
    return jnp.where(valid[None], bias, NEG_INF)


def _rope_lane_tables(seq):
    half = B_ROPE_DIM // 2
    inv = ROPE_THETA ** (-jnp.arange(0, B_ROPE_DIM, 2, dtype=F32) / B_ROPE_DIM)
    ang = jnp.arange(seq, dtype=F32)[:, None] * inv[None, :]
    cos, sin = jnp.cos(ang), jnp.sin(ang)
    ones = jnp.ones((seq, ROPE_LANE0), F32)
    zeros = jnp.zeros((seq, ROPE_LANE0), F32)
    pad = jnp.zeros((seq, LANES - ROPE_LANE0 - 2 * half), F32)
    cos_t = jnp.concatenate([ones, cos, cos, pad], axis=1)
    sin_t = jnp.concatenate([zeros, -sin, sin, pad], axis=1)
    return cos_t, sin_t


def _permute_heads(t, axis, order):
    parts = [lax.slice_in_dim(t, hh * HEAD_DIM, (hh + 1) * HEAD_DIM, axis=axis) for hh in order]
    return jnp.concatenate(parts, axis=axis)


IN_SEGMENTS = (("qa", 512), ("ka", 128), ("va", 128), ("cq", 384), ("ckv", 256), ("kr", 32),
               ("qc", 512), ("kc", 512), ("vc", 512), ("qd", 512), ("kd", 512), ("vd", 512))
IN_WIDTH = sum(size for _, size in IN_SEGMENTS)


def _relayout_w_in_kernel(w_ref, main_ref, d_ref):
    w = w_ref[0]
    cols = w.shape[1]
    seg = {}
    start = 0
    for name, size in IN_SEGMENTS:
        seg[name] = w[start:start + size, :]
        start += size
    qa = jnp.concatenate([seg["qa"][hh * HEAD_DIM:(hh + 1) * HEAD_DIM, :] for hh in A_HEAD_ORDER], axis=0)
    kr = jnp.concatenate([jnp.zeros((ROPE_LANE0, cols), F32), seg["kr"],
                          jnp.zeros((LANES - ROPE_LANE0 - B_ROPE_DIM, cols), F32)], axis=0)
    main = jnp.concatenate([qa, seg["qc"], seg["kc"], seg["cq"], seg["ka"], seg["vc"], seg["ckv"], seg["va"], kr],
                           axis=0)
    main_ref[0] = main.astype(BF16)
    d_ref[0] = jnp.concatenate([seg["qd"], seg["kd"], seg["vd"]], axis=0).astype(BF16)


def _relayout_w_in(w_in, *, tc):
    d = w_in.shape[0]
    return pl.pallas_call(
        _relayout_w_in_kernel,
        grid=(d, D_MODEL // tc),
        in_specs=[pl.BlockSpec((1, IN_WIDTH, tc), lambda l, i: (l, 0, i))],
        out_specs=[pl.BlockSpec((1, IN_WIDTH_P, tc), lambda l, i: (l, 0, i)),
                   pl.BlockSpec((1, IN_WIDTH_D, tc), lambda l, i: (l, 0, i))],
        out_shape=[jax.ShapeDtypeStruct((d, IN_WIDTH_P, D_MODEL), BF16),
                   jax.ShapeDtypeStruct((d, IN_WIDTH_D, D_MODEL), BF16)],
        compiler_params=_params("arbitrary", "arbitrary"),
        name="relayout_w_in",
    )(jnp.swapaxes(w_in, 1, 2))


def _relayout_mla(w_uq, w_ukv):
    d = w_uq.shape[0]
    half = B_ROPE_DIM // 2
    wq = w_uq.astype(BF16).reshape(d, B_Q_LORA, N_HEADS, B_NOPE_DIM + B_ROPE_DIM)
    nope, r1, r2 = wq[..., :B_NOPE_DIM], wq[..., B_NOPE_DIM:B_NOPE_DIM + half], wq[..., B_NOPE_DIM + half:]
    z32 = jnp.zeros(wq.shape[:3] + (LANES - B_NOPE_DIM - B_ROPE_DIM,), BF16)
    z64 = jnp.zeros(wq.shape[:3] + (B_NOPE_DIM,), BF16)
    wq_t = jnp.concatenate([nope, r1, r2, z32], axis=-1).reshape(d, B_Q_LORA, N_HEADS * LANES)
    wq_s = jnp.concatenate([z64, r2, r1, z32], axis=-1).reshape(d, B_Q_LORA, N_HEADS * LANES)
    wkv = w_ukv.astype(BF16).reshape(d, B_KV_LORA, N_HEADS, 2 * HEAD_DIM)
    zk = jnp.zeros(wkv.shape[:3] + (LANES - B_NOPE_DIM,), BF16)
    wk_t = jnp.concatenate([wkv[..., :B_NOPE_DIM], zk], axis=-1).reshape(d, B_KV_LORA, N_HEADS * LANES)
    wv = wkv[..., B_NOPE_DIM:].reshape(d, B_KV_LORA, GROUP_WIDTH)
    return wq_t, wq_s, wk_t, wv


def _rope_swap_matrix():
    half = B_ROPE_DIM // 2
    src = jnp.arange(LANES)[:, None]
    dst = jnp.arange(LANES)[None, :]
    first = (dst >= ROPE_LANE0) & (dst < ROPE_LANE0 + half) & (src == dst + half)
    second = (dst >= ROPE_LANE0 + half) & (dst < ROPE_LANE0 + 2 * half) & (src == dst - half)
    return (first | second).astype(BF16)


def _route(logits, tm):
    n = logits.shape[0]
    top_logits, top_idx = lax.top_k(logits, TOP_K)
    gates = jax.nn.softmax(top_logits, axis=-1)
    onehot = jax.nn.one_hot(top_idx, N_EXPERTS, dtype=jnp.int32)
    member = jnp.sum(onehot, axis=1)
    rank = jnp.cumsum(member, axis=0) - member
    counts = jnp.sum(member, axis=0)
    padded = ((counts + tm - 1) // tm) * tm
    ends = jnp.cumsum(padded)
    first = ends - counts
    pos = jnp.sum(onehot * (first[None, None, :] + rank[:, None, :]), axis=-1)
    n_rows = TOP_K * n + N_EXPERTS * tm
    src = (jnp.arange(n_rows, dtype=jnp.int32) % n).at[pos.reshape(-1)].set(
        jnp.repeat(jnp.arange(n, dtype=jnp.int32), TOP_K))

    tile_start = jnp.arange(n_rows // tm, dtype=jnp.int32) * tm
    tile_expert = jnp.minimum(jnp.sum((tile_start[:, None] >= ends[None, :]).astype(jnp.int32), axis=1),
                              N_EXPERTS - 1)
    onehot_te = jax.nn.one_hot(tile_expert, N_EXPERTS, dtype=jnp.int32)
    group_first = jnp.sum(onehot_te * first[None, :], axis=1)
    tile_rows = jnp.clip(tile_start + tm - group_first, 0, tm)
    tile_rows = jnp.where(tile_start < ends[-1], tile_rows, 0)
    return pos, gates, src, tile_expert.astype(jnp.int32), tile_rows.astype(jnp.int32)


def kernel(x, w_in, w_o, mla_q_norm, mla_kv_norm, mla_w_uq, mla_w_ukv, attn_sinks, rel_bias_table, mix_norm_g,
           ln1_g, ln1_b, ln2_g, ln2_b, ffn_w_gate, ffn_w_up, ffn_w_down, moe_router, moe_w_gate, moe_w_up,
           moe_w_down):
    batch, seq, _ = x.shape
    n = batch * seq

    w_in_p, w_in_d = _relayout_w_in(w_in, tc=RELAYOUT_COLS)
    w_o_p = jnp.concatenate([_permute_heads(w_o[:, :GROUP_WIDTH], 1, A_HEAD_ORDER), w_o[:, GROUP_WIDTH:]],
                            axis=1).astype(BF16)
    mix_g_p = jnp.concatenate([_permute_heads(mix_norm_g[:, :1], 2, A_HEAD_ORDER), mix_norm_g[:, 1:]], axis=1)
    wq_t, wq_s, wk_t, wv = _relayout_mla(mla_w_uq, mla_w_ukv)
    psw = _rope_swap_matrix()
    cos_t, sin_t = _rope_lane_tables(seq)
    tri = (jnp.arange(ATTN_TILE)[:, None] > jnp.arange(ATTN_TILE)[None, :]).astype(BF16)
    order = list(A_HEAD_ORDER)
    def with_first_block_variant(b):
        prev = jnp.arange(2 * BLOCK) < BLOCK
        return jnp.stack([b, jnp.where(prev, NEG_INF, b)], axis=0) * LOG2E

    bias_a = _band_bias_masked(rel_bias_table[:, :N_HEADS], 1, A_WINDOW - 1)
    bias_a_rows = with_first_block_variant(jnp.concatenate([bias_a[hh] for hh in order], axis=0))
    sink_rows = jnp.concatenate([jnp.broadcast_to(attn_sinks[:, hh, None, None] * LOG2E, (DEPTH, BLOCK, LANES))
                                 for hh in order], axis=1)
    biases_d = [with_first_block_variant(_band_bias_masked(rel_bias_table[:, N_HEADS:], rate, window // rate).reshape(
        N_HEADS // 2, 2 * BLOCK, 2 * BLOCK)) for window, rate in D_PATTERNS]
    router_p = jnp.pad(moe_router, ((0, 0), (0, 0), (0, LANES - N_EXPERTS))).astype(BF16)
    wd_d = ffn_w_down.astype(BF16)

    xf = x.reshape(n, D_MODEL)
    xb = xf
    for layer in range(DEPTH):
        h = matmul_ws(xb, w_in_p, layer, *IN_PROJ_TILE, BF16)
        hd = matmul_ws(xb, w_in_d, layer, *IN_PROJ_D_TILE, F32)
        h3 = h.reshape(batch, seq, IN_WIDTH_P)
        ga = swa_attention(h3, bias_a_rows, sink_rows[layer], batch=batch, seq=seq, blocks=SWA_BLOCKS)
        q_b, k_b, v_b = mla_up(h, mla_q_norm[layer][None], mla_kv_norm[layer][None], wq_t[layer], wq_s[layer],
                               wk_t[layer], wv[layer], psw, cos_t, sin_t, seq=seq, tm=MLA_UP_ROWS)
        gb = mla_attention(q_b.reshape(batch, seq, -1), k_b.reshape(batch, seq, -1),
                           v_b.reshape(batch, seq, -1), batch=batch, seq=seq, tq=ATTN_TILE, heads=MLA_HEADS)
        gc = stick_breaking_attention(h3, tri, batch=batch, seq=seq, tq=ATTN_TILE, pairs=STICK_PAIRS)
        gd = dilated_attention(hd.reshape(batch, seq, IN_WIDTH_D), biases_d, batch=batch, seq=seq)
        i = layer // 2
        router = router_p[i] if layer % 2 == 1 else jnp.zeros((D_MODEL, LANES), BF16)
        xf, xb, xp, logits = out_proj_ln(ga.reshape(n, -1), gb.reshape(n, -1), gc.reshape(n, -1), gd.reshape(n, -1),
                                     xf, mix_g_p[layer], w_o_p, layer, ln1_g[layer][None], ln1_b[layer][None],
                                     router, tm=OUT_PROJ_ROWS)
        if layer % 2 == 0:
            hmid = ffn_gate_up(xb, ffn_w_gate, ffn_w_up, i, tm=FFN_UP_TILE[0], tf=FFN_UP_TILE[1])
            xf, xb = ffn_down_ln(hmid, wd_d, i, xf, ln2_g[layer][None], ln2_b[layer][None],
                                 tm=FFN_DOWN_TILE[0], tk=FFN_DOWN_TILE[1])
        else:
            pos, gates, src, tile_expert, tile_rows = _route(logits[:, :N_EXPERTS], MOE_ROWS)
            xs = jnp.take(xp, src, axis=0, mode="clip")
            hs = moe_gate_up(tile_expert, tile_rows, xs, moe_w_gate, moe_w_up, i, tm=MOE_ROWS, tf=MOE_COLS)
            ys = moe_down(tile_expert, tile_rows, hs, moe_w_down, i, tm=MOE_ROWS, tn=MOE_COLS)
            y0 = jnp.take(ys, pos[:, 0], axis=0, mode="clip")
            y1 = jnp.take(ys, pos[:, 1], axis=0, mode="clip")
            xf, xb = combine_ln(xf, y0, y1, gates, ln2_g[layer][None], ln2_b[layer][None], tm=COMBINE_ROWS,
                                group=MOE_COLS // 2)
    return xf.reshape(batch, seq, D_MODEL)
```

```python
import functools
import math

import jax
import jax.numpy as jnp
from jax import lax
from jax.experimental import pallas as pl
from jax.experimental.pallas import tpu as pltpu

D_MODEL = 2048
DEPTH = 4
HEAD_DIM = 64
N_HEADS = 8
GROUP_WIDTH = N_HEADS * HEAD_DIM
BLOCK = 128
A_WINDOW = 128
B_NOPE_DIM = 64
B_ROPE_DIM = 32
B_Q_LORA = 384
B_KV_LORA = 256
ROPE_THETA = 10000.0
D_PATTERNS = ((128, 1), (512, 4), (2048, 16))
NUM_BUCKETS = 32
T5_MAX_DISTANCE = 2048
D_FF = 5632
N_EXPERTS = 8
TOP_K = 2
NEG_INF = -1e30
LN_EPS = 1e-5
RMS_EPS = 1e-6
ALPHA = (2 * DEPTH) ** 0.25

LANES = 128
VMEM_LIMIT = 56 * 1024 * 1024

IN_PROJ_TILE = (1024, 1024)
IN_PROJ_D_TILE = (1024, 768)
RELAYOUT_COLS = 256
SWA_BLOCKS = 16
MLA_UP_ROWS = 1024
ATTN_TILE = 256
MLA_HEADS = 8
STICK_PAIRS = 4
OUT_PROJ_ROWS = 256
FFN_UP_TILE = (1024, 512)
FFN_DOWN_TILE = (512, D_FF // 2)
MOE_ROWS = 1024
MOE_COLS = 512
MOE_ROW_STEP = 128
COMBINE_ROWS = 512

COL_QA, COL_QC, COL_KC, COL_CQ, COL_KA = 0, 512, 1024, 1536, 1920
COL_VC, COL_CKV, COL_VA, COL_KR = 2048, 2560, 2816, 2944
IN_WIDTH_P = 3072
IN_WIDTH_D = 3 * GROUP_WIDTH
A_HEAD_ORDER = (0, 4, 1, 5, 2, 6, 3, 7)
ROPE_LANE0 = 64
LOG2E = math.log2(math.e)
EXP2_UNDERFLOW = 151.0
QK_SCALE = HEAD_DIM ** -0.5 * LOG2E

BF16 = jnp.bfloat16
F32 = jnp.float32


def _params(*sem):
    return pltpu.CompilerParams(dimension_semantics=sem, vmem_limit_bytes=VMEM_LIMIT)


def _dot(a, b):
    return jnp.dot(a, b, preferred_element_type=F32)


def _dot_nt(a, b):
    return lax.dot_general(a, b, (((1,), (1,)), ((), ())), preferred_element_type=F32)


def _lo_lanes(rows):
    return lax.broadcasted_iota(jnp.int32, (rows, LANES), 1) < HEAD_DIM


def _split_pair(t, lo):
    zero = jnp.zeros_like(t)
    return jnp.concatenate([jnp.where(lo, t, zero), jnp.where(lo, zero, t)], axis=0)


def _matmul_kernel(a_ref, w_ref, o_ref):
    o_ref[...] = _dot_nt(a_ref[...].astype(BF16), w_ref[0]).astype(o_ref.dtype)


def matmul_ws(a, w, layer, tm, tn, out_dtype):
    m, k = a.shape
    n = w.shape[1]
    return pl.pallas_call(
        _matmul_kernel,
        grid=(n // tn, m // tm),
        in_specs=[pl.BlockSpec((tm, k), lambda j, i: (i, 0)),
                  pl.BlockSpec((1, tn, k), lambda j, i: (layer, j, 0))],
        out_specs=pl.BlockSpec((tm, tn), lambda j, i: (i, j)),
        out_shape=jax.ShapeDtypeStruct((m, n), out_dtype),
        compiler_params=_params("arbitrary", "arbitrary"),
        name="matmul_ws",
    )(a, w)


def _swa_kernel(q_ref, kp_ref, kc_ref, vp_ref, vc_ref, bias_ref, sink_ref, o_ref, *, blocks):
    lo = _lo_lanes(BLOCK)
    k_all = jnp.concatenate([kp_ref[0], kc_ref[0]], axis=0)
    v_all = jnp.concatenate([vp_ref[0], vc_ref[0]], axis=0)
    first = jnp.where(pl.program_id(1) == 0, 1, 0)
    for bi in range(blocks):
        q_rows = slice(bi * BLOCK, (bi + 1) * BLOCK)
        kb = k_all[bi * BLOCK:(bi + 2) * BLOCK]
        vb = v_all[bi * BLOCK:(bi + 2) * BLOCK]
        which = first if bi == 0 else 0
        for c in range(GROUP_WIDTH // LANES):
            rows = slice(2 * c * BLOCK, 2 * (c + 1) * BLOCK)
            qs = _split_pair(q_ref[0, q_rows, c * LANES:(c + 1) * LANES], lo) * QK_SCALE
            s = _dot_nt(qs, kb) + bias_ref[which, rows, :]
            sink = sink_ref[rows, :]
            m = jnp.maximum(jnp.max(s, axis=1, keepdims=True), sink)
            p = jnp.exp2(s - jnp.concatenate([m, m], axis=1))
            den = jnp.sum(p, axis=1, keepdims=True) + jnp.exp2(sink - m)
            o = _dot(p.astype(BF16), vb) / den
            o_ref[0, q_rows, c * LANES:(c + 1) * LANES] = jnp.where(lo, o[:BLOCK], o[BLOCK:]).astype(o_ref.dtype)


def swa_attention(h3, bias_rows, sink_rows, *, batch, seq, blocks):
    span = blocks * BLOCK
    kb, vb = COL_KA // LANES, COL_VA // LANES
    prev = lambda col: pl.BlockSpec((1, BLOCK, LANES), lambda b, n: (b, jnp.maximum(n * blocks - 1, 0), col))
    cur = lambda col: pl.BlockSpec((1, span, LANES), lambda b, n: (b, n, col))
    return pl.pallas_call(
        functools.partial(_swa_kernel, blocks=blocks),
        grid=(batch, seq // span),
        in_specs=[pl.BlockSpec((1, span, GROUP_WIDTH), lambda b, n: (b, n, COL_QA // GROUP_WIDTH)),
                  prev(kb), cur(kb), prev(vb), cur(vb),
                  pl.BlockSpec((2, N_HEADS * BLOCK, 2 * BLOCK), lambda b, n: (0, 0, 0)),
                  pl.BlockSpec((N_HEADS * BLOCK, LANES), lambda b, n: (0, 0))],
        out_specs=pl.BlockSpec((1, span, GROUP_WIDTH), lambda b, n: (b, n, 0)),
        out_shape=jax.ShapeDtypeStruct((batch, seq, GROUP_WIDTH), BF16),
        compiler_params=_params("arbitrary", "arbitrary"),
        name="swa_attn",
    )(h3, h3, h3, h3, h3, bias_rows, sink_rows)


def _dilated_kernel(q_ref, k_ref, v_ref, b0_ref, b1_ref, b2_ref, o_ref, osc, msc, dsc):
    seq = q_ref.shape[1]
    lo = _lo_lanes(BLOCK)
    for p,((_, rate), b_ref) in enumerate(zip(D_PATTERNS, (b0_ref, b1_ref, b2_ref))):
        nb = seq // (rate * BLOCK)

        def unit(u, carry, p=p, rate=rate, b_ref=b_ref, nb=nb):
            n = u // rate
            res = u - n * rate
            rows = pl.ds(n * (BLOCK * rate) + res, BLOCK, stride=rate)
            qs = _split_pair((q_ref[0, rows, :] * QK_SCALE).astype(BF16), lo)
            kc = k_ref[0, rows, :].astype(BF16)
            vc = v_ref[0, rows, :].astype(BF16)
            if nb > 1:
                prow = pl.ds(jnp.maximum(n - 1, 0) * (BLOCK * rate) + res, BLOCK, stride=rate)
                kb = jnp.concatenate([k_ref[0, prow, :].astype(BF16), kc], axis=0)
                vb = jnp.concatenate([v_ref[0, prow, :].astype(BF16), vc], axis=0)
                s = _dot_nt(qs, kb) + b_ref[jnp.where(n == 0, 1, 0), 0]
            else:
                vb = vc
                s = _dot_nt(qs, kc) + b_ref[0, 0, :, BLOCK:2 * BLOCK]
            m = jnp.max(s, axis=1, keepdims=True)
            e = jnp.exp2(s - m)
            den = jnp.broadcast_to(jnp.sum(e, axis=1, keepdims=True), (2 * BLOCK, LANES))
            mx = jnp.broadcast_to(m, (2 * BLOCK, LANES))
            o = _dot(e.astype(BF16), vb)
            osc[p, rows, :] = jnp.where(lo, o[:BLOCK], o[BLOCK:])
            msc[p, rows, :] = jnp.where(lo, mx[:BLOCK], mx[BLOCK:])
            dsc[p, rows, :] = jnp.where(lo, den[:BLOCK], den[BLOCK:])
            return carry

        lax.fori_loop(0, nb * rate, unit, 0, unroll=16)

    chunk = 2 * BLOCK

    def merge(t, carry):
        r = pl.ds(pl.multiple_of(t * chunk, chunk), chunk)
        m0, m1, m2 = msc[0, r, :], msc[1, r, :], msc[2, r, :]
        m = jnp.maximum(jnp.maximum(m0, m1), m2)
        e0, e1, e2 = jnp.exp2(m0 - m), jnp.exp2(m1 - m), jnp.exp2(m2 - m)
        o = ((e0 * osc[0, r, :] + e1 * osc[1, r, :] + e2 * osc[2, r, :])
             / (e0 * dsc[0, r, :] + e1 * dsc[1, r, :] + e2 * dsc[2, r, :]))
        o_ref[0, r, :] = o.astype(o_ref.dtype)
        return carry

    lax.fori_loop(0, seq // chunk, merge, 0)


def dilated_attention(hd3, biases, *, batch, seq):
    pairs = N_HEADS // 2
    bspec = pl.BlockSpec((2, 1, 2 * BLOCK, 2 * BLOCK), lambda b, p: (0, p, 0, 0))
    return pl.pallas_call(
        _dilated_kernel,
        grid=(batch, pairs),
        in_specs=[pl.BlockSpec((1, seq, LANES), lambda b, p: (b, 0, p)),
                  pl.BlockSpec((1, seq, LANES), lambda b, p: (b, 0, pairs + p)),
                  pl.BlockSpec((1, seq, LANES), lambda b, p: (b, 0, 2 * pairs + p)),
                  bspec, bspec, bspec],
        out_specs=pl.BlockSpec((1, seq, LANES), lambda b, p: (b, 0, p)),
        out_shape=jax.ShapeDtypeStruct((batch, seq, GROUP_WIDTH), BF16),
        scratch_shapes=[pltpu.VMEM((len(D_PATTERNS), seq, LANES), F32)] * 3,
        compiler_params=_params("arbitrary", "arbitrary"),
        name="dilated_attn",
    )(hd3, hd3, hd3, *biases)


def _mla_up_kernel(cq_ref, ckv_ref, kr_ref, qn_ref, kvn_ref, wq_ref, wqs_ref, wk_ref, wv_ref, psw_ref,
                   cos_ref, sin_ref, q_ref, k_ref, v_ref, *, scale):
    def rms(x_ref, g_ref):
        x = x_ref[...].astype(F32)
        return (x * lax.rsqrt(jnp.mean(x * x, axis=-1, keepdims=True) + RMS_EPS) * g_ref[...]).astype(BF16)

    xq = rms(cq_ref, qn_ref)
    xkv = rms(ckv_ref, kvn_ref)
    cos = cos_ref[...]
    sin = sin_ref[...]
    t = _dot(xq, wq_ref[...])
    ts = _dot(xq, wqs_ref[...])
    kn = _dot(xkv, wk_ref[...])
    kr = kr_ref[...]
    rk = kr.astype(F32) * cos + _dot(kr, psw_ref[...]) * sin
    for h in range(N_HEADS):
        sl = slice(h * LANES, (h + 1) * LANES)
        q_ref[:, sl] = ((t[:, sl] * cos + ts[:, sl] * sin) * scale).astype(BF16)
        k_ref[:, sl] = (kn[:, sl] + rk).astype(BF16)
    v_ref[...] = _dot(xkv, wv_ref[...]).astype(BF16)


def mla_up(h, q_norm, kv_norm, wq, wqs, wk, wv, psw, cos_t, sin_t, *, seq, tm):
    n = h.shape[0]
    w = N_HEADS * LANES
    const = lambda shape: pl.BlockSpec(shape, lambda i: (0, 0))
    spt = seq // tm
    return pl.pallas_call(
        functools.partial(_mla_up_kernel, scale=(B_NOPE_DIM + B_ROPE_DIM) ** -0.5 * LOG2E),
        grid=(n // tm,),
        in_specs=[pl.BlockSpec((tm, B_Q_LORA), lambda i: (i, COL_CQ // B_Q_LORA)),
                  pl.BlockSpec((tm, B_KV_LORA), lambda i: (i, COL_CKV // B_KV_LORA)),
                  pl.BlockSpec((tm, LANES), lambda i: (i, COL_KR // LANES)),
                  const((1, B_Q_LORA)), const((1, B_KV_LORA)),
                  const((B_Q_LORA, w)), const((B_Q_LORA, w)), const((B_KV_LORA, w)),
                  const((B_KV_LORA, GROUP_WIDTH)), const((LANES, LANES)),
                  pl.BlockSpec((tm, LANES), lambda i: (i % spt, 0)),
                  pl.BlockSpec((tm, LANES), lambda i: (i % spt, 0))],
        out_specs=[pl.BlockSpec((tm, w), lambda i: (i, 0)),
                   pl.BlockSpec((tm, w), lambda i: (i, 0)),
                   pl.BlockSpec((tm, GROUP_WIDTH), lambda i: (i, 0))],
        out_shape=[jax.ShapeDtypeStruct((n, w), BF16), jax.ShapeDtypeStruct((n, w), BF16),
                   jax.ShapeDtypeStruct((n, GROUP_WIDTH), BF16)],
        compiler_params=_params("arbitrary"),
        name="mla_up",
    )(h, h, h, q_norm, kv_norm, wq, wqs, wk, wv, psw, cos_t, sin_t)


def _mla_attn_kernel(q_ref, k_ref, v_ref, o_ref, *, tq, heads):
    i = pl.program_id(2)
    lo = _lo_lanes(tq)
    row = lax.broadcasted_iota(jnp.int32, (tq, tq), 0)
    col = lax.broadcasted_iota(jnp.int32, (tq, tq), 1)
    causal = col <= row

    def logits(j):
        start = pl.multiple_of(j * tq, tq)
        return tuple(_dot_nt(q_ref[0, :, hh * LANES:(hh + 1) * LANES],
                             k_ref[0, pl.ds(start, tq), hh * LANES:(hh + 1) * LANES]) for hh in range(heads))

    def update(j, stats, s_all, masked):
        start = pl.multiple_of(j * tq, tq)
        out = []
        for pp in range(heads // 2):
            v = v_ref[0, pl.ds(start, tq), pp * LANES:(pp + 1) * LANES]
            v_both = jnp.concatenate([jnp.where(lo, v, jnp.zeros_like(v)), jnp.where(lo, jnp.zeros_like(v), v)], axis=0)
            ps, scales = [], []
            for hh in (2 * pp, 2 * pp + 1):
                m, l = stats[5 * pp + 2 * (hh % 2)], stats[5 * pp + 2 * (hh % 2) + 1]
                s = s_all[hh]
                if masked:
                    s = jnp.where(causal, s, NEG_INF)
                m_new = jnp.maximum(m, jnp.max(s, axis=1, keepdims=True))
                p = jnp.exp2(s - m_new)
                a = jnp.exp2(m - m_new)
                part = functools.reduce(jnp.add, [p[:, c * LANES:(c + 1) * LANES] for c in range(tq // LANES)])
                ps.append(p.astype(BF16))
                scales.append(a)
                out += [m_new, a * l + part]
            acc = stats[5 * pp + 4]
            out.append(jnp.where(lo, scales[0], scales[1]) * acc + _dot(jnp.concatenate(ps, axis=1), v_both))
        return tuple(out)

    pair_init = (jnp.full((tq, 1), NEG_INF, F32), jnp.zeros((tq, LANES), F32)) * 2 + (jnp.zeros((tq, LANES), F32),)
    stats = lax.fori_loop(0, i, lambda j, st: update(j, st, logits(j), False), pair_init * (heads // 2))
    stats = update(i, stats, logits(i), True)
    for pp in range(heads // 2):
        _, l0, _, l1, acc = stats[5 * pp:5 * pp + 5]
        den = jnp.where(lo, jnp.sum(l0, axis=1, keepdims=True), jnp.sum(l1, axis=1, keepdims=True))
        o_ref[0, :, pp * LANES:(pp + 1) * LANES] = (acc / den).astype(o_ref.dtype)


def mla_attention(q, k, v, *, batch, seq, tq, heads):
    pairs = heads // 2
    return pl.pallas_call(
        functools.partial(_mla_attn_kernel, tq=tq, heads=heads),
        grid=(batch, N_HEADS // heads, seq // tq),
        in_specs=[pl.BlockSpec((1, tq, heads * LANES), lambda b, g, i: (b, i, g)),
                  pl.BlockSpec((1, seq, heads * LANES), lambda b, g, i: (b, 0, g)),
                  pl.BlockSpec((1, seq, pairs * LANES), lambda b, g, i: (b, 0, g))],
        out_specs=pl.BlockSpec((1, tq, pairs * LANES), lambda b, g, i: (b, i, g)),
        out_shape=jax.ShapeDtypeStruct((batch, seq, GROUP_WIDTH), BF16),
        compiler_params=_params("arbitrary", "arbitrary", "arbitrary"),
        name="mla_attn",
    )(q, k, v)


def _stick_kernel(q_ref, k_ref, v_ref, tri_ref, o_ref, *, tq, pairs):
    i = pl.program_id(2)
    lo = _lo_lanes(tq)
    row = lax.broadcasted_iota(jnp.int32, (2 * tq, tq), 0)
    col = lax.broadcasted_iota(jnp.int32, (2 * tq, tq), 1)
    strict = col < jnp.where(row >= tq, row - tq, row)
    tri = tri_ref[...]
    qs = [_split_pair(q_ref[0, :, pp * LANES:(pp + 1) * LANES], lo) * QK_SCALE for pp in range(pairs)]

    def chunk(pp, j, run, masked):
        start = pl.multiple_of(j * tq, tq)
        z = _dot_nt(qs[pp], k_ref[0, pl.ds(start, tq), pp * LANES:(pp + 1) * LANES])
        soft = jnp.log2(1.0 + jnp.exp2(-jnp.abs(z)))
        keep = -(jnp.maximum(z, 0.0) + soft)
        hit = jnp.minimum(z, 0.0) - soft
        if masked:
            keep = jnp.where(strict, keep, 0.0)
        hi = keep.astype(BF16)
        rest = (keep - hi.astype(F32)).astype(BF16)
        after = _dot(hi, tri) + _dot(rest, tri)
        a = jnp.exp2(hit + after)
        if masked:
            a = jnp.where(strict, a, 0.0)
        contrib = jnp.exp2(run) * _dot(a.astype(BF16), v_ref[0, pl.ds(start, tq), pp * LANES:(pp + 1) * LANES])
        return contrib, run + jnp.sum(keep, axis=1, keepdims=True)

    state = []
    has_prev = i > 0
    for pp in range(pairs):
        acc, run = chunk(pp, i, jnp.zeros((2 * tq, 1), F32), True)
        contrib, run_prev = chunk(pp, jnp.maximum(i - 1, 0), run, False)
        state += [jnp.where(has_prev, run_prev, run), acc + jnp.where(has_prev, contrib, 0.0)]

    def cond(carry):
        t = carry[0]
        live = carry[1]
        for pp in range(1, pairs):
            live = jnp.maximum(live, carry[1 + 2 * pp])
        return jnp.logical_and(t < i, jnp.max(live) > -EXP2_UNDERFLOW)

    def body(carry):
        t = carry[0]
        out = [t + 1]
        for pp in range(pairs):
            contrib, run = chunk(pp, i - 1 - t, carry[1 + 2 * pp], False)
            out += [run, carry[2 + 2 * pp] + contrib]
        return tuple(out)

    final = lax.while_loop(cond, body, (jnp.int32(1), *state))
    for pp in range(pairs):
        acc = final[2 + 2 * pp]
        o_ref[0, :, pp * LANES:(pp + 1) * LANES] = jnp.where(lo, acc[:tq], acc[tq:]).astype(o_ref.dtype)


def stick_breaking_attention(h3, tri, *, batch, seq, tq, pairs):
    w = pairs * LANES
    return pl.pallas_call(
        functools.partial(_stick_kernel, tq=tq, pairs=pairs),
        grid=(batch, GROUP_WIDTH // w, seq // tq),
        in_specs=[pl.BlockSpec((1, tq, w), lambda b, g, i: (b, i, COL_QC // w + g)),
                  pl.BlockSpec((1, seq, w), lambda b, g, i: (b, 0, COL_KC // w + g)),
                  pl.BlockSpec((1, seq, w), lambda b, g, i: (b, 0, COL_VC // w + g)),
                  pl.BlockSpec((tq, tq), lambda b, g, i: (0, 0))],
        out_specs=pl.BlockSpec((1, tq, w), lambda b, g, i: (b, i, g)),
        out_shape=jax.ShapeDtypeStruct((batch, seq, GROUP_WIDTH), BF16),
        compiler_params=_params("arbitrary", "arbitrary", "arbitrary"),
        name="stick_breaking",
    )(h3, h3, h3, tri)


def _layer_norm(r, g, b):
    mu = jnp.mean(r, axis=-1, keepdims=True)
    d = r - mu
    var = jnp.mean(d * d, axis=-1, keepdims=True)
    return d * lax.rsqrt(var + LN_EPS) * g + b


def _pack_bf16_pairs(xb):
    half = xb.shape[1] // 2
    lo = lax.bitcast_convert_type(xb[:, :half].astype(F32), jnp.uint32) >> 16
    hi = lax.bitcast_convert_type(xb[:, half:].astype(F32), jnp.uint32) & jnp.uint32(0xFFFF0000)
    return lax.bitcast_convert_type(hi | lo, F32)


def _unpack_bf16_pairs(xp):
    u = lax.bitcast_convert_type(xp, jnp.uint32)
    lo = lax.bitcast_convert_type(u << 16, F32).astype(BF16)
    hi = lax.bitcast_convert_type(u & jnp.uint32(0xFFFF0000), F32).astype(BF16)
    return jnp.concatenate([lo, hi], axis=1)


def _out_proj_kernel(ga_ref, gb_ref, gc_ref, gd_ref, x_ref, mixg_ref, wo_ref, g_ref, b_ref, rt_ref,
                     xo_ref, xb_ref, xp_ref, lg_ref):
    mixed = None
    for gi, grp in enumerate((ga_ref, gb_ref, gc_ref, gd_ref)):
        xg = grp[...].astype(F32)
        y = xg * lax.rsqrt(jnp.mean(xg * xg, axis=-1, keepdims=True) + RMS_EPS) * mixg_ref[gi:gi + 1, :]
        part = _dot(y.astype(BF16), wo_ref[0, gi * GROUP_WIDTH:(gi + 1) * GROUP_WIDTH, :])
        mixed = part if mixed is None else mixed + part
    x1 = _layer_norm(ALPHA * x_ref[...].astype(F32) + mixed, g_ref[...], b_ref[...])
    xo_ref[...] = x1
    xb = x1.astype(BF16)
    xb_ref[...] = xb
    xp_ref[...] = _pack_bf16_pairs(xb)
    lg_ref[...] = _dot(xb, rt_ref[...])


def out_proj_ln(ga, gb, gc, gd, x, mix_g, wo, layer, ln_g, ln_b, router, *, tm):
    n = x.shape[0]
    row = lambda w: pl.BlockSpec((tm, w), lambda i: (i, 0))
    const = lambda shape: pl.BlockSpec(shape, lambda i: (0, 0))
    return pl.pallas_call(
        _out_proj_kernel,
        grid=(n // tm,),
        in_specs=[row(GROUP_WIDTH)] * 4 + [row(D_MODEL), const((4, GROUP_WIDTH)),
                                          pl.BlockSpec((1, D_MODEL, D_MODEL), lambda i: (layer, 0, 0)),
                                          const((1, D_MODEL)), const((1, D_MODEL)), const((D_MODEL, LANES))],
        out_specs=[row(D_MODEL), row(D_MODEL), row(D_MODEL // 2), row(LANES)],
        out_shape=[jax.ShapeDtypeStruct((n, D_MODEL), F32), jax.ShapeDtypeStruct((n, D_MODEL), BF16),
                   jax.ShapeDtypeStruct((n, D_MODEL // 2), F32), jax.ShapeDtypeStruct((n, LANES), F32)],
        compiler_params=_params("arbitrary"),
        name="out_proj_ln",
    )(ga, gb, gc, gd, x, mix_g, wo, ln_g, ln_b, router)


def _silu_mul(g, u):
    return g * (1.0 / (1.0 + jnp.exp(-g))) * u


def _gate_up_kernel(x_ref, wg_ref, wu_ref, o_ref, wgb_ref, wub_ref):
    @pl.when(pl.program_id(1) == 0)
    def _():
        wgb_ref[...] = wg_ref[0].astype(BF16)
        wub_ref[...] = wu_ref[0].astype(BF16)

    x = x_ref[...]
    o_ref[...] = _silu_mul(_dot(x, wgb_ref[...]), _dot(x, wub_ref[...])).astype(o_ref.dtype)


def ffn_gate_up(xb, wg, wu, li, *, tm, tf):
    n = xb.shape[0]
    f = wg.shape[2]
    return pl.pallas_call(
        _gate_up_kernel,
        grid=(f // tf, n // tm),
        in_specs=[pl.BlockSpec((tm, D_MODEL), lambda j, i: (i, 0)),
                  pl.BlockSpec((1, D_MODEL, tf), lambda j, i: (li, 0, j)),
                  pl.BlockSpec((1, D_MODEL, tf), lambda j, i: (li, 0, j))],
        out_specs=pl.BlockSpec((tm, tf), lambda j, i: (i, j)),
        out_shape=jax.ShapeDtypeStruct((n, f), BF16),
        scratch_shapes=[pltpu.VMEM((D_MODEL, tf), BF16), pltpu.VMEM((D_MODEL, tf), BF16)],
        compiler_params=_params("arbitrary", "arbitrary"),
        name="ffn_gate_up",
    )(xb, wg, wu)


def _down_ln_kernel(h_ref, wd_ref, x_ref, g_ref, b_ref, xo_ref, xb_ref):
    kk = pl.program_id(1)
    part = _dot(h_ref[...], wd_ref[0])

    @pl.when(kk == 0)
    def _():
        xo_ref[...] = part

    @pl.when(kk > 0)
    def _():
        xo_ref[...] += part

    @pl.when(kk == pl.num_programs(1) - 1)
    def _():
        x2 = _layer_norm(ALPHA * x_ref[...] + xo_ref[...], g_ref[...], b_ref[...])
        xo_ref[...] = x2
        xb_ref[...] = x2.astype(BF16)


def ffn_down_ln(hmid, wd, li, x, ln_g, ln_b, *, tm, tk):
    n, f = hmid.shape
    return pl.pallas_call(
        _down_ln_kernel,
        grid=(n // tm, f // tk),
        in_specs=[pl.BlockSpec((tm, tk), lambda i, k: (i, k)),
                  pl.BlockSpec((1, tk, D_MODEL), lambda i, k: (li, k, 0)),
                  pl.BlockSpec((tm, D_MODEL), lambda i, k: (i, 0)),
                  pl.BlockSpec((1, D_MODEL), lambda i, k: (0, 0)),
                  pl.BlockSpec((1, D_MODEL), lambda i, k: (0, 0))],
        out_specs=[pl.BlockSpec((tm, D_MODEL), lambda i, k: (i, 0)),
                   pl.BlockSpec((tm, D_MODEL), lambda i, k: (i, 0))],
        out_shape=[jax.ShapeDtypeStruct((n, D_MODEL), F32), jax.ShapeDtypeStruct((n, D_MODEL), BF16)],
        compiler_params=_params("arbitrary", "arbitrary"),
        name="ffn_down_ln",
    )(hmid, wd, x, ln_g, ln_b)


def _new_expert(te_ref, t):
    return jnp.logical_or(t == 0, te_ref[t] != te_ref[jnp.maximum(t - 1, 0)])


def _by_valid_rows(valid, tm, rows):
    for n in range(0, tm + 1, MOE_ROW_STEP):
        @pl.when(jnp.logical_and(valid > n - MOE_ROW_STEP, valid <= n))
        def _(n=n):
            rows(n)


def _moe_gate_up_kernel(te_ref, tv_ref, x_ref, wg_ref, wu_ref, o_ref, wgb_ref, wub_ref):
    t = pl.program_id(1)

    @pl.when(_new_expert(te_ref, t))
    def _():
        wgb_ref[...] = wg_ref[0, 0].astype(BF16)
        wub_ref[...] = wu_ref[0, 0].astype(BF16)

    def rows(n):
        pad = o_ref.shape[0] - n
        if n:
            x = _unpack_bf16_pairs(x_ref[pad:, :])
            o_ref[pad:, :] = _silu_mul(_dot(x, wgb_ref[...]), _dot(x, wub_ref[...])).astype(o_ref.dtype)
        if pad:
            o_ref[0:pad, :] = jnp.zeros((pad, o_ref.shape[1]), o_ref.dtype)

    _by_valid_rows(tv_ref[t], x_ref.shape[0], rows)


def moe_gate_up(tile_expert, tile_rows, xs, wg, wu, li, *, tm, tf):
    p = xs.shape[0]
    f = wg.shape[3]
    grid_spec = pltpu.PrefetchScalarGridSpec(
        num_scalar_prefetch=2,
        grid=(f // tf, p // tm),
        in_specs=[pl.BlockSpec((tm, D_MODEL // 2), lambda j, t, te, tv: (jnp.where(tv[t] > 0, t, 0), 0)),
                  pl.BlockSpec((1, 1, D_MODEL, tf), lambda j, t, te, tv: (li, te[t], 0, j)),
                  pl.BlockSpec((1, 1, D_MODEL, tf), lambda j, t, te, tv: (li, te[t], 0, j))],
        out_specs=pl.BlockSpec((tm, tf), lambda j, t, te, tv: (t, j)),
        scratch_shapes=[pltpu.VMEM((D_MODEL, tf), BF16), pltpu.VMEM((D_MODEL, tf), BF16)],
    )
    return pl.pallas_call(
        _moe_gate_up_kernel,
        grid_spec=grid_spec,
        out_shape=jax.ShapeDtypeStruct((p, f), BF16),
        compiler_params=_params("arbitrary", "arbitrary"),
        name="moe_gate_up",
    )(tile_expert, tile_rows, xs, wg, wu)


def _moe_down_kernel(te_ref, tv_ref, h_ref, wd_ref, o_ref, wdb_ref):
    t = pl.program_id(1)

    @pl.when(_new_expert(te_ref, t))
    def _():
        wdb_ref[...] = wd_ref[0, 0].astype(BF16)

    def rows(n):
        pad = o_ref.shape[0] - n
        if n:
            o_ref[pad:, :] = _pack_bf16_pairs(_dot(h_ref[pad:, :], wdb_ref[...]).astype(BF16))
        if pad:
            o_ref[0:pad, :] = jnp.zeros((pad, o_ref.shape[1]), o_ref.dtype)

    _by_valid_rows(tv_ref[t], h_ref.shape[0], rows)


def moe_down(tile_expert, tile_rows, hs, wd, li, *, tm, tn):
    p, f = hs.shape
    grid_spec = pltpu.PrefetchScalarGridSpec(
        num_scalar_prefetch=2,
        grid=(D_MODEL // tn, p // tm),
        in_specs=[pl.BlockSpec((tm, f), lambda j, t, te, tv: (jnp.where(tv[t] > 0, t, 0), 0)),
                  pl.BlockSpec((1, 1, f, tn), lambda j, t, te, tv: (li, te[t], 0, j))],
        out_specs=pl.BlockSpec((tm, tn // 2), lambda j, t, te, tv: (t, j)),
        scratch_shapes=[pltpu.VMEM((f, tn), BF16)],
    )
    return pl.pallas_call(
        _moe_down_kernel,
        grid_spec=grid_spec,
        out_shape=jax.ShapeDtypeStruct((p, D_MODEL // 2), F32),
        compiler_params=_params("arbitrary", "arbitrary"),
        name="moe_down",
    )(tile_expert, tile_rows, hs, wd)


def _combine_ln_kernel(x_ref, y0_ref, y1_ref, gate_ref, g_ref, b_ref, xo_ref, xb_ref, *, group):
    def unpack(y_ref):
        parts = [_unpack_bf16_pairs(y_ref[:, c:c + group]) for c in range(0, y_ref.shape[1], group)]
        return jnp.concatenate(parts, axis=1).astype(F32)

    f = gate_ref[:, 0:1] * unpack(y0_ref) + gate_ref[:, 1:2] * unpack(y1_ref)
    x2 = _layer_norm(ALPHA * x_ref[...] + f, g_ref[...], b_ref[...])
    xo_ref[...] = x2
    xb_ref[...] = x2.astype(BF16)


def combine_ln(x, y0, y1, gates, ln_g, ln_b, *, tm, group):
    n = x.shape[0]
    row = pl.BlockSpec((tm, D_MODEL), lambda i: (i, 0))
    packed = pl.BlockSpec((tm, D_MODEL // 2), lambda i: (i, 0))
    const = pl.BlockSpec((1, D_MODEL), lambda i: (0, 0))
    return pl.pallas_call(
        functools.partial(_combine_ln_kernel, group=group),
        grid=(n // tm,),
        in_specs=[row, packed, packed, pl.BlockSpec((tm, TOP_K), lambda i: (i, 0)), const, const],
        out_specs=[row, row],
        out_shape=[jax.ShapeDtypeStruct((n, D_MODEL), F32), jax.ShapeDtypeStruct((n, D_MODEL), BF16)],
        compiler_params=_params("arbitrary"),
        name="combine_ln",
    )(x, y0, y1, gates, ln_g, ln_b)


def _t5_bucket(dist):
    max_exact = NUM_BUCKETS // 2
    d = jnp.maximum(dist, 1).astype(F32)
    large = max_exact + (jnp.log(d / max_exact) / math.log(T5_MAX_DISTANCE / max_exact)
                         * (NUM_BUCKETS - max_exact)).astype(jnp.int32)
    large = jnp.minimum(large, NUM_BUCKETS - 1)
    return jnp.where(dist < max_exact, dist, large)


def _band_bias_masked(table, stride, max_dist):
    dist = jnp.arange(BLOCK)[:, None] + BLOCK - jnp.arange(2 * BLOCK)[None, :]
    onehot = jax.nn.one_hot(_t5_bucket(jnp.maximum(dist, 0) * stride), NUM_BUCKETS, dtype=F32)
    bias = jnp.einsum("qkb,bh->hqk", onehot, table.astype(F32), precision=lax.Precision.HIGHEST)
    valid = (dist >= 0) & (dist <= max_dist)
    return jnp.where(valid[None], bias, NEG_INF)


def _rope_lane_tables(seq):
    half = B_ROPE_DIM // 2
    inv = ROPE_THETA ** (-jnp.arange(0, B_ROPE_DIM, 2, dtype=F32) / B_ROPE_DIM)
    ang = jnp.arange(seq, dtype=F32)[:, None] * inv[None, :]
    cos, sin = jnp.cos(ang), jnp.sin(ang)
    ones = jnp.ones((seq, ROPE_LANE0), F32)
    zeros = jnp.zeros((seq, ROPE_LANE0), F32)
    pad = jnp.zeros((seq, LANES - ROPE_LANE0 - 2 * half), F32)
    cos_t = jnp.concatenate([ones, cos, cos, pad], axis=1)
    sin_t = jnp.concatenate([zeros, -sin, sin, pad], axis=1)
    return cos_t, sin_t


def _permute_heads(t, axis, order):
    parts = [lax.slice_in_dim(t, hh * HEAD_DIM, (hh + 1) * HEAD_DIM, axis=axis) for hh in order]
    return jnp.concatenate(parts, axis=axis)


IN_SEGMENTS = (("qa", 512), ("ka", 128), ("va", 128), ("cq", 384), ("ckv", 256), ("kr", 32),
               ("qc", 512), ("kc", 512), ("vc", 512), ("qd", 512), ("kd", 512), ("vd", 512))
IN_WIDTH = sum(size for _, size in IN_SEGMENTS)


def _relayout_w_in_kernel(w_ref, main_ref, d_ref):
    w = w_ref[0]
    cols = w.shape[1]
    seg = {}
    start = 0
    for name, size in IN_SEGMENTS:
        seg[name] = w[start:start + size, :]
        start += size
    qa = jnp.concatenate([seg["qa"][hh * HEAD_DIM:(hh + 1) * HEAD_DIM, :] for hh in A_HEAD_ORDER], axis=0)
    kr = jnp.concatenate([jnp.zeros((ROPE_LANE0, cols), F32), seg["kr"],
                          jnp.zeros((LANES - ROPE_LANE0 - B_ROPE_DIM, cols), F32)], axis=0)
    main = jnp.concatenate([qa, seg["qc"], seg["kc"], seg["cq"], seg["ka"], seg["vc"], seg["ckv"], seg["va"], kr],
                           axis=0)
    main_ref[0] = main.astype(BF16)
    d_ref[0] = jnp.concatenate([seg["qd"], seg["kd"], seg["vd"]], axis=0).astype(BF16)


def _relayout_w_in(w_in, *, tc):
    d = w_in.shape[0]
    return pl.pallas_call(
        _relayout_w_in_kernel,
        grid=(d, D_MODEL // tc),
        in_specs=[pl.BlockSpec((1, IN_WIDTH, tc), lambda l, i: (l, 0, i))],
        out_specs=[pl.BlockSpec((1, IN_WIDTH_P, tc), lambda l, i: (l, 0, i)),
                   pl.BlockSpec((1, IN_WIDTH_D, tc), lambda l, i: (l, 0, i))],
        out_shape=[jax.ShapeDtypeStruct((d, IN_WIDTH_P, D_MODEL), BF16),
                   jax.ShapeDtypeStruct((d, IN_WIDTH_D, D_MODEL), BF16)],
        compiler_params=_params("arbitrary", "arbitrary"),
        name="relayout_w_in",
    )(jnp.swapaxes(w_in, 1, 2))


def _relayout_mla(w_uq, w_ukv):
    d = w_uq.shape[0]
    half = B_ROPE_DIM // 2
    wq = w_uq.astype(BF16).reshape(d, B_Q_LORA, N_HEADS, B_NOPE_DIM + B_ROPE_DIM)
    nope, r1, r2 = wq[..., :B_NOPE_DIM], wq[..., B_NOPE_DIM:B_NOPE_DIM + half], wq[..., B_NOPE_DIM + half:]
    z32 = jnp.zeros(wq.shape[:3] + (LANES - B_NOPE_DIM - B_ROPE_DIM,), BF16)
    z64 = jnp.zeros(wq.shape[:3] + (B_NOPE_DIM,), BF16)
    wq_t = jnp.concatenate([nope, r1, r2, z32], axis=-1).reshape(d, B_Q_LORA, N_HEADS * LANES)
    wq_s = jnp.concatenate([z64, r2, r1, z32], axis=-1).reshape(d, B_Q_LORA, N_HEADS * LANES)
    wkv = w_ukv.astype(BF16).reshape(d, B_KV_LORA, N_HEADS, 2 * HEAD_DIM)
    zk = jnp.zeros(wkv.shape[:3] + (LANES - B_NOPE_DIM,), BF16)
    wk_t = jnp.concatenate([wkv[..., :B_NOPE_DIM], zk], axis=-1).reshape(d, B_KV_LORA, N_HEADS * LANES)
    wv = wkv[..., B_NOPE_DIM:].reshape(d, B_KV_LORA, GROUP_WIDTH)
    return wq_t, wq_s, wk_t, wv


def _rope_swap_matrix():
    half = B_ROPE_DIM // 2
    src = jnp.arange(LANES)[:, None]
    dst = jnp.arange(LANES)[None, :]
    first = (dst >= ROPE_LANE0) & (dst < ROPE_LANE0 + half) & (src == dst + half)
    second = (dst >= ROPE_LANE0 + half) & (dst < ROPE_LANE0 + 2 * half) & (src == dst - half)
    return (first | second).astype(BF16)


def _route(logits, tm):
    n = logits.shape[0]
    top_logits, top_idx = lax.top_k(logits, TOP_K)
    gates = jax.nn.softmax(top_logits, axis=-1)
    onehot = jax.nn.one_hot(top_idx, N_EXPERTS, dtype=jnp.int32)
    member = jnp.sum(onehot, axis=1)
    rank = jnp.cumsum(member, axis=0) - member
    counts = jnp.sum(member, axis=0)
    padded = ((counts + tm - 1) // tm) * tm
    ends = jnp.cumsum(padded)
    first = ends - counts
    pos = jnp.sum(onehot * (first[None, None, :] + rank[:, None, :]), axis=-1)
    n_rows = TOP_K * n + N_EXPERTS * tm
    src = (jnp.arange(n_rows, dtype=jnp.int32) % n).at[pos.reshape(-1)].set(
        jnp.repeat(jnp.arange(n, dtype=jnp.int32), TOP_K))

    tile_start = jnp.arange(n_rows // tm, dtype=jnp.int32) * tm
    tile_expert = jnp.minimum(jnp.sum((tile_start[:, None] >= ends[None, :]).astype(jnp.int32), axis=1),
                              N_EXPERTS - 1)
    onehot_te = jax.nn.one_hot(tile_expert, N_EXPERTS, dtype=jnp.int32)
    group_first = jnp.sum(onehot_te * first[None, :], axis=1)
    tile_rows = jnp.clip(tile_start + tm - group_first, 0, tm)
    tile_rows = jnp.where(tile_start < ends[-1], tile_rows, 0)
    return pos, gates, src, tile_expert.astype(jnp.int32), tile_rows.astype(jnp.int32)


def kernel(x, w_in, w_o, mla_q_norm, mla_kv_norm, mla_w_uq, mla_w_ukv, attn_sinks, rel_bias_table, mix_norm_g,
           ln1_g, ln1_b, ln2_g, ln2_b, ffn_w_gate, ffn_w_up, ffn_w_down, moe_router, moe_w_gate, moe_w_up,
           moe_w_down):
    batch, seq, _ = x.shape
    n = batch * seq

    w_in_p, w_in_d = _relayout_w_in(w_in, tc=RELAYOUT_COLS)
    w_o_p = jnp.concatenate([_permute_heads(w_o[:, :GROUP_WIDTH], 1, A_HEAD_ORDER), w_o[:, GROUP_WIDTH:]],
                            axis=1).astype(BF16)
    mix_g_p = jnp.concatenate([_permute_heads(mix_norm_g[:, :1], 2, A_HEAD_ORDER), mix_norm_g[:, 1:]], axis=1)
    wq_t, wq_s, wk_t, wv = _relayout_mla(mla_w_uq, mla_w_ukv)
    psw = _rope_swap_matrix()
    cos_t, sin_t = _rope_lane_tables(seq)
    tri = (jnp.arange(ATTN_TILE)[:, None] > jnp.arange(ATTN_TILE)[None, :]).astype(BF16)
    order = list(A_HEAD_ORDER)
    def with_first_block_variant(b):
        prev = jnp.arange(2 * BLOCK) < BLOCK
        return jnp.stack([b, jnp.where(prev, NEG_INF, b)], axis=0) * LOG2E

    bias_a = _band_bias_masked(rel_bias_table[:, :N_HEADS], 1, A_WINDOW - 1)
    bias_a_rows = with_first_block_variant(jnp.concatenate([bias_a[hh] for hh in order], axis=0))
    sink_rows = jnp.concatenate([jnp.broadcast_to(attn_sinks[:, hh, None, None] * LOG2E, (DEPTH, BLOCK, LANES))
                                 for hh in order], axis=1)
    biases_d = [with_first_block_variant(_band_bias_masked(rel_bias_table[:, N_HEADS:], rate, window // rate).reshape(
        N_HEADS // 2, 2 * BLOCK, 2 * BLOCK)) for window, rate in D_PATTERNS]
    router_p = jnp.pad(moe_router, ((0, 0), (0, 0), (0, LANES - N_EXPERTS))).astype(BF16)
    wd_d = ffn_w_down.astype(BF16)

    xf = x.reshape(n, D_MODEL)
    xb = xf
    for layer in range(DEPTH):
        h = matmul_ws(xb, w_in_p, layer, *IN_PROJ_TILE, BF16)
        hd = matmul_ws(xb, w_in_d, layer, *IN_PROJ_D_TILE, F32)
        h3 = h.reshape(batch, seq, IN_WIDTH_P)
        ga = swa_attention(h3, bias_a_rows, sink_rows[layer], batch=batch, seq=seq, blocks=SWA_BLOCKS)
        q_b, k_b, v_b = mla_up(h, mla_q_norm[layer][None], mla_kv_norm[layer][None], wq_t[layer], wq_s[layer],
                               wk_t[layer], wv[layer], psw, cos_t, sin_t, seq=seq, tm=MLA_UP_ROWS)
        gb = mla_attention(q_b.reshape(batch, seq, -1), k_b.reshape(batch, seq, -1),
                           v_b.reshape(batch, seq, -1), batch=batch, seq=seq, tq=ATTN_TILE, heads=MLA_HEADS)
        gc = stick_breaking_attention(h3, tri, batch=batch, seq=seq, tq=ATTN_TILE, pairs=STICK_PAIRS)
        gd = dilated_attention(hd.reshape(batch, seq, IN_WIDTH_D), biases_d, batch=batch, seq=seq)
        i = layer // 2
        router = router_p[i] if layer % 2 == 1 else jnp.zeros((D_MODEL, LANES), BF16)
        xf, xb, xp, logits = out_proj_ln(ga.reshape(n, -1), gb.reshape(n, -1), gc.reshape(n, -1), gd.reshape(n, -1),
                                     xf, mix_g_p[layer], w_o_p, layer, ln1_g[layer][None], ln1_b[layer][None],
                                     router, tm=OUT_PROJ_ROWS)
        if layer % 2 == 0:
            hmid = ffn_gate_up(xb, ffn_w_gate, ffn_w_up, i, tm=FFN_UP_TILE[0], tf=FFN_UP_TILE[1])
            xf, xb = ffn_down_ln(hmid, wd_d, i, xf, ln2_g[layer][None], ln2_b[layer][None],
                                 tm=FFN_DOWN_TILE[0], tk=FFN_DOWN_TILE[1])
        else:
            pos, gates, src, tile_expert, tile_rows = _route(logits[:, :N_EXPERTS], MOE_ROWS)
            xs = jnp.take(xp, src, axis=0, mode="clip")
            hs = moe_gate_up(tile_expert, tile_rows, xs, moe_w_gate, moe_w_up, i, tm=MOE_ROWS, tf=MOE_COLS)
            ys = moe_down(tile_expert, tile_rows, hs, moe_w_down, i, tm=MOE_ROWS, tn=MOE_COLS)
            y0 = jnp.take(ys, pos[:, 0], axis=0, mode="clip")
            y1 = jnp.take(ys, pos[:, 1], axis=0, mode="clip")
            xf, xb = combine_ln(xf, y0, y1, gates, ln2_g[layer][None], ln2_b[layer][None], tm=COMBINE_ROWS,
                                group=MOE_COLS // 2)
    return xf.reshape(batch, seq, D_MODEL)
```
